```python
import math
import jax
import jax.numpy as jnp
from jax import lax
import numpy as np

D_MODEL = 1024
BATCH = 16
SEQ = 256
DEPTH = 2
DEC_BATCH = 4
DEC_SEQ = 2048
PAST_LEN = 512

GRID_W = 64
BRANCH_W = 512
N_BRANCH = 4
H_A = 4
DH_A = 128
MLSTM_CHUNK = 128
H_B = 8
DH_B = 64
NA_ROWS = 8
NA_COLS = 16
NA_QCB = 16
NA_KSPAN = 32
H_C = 8
KV_C = 2
DH_C = 64
SWA_WINDOW = 128
SWA_BLK = 128
H_D = 4
DH_D = 64
QBLK = 128
N_GROUPS = 4
EXP_PER_GROUP = 4
N_EXPERTS = N_GROUPS * EXP_PER_GROUP
TOP_K_INNER = 2
D_EXPERT = 256
ROPE_BASE = 10000.0
EPS = 1e-6
IN_SPLITS = (H_A * DH_A, H_A * DH_A, H_A * DH_A, H_A * DH_A, 2 * H_A, 2 * H_A,
             H_B * DH_B, H_B * DH_B, H_B * DH_B,
             H_C * DH_C, KV_C * DH_C, KV_C * DH_C,
             H_D * 2 * DH_D, H_D * 2 * DH_D, H_D * 2 * DH_D,
             N_BRANCH * D_MODEL)
N_IN = sum(IN_SPLITS)

kernel_name = 'hybrid_diffusion_trunk_step'

F32 = jnp.float32


def rmsnorm(x, g):
    xf = x.astype(F32)
    y = xf * lax.rsqrt(jnp.mean(xf * xf, axis=-1, keepdims=True) + EPS)
    return (y * g.astype(F32)).astype(x.dtype)


def axial_rope(x):
    T, dh = x.shape[1], x.shape[-1]
    half = dh // 2
    nf = half // 2
    inv = ROPE_BASE ** (-jnp.arange(nf, dtype=F32) / nf)
    t = jnp.arange(T)

    def rot(xh, pos):
        ang = pos.astype(F32)[:, None] * inv[None, :]
        cos = jnp.cos(ang)[:, None, :].astype(x.dtype)
        sin = jnp.sin(ang)[:, None, :].astype(x.dtype)
        x1, x2 = xh[..., :nf], xh[..., nf:]
        return jnp.concatenate([x1 * cos - x2 * sin, x1 * sin + x2 * cos], axis=-1)

    return jnp.concatenate([rot(x[..., :half], t // GRID_W), rot(x[..., half:], t % GRID_W)], axis=-1)


def mlstm_chunkwise(q, k, v, ig, fg, C0, n0, m0):
    B, T, H, dh = q.shape
    nc = T // MLSTM_CHUNK

    def chunks(a):
        a = a.reshape((B, nc, MLSTM_CHUNK) + a.shape[2:])
        return jnp.moveaxis(a, (1, 3), (0, 2))

    qc = chunks(q.astype(F32))
    kc = chunks(k.astype(F32) * (dh ** -0.5))
    vc = chunks(v.astype(F32))
    ic = chunks(ig.astype(F32))
    lfc = chunks(jax.nn.log_sigmoid(fg.astype(F32)))
    causal = jnp.tril(jnp.ones((MLSTM_CHUNK, MLSTM_CHUNK), bool))

    def step(carry, inp):
        C, n, m = carry
        qx, kx, vx, ix, lf = inp
        b = jnp.cumsum(lf, axis=-1)
        logd = jnp.where(causal, b[..., :, None] - b[..., None, :] + ix[..., None, :], -jnp.inf)
        inter = b + m[..., None]
        m_t = jnp.maximum(inter, jnp.max(logd, axis=-1))
        w = jnp.exp(logd - m_t[..., None])
        a = jnp.exp(inter - m_t)
        s = jnp.einsum('bhtk,bhsk->bhts', qx, kx) * w
        num = a[..., None] * jnp.einsum('bhvk,bhtk->bhtv', C, qx) + jnp.einsum('bhts,bhsv->bhtv', s, vx)
        den = a * jnp.einsum('bhk,bhtk->bht', n, qx) + jnp.sum(s, axis=-1)
        h = num / jnp.maximum(jnp.abs(den), jnp.exp(-m_t))[..., None]
        bL = b[..., -1]
        g = bL[..., None] - b + ix
        m_new = jnp.maximum(bL + m, jnp.max(g, axis=-1))
        decay = jnp.exp(bL + m - m_new)
        wk = jnp.exp(g - m_new[..., None])
        C_new = decay[..., None, None] * C + jnp.einsum('bhs,bhsv,bhsk->bhvk', wk, vx, kx)
        n_new = decay[..., None] * n + jnp.einsum('bhs,bhsk->bhk', wk, kx)
        return (C_new, n_new, m_new), h

    (C, n, m), hs = lax.scan(step, (C0.astype(F32), n0.astype(F32), m0.astype(F32)), (qc, kc, vc, ic, lfc))
    h = jnp.moveaxis(hs, (0, 2), (1, 3)).reshape(B, T, H, dh)
    return h.astype(q.dtype), C, n, m


def mlstm_bidir(q, k, v, ig, fg, C0, n0, m0):
    flip = lambda a: jnp.flip(a, axis=1)
    h_f, C_f, n_f, m_f = mlstm_chunkwise(q, k, v, ig[:, :, 0], fg[:, :, 0], C0[:, 0], n0[:, 0], m0[:, 0])
    h_b, C_b, n_b, m_b = mlstm_chunkwise(flip(q), flip(k), flip(v), flip(ig[:, :, 1]), flip(fg[:, :, 1]),
                                         C0[:, 1], n0[:, 1], m0[:, 1])
    h = h_f + flip(h_b)
    return h, jnp.stack([C_f, C_b], axis=1), jnp.stack([n_f, n_b], axis=1), jnp.stack([m_f, m_b], axis=1)


def ctx_attend(q, k, v, sink=None):
    B, T = q.shape[:2]
    nq = T // QBLK
    qb = jnp.moveaxis(q.reshape((B, nq, QBLK) + q.shape[2:]), 1, 0)
    scale = q.shape[-1] ** -0.5

    def one(qx):
        s = jnp.einsum('bqhgd,bkhd->bhgqk', qx, k).astype(F32) * scale
        if sink is not None:
            sk = jnp.broadcast_to(sink.astype(F32)[None, :, :, None, None], s.shape[:-1] + (1,))
            p = jax.nn.softmax(jnp.concatenate([s, sk], axis=-1), axis=-1)[..., :-1]
        else:
            p = jax.nn.softmax(s, axis=-1)
        return jnp.einsum('bhgqk,bkhd->bqhgd', p.astype(v.dtype), v)

    out = lax.map(one, qb)
    return jnp.moveaxis(out, 0, 1).reshape(B, T, -1)


def nat_latent(q, k, v, kc, vc, rpb):
    B, T, H, d = q.shape
    rows = T // GRID_W
    kr = min(NA_ROWS, rows)
    ncb = GRID_W // NA_QCB
    r = np.arange(rows)
    rs = np.clip(r - kr // 2, 0, rows - kr)
    krow = rs[:, None] + np.arange(kr)[None, :]
    j = np.arange(ncb)
    ks = np.clip(j * NA_QCB - NA_COLS // 2, 0, GRID_W - NA_KSPAN)
    kcol = ks[:, None] + np.arange(NA_KSPAN)[None, :]
    tok = krow[:, None, :, None] * GRID_W + kcol[None, :, None, :]
    nkey = kr * NA_KSPAN
    kb = jnp.take(k, tok.reshape(-1), axis=1).reshape(B, rows, ncb, nkey, H, d)
    vb = jnp.take(v, tok.reshape(-1), axis=1).reshape(B, rows, ncb, nkey, H, d)
    qb = q.reshape(B, rows, ncb, NA_QCB, H, d)
    qcol = j[:, None] * NA_QCB + np.arange(NA_QCB)[None, :]
    cs = np.clip(qcol - NA_COLS // 2, 0, GRID_W - NA_COLS)
    colvalid = (kcol[:, None, :] >= cs[:, :, None]) & (kcol[:, None, :] < cs[:, :, None] + NA_COLS)
    valid = np.broadcast_to(colvalid[:, :, None, :], (ncb, NA_QCB, kr, NA_KSPAN)).reshape(ncb, NA_QCB, nkey)
    ri = krow - r[:, None] + (NA_ROWS - 1)
    ci = np.clip(kcol[:, None, :] - qcol[:, :, None] + (NA_COLS - 1), 0, 2 * NA_COLS - 2)
    bias = rpb[:, ri[:, None, None, :, None], ci[None, :, :, None, :]]
    bias = jnp.transpose(bias.reshape(H, rows, ncb, NA_QCB, nkey), (1, 2, 0, 3, 4))
    scale = d ** -0.5
    s_nb = jnp.einsum('brjqhd,brjkhd->brjhqk', qb, kb).astype(F32) * scale + bias.astype(F32)[None]
    s_nb = jnp.where(valid[None, None, :, None], s_nb, -jnp.inf)
    s_ctx = jnp.einsum('brjqhd,bphd->brjhqp', qb, kc).astype(F32) * scale
    p = jax.nn.softmax(jnp.concatenate([s_nb, s_ctx], axis=-1), axis=-1).astype(v.dtype)
    out = (jnp.einsum('brjhqk,brjkhd->brjqhd', p[..., :nkey], vb)
           + jnp.einsum('brjhqp,bphd->brjqhd', p[..., nkey:], vc))
    return out.reshape(B, T, H * d)


def swa_latent(q, k, v, kc, vc, sink):
    B, T, KV, G, d = q.shape
    nb = T // SWA_BLK

    def slabs(a):
        ap = jnp.pad(a, ((0, 0), (SWA_BLK, SWA_BLK), (0, 0), (0, 0))).reshape(B, nb + 2, SWA_BLK, KV, d)
        return jnp.concatenate([ap[:, :nb], ap[:, 1:nb + 1], ap[:, 2:]], axis=2)

    ksl, vsl = slabs(k), slabs(v)
    qb = q.reshape(B, nb, SWA_BLK, KV, G, d)
    scale = d ** -0.5
    s_band = jnp.einsum('bnqhgd,bnkhd->bnhgqk', qb, ksl).astype(F32) * scale
    qi = np.arange(SWA_BLK)[:, None]
    kj = np.arange(3 * SWA_BLK)[None, :]
    kpos = np.arange(nb)[:, None, None] * SWA_BLK - SWA_BLK + kj[None]
    valid = (np.abs(kj - SWA_BLK - qi)[None] <= SWA_WINDOW) & (kpos >= 0) & (kpos < T)
    s_band = jnp.where(valid[None, :, None, None], s_band, -jnp.inf)
    s_ctx = jnp.einsum('bnqhgd,bphd->bnhgqp', qb, kc).astype(F32) * scale
    sk = jnp.broadcast_to(sink.astype(F32)[None, None, :, :, None, None], s_band.shape[:-1] + (1,))
    p = jax.nn.softmax(jnp.concatenate([s_band, s_ctx, sk], axis=-1), axis=-1).astype(v.dtype)
    nk = 3 * SWA_BLK
    P = kc.shape[1]
    out = (jnp.einsum('bnhgqk,bnkhd->bnqhgd', p[..., :nk], vsl)
           + jnp.einsum('bnhgqp,bphd->bnqhgd', p[..., nk:nk + P], vc))
    return out.reshape(B, T, KV * G * d)


def diff_attend(q, k, v, lam):
    B, T, H = q.shape[:3]
    nq = T // QBLK
    qb = jnp.moveaxis(q.reshape((B, nq, QBLK) + q.shape[2:]), 1, 0)
    scale = q.shape[-1] ** -0.5

    def one(qx):
        s = jnp.einsum('bqhcd,bkhcd->bhcqk', qx, k).astype(F32) * scale
        p = jax.nn.softmax(s, axis=-1)
        pd = p[:, :, 0] - lam * p[:, :, 1]
        return jnp.einsum('bhqk,bkhe->bqhe', pd.astype(v.dtype), v)

    out = lax.map(one, qb)
    return jnp.moveaxis(out, 0, 1).reshape(B, T, H, v.shape[-1])


def mixer_sublayer(h, lp, ctx, lam_init):
    (w_in, i_bias, f_bias, a_norm, rpb, sink, lq1, lk1, lq2, lk2, subln, w_branch, w_out) = lp
    B, T, _ = h.shape
    offs = np.cumsum(IN_SPLITS)[:-1].tolist()
    (aq, ak, av, ao, aig, afg, bq, bk, bv, cq, ck, cv, dq, dk, dv, gt) = jnp.split(h @ w_in, offs, axis=-1)
    aq, ak, av = (a.reshape(B, T, H_A, DH_A) for a in (aq, ak, av))
    aig = aig.reshape(B, T, 2, H_A) + i_bias
    afg = afg.reshape(B, T, 2, H_A) + f_bias
    bq, bk, bv = (a.reshape(B, T, H_B, DH_B) for a in (bq, bk, bv))
    cq = cq.reshape(B, T, KV_C, H_C // KV_C, DH_C)
    ck = ck.reshape(B, T, KV_C, DH_C)
    cv = cv.reshape(B, T, KV_C, DH_C)
    dq = dq.reshape(B, T, H_D, 2, DH_D)
    dk = dk.reshape(B, T, H_D, 2, DH_D)
    dv = dv.reshape(B, T, H_D, 2 * DH_D)
    sink = sink.reshape(KV_C, H_C // KV_C)
    lam = jnp.exp(jnp.sum(lq1.astype(F32) * lk1.astype(F32))) - jnp.exp(jnp.sum(lq2.astype(F32) * lk2.astype(F32))) + lam_init
    if ctx is None:
        C0 = jnp.zeros((B, 2, H_A, DH_A, DH_A), F32)
        n0 = jnp.zeros((B, 2, H_A, DH_A), F32)
        m0 = jnp.zeros((B, 2, H_A), F32)
        ya, Cs, ns, ms = mlstm_bidir(aq, ak, av, aig, afg, C0, n0, m0)
        yb = ctx_attend(bq[:, :, :, None], bk, bv)
        yc = ctx_attend(cq, ck, cv, sink)
        yd = diff_attend(dq, dk, dv, lam)
        new = (bk, bv, ck, cv, dk.reshape(B, T, H_D, 2 * DH_D), dv, Cs, ns, ms)
    else:
        (kb_c, vb_c, kc_c, vc_c, kd_c, vd_c, C0, n0, m0) = ctx
        ya, _, _, _ = mlstm_bidir(aq, ak, av, aig, afg, C0, n0, m0)
        yb = nat_latent(bq, bk, bv, kb_c, vb_c, rpb)
        cq_r = axial_rope(cq.reshape(B, T, H_C, DH_C)).reshape(cq.shape)
        yc = swa_latent(cq_r, axial_rope(ck), cv, kc_c, vc_c, sink)
        dq_r = axial_rope(dq.reshape(B, T, 2 * H_D, DH_D)).reshape(dq.shape)
        dk_r = axial_rope(dk.reshape(B, T, 2 * H_D, DH_D)).reshape(dk.shape)
        kd = jnp.concatenate([dk_r, kd_c.reshape(B, -1, H_D, 2, DH_D)], axis=1)
        vd = jnp.concatenate([dv, vd_c], axis=1)
        yd = diff_attend(dq_r, kd, vd, lam)
        new = None
    ya = jax.nn.sigmoid(ao) * rmsnorm(ya, a_norm.reshape(H_A, DH_A)).reshape(B, T, BRANCH_W)
    yd = (rmsnorm(yd, subln) * (1.0 - lam_init)).reshape(B, T, BRANCH_W)
    branches = jnp.stack([ya, yb, yc, yd], axis=2)
    proj = jnp.einsum('btnw,nwd->btnd', branches, w_branch)
    gates = jax.nn.sigmoid(gt.reshape(B, T, N_BRANCH, D_MODEL))
    out = jnp.sum(gates * proj, axis=2) @ w_out
    return out, new


def hier_moe(h, w_rg, w_re, w_g, w_u, w_d):
    lead = h.shape[:-1]
    x = h.reshape(-1, D_MODEL)
    glog = (x @ w_rg).astype(F32)
    gprob = jax.nn.softmax(glog, axis=-1)
    gsel = jnp.argmax(glog, axis=-1)
    gw = jnp.take_along_axis(gprob, gsel[:, None], axis=-1)
    elog = (x @ w_re).astype(F32).reshape(-1, N_GROUPS, EXP_PER_GROUP)
    elog_g = jnp.take_along_axis(elog, gsel[:, None, None], axis=1)[:, 0]
    top_p, top_i = lax.top_k(jax.nn.softmax(elog_g, axis=-1), TOP_K_INNER)
    top_p = top_p / jnp.sum(top_p, axis=-1, keepdims=True)
    eidx = gsel[:, None] * EXP_PER_GROUP + top_i
    gate = jnp.sum(jax.nn.one_hot(eidx, N_EXPERTS, dtype=F32) * (gw * top_p)[..., None], axis=1)
    hid = jax.nn.silu(jnp.einsum('nd,edf->nef', x, w_g)) * jnp.einsum('nd,edf->nef', x, w_u)
    y = jnp.einsum('nef,efd->nd', hid * gate[..., None].astype(hid.dtype), w_d)
    return y.reshape(lead + (D_MODEL,))


def trunk_layer(x, cond, lp_norm, lp_mix, lp_moe, ctx, lam_init):
    norm_mix_l, norm_ffn_l, w_mod_l, b_mod_l = lp_norm
    mod = jax.nn.silu(cond) @ w_mod_l + b_mod_l
    sh1, sc1, g1, sh2, sc2, g2 = jnp.split(mod[..., None, :], 6, axis=-1)
    h = rmsnorm(x, norm_mix_l) * (1.0 + sc1) + sh1
    y, new = mixer_sublayer(h, lp_mix, ctx, lam_init)
    x = x + g1 * y
    h = rmsnorm(x, norm_ffn_l) * (1.0 + sc2) + sh2
    x = x + g2 * hier_moe(h, *lp_moe)
    return x, new


def setup_inputs(seed: int = 0) -> dict:
    key = jax.random.key(seed)
    ks = iter(jax.random.split(key, 48))
    nrm = lambda shape, s=1.0: jax.random.normal(next(ks), shape, F32) * s
    D = D_MODEL
    return {
        'x_prompt': nrm((BATCH, SEQ, D)),
        'x_sample': nrm((DEC_BATCH, DEC_SEQ, D)),
        'cache_nat_k': nrm((DEC_BATCH, DEPTH, PAST_LEN, H_B, DH_B)),
        'cache_nat_v': nrm((DEC_BATCH, DEPTH, PAST_LEN, H_B, DH_B)),
        'cache_swa_k': nrm((DEC_BATCH, DEPTH, PAST_LEN, KV_C, DH_C)),
        'cache_swa_v': nrm((DEC_BATCH, DEPTH, PAST_LEN, KV_C, DH_C)),
        'cache_diff_k': nrm((DEC_BATCH, DEPTH, PAST_LEN, H_D, 2 * DH_D)),
        'cache_diff_v': nrm((DEC_BATCH, DEPTH, PAST_LEN, H_D, 2 * DH_D)),
        'state_mlstm_C': nrm((DEC_BATCH, DEPTH, 2, H_A, DH_A, DH_A), 0.1),
        'state_mlstm_n': nrm((DEC_BATCH, DEPTH, 2, H_A, DH_A), 0.1),
        'state_mlstm_m': nrm((DEC_BATCH, DEPTH, 2, H_A), 0.5),
        'c': nrm((DEC_BATCH, D)),
        'c_ctx': nrm((D,)),
        'norm_mix': 1.0 + nrm((DEPTH, D), 0.02),
        'norm_ffn': 1.0 + nrm((DEPTH, D), 0.02),
        'norm_final': 1.0 + nrm((D,), 0.02),
        'w_mod': nrm((DEPTH, D, 6 * D), 0.5 * D ** -0.5),
        'b_mod': nrm((DEPTH, 6 * D), 0.02),
        'w_in': nrm((DEPTH, D, N_IN), D ** -0.5),
        'mlstm_i_bias': nrm((DEPTH, 2, H_A), 0.1),
        'mlstm_f_bias': 3.0 + nrm((DEPTH, 2, H_A), 0.5),
        'mlstm_norm': 1.0 + nrm((DEPTH, H_A * DH_A), 0.02),
        'nat_rpb': nrm((DEPTH, H_B, 2 * NA_ROWS - 1, 2 * NA_COLS - 1), 0.02),
        'swa_sink': nrm((DEPTH, H_C), 0.5),
        'diff_lq1': nrm((DEPTH, DH_D), 0.1),
        'diff_lk1': nrm((DEPTH, DH_D), 0.1),
        'diff_lq2': nrm((DEPTH, DH_D), 0.1),
        'diff_lk2': nrm((DEPTH, DH_D), 0.1),
        'diff_subln': 1.0 + nrm((DEPTH, 2 * DH_D), 0.02),
        'w_branch': nrm((DEPTH, N_BRANCH, BRANCH_W, D), BRANCH_W ** -0.5),
        'w_out': nrm((DEPTH, D, D), D ** -0.5),
        'router_group': nrm((DEPTH, D, N_GROUPS), D ** -0.5),
        'router_expert': nrm((DEPTH, D, N_EXPERTS), D ** -0.5),
        'w_exp_gate': nrm((DEPTH, N_EXPERTS, D, D_EXPERT), D ** -0.5),
        'w_exp_up': nrm((DEPTH, N_EXPERTS, D, D_EXPERT), D ** -0.5),
        'w_exp_down': nrm((DEPTH, N_EXPERTS, D_EXPERT, D), D_EXPERT ** -0.5),
    }


def reference(x_prompt, x_sample, cache_nat_k, cache_nat_v, cache_swa_k, cache_swa_v, cache_diff_k,
              cache_diff_v, state_mlstm_C, state_mlstm_n, state_mlstm_m, c, c_ctx, norm_mix, norm_ffn,
              norm_final, w_mod, b_mod, w_in, mlstm_i_bias, mlstm_f_bias, mlstm_norm, nat_rpb, swa_sink,
              diff_lq1, diff_lk1, diff_lq2, diff_lk2, diff_subln, w_branch, w_out, router_group,
              router_expert, w_exp_gate, w_exp_up, w_exp_down):
    yp, ys = x_prompt, x_sample
    nk, nv, sk, sv, dk, dv, mC, mn, mm = [], [], [], [], [], [], [], [], []
    for l in range(DEPTH):
        lam_init = 0.8 - 0.6 * math.exp(-0.3 * l)
        lp_norm = (norm_mix[l], norm_ffn[l], w_mod[l], b_mod[l])
        lp_mix = (w_in[l], mlstm_i_bias[l], mlstm_f_bias[l], mlstm_norm[l], nat_rpb[l], swa_sink[l],
                  diff_lq1[l], diff_lk1[l], diff_lq2[l], diff_lk2[l], diff_subln[l], w_branch[l], w_out[l])
        lp_moe = (router_group[l], router_expert[l], w_exp_gate[l], w_exp_up[l], w_exp_down[l])
        yp, ctx_new = trunk_layer(yp, c_ctx, lp_norm, lp_mix, lp_moe, None, lam_init)
        for lst, t in zip((nk, nv, sk, sv, dk, dv, mC, mn, mm), ctx_new):
            lst.append(t)
        ctx_l = (cache_nat_k[:, l], cache_nat_v[:, l], cache_swa_k[:, l], cache_swa_v[:, l],
                 cache_diff_k[:, l], cache_diff_v[:, l], state_mlstm_C[:, l], state_mlstm_n[:, l],
                 state_mlstm_m[:, l])
        ys, _ = trunk_layer(ys, c, lp_norm, lp_mix, lp_moe, ctx_l, lam_init)
    y_prompt = rmsnorm(yp, norm_final)
    y_sample = rmsnorm(ys, norm_final)
    new_nat_k = jnp.stack(nk, axis=1)
    new_nat_v = jnp.stack(nv, axis=1)
    new_swa_k = jnp.stack(sk, axis=1)
    new_swa_v = jnp.stack(sv, axis=1)
    new_diff_k = jnp.stack(dk, axis=1)
    new_diff_v = jnp.stack(dv, axis=1)
    new_mlstm_C = jnp.stack(mC, axis=1)
    new_mlstm_n = jnp.stack(mn, axis=1)
    new_mlstm_m = jnp.stack(mm, axis=1)
    return (y_prompt, y_sample, new_nat_k, new_nat_v, new_swa_k, new_swa_v, new_diff_k, new_diff_v,
            new_mlstm_C, new_mlstm_n, new_mlstm_m)
```

```python
import functools
import math

import numpy as np
import jax
import jax.numpy as jnp
from jax import lax
from jax.experimental import pallas as pl
from jax.experimental.pallas import tpu as pltpu

F32 = jnp.float32
BF16 = jnp.bfloat16

D_MODEL = 1024
DEPTH = 2
GRID_W = 64
BRANCH_W = 512
N_BRANCH = 4
H_A, DH_A, MLSTM_CHUNK = 4, 128, 128
H_B, DH_B = 8, 64
NA_ROWS, NA_COLS = 8, 16
H_C, KV_C, DH_C = 8, 2, 64
SWA_WINDOW = 128
H_D, DH_D = 4, 64
N_GROUPS, EXP_PER_GROUP = 4, 4
N_EXPERTS = N_GROUPS * EXP_PER_GROUP
D_EXPERT = 256
ROPE_BASE = 10000.0
EPS = 1e-6

LANES = 128
HALF = 64
COL_TILE = 512
N_COL_TILES = 20
N_PROJ = COL_TILE * N_COL_TILES
T_AQ, T_AK, T_AV, T_AO, T_BQ, T_BK, T_BV, T_CQ, T_DQ, T_DK, T_DV, T_MISC, T_GT = range(13)
MISC_GATE_OFF = 256
VMEM_LIMIT = 56 * 1024 * 1024
NEG_INF = float("-inf")

_NT = (((1,), (1,)), ((), ()))
_TN = (((0,), (0,)), ((), ()))


def _cparams(*sem):
    return pltpu.CompilerParams(dimension_semantics=sem, vmem_limit_bytes=VMEM_LIMIT)


def _dot(a, b):
    return jnp.dot(a, b, preferred_element_type=F32)


def _dot_nt(a, b):
    return lax.dot_general(a, b, _NT, preferred_element_type=F32)


def _dot_tn(a, b):
    return lax.dot_general(a, b, _TN, preferred_element_type=F32)


def _rms(x, g):
    return x * lax.rsqrt(jnp.mean(x * x, axis=-1, keepdims=True) + EPS) * g


def _mod_kernel(c_ref, w_ref, b_ref, o_ref):
    c = c_ref[...]
    s = (c * jax.nn.sigmoid(c)).astype(BF16)
    o_ref[0] = _dot(s, w_ref[0].astype(BF16)) + b_ref[0]


def _modulation(cond8, w_mod, b_mod):
    tn = 1536
    n_out = 6 * D_MODEL
    return pl.pallas_call(
        _mod_kernel,
        out_shape=jax.ShapeDtypeStruct((DEPTH, 8, n_out), F32),
        grid=(DEPTH, n_out // tn),
        in_specs=[
            pl.BlockSpec((8, D_MODEL), lambda l, j: (0, 0)),
            pl.BlockSpec((1, D_MODEL, tn), lambda l, j: (l, 0, j)),
            pl.BlockSpec((1, 1, tn), lambda l, j: (l, 0, j)),
        ],
        out_specs=pl.BlockSpec((1, 8, tn), lambda l, j: (l, 0, j)),
        compiler_params=_cparams("parallel", "parallel"),
        name="modulation",
    )(cond8, w_mod, b_mod.reshape(DEPTH, 1, n_out))


def _rope128(x, cos, sin):
    lane = lax.broadcasted_iota(jnp.int32, x.shape, 1)
    partner = jnp.where((lane % 32) < 16, pltpu.roll(x, LANES - 16, 1), pltpu.roll(x, 16, 1))
    return x * cos + partner * sin


def _qkv_kernel(x_ref, nw_ref, mod_ref, w_ref, *rest, rope):
    if rope:
        cos_ref, sin_ref, o_ref, h_scr = rest
    else:
        o_ref, h_scr = rest
    j = pl.program_id(1)

    @pl.when(j == 0)
    def _():
        h = _rms(x_ref[...], nw_ref[...]) * (1.0 + mod_ref[0, 1:2, :]) + mod_ref[0, 0:1, :]
        h_scr[...] = h.astype(BF16)

    acc = _dot(h_scr[...], w_ref[...])
    if not rope:
        o_ref[...] = acc
        return
    full = (j == T_CQ) | (j == T_DQ) | (j == T_DK)
    part = j == T_MISC

    @pl.when(full)
    def _():
        for c in range(COL_TILE // LANES):
            sl = slice(c * LANES, (c + 1) * LANES)
            o_ref[:, sl] = _rope128(acc[:, sl], cos_ref[...], sin_ref[...])

    @pl.when(part)
    def _():
        o_ref[...] = acc
        o_ref[:, :LANES] = _rope128(acc[:, :LANES], cos_ref[...], sin_ref[...])

    @pl.when(jnp.logical_not(full | part))
    def _():
        o_ref[...] = acc


def _qkv_proj(x, norm_w, mod3, w_in_r, rows_per_batch, rope_tabs):
    n = x.shape[0]
    tm = min(1024, rows_per_batch)
    tiles_per_batch = rows_per_batch // tm
    rope = rope_tabs is not None
    in_specs = [
        pl.BlockSpec((tm, D_MODEL), lambda i, j: (i, 0)),
        pl.BlockSpec((1, D_MODEL), lambda i, j: (0, 0)),
        pl.BlockSpec((1, 6, D_MODEL), lambda i, j: (i // tiles_per_batch, 0, 0)),
        pl.BlockSpec((D_MODEL, COL_TILE), lambda i, j: (0, j)),
    ]
    args = [x, norm_w.reshape(1, D_MODEL), mod3, w_in_r]
    if rope:
        in_specs += [pl.BlockSpec((tm, LANES), lambda i, j: (i % tiles_per_batch, 0))] * 2
        args += list(rope_tabs)
    return pl.pallas_call(
        functools.partial(_qkv_kernel, rope=rope),
        out_shape=jax.ShapeDtypeStruct((n, N_PROJ), F32),
        grid=(n // tm, N_COL_TILES),
        in_specs=in_specs,
        out_specs=pl.BlockSpec((tm, COL_TILE), lambda i, j: (i, j)),
        scratch_shapes=[pltpu.VMEM((tm, D_MODEL), BF16)],
        compiler_params=_cparams("parallel", "arbitrary"),
        name="qkv_rope" if rope else "qkv",
    )(*args)


def _log_sigmoid(x):
    return jnp.minimum(x, 0.0) - jnp.log1p(jnp.exp(-jnp.abs(x)))


def _mlstm_kernel(q_ref, k_ref, v_ref, gr_ref, gc_ref, c0_ref, n0_ref, m0_ref,
                  h_ref, cs_ref, ns_ref, ms_ref, *, nc):
    L = MLSTM_CHUNK
    row = lax.broadcasted_iota(jnp.int32, (L, L), 0)
    col = lax.broadcasted_iota(jnp.int32, (L, L), 1)
    for d in range(2):
        tri = (col <= row) if d == 0 else (col >= row)
        tri_t = (row <= col) if d == 0 else (row >= col)

        def body(ci, carry, d=d, tri=tri, tri_t=tri_t):
            C, n, m = carry
            c = ci if d == 0 else nc - 1 - ci
            off = pl.multiple_of(c * L, L)
            qx = q_ref[pl.ds(off, L), :]
            kx = k_ref[pl.ds(off, L), :] * (DH_A ** -0.5)
            vx = v_ref[pl.ds(off, L), :]
            g_rows = gr_ref[0, c]
            g_cols = gc_ref[0, c]
            ig_r = g_rows[2 * d:2 * d + 1, :]
            lf_r = _log_sigmoid(g_rows[2 * d + 1:2 * d + 2, :])
            ig_c = g_cols[:, 2 * d:2 * d + 1]
            lf_c = _log_sigmoid(g_cols[:, 2 * d + 1:2 * d + 2])
            b_c = jnp.sum(jnp.where(tri, lf_r, 0.0), axis=1, keepdims=True)
            b_r = jnp.sum(jnp.where(tri_t, lf_c, 0.0), axis=0, keepdims=True)
            logd = jnp.where(tri, b_c - b_r + ig_r, NEG_INF)
            inter = b_c + m
            m_t = jnp.maximum(inter, jnp.max(logd, axis=1, keepdims=True))
            w = jnp.exp(logd - m_t)
            a = jnp.exp(inter - m_t)
            qb = qx.astype(BF16)
            kb = kx.astype(BF16)
            s = _dot_nt(qb, kb) * w
            num = a * _dot_nt(qb, C.astype(BF16)) + _dot(s.astype(BF16), vx.astype(BF16))
            den = a * jnp.sum(qx * n, axis=1, keepdims=True) + jnp.sum(s, axis=1, keepdims=True)
            h = num / jnp.maximum(jnp.abs(den), jnp.exp(-m_t))
            if d == 0:
                h_ref[pl.ds(off, L), :] = h
            else:
                h_ref[pl.ds(off, L), :] += h
            b_tot = jnp.sum(lf_r, axis=1, keepdims=True)
            g_c = b_tot - b_c + ig_c
            m_new = jnp.maximum(b_tot + m, jnp.max(g_c, axis=0, keepdims=True))
            decay = jnp.exp(b_tot + m - m_new)
            wk = jnp.exp(g_c - m_new)
            C_new = decay * C + _dot_tn((wk * vx).astype(BF16), kb)
            n_new = decay * n + jnp.sum(wk * kx, axis=0, keepdims=True)
            return C_new, n_new, m_new

        init = (c0_ref[0, d, 0], n0_ref[0, 0, d:d + 1, :], m0_ref[0, 0, d:d + 1, 0:1])
        C, n, m = lax.fori_loop(0, nc, body, init)
        cs_ref[0, d, 0] = C
        ns_ref[0, 0, d:d + 1, :] = n
        ms_ref[0, 0, d:d + 1, :] = jnp.broadcast_to(m, (1, LANES))


def _mlstm(qkv, gates, C0, n0, m0, batch, seq):
    nc = seq // MLSTM_CHUNK
    L = MLSTM_CHUNK
    ig = gates[:, :8].reshape(batch, nc, L, 2, H_A)
    fg = gates[:, 8:].reshape(batch, nc, L, 2, H_A)
    g4 = jnp.stack([ig[..., 0, :], fg[..., 0, :], ig[..., 1, :], fg[..., 1, :]], axis=-1)
    g_cols = jnp.transpose(g4, (3, 0, 1, 2, 4)).reshape(H_A, batch * nc, L, 4)
    g_rows = jnp.transpose(g4, (3, 0, 1, 4, 2)).reshape(H_A, batch * nc, 4, L)
    n0t = jnp.transpose(n0, (0, 2, 1, 3))
    m0t = jnp.broadcast_to(jnp.transpose(m0, (0, 2, 1))[..., None], (batch, H_A, 2, LANES))
    qkv_spec = lambda tile: pl.BlockSpec((seq, DH_A), lambda b, h: (b, tile * H_A + h))
    state_spec = pl.BlockSpec((1, 1, 2, LANES), lambda b, h: (b, h, 0, 0))
    h, Cs, ns, ms = pl.pallas_call(
        functools.partial(_mlstm_kernel, nc=nc),
        out_shape=(
            jax.ShapeDtypeStruct((batch * seq, H_A * DH_A), F32),
            jax.ShapeDtypeStruct((batch, 2, H_A, DH_A, DH_A), F32),
            jax.ShapeDtypeStruct((batch, H_A, 2, DH_A), F32),
            jax.ShapeDtypeStruct((batch, H_A, 2, LANES), F32),
        ),
        grid=(batch, H_A),
        in_specs=[
            qkv_spec(T_AQ), qkv_spec(T_AK), qkv_spec(T_AV),
            pl.BlockSpec((1, nc, 4, L), lambda b, h: (h, b, 0, 0)),
            pl.BlockSpec((1, nc, L, 4), lambda b, h: (h, b, 0, 0)),
            pl.BlockSpec((1, 2, 1, DH_A, DH_A), lambda b, h: (b, 0, h, 0, 0)),
            state_spec, state_spec,
        ],
        out_specs=(
            pl.BlockSpec((seq, DH_A), lambda b, h: (b, h)),
            pl.BlockSpec((1, 2, 1, DH_A, DH_A), lambda b, h: (b, 0, h, 0, 0)),
            state_spec, state_spec,
        ),
        compiler_params=_cparams("parallel", "parallel"),
        name="mlstm",
    )(qkv, qkv, qkv, g_rows, g_cols, C0, n0t, m0t)
    return h, Cs, jnp.transpose(ns, (0, 2, 1, 3)), jnp.transpose(ms[..., 0], (0, 2, 1))


def _half_mask(shape, half):
    lane = lax.broadcasted_iota(jnp.int32, shape, 1)
    return (lane >= HALF) if half else (lane < HALF)


def _softmax_parts(scores, extra=None):
    m = functools.reduce(jnp.maximum, [jnp.max(s, axis=1, keepdims=True) for s in scores])
    if extra is not None:
        m = jnp.maximum(m, extra)
    es = [jnp.exp(s - m) for s in scores]
    l = functools.reduce(lambda x, y: x + y, [jnp.sum(e, axis=1, keepdims=True) for e in es])
    if extra is not None:
        l = l + jnp.exp(extra - m)
    return es, l


def _pv(es, vs):
    return functools.reduce(lambda x, y: x + y, [_dot(e.astype(BF16), v) for e, v in zip(es, vs)])


def _diff_head(q, ks, vs, lam):
    parts = []
    for comp in range(2):
        qm = jnp.where(_half_mask(q.shape, comp), q, 0.0).astype(BF16)
        parts.append(_softmax_parts([_dot_nt(qm, k) for k in ks]))
    (e0, l0), (e1, l1) = parts
    coef = lam * l0 / l1
    pd = [a - coef * b for a, b in zip(e0, e1)]
    return _pv(pd, vs) / l0


def _ctx_attn_kernel(lam_ref, sink_ref, bq, bk, bv, cq, misc, dq, dk, dv, yb_ref, yc_ref, yd_ref):
    scale = DH_B ** -0.5
    for p in range(H_B // 2):
        sl = slice(p * LANES, (p + 1) * LANES)
        q = bq[:, sl] * scale
        k = bk[:, sl].astype(BF16)
        v = bv[:, sl].astype(BF16)
        outs = []
        for half in range(2):
            qm = jnp.where(_half_mask(q.shape, half), q, 0.0).astype(BF16)
            es, l = _softmax_parts([_dot_nt(qm, k)])
            outs.append(_pv(es, [v]) / l)
        yb_ref[:, sl] = jnp.where(_half_mask(q.shape, 0), outs[0], outs[1])

    kc = misc[:, 0:LANES].astype(BF16)
    vc = misc[:, LANES:2 * LANES].astype(BF16)
    for p in range(H_C // 2):
        sl = slice(p * LANES, (p + 1) * LANES)
        q = cq[:, sl] * scale
        outs = []
        for half in range(2):
            head = 2 * p + half
            kv = head // (H_C // KV_C)
            qm = jnp.where(_half_mask(q.shape, half), q, 0.0)
            if kv != half:
                qm = pltpu.roll(qm, HALF, 1)
            es, l = _softmax_parts([_dot_nt(qm.astype(BF16), kc)], extra=sink_ref[head])
            o = _pv(es, [vc]) / l
            if kv != half:
                o = pltpu.roll(o, HALF, 1)
            outs.append(o)
        yc_ref[:, sl] = jnp.where(_half_mask(q.shape, 0), outs[0], outs[1])

    lam = lam_ref[0]
    for h in range(H_D):
        sl = slice(h * LANES, (h + 1) * LANES)
        yd_ref[:, sl] = _diff_head(dq[:, sl] * scale, [dk[:, sl].astype(BF16)], [dv[:, sl].astype(BF16)], lam)


def _ctx_attention(qkv, lam, sink, batch, seq):
    tile = lambda t: pl.BlockSpec((seq, COL_TILE), lambda b: (b, t))
    smem = pl.BlockSpec(memory_space=pltpu.SMEM)
    out = jax.ShapeDtypeStruct((batch * seq, BRANCH_W), F32)
    out_spec = pl.BlockSpec((seq, BRANCH_W), lambda b: (b, 0))
    return pl.pallas_call(
        _ctx_attn_kernel,
        out_shape=(out, out, out),
        grid=(batch,),
        in_specs=[smem, smem, tile(T_BQ), tile(T_BK), tile(T_BV), tile(T_CQ), tile(T_MISC),
                  tile(T_DQ), tile(T_DK), tile(T_DV)],
        out_specs=(out_spec, out_spec, out_spec),
        compiler_params=_cparams("parallel"),
        name="ctx_attention",
    )(lam, sink, *([qkv] * 8))


NAT_QROWS = 4
NAT_KROWS = 12
NAT_RI = 2 * NA_ROWS - 1


def _nat_kernel(q_ref, k_ref, v_ref, kc_ref, vc_ref, cb_ref, o_ref):
    rows = k_ref.shape[0] // GRID_W
    r0 = pl.program_id(2) * NAT_QROWS
    kb = jnp.clip(r0 - NA_ROWS // 2, 0, rows - NAT_KROWS)
    koff = pl.multiple_of(kb * GRID_W, GRID_W)
    nq = NAT_QROWS * GRID_W
    nk = NAT_KROWS * GRID_W
    q = q_ref[...] * (DH_B ** -0.5)
    k = k_ref[pl.ds(koff, nk), :].astype(BF16)
    v = v_ref[pl.ds(koff, nk), :].astype(BF16)
    kc = kc_ref[0].astype(BF16)
    vc = vc_ref[0].astype(BF16)
    left = _half_mask((GRID_W, LANES), 0)
    outs = []
    for half in range(2):
        bias_rows = []
        for a in range(NAT_QROWS):
            r = r0 + a
            rs = jnp.clip(r - NA_ROWS // 2, 0, rows - NA_ROWS)
            blocks = []
            for ip in range(NAT_KROWS // 2):
                sides = []
                for side in range(2):
                    kr = kb + 2 * ip + side
                    ok = (kr >= rs) & (kr < rs + NA_ROWS)
                    ri = jnp.clip(kr - r + NA_ROWS - 1, 0, NAT_RI - 1)
                    sides.append(jnp.where(ok, cb_ref[half, ri], NEG_INF))
                blocks.append(jnp.where(left, sides[0], sides[1]))
            bias_rows.append(jnp.concatenate(blocks, axis=1))
        bias = jnp.concatenate(bias_rows, axis=0)
        qm = jnp.where(_half_mask(q.shape, half), q, 0.0).astype(BF16)
        s_nb = _dot_nt(qm, k) + bias
        s_ctx = _dot_nt(qm, kc)
        es, l = _softmax_parts([s_nb, s_ctx])
        outs.append(_pv(es, [v, vc]) / l)
    o_ref[...] = jnp.where(_half_mask((nq, LANES), 0), outs[0], outs[1])


def _nat_col_table(rpb):
    qc = np.arange(GRID_W)[:, None]
    kc = np.arange(GRID_W)[None, :]
    cs = np.clip(qc - NA_COLS // 2, 0, GRID_W - NA_COLS)
    valid = (kc >= cs) & (kc < cs + NA_COLS)
    ci = np.clip(kc - qc + NA_COLS - 1, 0, 2 * NA_COLS - 2)
    tab = jnp.where(valid[None, None], rpb[:, :, ci], NEG_INF)
    return jnp.concatenate([tab, tab], axis=-1)


def _nat_latent(qkv, kc, vc, rpb, batch, seq):
    nq = NAT_QROWS * GRID_W
    qblocks = seq // nq
    tpb = seq // nq
    past = kc.shape[1]
    cb = _nat_col_table(rpb)
    return pl.pallas_call(
        _nat_kernel,
        out_shape=jax.ShapeDtypeStruct((batch * seq, BRANCH_W), F32),
        grid=(batch, H_B // 2, qblocks),
        in_specs=[
            pl.BlockSpec((nq, LANES), lambda b, p, i: (b * tpb + i, T_BQ * 4 + p)),
            pl.BlockSpec((seq, LANES), lambda b, p, i: (b, T_BK * 4 + p)),
            pl.BlockSpec((seq, LANES), lambda b, p, i: (b, T_BV * 4 + p)),
            pl.BlockSpec((1, past, LANES), lambda b, p, i: (b, 0, p)),
            pl.BlockSpec((1, past, LANES), lambda b, p, i: (b, 0, p)),
            pl.BlockSpec((2, NAT_RI, GRID_W, LANES), lambda b, p, i: (p, 0, 0, 0)),
        ],
        out_specs=pl.BlockSpec((nq, LANES), lambda b, p, i: (b * tpb + i, p)),
        compiler_params=_cparams("parallel", "parallel", "parallel"),
        name="nat_latent",
    )(qkv, qkv, qkv, kc, vc, cb)


SWA_QBLK = 256
SWA_KBLK = SWA_QBLK + 2 * SWA_WINDOW


def _swa_kernel(sink_ref, q_ref, kv_ref, kc_ref, vc_ref, o_ref):
    seq = kv_ref.shape[0]
    p = pl.program_id(1)
    t0 = pl.program_id(2) * SWA_QBLK
    k0 = pl.multiple_of(jnp.clip(t0 - SWA_WINDOW, 0, seq - SWA_KBLK), SWA_WINDOW)
    q = q_ref[...] * (DH_C ** -0.5)
    k = kv_ref[pl.ds(k0, SWA_KBLK), 0:LANES].astype(BF16)
    v = kv_ref[pl.ds(k0, SWA_KBLK), LANES:2 * LANES].astype(BF16)
    kc = kc_ref[0].astype(BF16)
    vc = vc_ref[0].astype(BF16)
    tq = t0 + lax.broadcasted_iota(jnp.int32, (SWA_QBLK, SWA_KBLK), 0)
    tk = k0 + lax.broadcasted_iota(jnp.int32, (SWA_QBLK, SWA_KBLK), 1)
    inside = jnp.abs(tq - tk) <= SWA_WINDOW
    kv_half = p // ((H_C // 2) // KV_C)
    outs = []
    for half in range(2):
        aligned = kv_half == half
        qm = jnp.where(_half_mask(q.shape, half), q, 0.0)
        qm = jnp.where(aligned, qm, pltpu.roll(qm, HALF, 1)).astype(BF16)
        s_band = jnp.where(inside, _dot_nt(qm, k), NEG_INF)
        s_ctx = _dot_nt(qm, kc)
        es, l = _softmax_parts([s_band, s_ctx], extra=sink_ref[2 * p + half])
        o = _pv(es, [v, vc]) / l
        outs.append(jnp.where(aligned, o, pltpu.roll(o, HALF, 1)))
    o_ref[...] = jnp.where(_half_mask(q.shape, 0), outs[0], outs[1])


def _swa_latent(qkv, kc, vc, sink, batch, seq):
    tpb = seq // SWA_QBLK
    past = kc.shape[1]
    return pl.pallas_call(
        _swa_kernel,
        out_shape=jax.ShapeDtypeStruct((batch * seq, BRANCH_W), F32),
        grid=(batch, H_C // 2, tpb),
        in_specs=[
            pl.BlockSpec(memory_space=pltpu.SMEM),
            pl.BlockSpec((SWA_QBLK, LANES), lambda b, p, i: (b * tpb + i, T_CQ * 4 + p)),
            pl.BlockSpec((seq, COL_TILE), lambda b, p, i: (b, T_MISC)),
            pl.BlockSpec((1, past, LANES), lambda b, p, i: (b, 0, 0)),
            pl.BlockSpec((1, past, LANES), lambda b, p, i: (b, 0, 0)),
        ],
        out_specs=pl.BlockSpec((SWA_QBLK, LANES), lambda b, p, i: (b * tpb + i, p)),
        compiler_params=_cparams("parallel", "parallel", "parallel"),
        name="swa_latent",
    )(sink, qkv, qkv, kc, vc)


DIFF_QBLK = 256


def _diff_kernel(lam_ref, q_ref, k_ref, v_ref, kc_ref, vc_ref, o_ref):
    q = q_ref[...] * (DH_D ** -0.5)
    ks = [k_ref[...].astype(BF16), kc_ref[0].astype(BF16)]
    vs = [v_ref[...].astype(BF16), vc_ref[0].astype(BF16)]
    o_ref[...] = _diff_head(q, ks, vs, lam_ref[0])


def _diff_latent(qkv, kc, vc, lam, batch, seq):
    tpb = seq // DIFF_QBLK
    past = kc.shape[1]
    return pl.pallas_call(
        _diff_kernel,
        out_shape=jax.ShapeDtypeStruct((batch * seq, BRANCH_W), F32),
        grid=(batch, H_D, tpb),
        in_specs=[
            pl.BlockSpec(memory_space=pltpu.SMEM),
            pl.BlockSpec((DIFF_QBLK, LANES), lambda b, h, i: (b * tpb + i, T_DQ * 4 + h)),
            pl.BlockSpec((seq, LANES), lambda b, h, i: (b, T_DK * 4 + h)),
            pl.BlockSpec((seq, LANES), lambda b, h, i: (b, T_DV * 4 + h)),
            pl.BlockSpec((1, past, LANES), lambda b, h, i: (b, 0, h)),
            pl.BlockSpec((1, past, LANES), lambda b, h, i: (b, 0, h)),
        ],
        out_specs=pl.BlockSpec((DIFF_QBLK, LANES), lambda b, h, i: (b * tpb + i, h)),
        compiler_params=_cparams("parallel", "parallel", "parallel"),
        name="diff_latent",
    )(lam, qkv, qkv, qkv, kc, vc)


def _rms_heads(y, g_ref, per_head_gain):
    outs = []
    for c in range(BRANCH_W // LANES):
        sl = slice(c * LANES, (c + 1) * LANES)
        g = g_ref[:, sl] if per_head_gain else g_ref[...]
        outs.append(_rms(y[:, sl], g))
    return jnp.concatenate(outs, axis=1)


def _route(logits):
    lane = lax.broadcasted_iota(jnp.int32, logits.shape, 1)
    lane_f = lane.astype(F32)
    first = lambda hit: jnp.min(jnp.where(hit, lane_f, float(LANES)), axis=1, keepdims=True)
    gl = jnp.where(lane < N_GROUPS, logits, NEG_INF)
    gmax = jnp.max(gl, axis=1, keepdims=True)
    gsel = first(gl == gmax)
    gw = 1.0 / jnp.sum(jnp.exp(gl - gmax), axis=1, keepdims=True)
    e_lane = lane - N_GROUPS
    e_group = lax.shift_right_arithmetic(e_lane, EXP_PER_GROUP.bit_length() - 1).astype(F32)
    in_group = (e_lane >= 0) & (e_lane < N_EXPERTS) & (e_group == gsel)
    el = jnp.where(in_group, logits, NEG_INF)
    e1 = jnp.max(el, axis=1, keepdims=True)
    i1 = first(el == e1)
    el2 = jnp.where(lane_f == i1, NEG_INF, el)
    e2 = jnp.max(el2, axis=1, keepdims=True)
    i2 = first(el2 == e2)
    t = jnp.exp(e2 - e1)
    w1 = gw / (1.0 + t)
    return jnp.where(lane_f == i1, w1, jnp.where(lane_f == i2, w1 * t, 0.0))


def _merge_kernel(ya, yb, yc, yd, ao, gt0, gt1, gt2, gt3, x_ref, mod_ref, anorm, subln, nffn,
                  wbr, wout, wr, xo_ref, h2_ref, gate_ref, *, yd_scale):
    a = jax.nn.sigmoid(ao[...]) * _rms_heads(ya[...], anorm, True)
    d = _rms_heads(yd[...], subln, False) * yd_scale
    branches = (a, yb[...], yc[...], d)
    mix = None
    for n, (br, gt) in enumerate(zip(branches, (gt0, gt1, gt2, gt3))):
        term = jax.nn.sigmoid(gt[...]) * _dot(br.astype(BF16), wbr[n])
        mix = term if mix is None else mix + term
    out = _dot(mix.astype(BF16), wout[...])
    xn = x_ref[...] + mod_ref[0, 2:3, :] * out
    xo_ref[...] = xn
    h2 = (_rms(xn, nffn[...]) * (1.0 + mod_ref[0, 4:5, :]) + mod_ref[0, 3:4, :]).astype(BF16)
    h2_ref[...] = h2
    gate_ref[...] = _route(_dot(h2, wr[...]))


def _merge(ys, qkv, x, mod3, rows_per_batch, a_norm, subln, norm_ffn, w_branch, w_out, w_route, lam_init):
    n = x.shape[0]
    tm = 256
    tpb = rows_per_batch // tm
    y_spec = pl.BlockSpec((tm, BRANCH_W), lambda i: (i, 0))
    gt_spec = lambda k: pl.BlockSpec((tm, D_MODEL), lambda i: (i, T_GT * COL_TILE // D_MODEL + k))
    full = lambda shape: pl.BlockSpec(shape, lambda i: (0,) * len(shape))
    x_spec = pl.BlockSpec((tm, D_MODEL), lambda i: (i, 0))
    return pl.pallas_call(
        functools.partial(_merge_kernel, yd_scale=1.0 - lam_init),
        out_shape=(
            jax.ShapeDtypeStruct((n, D_MODEL), F32),
            jax.ShapeDtypeStruct((n, D_MODEL), BF16),
            jax.ShapeDtypeStruct((n, LANES), F32),
        ),
        grid=(n // tm,),
        in_specs=[
            y_spec, y_spec, y_spec, y_spec,
            pl.BlockSpec((tm, COL_TILE), lambda i: (i, T_AO)),
            gt_spec(0), gt_spec(1), gt_spec(2), gt_spec(3),
            x_spec,
            pl.BlockSpec((1, 6, D_MODEL), lambda i: (i // tpb, 0, 0)),
            full((1, BRANCH_W)), full((1, LANES)), full((1, D_MODEL)),
            full((N_BRANCH, BRANCH_W, D_MODEL)), full((D_MODEL, D_MODEL)), full((D_MODEL, LANES)),
        ],
        out_specs=(x_spec, x_spec, pl.BlockSpec((tm, LANES), lambda i: (i, 0))),
        compiler_params=_cparams("parallel"),
        name="merge_route",
    )(*ys, qkv, qkv, qkv, qkv, qkv, x, mod3, a_norm.reshape(1, BRANCH_W), subln.reshape(1, LANES),
      norm_ffn.reshape(1, D_MODEL), w_branch, w_out, w_route)


def _moe_kernel(h_ref, gate_ref, x_ref, mod_ref, wg_ref, wu_ref, wd_ref, nf_ref, o_ref, acc_ref, *, final_norm):
    e = pl.program_id(1)

    @pl.when(e == 0)
    def _():
        acc_ref[...] = jnp.zeros_like(acc_ref)

    h = h_ref[...]
    gates = gate_ref[...]
    lane = lax.broadcasted_iota(jnp.int32, gates.shape, 1)
    g = jnp.sum(jnp.where(lane == e + N_GROUPS, gates, 0.0), axis=1, keepdims=True)
    hg = _dot(h, wg_ref[0])
    hid = hg * jax.nn.sigmoid(hg) * _dot(h, wu_ref[0]) * g
    acc_ref[...] += _dot(hid.astype(BF16), wd_ref[0])

    @pl.when(e == N_EXPERTS - 1)
    def _():
        y = x_ref[...] + mod_ref[0, 5:6, :] * acc_ref[...]
        o_ref[...] = _rms(y, nf_ref[...]) if final_norm else y


def _moe(h2, gates, x, mod3, rows_per_batch, wg, wu, wd, norm_final, final_norm):
    n = x.shape[0]
    tm = min(1024, rows_per_batch)
    tpb = rows_per_batch // tm
    row = lambda w: pl.BlockSpec((tm, w), lambda i, e: (i, 0))
    return pl.pallas_call(
        functools.partial(_moe_kernel, final_norm=final_norm),
        out_shape=jax.ShapeDtypeStruct((n, D_MODEL), F32),
        grid=(n // tm, N_EXPERTS),
        in_specs=[
            row(D_MODEL), row(LANES), row(D_MODEL),
            pl.BlockSpec((1, 6, D_MODEL), lambda i, e: (i // tpb, 0, 0)),
            pl.BlockSpec((1, D_MODEL, D_EXPERT), lambda i, e: (e, 0, 0)),
            pl.BlockSpec((1, D_MODEL, D_EXPERT), lambda i, e: (e, 0, 0)),
            pl.BlockSpec((1, D_EXPERT, D_MODEL), lambda i, e: (e, 0, 0)),
            pl.BlockSpec((1, D_MODEL), lambda i, e: (0, 0)),
        ],
        out_specs=row(D_MODEL),
        scratch_shapes=[pltpu.VMEM((tm, D_MODEL), F32)],
        compiler_params=_cparams("parallel", "arbitrary"),
        name="moe",
    )(h2, gates, x, mod3, wg, wu, wd, norm_final.reshape(1, D_MODEL))


def _reorder_w_in(w):
    pad = jnp.zeros((D_MODEL, COL_TILE - 2 * LANES - 16), w.dtype)
    cols = [w[:, 0:2048], w[:, 2064:4112], w[:, 4368:5904], w[:, 4112:4368], w[:, 2048:2064], pad, w[:, 5904:]]
    return jnp.concatenate(cols, axis=1).astype(BF16)


def _rope_tables(seq):
    nf = DH_C // 4
    inv = ROPE_BASE ** (-jnp.arange(nf, dtype=F32) / nf)
    t = jnp.arange(seq)
    ang_r = (t // GRID_W).astype(F32)[:, None] * inv[None, :]
    ang_c = (t % GRID_W).astype(F32)[:, None] * inv[None, :]
    cos = jnp.concatenate([jnp.cos(ang_r)] * 2 + [jnp.cos(ang_c)] * 2, axis=1)
    sin = jnp.concatenate([-jnp.sin(ang_r), jnp.sin(ang_r), -jnp.sin(ang_c), jnp.sin(ang_c)], axis=1)
    return jnp.tile(cos, (1, 2)), jnp.tile(sin, (1, 2))


def _layer(l, x, mod3, seq, lat, p):
    batch = x.shape[0] // seq
    rows_per_batch = x.shape[0] // mod3.shape[0]
    lam_init = 0.8 - 0.6 * math.exp(-0.3 * l)
    qkv = _qkv_proj(x, p["norm_mix"][l], mod3, p["w_in"][l], rows_per_batch,
                    p["rope"] if lat is not None else None)
    g0 = T_MISC * COL_TILE + MISC_GATE_OFF
    gates = qkv[:, g0:g0 + 16] + jnp.concatenate([p["i_bias"][l].reshape(-1), p["f_bias"][l].reshape(-1)])[None]
    lam = (jnp.exp(jnp.sum(p["lq1"][l] * p["lk1"][l])) - jnp.exp(jnp.sum(p["lq2"][l] * p["lk2"][l]))
           + lam_init).reshape(1).astype(F32)
    sink = p["sink"][l]
    new = None
    if lat is None:
        C0 = jnp.zeros((batch, 2, H_A, DH_A, DH_A), F32)
        n0 = jnp.zeros((batch, 2, H_A, DH_A), F32)
        m0 = jnp.zeros((batch, 2, H_A), F32)
        ya, Cs, ns, ms = _mlstm(qkv, gates, C0, n0, m0, batch, seq)
        yb, yc, yd = _ctx_attention(qkv, lam, sink, batch, seq)
        tile = lambda t, w=COL_TILE, o=0: qkv[:, t * COL_TILE + o:t * COL_TILE + o + w]
        new = (tile(T_BK).reshape(batch, seq, H_B, DH_B), tile(T_BV).reshape(batch, seq, H_B, DH_B),
               tile(T_MISC, LANES).reshape(batch, seq, KV_C, DH_C),
               tile(T_MISC, LANES, LANES).reshape(batch, seq, KV_C, DH_C),
               tile(T_DK).reshape(batch, seq, H_D, 2 * DH_D), tile(T_DV).reshape(batch, seq, H_D, 2 * DH_D),
               Cs, ns, ms)
    else:
        nat_k, nat_v, swa_k, swa_v, diff_k, diff_v, C0, n0, m0 = lat
        past = nat_k.shape[1]
        flat = lambda a: a.reshape(batch, past, -1)
        ya, _, _, _ = _mlstm(qkv, gates, C0, n0, m0, batch, seq)
        yb = _nat_latent(qkv, flat(nat_k), flat(nat_v), p["rpb"][l], batch, seq)
        yc = _swa_latent(qkv, flat(swa_k), flat(swa_v), sink, batch, seq)
        yd = _diff_latent(qkv, flat(diff_k), flat(diff_v), lam, batch, seq)
    xn, h2, gate_w = _merge((ya, yb, yc, yd), qkv, x, mod3, rows_per_batch, p["a_norm"][l], p["subln"][l],
                            p["norm_ffn"][l], p["w_branch"][l], p["w_out"][l], p["w_route"][l], lam_init)
    y = _moe(h2, gate_w, xn, mod3, rows_per_batch, p["wg"][l], p["wu"][l], p["wd"][l], p["norm_final"],
             final_norm=(l == DEPTH - 1))
    return y, new


def kernel(x_prompt, x_sample, cache_nat_k, cache_nat_v, cache_swa_k, cache_swa_v, cache_diff_k, cache_diff_v, state_mlstm_C, state_mlstm_n, state_mlstm_m, c, c_ctx, norm_mix, norm_ffn, norm_final, w_mod, b_mod, w_in, mlstm_i_bias, mlstm_f_bias, mlstm_norm, nat_rpb, swa_sink, diff_lq1, diff_lk1, diff_lq2, diff_lk2, diff_subln, w_branch, w_out, router_group, router_expert, w_exp_gate, w_exp_up, w_exp_down):
    b_ctx, s_ctx, _ = x_prompt.shape
    b_lat, s_lat, _ = x_sample.shape
    cond8 = jnp.concatenate([c_ctx[None], c, jnp.zeros((8 - 1 - b_lat, D_MODEL), F32)], axis=0)
    mod = _modulation(cond8, w_mod, b_mod).reshape(DEPTH, 8, 6, D_MODEL)
    route_pad = jnp.zeros((DEPTH, D_MODEL, LANES - N_GROUPS - N_EXPERTS), F32)
    p = dict(
        norm_mix=norm_mix, norm_ffn=norm_ffn, norm_final=norm_final,
        w_in=[_reorder_w_in(w_in[l]) for l in range(DEPTH)],
        i_bias=mlstm_i_bias, f_bias=mlstm_f_bias, a_norm=mlstm_norm, rpb=nat_rpb, sink=swa_sink,
        lq1=diff_lq1, lk1=diff_lk1, lq2=diff_lq2, lk2=diff_lk2, subln=diff_subln,
        w_branch=w_branch.astype(BF16), w_out=w_out.astype(BF16),
        w_route=jnp.concatenate([router_group, router_expert, route_pad], axis=-1).astype(BF16),
        wg=w_exp_gate.astype(BF16), wu=w_exp_up.astype(BF16), wd=w_exp_down.astype(BF16),
        rope=_rope_tables(s_lat),
    )
    yp = x_prompt.reshape(b_ctx * s_ctx, D_MODEL)
    ys = x_sample.reshape(b_lat * s_lat, D_MODEL)
    news = []
    for l in range(DEPTH):
        yp, new = _layer(l, yp, mod[l, 0:1], s_ctx, None, p)
        news.append(new)
        lat = (cache_nat_k[:, l], cache_nat_v[:, l], cache_swa_k[:, l], cache_swa_v[:, l],
               cache_diff_k[:, l], cache_diff_v[:, l], state_mlstm_C[:, l], state_mlstm_n[:, l],
               state_mlstm_m[:, l])
        ys, _ = _layer(l, ys, mod[l, 1:1 + b_lat], s_lat, lat, p)
    stacked = tuple(jnp.stack([news[l][k] for l in range(DEPTH)], axis=1) for k in range(9))
    return (yp.reshape(b_ctx, s_ctx, D_MODEL), ys.reshape(b_lat, s_lat, D_MODEL)) + stacked
```

```python
import functools
import math

import numpy as np
import jax
import jax.numpy as jnp
from jax import lax
from jax.experimental import pallas as pl
from jax.experimental.pallas import tpu as pltpu

F32 = jnp.float32
BF16 = jnp.bfloat16

D_MODEL = 1024
DEPTH = 2
GRID_W = 64
BRANCH_W = 512
N_BRANCH = 4
H_A, DH_A, MLSTM_CHUNK = 4, 128, 128
H_B, DH_B = 8, 64
NA_ROWS, NA_COLS = 8, 16
H_C, KV_C, DH_C = 8, 2, 64
SWA_WINDOW = 128
H_D, DH_D = 4, 64
N_GROUPS, EXP_PER_GROUP = 4, 4
N_EXPERTS = N_GROUPS * EXP_PER_GROUP
D_EXPERT = 256
ROPE_BASE = 10000.0
EPS = 1e-6

LANES = 128
HALF = 64
COL_TILE = 512
N_COL_TILES = 20
N_PROJ = COL_TILE * N_COL_TILES
T_AQ, T_AK, T_AV, T_AO, T_BQ, T_BK, T_BV, T_CQ, T_DQ, T_DK, T_DV, T_MISC, T_GT = range(13)
MISC_GATE_OFF = 256
VMEM_LIMIT = 56 * 1024 * 1024
NEG_INF = float("-inf")

_NT = (((1,), (1,)), ((), ()))
_TN = (((0,), (0,)), ((), ()))


def _cparams(*sem):
    return pltpu.CompilerParams(dimension_semantics=sem, vmem_limit_bytes=VMEM_LIMIT)


def _dot(a, b):
    return jnp.dot(a, b, preferred_element_type=F32)


def _dot_nt(a, b):
    return lax.dot_general(a, b, _NT, preferred_element_type=F32)


def _dot_tn(a, b):
    return lax.dot_general(a, b, _TN, preferred_element_type=F32)


def _rms(x, g):
    return x * lax.rsqrt(jnp.mean(x * x, axis=-1, keepdims=True) + EPS) * g


def _mod_kernel(c_ref, w_ref, b_ref, o_ref):
    c = c_ref[...]
    s = (c * jax.nn.sigmoid(c)).astype(BF16)
    o_ref[0] = _dot(s, w_ref[0].astype(BF16)) + b_ref[0]


def _modulation(cond8, w_mod, b_mod):
    tn = 1536
    n_out = 6 * D_MODEL
    return pl.pallas_call(
        _mod_kernel,
        out_shape=jax.ShapeDtypeStruct((DEPTH, 8, n_out), F32),
        grid=(DEPTH, n_out // tn),
        in_specs=[
            pl.BlockSpec((8, D_MODEL), lambda l, j: (0, 0)),
            pl.BlockSpec((1, D_MODEL, tn), lambda l, j: (l, 0, j)),
            pl.BlockSpec((1, 1, tn), lambda l, j: (l, 0, j)),
        ],
        out_specs=pl.BlockSpec((1, 8, tn), lambda l, j: (l, 0, j)),
        compiler_params=_cparams("parallel", "parallel"),
        name="modulation",
    )(cond8, w_mod, b_mod.reshape(DEPTH, 1, n_out))


CACHE_TILES = (T_BK, T_BV, T_DK, T_DV)


def _rope128(x, cos, sin):
    lane = lax.broadcasted_iota(jnp.int32, x.shape, 1)
    partner = jnp.where((lane % 32) < 16, pltpu.roll(x, LANES - 16, 1), pltpu.roll(x, 16, 1))
    return x * cos + partner * sin


def _qkv_kernel(x_ref, nw_ref, mod_ref, w_ref, *rest, rope, cache):
    rest = list(rest)
    if rope:
        cos_ref, sin_ref = rest[:2]
        rest = rest[2:]
    o_ref, g_ref = rest[:2]
    cache_refs = rest[2:-1]
    h_scr = rest[-1]
    j = pl.program_id(1)

    @pl.when(j == 0)
    def _():
        h = _rms(x_ref[...], nw_ref[...]) * (1.0 + mod_ref[0, 1:2, :]) + mod_ref[0, 0:1, :]
        h_scr[...] = h.astype(BF16)

    acc = _dot(h_scr[...], w_ref[...])
    is_ak = j == T_AK
    is_misc = j == T_MISC
    plain = jnp.logical_not(is_ak | is_misc)

    @pl.when(is_ak)
    def _():
        o_ref[...] = (acc * (DH_A ** -0.5)).astype(BF16)

    @pl.when(is_misc)
    def _():
        o_ref[...] = acc.astype(BF16)
        if rope:
            o_ref[:, :LANES] = _rope128(acc[:, :LANES], cos_ref[...], sin_ref[...]).astype(BF16)
        g_ref[...] = acc[:, MISC_GATE_OFF:MISC_GATE_OFF + LANES]
        if cache:
            cache_refs[-1][...] = acc[:, :2 * LANES]

    if rope:
        is_rope = (j == T_CQ) | (j == T_DQ) | (j == T_DK)
        plain = plain & jnp.logical_not(is_rope)

        @pl.when(is_rope)
        def _():
            for c in range(COL_TILE // LANES):
                sl = slice(c * LANES, (c + 1) * LANES)
                o_ref[:, sl] = _rope128(acc[:, sl], cos_ref[...], sin_ref[...]).astype(BF16)

    @pl.when(plain)
    def _():
        o_ref[...] = acc.astype(BF16)

    if cache:
        for t, ref in zip(CACHE_TILES, cache_refs[:-1]):
            @pl.when(j == t)
            def _(ref=ref):
                ref[...] = acc


def _qkv_proj(x, norm_w, mod3, w_in_r, rope_tabs, cache, tm):
    n = x.shape[0]
    tiles_per_mod = n // mod3.shape[0] // tm
    rope = rope_tabs is not None
    in_specs = [
        pl.BlockSpec((tm, D_MODEL), lambda i, j: (i, 0)),
        pl.BlockSpec((1, D_MODEL), lambda i, j: (0, 0)),
        pl.BlockSpec((1, 6, D_MODEL), lambda i, j: (i // tiles_per_mod, 0, 0)),
        pl.BlockSpec((D_MODEL, COL_TILE), lambda i, j: (0, j)),
    ]
    args = [x, norm_w.reshape(1, D_MODEL), mod3, w_in_r]
    if rope:
        tiles_per_seq = rope_tabs[0].shape[0] // tm
        in_specs += [pl.BlockSpec((tm, LANES), lambda i, j: (i % tiles_per_seq, 0))] * 2
        args += list(rope_tabs)
    row = lambda w: pl.BlockSpec((tm, w), lambda i, j: (i, 0))
    out_shape = [jax.ShapeDtypeStruct((n, N_PROJ), BF16), jax.ShapeDtypeStruct((n, LANES), F32)]
    out_specs = [pl.BlockSpec((tm, COL_TILE), lambda i, j: (i, j)), row(LANES)]
    if cache:
        out_shape += [jax.ShapeDtypeStruct((n, COL_TILE), F32)] * len(CACHE_TILES)
        out_shape += [jax.ShapeDtypeStruct((n, 2 * LANES), F32)]
        out_specs += [row(COL_TILE)] * len(CACHE_TILES) + [row(2 * LANES)]
    return pl.pallas_call(
        functools.partial(_qkv_kernel, rope=rope, cache=cache),
        out_shape=tuple(out_shape),
        grid=(n // tm, N_COL_TILES),
        in_specs=in_specs,
        out_specs=tuple(out_specs),
        scratch_shapes=[pltpu.VMEM((tm, D_MODEL), BF16)],
        compiler_params=_cparams("parallel", "arbitrary"),
        name="qkv_rope" if rope else "qkv",
    )(*args)


def _log_sigmoid(x):
    return jnp.minimum(x, 0.0) - jnp.log1p(jnp.exp(-jnp.abs(x)))


def _mlstm_chunk(c, h, d, state, q_ref, k_ref, v_ref, gr_ref, gc_ref, out_ref):
    L = MLSTM_CHUNK
    C, n, m = state
    row = lax.broadcasted_iota(jnp.int32, (L, L), 0)
    col = lax.broadcasted_iota(jnp.int32, (L, L), 1)
    tri = (col <= row) if d == 0 else (col >= row)
    tri_t = (row <= col) if d == 0 else (row >= col)
    off = pl.multiple_of(c * L, L)
    hs = slice(h * DH_A, (h + 1) * DH_A)
    qb = q_ref[pl.ds(off, L), hs]
    kb = k_ref[pl.ds(off, L), hs]
    vb = v_ref[pl.ds(off, L), hs]
    qx, kx, vx = qb.astype(F32), kb.astype(F32), vb.astype(F32)
    g_rows = gr_ref[h, c]
    g_cols = gc_ref[h, c]
    ig_r = g_rows[2 * d:2 * d + 1, :]
    lf_r = _log_sigmoid(g_rows[2 * d + 1:2 * d + 2, :])
    ig_c = g_cols[:, 2 * d:2 * d + 1]
    lf_c = _log_sigmoid(g_cols[:, 2 * d + 1:2 * d + 2])
    b_c = jnp.sum(jnp.where(tri, lf_r, 0.0), axis=1, keepdims=True)
    b_r = jnp.sum(jnp.where(tri_t, lf_c, 0.0), axis=0, keepdims=True)
    qk = _dot_nt(qb, kb)
    qc = _dot_nt(qb, C.astype(BF16))
    yield
    logd = jnp.where(tri, b_c - b_r + ig_r, NEG_INF)
    inter = b_c + m
    m_t = jnp.maximum(inter, jnp.max(logd, axis=1, keepdims=True))
    yield
    w = jnp.exp(logd - m_t)
    a = jnp.exp(inter - m_t)
    s = qk * w
    num = a * qc + _dot(s.astype(BF16), vb)
    den = a * jnp.sum(qx * n, axis=1, keepdims=True) + jnp.sum(s, axis=1, keepdims=True)
    yield
    out_ref[pl.ds(off, L), hs] = num / jnp.maximum(jnp.abs(den), jnp.exp(-m_t))
    b_tot = jnp.sum(lf_r, axis=1, keepdims=True)
    g_c = b_tot - b_c + ig_c
    m_new = jnp.maximum(b_tot + m, jnp.max(g_c, axis=0, keepdims=True))
    yield
    decay = jnp.exp(b_tot + m - m_new)
    wk = jnp.exp(g_c - m_new)
    C_new = decay * C + _dot_tn((wk * vx).astype(BF16), kb)
    n_new = decay * n + jnp.sum(wk * kx, axis=0, keepdims=True)
    return C_new, n_new, m_new


def _run_staged(gens):
    results = [None] * len(gens)
    live = list(range(len(gens)))
    while live:
        for i in list(live):
            try:
                next(gens[i])
            except StopIteration as stop:
                results[i] = stop.value
                live.remove(i)
    return results


def _mlstm_kernel(q_ref, k_ref, v_ref, gr_ref, gc_ref, c0_ref, n0_ref, m0_ref,
                  h_ref, cs_ref, ns_ref, ms_ref, hb_scr, *, nc):
    chains = [(h, d) for h in range(H_A) for d in range(2)]

    def body(ci, states):
        gens = [_mlstm_chunk(ci if d == 0 else nc - 1 - ci, h, d, state, q_ref, k_ref, v_ref, gr_ref, gc_ref,
                             h_ref if d == 0 else hb_scr) for (h, d), state in zip(chains, states)]
        return tuple(_run_staged(gens))

    init = tuple((c0_ref[0, d, h], n0_ref[0, h, d:d + 1, :], m0_ref[0, h, d:d + 1, 0:1]) for h, d in chains)
    final = lax.fori_loop(0, nc, body, init)
    for (h, d), (C, n, m) in zip(chains, final):
        cs_ref[0, d, h] = C
        ns_ref[0, h, d:d + 1, :] = n
        ms_ref[0, h, d:d + 1, :] = jnp.broadcast_to(m, (1, LANES))
    h_ref[...] += hb_scr[...]


def _mlstm(qkv, gates, C0, n0, m0, batch, seq):
    nc = seq // MLSTM_CHUNK
    L = MLSTM_CHUNK
    ig = gates[:, :8].reshape(batch, nc, L, 2, H_A)
    fg = gates[:, 8:].reshape(batch, nc, L, 2, H_A)
    g4 = jnp.stack([ig[..., 0, :], fg[..., 0, :], ig[..., 1, :], fg[..., 1, :]], axis=-1)
    g_cols = jnp.transpose(g4, (3, 0, 1, 2, 4)).reshape(H_A, batch * nc, L, 4)
    g_rows = jnp.transpose(g4, (3, 0, 1, 4, 2)).reshape(H_A, batch * nc, 4, L)
    n0t = jnp.transpose(n0, (0, 2, 1, 3))
    m0t = jnp.broadcast_to(jnp.transpose(m0, (0, 2, 1))[..., None], (batch, H_A, 2, LANES))
    qkv_spec = lambda tile: pl.BlockSpec((seq, COL_TILE), lambda b: (b, tile))
    c_spec = pl.BlockSpec((1, 2, H_A, DH_A, DH_A), lambda b: (b, 0, 0, 0, 0))
    state_spec = pl.BlockSpec((1, H_A, 2, LANES), lambda b: (b, 0, 0, 0))
    h, Cs, ns, ms = pl.pallas_call(
        functools.partial(_mlstm_kernel, nc=nc),
        out_shape=(
            jax.ShapeDtypeStruct((batch * seq, H_A * DH_A), F32),
            jax.ShapeDtypeStruct((batch, 2, H_A, DH_A, DH_A), F32),
            jax.ShapeDtypeStruct((batch, H_A, 2, DH_A), F32),
            jax.ShapeDtypeStruct((batch, H_A, 2, LANES), F32),
        ),
        grid=(batch,),
        in_specs=[
            qkv_spec(T_AQ), qkv_spec(T_AK), qkv_spec(T_AV),
            pl.BlockSpec((H_A, nc, 4, L), lambda b: (0, b, 0, 0)),
            pl.BlockSpec((H_A, nc, L, 4), lambda b: (0, b, 0, 0)),
            c_spec, state_spec, state_spec,
        ],
        out_specs=(pl.BlockSpec((seq, H_A * DH_A), lambda b: (b, 0)), c_spec, state_spec, state_spec),
        scratch_shapes=[pltpu.VMEM((seq, H_A * DH_A), F32)],
        compiler_params=_cparams("parallel"),
        name="mlstm",
    )(qkv, qkv, qkv, g_rows, g_cols, C0, n0t, m0t)
    return h, Cs, jnp.transpose(ns, (0, 2, 1, 3)), jnp.transpose(ms[..., 0], (0, 2, 1))


ATTN_SCALE = DH_B ** -0.5


def _half_mask(shape, half):
    lane = lax.broadcasted_iota(jnp.int32, shape, 1)
    return (lane >= HALF) if half else (lane < HALF)


def _select_half(q, half):
    return jnp.where(_half_mask(q.shape, half), q, jnp.zeros_like(q))


def _swap_halves(x):
    return pltpu.roll(x.astype(F32), HALF, 1).astype(x.dtype)


def _softmax_parts(scores, extra=None):
    m = functools.reduce(jnp.maximum, [jnp.max(s, axis=1, keepdims=True) for s in scores])
    if extra is not None:
        m = jnp.maximum(m, extra)
    es = [jnp.exp(s - m) for s in scores]
    l = functools.reduce(lambda x, y: x + y, [jnp.sum(e, axis=1, keepdims=True) for e in es])
    if extra is not None:
        l = l + jnp.exp(extra - m)
    return es, l


def _pv(es, vs):
    return functools.reduce(lambda x, y: x + y, [_dot(e.astype(BF16), v) for e, v in zip(es, vs)])


def _diff_head(q, ks, vs, lam):
    parts = [_softmax_parts([_dot_nt(_select_half(q, comp), k) for k in ks]) for comp in range(2)]
    (e0, l0), (e1, l1) = parts
    coef = lam * l0 / l1
    pd = [a - coef * b for a, b in zip(e0, e1)]
    return _pv(pd, vs) / l0


def _ctx_attn_kernel(lam_ref, sink_ref, bq, bk, bv, cq, misc, dq, dk, dv, yb_ref, yc_ref, yd_ref):
    for p in range(H_B // 2):
        sl = slice(p * LANES, (p + 1) * LANES)
        q = bq[:, sl] * ATTN_SCALE
        outs = []
        for half in range(2):
            es, l = _softmax_parts([_dot_nt(_select_half(q, half), bk[:, sl])])
            outs.append(_pv(es, [bv[:, sl]]) / l)
        yb_ref[:, sl] = jnp.where(_half_mask(q.shape, 0), outs[0], outs[1]).astype(yb_ref.dtype)

    kc = misc[:, 0:LANES]
    vc = misc[:, LANES:2 * LANES]
    for p in range(H_C // 2):
        sl = slice(p * LANES, (p + 1) * LANES)
        q = cq[:, sl] * ATTN_SCALE
        outs = []
        for half in range(2):
            head = 2 * p + half
            kv = head // (H_C // KV_C)
            qm = _select_half(q, half)
            if kv != half:
                qm = _swap_halves(qm)
            es, l = _softmax_parts([_dot_nt(qm, kc)], extra=sink_ref[head])
            o = _pv(es, [vc]) / l
            outs.append(o if kv == half else _swap_halves(o))
        yc_ref[:, sl] = jnp.where(_half_mask(q.shape, 0), outs[0], outs[1]).astype(yc_ref.dtype)

    lam = lam_ref[0]
    for h in range(H_D):
        sl = slice(h * LANES, (h + 1) * LANES)
        yd_ref[:, sl] = _diff_head(dq[:, sl] * ATTN_SCALE, [dk[:, sl]], [dv[:, sl]], lam)


def _ctx_attention(qkv, lam, sink, batch, seq):
    tile = lambda t: pl.BlockSpec((seq, COL_TILE), lambda b: (b, t))
    smem = pl.BlockSpec(memory_space=pltpu.SMEM)
    out = lambda dt: jax.ShapeDtypeStruct((batch * seq, BRANCH_W), dt)
    out_spec = pl.BlockSpec((seq, BRANCH_W), lambda b: (b, 0))
    return pl.pallas_call(
        _ctx_attn_kernel,
        out_shape=(out(BF16), out(BF16), out(F32)),
        grid=(batch,),
        in_specs=[smem, smem, tile(T_BQ), tile(T_BK), tile(T_BV), tile(T_CQ), tile(T_MISC),
                  tile(T_DQ), tile(T_DK), tile(T_DV)],
        out_specs=(out_spec, out_spec, out_spec),
        compiler_params=_cparams("parallel"),
        name="ctx_attention",
    )(lam, sink, *([qkv] * 8))


NAT_QROWS = 4
NAT_KROWS = 12
NAT_RI = 2 * NA_ROWS - 1


def _nat_kernel(q_ref, k_ref, v_ref, kc_ref, vc_ref, cb_ref, o_ref):
    rows = k_ref.shape[0] // GRID_W
    r0 = pl.program_id(2) * NAT_QROWS
    kb = jnp.clip(r0 - NA_ROWS // 2, 0, rows - NAT_KROWS)
    koff = pl.multiple_of(kb * GRID_W, GRID_W)
    nq = NAT_QROWS * GRID_W
    nk = NAT_KROWS * GRID_W
    q = q_ref[...] * ATTN_SCALE
    k = k_ref[pl.ds(koff, nk), :]
    v = v_ref[pl.ds(koff, nk), :]
    kc = kc_ref[0].astype(BF16)
    vc = vc_ref[0].astype(BF16)
    left = _half_mask((GRID_W, LANES), 0)
    outs = []
    for half in range(2):
        bias_rows = []
        for a in range(NAT_QROWS):
            r = r0 + a
            rs = jnp.clip(r - NA_ROWS // 2, 0, rows - NA_ROWS)
            blocks = []
            for ip in range(NAT_KROWS // 2):
                sides = []
                for side in range(2):
                    kr = kb + 2 * ip + side
                    ok = (kr >= rs) & (kr < rs + NA_ROWS)
                    ri = jnp.clip(kr - r + NA_ROWS - 1, 0, NAT_RI - 1)
                    sides.append(jnp.where(ok, cb_ref[half, ri], NEG_INF))
                blocks.append(jnp.where(left, sides[0], sides[1]))
            bias_rows.append(jnp.concatenate(blocks, axis=1))
        bias = jnp.concatenate(bias_rows, axis=0)
        qm = _select_half(q, half)
        s_nb = _dot_nt(qm, k) + bias
        s_ctx = _dot_nt(qm, kc)
        es, l = _softmax_parts([s_nb, s_ctx])
        outs.append(_pv(es, [v, vc]) / l)
    o_ref[...] = jnp.where(_half_mask((nq, LANES), 0), outs[0], outs[1]).astype(o_ref.dtype)


def _nat_col_table(rpb):
    qc = np.arange(GRID_W)[:, None]
    kc = np.arange(GRID_W)[None, :]
    cs = np.clip(qc - NA_COLS // 2, 0, GRID_W - NA_COLS)
    valid = (kc >= cs) & (kc < cs + NA_COLS)
    ci = np.clip(kc - qc + NA_COLS - 1, 0, 2 * NA_COLS - 2)
    tab = jnp.where(valid[None, None], rpb[:, :, ci], NEG_INF)
    return jnp.concatenate([tab, tab], axis=-1)


def _nat_latent(qkv, kc, vc, rpb, batch, seq):
    nq = NAT_QROWS * GRID_W
    tpb = seq // nq
    past = kc.shape[1]
    cb = _nat_col_table(rpb)
    return pl.pallas_call(
        _nat_kernel,
        out_shape=jax.ShapeDtypeStruct((batch * seq, BRANCH_W), BF16),
        grid=(batch, H_B // 2, tpb),
        in_specs=[
            pl.BlockSpec((nq, LANES), lambda b, p, i: (b * tpb + i, T_BQ * 4 + p)),
            pl.BlockSpec((seq, LANES), lambda b, p, i: (b, T_BK * 4 + p)),
            pl.BlockSpec((seq, LANES), lambda b, p, i: (b, T_BV * 4 + p)),
            pl.BlockSpec((1, past, LANES), lambda b, p, i: (b, 0, p)),
            pl.BlockSpec((1, past, LANES), lambda b, p, i: (b, 0, p)),
            pl.BlockSpec((2, NAT_RI, GRID_W, LANES), lambda b, p, i: (p, 0, 0, 0)),
        ],
        out_specs=pl.BlockSpec((nq, LANES), lambda b, p, i: (b * tpb + i, p)),
        compiler_params=_cparams("parallel", "parallel", "parallel"),
        name="nat_latent",
    )(qkv, qkv, qkv, kc, vc, cb)


SWA_QBLK = 256
SWA_KBLK = SWA_QBLK + 2 * SWA_WINDOW


def _swa_kernel(sink_ref, q_ref, kv_ref, kc_ref, vc_ref, o_ref):
    seq = kv_ref.shape[0]
    p = pl.program_id(1)
    t0 = pl.program_id(2) * SWA_QBLK
    k0 = pl.multiple_of(jnp.clip(t0 - SWA_WINDOW, 0, seq - SWA_KBLK), SWA_WINDOW)
    q = q_ref[...] * ATTN_SCALE
    k = kv_ref[pl.ds(k0, SWA_KBLK), 0:LANES]
    v = kv_ref[pl.ds(k0, SWA_KBLK), LANES:2 * LANES]
    kc = kc_ref[0].astype(BF16)
    vc = vc_ref[0].astype(BF16)
    tq = t0 + lax.broadcasted_iota(jnp.int32, (SWA_QBLK, SWA_KBLK), 0)
    tk = k0 + lax.broadcasted_iota(jnp.int32, (SWA_QBLK, SWA_KBLK), 1)
    inside = jnp.abs(tq - tk) <= SWA_WINDOW
    kv_half = p // ((H_C // 2) // KV_C)
    outs = []
    for half in range(2):
        aligned = kv_half == half
        qm = _select_half(q, half)
        qm = jnp.where(aligned, qm, _swap_halves(qm))
        s_band = jnp.where(inside, _dot_nt(qm, k), NEG_INF)
        s_ctx = _dot_nt(qm, kc)
        es, l = _softmax_parts([s_band, s_ctx], extra=sink_ref[2 * p + half])
        o = _pv(es, [v, vc]) / l
        outs.append(jnp.where(aligned, o, _swap_halves(o)))
    o_ref[...] = jnp.where(_half_mask(q.shape, 0), outs[0], outs[1]).astype(o_ref.dtype)


def _swa_latent(qkv, kc, vc, sink, batch, seq):
    tpb = seq // SWA_QBLK
    past = kc.shape[1]
    return pl.pallas_call(
        _swa_kernel,
        out_shape=jax.ShapeDtypeStruct((batch * seq, BRANCH_W), BF16),
        grid=(batch, H_C // 2, tpb),
        in_specs=[
            pl.BlockSpec(memory_space=pltpu.SMEM),
            pl.BlockSpec((SWA_QBLK, LANES), lambda b, p, i: (b * tpb + i, T_CQ * 4 + p)),
            pl.BlockSpec((seq, COL_TILE), lambda b, p, i: (b, T_MISC)),
            pl.BlockSpec((1, past, LANES), lambda b, p, i: (b, 0, 0)),
            pl.BlockSpec((1, past, LANES), lambda b, p, i: (b, 0, 0)),
        ],
        out_specs=pl.BlockSpec((SWA_QBLK, LANES), lambda b, p, i: (b * tpb + i, p)),
        compiler_params=_cparams("parallel", "parallel", "parallel"),
        name="swa_latent",
    )(sink, qkv, qkv, kc, vc)


DIFF_QBLK = 256


def _diff_kernel(lam_ref, q_ref, k_ref, v_ref, kc_ref, vc_ref, o_ref):
    q = q_ref[...] * ATTN_SCALE
    ks = [k_ref[...], kc_ref[0].astype(BF16)]
    vs = [v_ref[...], vc_ref[0].astype(BF16)]
    o_ref[...] = _diff_head(q, ks, vs, lam_ref[0])


def _diff_latent(qkv, kc, vc, lam, batch, seq):
    tpb = seq // DIFF_QBLK
    past = kc.shape[1]
    return pl.pallas_call(
        _diff_kernel,
        out_shape=jax.ShapeDtypeStruct((batch * seq, BRANCH_W), F32),
        grid=(batch, H_D, tpb),
        in_specs=[
            pl.BlockSpec(memory_space=pltpu.SMEM),
            pl.BlockSpec((DIFF_QBLK, LANES), lambda b, h, i: (b * tpb + i, T_DQ * 4 + h)),
            pl.BlockSpec((seq, LANES), lambda b, h, i: (b, T_DK * 4 + h)),
            pl.BlockSpec((seq, LANES), lambda b, h, i: (b, T_DV * 4 + h)),
            pl.BlockSpec((1, past, LANES), lambda b, h, i: (b, 0, h)),
            pl.BlockSpec((1, past, LANES), lambda b, h, i: (b, 0, h)),
        ],
        out_specs=pl.BlockSpec((DIFF_QBLK, LANES), lambda b, h, i: (b * tpb + i, h)),
        compiler_params=_cparams("parallel", "parallel", "parallel"),
        name="diff_latent",
    )(lam, qkv, qkv, qkv, kc, vc)


def _rms_heads(y, g_ref, per_head_gain):
    outs = []
    for c in range(BRANCH_W // LANES):
        sl = slice(c * LANES, (c + 1) * LANES)
        g = g_ref[:, sl] if per_head_gain else g_ref[...]
        outs.append(_rms(y[:, sl], g))
    return jnp.concatenate(outs, axis=1)


def _route(logits):
    lane = lax.broadcasted_iota(jnp.int32, logits.shape, 1)
    lane_f = lane.astype(F32)
    first = lambda hit: jnp.min(jnp.where(hit, lane_f, float(LANES)), axis=1, keepdims=True)
    gl = jnp.where(lane < N_GROUPS, logits, NEG_INF)
    gmax = jnp.max(gl, axis=1, keepdims=True)
    gsel = first(gl == gmax)
    gw = 1.0 / jnp.sum(jnp.exp(gl - gmax), axis=1, keepdims=True)
    e_lane = lane - N_GROUPS
    e_group = lax.shift_right_arithmetic(e_lane, EXP_PER_GROUP.bit_length() - 1).astype(F32)
    in_group = (e_lane >= 0) & (e_lane < N_EXPERTS) & (e_group == gsel)
    el = jnp.where(in_group, logits, NEG_INF)
    e1 = jnp.max(el, axis=1, keepdims=True)
    i1 = first(el == e1)
    el2 = jnp.where(lane_f == i1, NEG_INF, el)
    e2 = jnp.max(el2, axis=1, keepdims=True)
    i2 = first(el2 == e2)
    t = jnp.exp(e2 - e1)
    w1 = gw / (1.0 + t)
    return jnp.where(lane_f == i1, w1, jnp.where(lane_f == i2, w1 * t, 0.0))


def _merge_kernel(ya, yb, yc, yd, ao, gt0, gt1, gt2, gt3, x_ref, mod_ref, anorm, subln, nffn,
                  wbr, wout, wr, xo_ref, h2_ref, gate_ref, *, yd_scale):
    a = jax.nn.sigmoid(ao[...].astype(F32)) * _rms_heads(ya[...], anorm, True)
    d = _rms_heads(yd[...], subln, False) * yd_scale
    branches = (a, yb[...], yc[...], d)
    mix = None
    for n, (br, gt) in enumerate(zip(branches, (gt0, gt1, gt2, gt3))):
        term = jax.nn.sigmoid(gt[...].astype(F32)) * _dot(br.astype(BF16), wbr[n])
        mix = term if mix is None else mix + term
    out = _dot(mix.astype(BF16), wout[...])
    xn = x_ref[...] + mod_ref[0, 2:3, :] * out
    xo_ref[...] = xn
    h2 = (_rms(xn, nffn[...]) * (1.0 + mod_ref[0, 4:5, :]) + mod_ref[0, 3:4, :]).astype(BF16)
    h2_ref[...] = h2
    gate_ref[...] = _route(_dot(h2, wr[...]))


def _merge(ys, qkv, x, mod3, a_norm, subln, norm_ffn, w_branch, w_out, w_route, lam_init):
    n = x.shape[0]
    tm = 256
    tiles_per_mod = n // mod3.shape[0] // tm
    y_spec = pl.BlockSpec((tm, BRANCH_W), lambda i: (i, 0))
    gt_spec = lambda k: pl.BlockSpec((tm, D_MODEL), lambda i: (i, T_GT * COL_TILE // D_MODEL + k))
    full = lambda shape: pl.BlockSpec(shape, lambda i: (0,) * len(shape))
    x_spec = pl.BlockSpec((tm, D_MODEL), lambda i: (i, 0))
    return pl.pallas_call(
        functools.partial(_merge_kernel, yd_scale=1.0 - lam_init),
        out_shape=(
            jax.ShapeDtypeStruct((n, D_MODEL), F32),
            jax.ShapeDtypeStruct((n, D_MODEL), BF16),
            jax.ShapeDtypeStruct((n, LANES), F32),
        ),
        grid=(n // tm,),
        in_specs=[
            y_spec, y_spec, y_spec, y_spec,
            pl.BlockSpec((tm, COL_TILE), lambda i: (i, T_AO)),
            gt_spec(0), gt_spec(1), gt_spec(2), gt_spec(3),
            x_spec,
            pl.BlockSpec((1, 6, D_MODEL), lambda i: (i // tiles_per_mod, 0, 0)),
            full((1, BRANCH_W)), full((1, LANES)), full((1, D_MODEL)),
            full((N_BRANCH, BRANCH_W, D_MODEL)), full((D_MODEL, D_MODEL)), full((D_MODEL, LANES)),
        ],
        out_specs=(x_spec, x_spec, pl.BlockSpec((tm, LANES), lambda i: (i, 0))),
        compiler_params=_cparams("parallel"),
        name="merge_route",
    )(*ys, qkv, qkv, qkv, qkv, qkv, x, mod3, a_norm.reshape(1, BRANCH_W), subln.reshape(1, LANES),
      norm_ffn.reshape(1, D_MODEL), w_branch, w_out, w_route)


def _moe_kernel(h_ref, gate_ref, x_ref, mod_ref, wg_ref, wu_ref, wd_ref, nf_ref, o_ref, acc_ref, *, final_norm):
    e = pl.program_id(1)

    @pl.when(e == 0)
    def _():
        acc_ref[...] = jnp.zeros_like(acc_ref)

    h = h_ref[...]
    gates = gate_ref[...]
    lane = lax.broadcasted_iota(jnp.int32, gates.shape, 1)
    g = jnp.sum(jnp.where(lane == e + N_GROUPS, gates, 0.0), axis=1, keepdims=True)
    hg = _dot(h, wg_ref[0])
    hid = hg * jax.nn.sigmoid(hg) * _dot(h, wu_ref[0]) * g
    acc_ref[...] += _dot(hid.astype(BF16), wd_ref[0])

    @pl.when(e == N_EXPERTS - 1)
    def _():
        y = x_ref[...] + mod_ref[0, 5:6, :] * acc_ref[...]
        o_ref[...] = _rms(y, nf_ref[...]) if final_norm else y


def _moe(h2, gates, x, mod3, wg, wu, wd, norm_final, final_norm):
    n = x.shape[0]
    tm = 1024
    tiles_per_mod = n // mod3.shape[0] // tm
    row = lambda w: pl.BlockSpec((tm, w), lambda i, e: (i, 0))
    return pl.pallas_call(
        functools.partial(_moe_kernel, final_norm=final_norm),
        out_shape=jax.ShapeDtypeStruct((n, D_MODEL), F32),
        grid=(n // tm, N_EXPERTS),
        in_specs=[
            row(D_MODEL), row(LANES), row(D_MODEL),
            pl.BlockSpec((1, 6, D_MODEL), lambda i, e: (i // tiles_per_mod, 0, 0)),
            pl.BlockSpec((1, D_MODEL, D_EXPERT), lambda i, e: (e, 0, 0)),
            pl.BlockSpec((1, D_MODEL, D_EXPERT), lambda i, e: (e, 0, 0)),
            pl.BlockSpec((1, D_EXPERT, D_MODEL), lambda i, e: (e, 0, 0)),
            pl.BlockSpec((1, D_MODEL), lambda i, e: (0, 0)),
        ],
        out_specs=row(D_MODEL),
        scratch_shapes=[pltpu.VMEM((tm, D_MODEL), F32)],
        compiler_params=_cparams("parallel", "arbitrary"),
        name="moe",
    )(h2, gates, x, mod3, wg, wu, wd, norm_final.reshape(1, D_MODEL))


def _reorder_w_in(w):
    pad = jnp.zeros((D_MODEL, COL_TILE - 2 * LANES - 16), w.dtype)
    cols = [w[:, 0:2048], w[:, 2064:4112], w[:, 4368:5904], w[:, 4112:4368], w[:, 2048:2064], pad, w[:, 5904:]]
    return jnp.concatenate(cols, axis=1).astype(BF16)


def _rope_tables(seq):
    nf = DH_C // 4
    inv = ROPE_BASE ** (-jnp.arange(nf, dtype=F32) / nf)
    t = jnp.arange(seq)
    ang_r = (t // GRID_W).astype(F32)[:, None] * inv[None, :]
    ang_c = (t % GRID_W).astype(F32)[:, None] * inv[None, :]
    cos = jnp.concatenate([jnp.cos(ang_r)] * 2 + [jnp.cos(ang_c)] * 2, axis=1)
    sin = jnp.concatenate([-jnp.sin(ang_r), jnp.sin(ang_r), -jnp.sin(ang_c), jnp.sin(ang_c)], axis=1)
    return jnp.tile(cos, (1, 2)), jnp.tile(sin, (1, 2))


def _layer(l, x, mod3, seq, lat, p):
    batch = x.shape[0] // seq
    lam_init = 0.8 - 0.6 * math.exp(-0.3 * l)
    ctx = lat is None
    proj = _qkv_proj(x, p["norm_mix"][l], mod3, p["w_in"][l], None if ctx else p["rope"], ctx,
                     tm=1024 if ctx else 2048)
    qkv, gates_raw = proj[:2]
    gates = gates_raw[:, :16] + jnp.concatenate([p["i_bias"][l].reshape(-1), p["f_bias"][l].reshape(-1)])[None]
    lam = (jnp.exp(jnp.sum(p["lq1"][l] * p["lk1"][l])) - jnp.exp(jnp.sum(p["lq2"][l] * p["lk2"][l]))
           + lam_init).reshape(1).astype(F32)
    sink = p["sink"][l]
    new = None
    if ctx:
        C0 = jnp.zeros((batch, 2, H_A, DH_A, DH_A), F32)
        n0 = jnp.zeros((batch, 2, H_A, DH_A), F32)
        m0 = jnp.zeros((batch, 2, H_A), F32)
        ya, Cs, ns, ms = _mlstm(qkv, gates, C0, n0, m0, batch, seq)
        yb, yc, yd = _ctx_attention(qkv, lam, sink, batch, seq)
        bk, bv, dk, dv, ckv = proj[2:]
        new = (bk.reshape(batch, seq, H_B, DH_B), bv.reshape(batch, seq, H_B, DH_B),
               ckv[:, :LANES].reshape(batch, seq, KV_C, DH_C), ckv[:, LANES:].reshape(batch, seq, KV_C, DH_C),
               dk.reshape(batch, seq, H_D, 2 * DH_D), dv.reshape(batch, seq, H_D, 2 * DH_D), Cs, ns, ms)
    else:
        nat_k, nat_v, swa_k, swa_v, diff_k, diff_v, C0, n0, m0 = lat
        past = nat_k.shape[1]
        flat = lambda a: a.reshape(batch, past, -1)
        ya, _, _, _ = _mlstm(qkv, gates, C0, n0, m0, batch, seq)
        yb = _nat_latent(qkv, flat(nat_k), flat(nat_v), p["rpb"][l], batch, seq)
        yc = _swa_latent(qkv, flat(swa_k), flat(swa_v), sink, batch, seq)
        yd = _diff_latent(qkv, flat(diff_k), flat(diff_v), lam, batch, seq)
    xn, h2, gate_w = _merge((ya, yb, yc, yd), qkv, x, mod3, p["a_norm"][l], p["subln"][l],
                            p["norm_ffn"][l], p["w_branch"][l], p["w_out"][l], p["w_route"][l], lam_init)
    y = _moe(h2, gate_w, xn, mod3, p["wg"][l], p["wu"][l], p["wd"][l], p["norm_final"],
             final_norm=(l == DEPTH - 1))
    return y, new


def kernel(x_prompt, x_sample, cache_nat_k, cache_nat_v, cache_swa_k, cache_swa_v, cache_diff_k, cache_diff_v, state_mlstm_C, state_mlstm_n, state_mlstm_m, c, c_ctx, norm_mix, norm_ffn, norm_final, w_mod, b_mod, w_in, mlstm_i_bias, mlstm_f_bias, mlstm_norm, nat_rpb, swa_sink, diff_lq1, diff_lk1, diff_lq2, diff_lk2, diff_subln, w_branch, w_out, router_group, router_expert, w_exp_gate, w_exp_up, w_exp_down):
    b_ctx, s_ctx, _ = x_prompt.shape
    b_lat, s_lat, _ = x_sample.shape
    cond8 = jnp.concatenate([c_ctx[None], c, jnp.zeros((8 - 1 - b_lat, D_MODEL), F32)], axis=0)
    mod = _modulation(cond8, w_mod, b_mod).reshape(DEPTH, 8, 6, D_MODEL)
    route_pad = jnp.zeros((DEPTH, D_MODEL, LANES - N_GROUPS - N_EXPERTS), F32)
    p = dict(
        norm_mix=norm_mix, norm_ffn=norm_ffn, norm_final=norm_final,
        w_in=[_reorder_w_in(w_in[l]) for l in range(DEPTH)],
        i_bias=mlstm_i_bias, f_bias=mlstm_f_bias, a_norm=mlstm_norm, rpb=nat_rpb, sink=swa_sink,
        lq1=diff_lq1, lk1=diff_lk1, lq2=diff_lq2, lk2=diff_lk2, subln=diff_subln,
        w_branch=w_branch.astype(BF16), w_out=w_out.astype(BF16),
        w_route=jnp.concatenate([router_group, router_expert, route_pad], axis=-1).astype(BF16),
        wg=w_exp_gate.astype(BF16), wu=w_exp_up.astype(BF16), wd=w_exp_down.astype(BF16),
        rope=_rope_tables(s_lat),
    )
    yp = x_prompt.reshape(b_ctx * s_ctx, D_MODEL)
    ys = x_sample.reshape(b_lat * s_lat, D_MODEL)
    news = []
    for l in range(DEPTH):
        yp, new = _layer(l, yp, mod[l, 0:1], s_ctx, None, p)
        news.append(new)
        lat = (cache_nat_k[:, l], cache_nat_v[:, l], cache_swa_k[:, l], cache_swa_v[:, l],
               cache_diff_k[:, l], cache_diff_v[:, l], state_mlstm_C[:, l], state_mlstm_n[:, l],
               state_mlstm_m[:, l])
        ys, _ = _layer(l, ys, mod[l, 1:1 + b_lat], s_lat, lat, p)
    stacked = tuple(jnp.stack([news[l][k] for l in range(DEPTH)], axis=1) for k in range(9))
    return (yp.reshape(b_ctx, s_ctx, D_MODEL), ys.reshape(b_lat, s_lat, D_MODEL)) + stacked
```

```python
import functools
import math

import numpy as np
import jax
import jax.numpy as jnp
from jax import lax
from jax.experimental import pallas as pl
from jax.experimental.pallas import tpu as pltpu

F32 = jnp.float32
BF16 = jnp.bfloat16

D_MODEL = 1024
DEPTH = 2
GRID_W = 64
BRANCH_W = 512
N_BRANCH = 4
H_A, DH_A, MLSTM_CHUNK = 4, 128, 128
H_B, DH_B = 8, 64
NA_ROWS, NA_COLS = 8, 16
H_C, KV_C, DH_C = 8, 2, 64
SWA_WINDOW = 128
H_D, DH_D = 4, 64
N_GROUPS, EXP_PER_GROUP = 4, 4
N_EXPERTS = N_GROUPS * EXP_PER_GROUP
D_EXPERT = 256
ROPE_BASE = 10000.0
EPS = 1e-6

LANES = 128
HALF = 64
COL_TILE = 512
N_COL_TILES = 20
N_PROJ = COL_TILE * N_COL_TILES
T_AQ, T_AK, T_AV, T_AO, T_BQ, T_BK, T_BV, T_CQ, T_DQ, T_DK, T_DV, T_MISC, T_GT = range(13)
MISC_GATE_OFF = 256
VMEM_LIMIT = 56 * 1024 * 1024
NEG_INF = float("-inf")

_NT = (((1,), (1,)), ((), ()))
_TN = (((0,), (0,)), ((), ()))


def _cparams(*sem):
    return pltpu.CompilerParams(dimension_semantics=sem, vmem_limit_bytes=VMEM_LIMIT)


def _dot(a, b):
    return jnp.dot(a, b, preferred_element_type=F32)


def _dot_nt(a, b):
    return lax.dot_general(a, b, _NT, preferred_element_type=F32)


def _dot_tn(a, b):
    return lax.dot_general(a, b, _TN, preferred_element_type=F32)


def _rms(x, g):
    return x * lax.rsqrt(jnp.mean(x * x, axis=-1, keepdims=True) + EPS) * g


def _sigmoid(x):
    return 0.5 * jnp.tanh(0.5 * x) + 0.5


def _mod_kernel(c_ref, w_ref, b_ref, o_ref):
    c = c_ref[...]
    s = (c * _sigmoid(c)).astype(BF16)
    o_ref[0] = _dot(s, w_ref[0].astype(BF16)) + b_ref[0]


def _modulation(cond8, w_mod, b_mod):
    tn = 1536
    n_out = 6 * D_MODEL
    return pl.pallas_call(
        _mod_kernel,
        out_shape=jax.ShapeDtypeStruct((DEPTH, 8, n_out), F32),
        grid=(DEPTH, n_out // tn),
        in_specs=[
            pl.BlockSpec((8, D_MODEL), lambda l, j: (0, 0)),
            pl.BlockSpec((1, D_MODEL, tn), lambda l, j: (l, 0, j)),
            pl.BlockSpec((1, 1, tn), lambda l, j: (l, 0, j)),
        ],
        out_specs=pl.BlockSpec((1, 8, tn), lambda l, j: (l, 0, j)),
        compiler_params=_cparams("parallel", "parallel"),
        name="modulation",
    )(cond8, w_mod, b_mod.reshape(DEPTH, 1, n_out))


CACHE_TILES = (T_BK, T_BV, T_DK, T_DV)
QKV_ROW_CHUNK = 512


def _rope128(x, cos, sin):
    lane = lax.broadcasted_iota(jnp.int32, x.shape, 1)
    partner = jnp.where((lane % 32) < 16, pltpu.roll(x, LANES - 16, 1), pltpu.roll(x, 16, 1))
    return x * cos + partner * sin


def _qkv_kernel(x_ref, nw_ref, mod_ref, w_ref, *rest, rope, cache):
    rest = list(rest)
    if rope:
        cos_ref, sin_ref = rest[:2]
        rest = rest[2:]
    o_ref, g_ref = rest[:2]
    cache_refs = rest[2:-1]
    h_scr = rest[-1]
    j = pl.program_id(1)

    @pl.when(j == 0)
    def _():
        h = _rms(x_ref[...], nw_ref[...]) * (1.0 + mod_ref[0, 1:2, :]) + mod_ref[0, 0:1, :]
        h_scr[...] = h.astype(BF16)

    def tile_kind(cond, store):
        @pl.when(cond)
        def _():
            for r in range(h_scr.shape[0] // QKV_ROW_CHUNK):
                rows = slice(r * QKV_ROW_CHUNK, (r + 1) * QKV_ROW_CHUNK)
                store(rows, _dot(h_scr[rows, :], w_ref[...]))

    def roped(rows, acc, c):
        sl = slice(c * LANES, (c + 1) * LANES)
        return _rope128(acc[:, sl], cos_ref[rows, :], sin_ref[rows, :]).astype(BF16)

    def store_plain(rows, acc):
        o_ref[rows, :] = acc.astype(BF16)

    def store_ak(rows, acc):
        o_ref[rows, :] = (acc * (DH_A ** -0.5)).astype(BF16)

    def store_misc(rows, acc):
        o_ref[rows, :] = acc.astype(BF16)
        if rope:
            o_ref[rows, :LANES] = roped(rows, acc, 0)
        g_ref[rows, :] = acc[:, MISC_GATE_OFF:MISC_GATE_OFF + LANES]
        if cache:
            cache_refs[-1][rows, :] = acc[:, :2 * LANES]

    def store_rope(rows, acc):
        for c in range(COL_TILE // LANES):
            o_ref[rows, c * LANES:(c + 1) * LANES] = roped(rows, acc, c)

    special = (j == T_AK) | (j == T_MISC)
    tile_kind(j == T_AK, store_ak)
    tile_kind(j == T_MISC, store_misc)
    if rope:
        is_rope = (j == T_CQ) | (j == T_DQ) | (j == T_DK)
        special = special | is_rope
        tile_kind(is_rope, store_rope)
    if cache:
        for t, ref in zip(CACHE_TILES, cache_refs[:-1]):
            def store_cached(rows, acc, ref=ref):
                o_ref[rows, :] = acc.astype(BF16)
                ref[rows, :] = acc
            special = special | (j == t)
            tile_kind(j == t, store_cached)
    tile_kind(jnp.logical_not(special), store_plain)


def _qkv_proj(x, norm_w, mod3, w_in_r, rope_tabs, cache, tm):
    n = x.shape[0]
    tiles_per_mod = n // mod3.shape[0] // tm
    rope = rope_tabs is not None
    in_specs = [
        pl.BlockSpec((tm, D_MODEL), lambda i, j: (i, 0)),
        pl.BlockSpec((1, D_MODEL), lambda i, j: (0, 0)),
        pl.BlockSpec((1, 6, D_MODEL), lambda i, j: (i // tiles_per_mod, 0, 0)),
        pl.BlockSpec((D_MODEL, COL_TILE), lambda i, j: (0, j)),
    ]
    args = [x, norm_w.reshape(1, D_MODEL), mod3, w_in_r]
    if rope:
        tiles_per_seq = rope_tabs[0].shape[0] // tm
        in_specs += [pl.BlockSpec((tm, LANES), lambda i, j: (i % tiles_per_seq, 0))] * 2
        args += list(rope_tabs)
    row = lambda w: pl.BlockSpec((tm, w), lambda i, j: (i, 0))
    out_shape = [jax.ShapeDtypeStruct((n, N_PROJ), BF16), jax.ShapeDtypeStruct((n, LANES), F32)]
    out_specs = [pl.BlockSpec((tm, COL_TILE), lambda i, j: (i, j)), row(LANES)]
    if cache:
        out_shape += [jax.ShapeDtypeStruct((n, COL_TILE), F32)] * len(CACHE_TILES)
        out_shape += [jax.ShapeDtypeStruct((n, 2 * LANES), F32)]
        out_specs += [row(COL_TILE)] * len(CACHE_TILES) + [row(2 * LANES)]
    return pl.pallas_call(
        functools.partial(_qkv_kernel, rope=rope, cache=cache),
        out_shape=tuple(out_shape),
        grid=(n // tm, N_COL_TILES),
        in_specs=in_specs,
        out_specs=tuple(out_specs),
        scratch_shapes=[pltpu.VMEM((tm, D_MODEL), BF16)],
        compiler_params=_cparams("parallel", "arbitrary"),
        name="qkv_rope" if rope else "qkv",
    )(*args)


def _log_sigmoid(x):
    return jnp.minimum(x, 0.0) - jnp.log1p(jnp.exp(-jnp.abs(x)))


def _mlstm_chunk(c, h, d, state, q_ref, k_ref, v_ref, gr_ref, gc_ref, out_ref):
    L = MLSTM_CHUNK
    C, n, m = state
    row = lax.broadcasted_iota(jnp.int32, (L, L), 0)
    col = lax.broadcasted_iota(jnp.int32, (L, L), 1)
    tri = (col <= row) if d == 0 else (col >= row)
    tri_t = (row <= col) if d == 0 else (row >= col)
    off = pl.multiple_of(c * L, L)
    hs = slice(h * DH_A, (h + 1) * DH_A)
    qb = q_ref[pl.ds(off, L), hs]
    kb = k_ref[pl.ds(off, L), hs]
    vb = v_ref[pl.ds(off, L), hs]
    qx, kx, vx = qb.astype(F32), kb.astype(F32), vb.astype(F32)
    g_rows = gr_ref[h, c]
    g_cols = gc_ref[h, c]
    ig_r = g_rows[2 * d:2 * d + 1, :]
    lf_r = _log_sigmoid(g_rows[2 * d + 1:2 * d + 2, :])
    ig_c = g_cols[:, 2 * d:2 * d + 1]
    lf_c = _log_sigmoid(g_cols[:, 2 * d + 1:2 * d + 2])
    b_c = jnp.sum(jnp.where(tri, lf_r, 0.0), axis=1, keepdims=True)
    b_r = jnp.sum(jnp.where(tri_t, lf_c, 0.0), axis=0, keepdims=True)
    qk = _dot_nt(qb, kb)
    qc = _dot_nt(qb, C.astype(BF16))
    yield
    logd = jnp.where(tri, b_c - b_r + ig_r, NEG_INF)
    inter = b_c + m
    m_t = jnp.maximum(inter, jnp.max(logd, axis=1, keepdims=True))
    yield
    w = jnp.exp(logd - m_t)
    a = jnp.exp(inter - m_t)
    s = qk * w
    num = a * qc + _dot(s.astype(BF16), vb)
    den = a * jnp.sum(qx * n, axis=1, keepdims=True) + jnp.sum(s, axis=1, keepdims=True)
    yield
    out_ref[pl.ds(off, L), hs] = num / jnp.maximum(jnp.abs(den), jnp.exp(-m_t))
    b_tot = jnp.sum(lf_r, axis=1, keepdims=True)
    g_c = b_tot - b_c + ig_c
    m_new = jnp.maximum(b_tot + m, jnp.max(g_c, axis=0, keepdims=True))
    yield
    decay = jnp.exp(b_tot + m - m_new)
    wk = jnp.exp(g_c - m_new)
    C_new = decay * C + _dot_tn((wk * vx).astype(BF16), kb)
    n_new = decay * n + jnp.sum(wk * kx, axis=0, keepdims=True)
    return C_new, n_new, m_new


def _run_staged(gens):
    results = [None] * len(gens)
    live = list(range(len(gens)))
    while live:
        for i in list(live):
            try:
                next(gens[i])
            except StopIteration as stop:
                results[i] = stop.value
                live.remove(i)
    return results


def _mlstm_kernel(q_ref, k_ref, v_ref, gr_ref, gc_ref, c0_ref, n0_ref, m0_ref,
                  h_ref, cs_ref, ns_ref, ms_ref, hb_scr, *, nc):
    chains = [(h, d) for h in range(H_A) for d in range(2)]

    def body(ci, states):
        gens = [_mlstm_chunk(ci if d == 0 else nc - 1 - ci, h, d, state, q_ref, k_ref, v_ref, gr_ref, gc_ref,
                             h_ref if d == 0 else hb_scr) for (h, d), state in zip(chains, states)]
        return tuple(_run_staged(gens))

    init = tuple((c0_ref[0, d, h], n0_ref[0, h, d:d + 1, :], m0_ref[0, h, d:d + 1, 0:1]) for h, d in chains)
    final = lax.fori_loop(0, nc, body, init)
    for (h, d), (C, n, m) in zip(chains, final):
        cs_ref[0, d, h] = C
        ns_ref[0, h, d:d + 1, :] = n
        ms_ref[0, h, d:d + 1, :] = jnp.broadcast_to(m, (1, LANES))
    h_ref[...] += hb_scr[...]


def _mlstm(qkv, gates, C0, n0, m0, batch, seq):
    nc = seq // MLSTM_CHUNK
    L = MLSTM_CHUNK
    ig = gates[:, :8].reshape(batch, nc, L, 2, H_A)
    fg = gates[:, 8:].reshape(batch, nc, L, 2, H_A)
    g4 = jnp.stack([ig[..., 0, :], fg[..., 0, :], ig[..., 1, :], fg[..., 1, :]], axis=-1)
    g_cols = jnp.transpose(g4, (3, 0, 1, 2, 4)).reshape(H_A, batch * nc, L, 4)
    g_rows = jnp.transpose(g4, (3, 0, 1, 4, 2)).reshape(H_A, batch * nc, 4, L)
    n0t = jnp.transpose(n0, (0, 2, 1, 3))
    m0t = jnp.broadcast_to(jnp.transpose(m0, (0, 2, 1))[..., None], (batch, H_A, 2, LANES))
    qkv_spec = lambda tile: pl.BlockSpec((seq, COL_TILE), lambda b: (b, tile))
    c_spec = pl.BlockSpec((1, 2, H_A, DH_A, DH_A), lambda b: (b, 0, 0, 0, 0))
    state_spec = pl.BlockSpec((1, H_A, 2, LANES), lambda b: (b, 0, 0, 0))
    h, Cs, ns, ms = pl.pallas_call(
        functools.partial(_mlstm_kernel, nc=nc),
        out_shape=(
            jax.ShapeDtypeStruct((batch * seq, H_A * DH_A), F32),
            jax.ShapeDtypeStruct((batch, 2, H_A, DH_A, DH_A), F32),
            jax.ShapeDtypeStruct((batch, H_A, 2, DH_A), F32),
            jax.ShapeDtypeStruct((batch, H_A, 2, LANES), F32),
        ),
        grid=(batch,),
        in_specs=[
            qkv_spec(T_AQ), qkv_spec(T_AK), qkv_spec(T_AV),
            pl.BlockSpec((H_A, nc, 4, L), lambda b: (0, b, 0, 0)),
            pl.BlockSpec((H_A, nc, L, 4), lambda b: (0, b, 0, 0)),
            c_spec, state_spec, state_spec,
        ],
        out_specs=(pl.BlockSpec((seq, H_A * DH_A), lambda b: (b, 0)), c_spec, state_spec, state_spec),
        scratch_shapes=[pltpu.VMEM((seq, H_A * DH_A), F32)],
        compiler_params=_cparams("parallel"),
        name="mlstm",
    )(qkv, qkv, qkv, g_rows, g_cols, C0, n0t, m0t)
    return h, Cs, jnp.transpose(ns, (0, 2, 1, 3)), jnp.transpose(ms[..., 0], (0, 2, 1))


ATTN_SCALE = DH_B ** -0.5


def _half_mask(shape, half):
    lane = lax.broadcasted_iota(jnp.int32, shape, 1)
    return (lane >= HALF) if half else (lane < HALF)


def _select_half(q, half):
    return jnp.where(_half_mask(q.shape, half), q, jnp.zeros_like(q))


def _swap_halves(x):
    return pltpu.roll(x.astype(F32), HALF, 1).astype(x.dtype)


def _softmax_parts(scores, extra=None):
    m = functools.reduce(jnp.maximum, [jnp.max(s, axis=1, keepdims=True) for s in scores])
    if extra is not None:
        m = jnp.maximum(m, extra)
    es = [jnp.exp(s - m) for s in scores]
    l = functools.reduce(lambda x, y: x + y, [jnp.sum(e, axis=1, keepdims=True) for e in es])
    if extra is not None:
        l = l + jnp.exp(extra - m)
    return es, l


def _pv(es, vs):
    return functools.reduce(lambda x, y: x + y, [_dot(e.astype(BF16), v) for e, v in zip(es, vs)])


def _diff_head(q, ks, vs, lam):
    parts = [_softmax_parts([_dot_nt(_select_half(q, comp), k) for k in ks]) for comp in range(2)]
    (e0, l0), (e1, l1) = parts
    coef = lam * l0 / l1
    pd = [a - coef * b for a, b in zip(e0, e1)]
    return _pv(pd, vs) / l0


def _ctx_attn_kernel(lam_ref, sink_ref, bq, bk, bv, cq, misc, dq, dk, dv, yb_ref, yc_ref, yd_ref):
    for p in range(H_B // 2):
        sl = slice(p * LANES, (p + 1) * LANES)
        q = bq[:, sl] * ATTN_SCALE
        outs = []
        for half in range(2):
            es, l = _softmax_parts([_dot_nt(_select_half(q, half), bk[:, sl])])
            outs.append(_pv(es, [bv[:, sl]]) / l)
        yb_ref[:, sl] = jnp.where(_half_mask(q.shape, 0), outs[0], outs[1]).astype(yb_ref.dtype)

    kc = misc[:, 0:LANES]
    vc = misc[:, LANES:2 * LANES]
    for p in range(H_C // 2):
        sl = slice(p * LANES, (p + 1) * LANES)
        q = cq[:, sl] * ATTN_SCALE
        outs = []
        for half in range(2):
            head = 2 * p + half
            kv = head // (H_C // KV_C)
            qm = _select_half(q, half)
            if kv != half:
                qm = _swap_halves(qm)
            es, l = _softmax_parts([_dot_nt(qm, kc)], extra=sink_ref[head])
            o = _pv(es, [vc]) / l
            outs.append(o if kv == half else _swap_halves(o))
        yc_ref[:, sl] = jnp.where(_half_mask(q.shape, 0), outs[0], outs[1]).astype(yc_ref.dtype)

    lam = lam_ref[0]
    for h in range(H_D):
        sl = slice(h * LANES, (h + 1) * LANES)
        yd_ref[:, sl] = _diff_head(dq[:, sl] * ATTN_SCALE, [dk[:, sl]], [dv[:, sl]], lam)


def _ctx_attention(qkv, lam, sink, batch, seq):
    tile = lambda t: pl.BlockSpec((seq, COL_TILE), lambda b: (b, t))
    smem = pl.BlockSpec(memory_space=pltpu.SMEM)
    out = lambda dt: jax.ShapeDtypeStruct((batch * seq, BRANCH_W), dt)
    out_spec = pl.BlockSpec((seq, BRANCH_W), lambda b: (b, 0))
    return pl.pallas_call(
        _ctx_attn_kernel,
        out_shape=(out(BF16), out(BF16), out(F32)),
        grid=(batch,),
        in_specs=[smem, smem, tile(T_BQ), tile(T_BK), tile(T_BV), tile(T_CQ), tile(T_MISC),
                  tile(T_DQ), tile(T_DK), tile(T_DV)],
        out_specs=(out_spec, out_spec, out_spec),
        compiler_params=_cparams("parallel"),
        name="ctx_attention",
    )(lam, sink, *([qkv] * 8))


NAT_QROWS = 4
NAT_KROWS = 12
NAT_RI = 2 * NA_ROWS - 1


def _nat_kernel(q_ref, k_ref, v_ref, kc_ref, vc_ref, cb_ref, o_ref):
    rows = k_ref.shape[0] // GRID_W
    r0 = pl.program_id(2) * NAT_QROWS
    kb = jnp.clip(r0 - NA_ROWS // 2, 0, rows - NAT_KROWS)
    koff = pl.multiple_of(kb * GRID_W, GRID_W)
    nq = NAT_QROWS * GRID_W
    nk = NAT_KROWS * GRID_W
    q = q_ref[...] * ATTN_SCALE
    k = k_ref[pl.ds(koff, nk), :]
    v = v_ref[pl.ds(koff, nk), :]
    kc = kc_ref[0].astype(BF16)
    vc = vc_ref[0].astype(BF16)
    left = _half_mask((GRID_W, LANES), 0)
    outs = []
    for half in range(2):
        bias_rows = []
        for a in range(NAT_QROWS):
            r = r0 + a
            rs = jnp.clip(r - NA_ROWS // 2, 0, rows - NA_ROWS)
            blocks = []
            for ip in range(NAT_KROWS // 2):
                sides = []
                for side in range(2):
                    kr = kb + 2 * ip + side
                    ok = (kr >= rs) & (kr < rs + NA_ROWS)
                    ri = jnp.clip(kr - r + NA_ROWS - 1, 0, NAT_RI - 1)
                    sides.append(jnp.where(ok, cb_ref[half, ri], NEG_INF))
                blocks.append(jnp.where(left, sides[0], sides[1]))
            bias_rows.append(jnp.concatenate(blocks, axis=1))
        bias = jnp.concatenate(bias_rows, axis=0)
        qm = _select_half(q, half)
        s_nb = _dot_nt(qm, k) + bias
        s_ctx = _dot_nt(qm, kc)
        es, l = _softmax_parts([s_nb, s_ctx])
        outs.append(_pv(es, [v, vc]) / l)
    o_ref[...] = jnp.where(_half_mask((nq, LANES), 0), outs[0], outs[1]).astype(o_ref.dtype)


def _nat_col_table(rpb):
    qc = np.arange(GRID_W)[:, None]
    kc = np.arange(GRID_W)[None, :]
    cs = np.clip(qc - NA_COLS // 2, 0, GRID_W - NA_COLS)
    valid = (kc >= cs) & (kc < cs + NA_COLS)
    n_ci = 2 * NA_COLS - 1
    lo = GRID_W - 1 - (NA_COLS - 1)
    v = jnp.pad(rpb, ((0, 0), (0, 0), (lo, 2 * GRID_W - n_ci - lo)))
    flat = jnp.tile(v, (1, 1, GRID_W))[..., :GRID_W * (2 * GRID_W - 1)]
    toep = flat.reshape(rpb.shape[0], rpb.shape[1], GRID_W, 2 * GRID_W - 1)[..., GRID_W - 1:]
    tab = jnp.where(valid[None, None], toep, NEG_INF)
    return jnp.concatenate([tab, tab], axis=-1)


def _nat_latent(qkv, kc, vc, rpb, batch, seq):
    nq = NAT_QROWS * GRID_W
    tpb = seq // nq
    past = kc.shape[1]
    cb = _nat_col_table(rpb)
    return pl.pallas_call(
        _nat_kernel,
        out_shape=jax.ShapeDtypeStruct((batch * seq, BRANCH_W), BF16),
        grid=(batch, H_B // 2, tpb),
        in_specs=[
            pl.BlockSpec((nq, LANES), lambda b, p, i: (b * tpb + i, T_BQ * 4 + p)),
            pl.BlockSpec((seq, LANES), lambda b, p, i: (b, T_BK * 4 + p)),
            pl.BlockSpec((seq, LANES), lambda b, p, i: (b, T_BV * 4 + p)),
            pl.BlockSpec((1, past, LANES), lambda b, p, i: (b, 0, p)),
            pl.BlockSpec((1, past, LANES), lambda b, p, i: (b, 0, p)),
            pl.BlockSpec((2, NAT_RI, GRID_W, LANES), lambda b, p, i: (p, 0, 0, 0)),
        ],
        out_specs=pl.BlockSpec((nq, LANES), lambda b, p, i: (b * tpb + i, p)),
        compiler_params=_cparams("parallel", "parallel", "parallel"),
        name="nat_latent",
    )(qkv, qkv, qkv, kc, vc, cb)


SWA_QBLK = 256
SWA_KBLK = SWA_QBLK + 2 * SWA_WINDOW


def _swa_kernel(sink_ref, q_ref, kv_ref, kc_ref, vc_ref, o_ref):
    seq = kv_ref.shape[0]
    p = pl.program_id(1)
    t0 = pl.program_id(2) * SWA_QBLK
    k0 = pl.multiple_of(jnp.clip(t0 - SWA_WINDOW, 0, seq - SWA_KBLK), SWA_WINDOW)
    q = q_ref[...] * ATTN_SCALE
    k = kv_ref[pl.ds(k0, SWA_KBLK), 0:LANES]
    v = kv_ref[pl.ds(k0, SWA_KBLK), LANES:2 * LANES]
    kc = kc_ref[0].astype(BF16)
    vc = vc_ref[0].astype(BF16)
    tq = t0 + lax.broadcasted_iota(jnp.int32, (SWA_QBLK, SWA_KBLK), 0)
    tk = k0 + lax.broadcasted_iota(jnp.int32, (SWA_QBLK, SWA_KBLK), 1)
    inside = jnp.abs(tq - tk) <= SWA_WINDOW
    kv_half = p // ((H_C // 2) // KV_C)
    outs = []
    for half in range(2):
        aligned = kv_half == half
        qm = _select_half(q, half)
        qm = jnp.where(aligned, qm, _swap_halves(qm))
        s_band = jnp.where(inside, _dot_nt(qm, k), NEG_INF)
        s_ctx = _dot_nt(qm, kc)
        es, l = _softmax_parts([s_band, s_ctx], extra=sink_ref[2 * p + half])
        o = _pv(es, [v, vc]) / l
        outs.append(jnp.where(aligned, o, _swap_halves(o)))
    o_ref[...] = jnp.where(_half_mask(q.shape, 0), outs[0], outs[1]).astype(o_ref.dtype)


def _swa_latent(qkv, kc, vc, sink, batch, seq):
    tpb = seq // SWA_QBLK
    past = kc.shape[1]
    return pl.pallas_call(
        _swa_kernel,
        out_shape=jax.ShapeDtypeStruct((batch * seq, BRANCH_W), BF16),
        grid=(batch, H_C // 2, tpb),
        in_specs=[
            pl.BlockSpec(memory_space=pltpu.SMEM),
            pl.BlockSpec((SWA_QBLK, LANES), lambda b, p, i: (b * tpb + i, T_CQ * 4 + p)),
            pl.BlockSpec((seq, COL_TILE), lambda b, p, i: (b, T_MISC)),
            pl.BlockSpec((1, past, LANES), lambda b, p, i: (b, 0, 0)),
            pl.BlockSpec((1, past, LANES), lambda b, p, i: (b, 0, 0)),
        ],
        out_specs=pl.BlockSpec((SWA_QBLK, LANES), lambda b, p, i: (b * tpb + i, p)),
        compiler_params=_cparams("parallel", "parallel", "parallel"),
        name="swa_latent",
    )(sink, qkv, qkv, kc, vc)


DIFF_QBLK = 256


def _diff_kernel(lam_ref, q_ref, k_ref, v_ref, kc_ref, vc_ref, o_ref):
    q = q_ref[...] * ATTN_SCALE
    ks = [k_ref[...], kc_ref[0].astype(BF16)]
    vs = [v_ref[...], vc_ref[0].astype(BF16)]
    o_ref[...] = _diff_head(q, ks, vs, lam_ref[0])


def _diff_latent(qkv, kc, vc, lam, batch, seq):
    tpb = seq // DIFF_QBLK
    past = kc.shape[1]
    return pl.pallas_call(
        _diff_kernel,
        out_shape=jax.ShapeDtypeStruct((batch * seq, BRANCH_W), F32),
        grid=(batch, H_D, tpb),
        in_specs=[
            pl.BlockSpec(memory_space=pltpu.SMEM),
            pl.BlockSpec((DIFF_QBLK, LANES), lambda b, h, i: (b * tpb + i, T_DQ * 4 + h)),
            pl.BlockSpec((seq, LANES), lambda b, h, i: (b, T_DK * 4 + h)),
            pl.BlockSpec((seq, LANES), lambda b, h, i: (b, T_DV * 4 + h)),
            pl.BlockSpec((1, past, LANES), lambda b, h, i: (b, 0, h)),
            pl.BlockSpec((1, past, LANES), lambda b, h, i: (b, 0, h)),
        ],
        out_specs=pl.BlockSpec((DIFF_QBLK, LANES), lambda b, h, i: (b * tpb + i, h)),
        compiler_params=_cparams("parallel", "parallel", "parallel"),
        name="diff_latent",
    )(lam, qkv, qkv, qkv, kc, vc)


def _rms_heads(y, g_ref, per_head_gain):
    outs = []
    for c in range(BRANCH_W // LANES):
        sl = slice(c * LANES, (c + 1) * LANES)
        g = g_ref[:, sl] if per_head_gain else g_ref[...]
        outs.append(_rms(y[:, sl], g))
    return jnp.concatenate(outs, axis=1)


def _route(logits):
    lane = lax.broadcasted_iota(jnp.int32, logits.shape, 1)
    lane_f = lane.astype(F32)
    first = lambda hit: jnp.min(jnp.where(hit, lane_f, float(LANES)), axis=1, keepdims=True)
    gl = jnp.where(lane < N_GROUPS, logits, NEG_INF)
    gmax = jnp.max(gl, axis=1, keepdims=True)
    gsel = first(gl == gmax)
    gw = 1.0 / jnp.sum(jnp.exp(gl - gmax), axis=1, keepdims=True)
    e_lane = lane - N_GROUPS
    e_group = lax.shift_right_arithmetic(e_lane, EXP_PER_GROUP.bit_length() - 1).astype(F32)
    in_group = (e_lane >= 0) & (e_lane < N_EXPERTS) & (e_group == gsel)
    el = jnp.where(in_group, logits, NEG_INF)
    e1 = jnp.max(el, axis=1, keepdims=True)
    i1 = first(el == e1)
    el2 = jnp.where(lane_f == i1, NEG_INF, el)
    e2 = jnp.max(el2, axis=1, keepdims=True)
    i2 = first(el2 == e2)
    t = jnp.exp(e2 - e1)
    w1 = gw / (1.0 + t)
    return jnp.where(lane_f == i1, w1, jnp.where(lane_f == i2, w1 * t, 0.0))


def _merge_kernel(ya, yb, yc, yd, ao, gt0, gt1, gt2, gt3, x_ref, mod_ref, anorm, subln, nffn,
                  wbr, wout, wr, xo_ref, h2_ref, gate_ref, *, yd_scale):
    a = _sigmoid(ao[...].astype(F32)) * _rms_heads(ya[...], anorm, True)
    d = _rms_heads(yd[...], subln, False) * yd_scale
    branches = (a, yb[...], yc[...], d)
    mix = None
    for n, (br, gt) in enumerate(zip(branches, (gt0, gt1, gt2, gt3))):
        term = _sigmoid(gt[...].astype(F32)) * _dot(br.astype(BF16), wbr[n])
        mix = term if mix is None else mix + term
    out = _dot(mix.astype(BF16), wout[...])
    xn = x_ref[...] + mod_ref[0, 2:3, :] * out
    xo_ref[...] = xn
    h2 = (_rms(xn, nffn[...]) * (1.0 + mod_ref[0, 4:5, :]) + mod_ref[0, 3:4, :]).astype(BF16)
    h2_ref[...] = h2
    gate_ref[...] = _route(_dot(h2, wr[...]))


def _merge(ys, qkv, x, mod3, a_norm, subln, norm_ffn, w_branch, w_out, w_route, lam_init):
    n = x.shape[0]
    tm = 512
    tiles_per_mod = n // mod3.shape[0] // tm
    y_spec = pl.BlockSpec((tm, BRANCH_W), lambda i: (i, 0))
    gt_spec = lambda k: pl.BlockSpec((tm, D_MODEL), lambda i: (i, T_GT * COL_TILE // D_MODEL + k))
    full = lambda shape: pl.BlockSpec(shape, lambda i: (0,) * len(shape))
    x_spec = pl.BlockSpec((tm, D_MODEL), lambda i: (i, 0))
    return pl.pallas_call(
        functools.partial(_merge_kernel, yd_scale=1.0 - lam_init),
        out_shape=(
            jax.ShapeDtypeStruct((n, D_MODEL), F32),
            jax.ShapeDtypeStruct((n, D_MODEL), BF16),
            jax.ShapeDtypeStruct((n, LANES), F32),
        ),
        grid=(n // tm,),
        in_specs=[
            y_spec, y_spec, y_spec, y_spec,
            pl.BlockSpec((tm, COL_TILE), lambda i: (i, T_AO)),
            gt_spec(0), gt_spec(1), gt_spec(2), gt_spec(3),
            x_spec,
            pl.BlockSpec((1, 6, D_MODEL), lambda i: (i // tiles_per_mod, 0, 0)),
            full((1, BRANCH_W)), full((1, LANES)), full((1, D_MODEL)),
            full((N_BRANCH, BRANCH_W, D_MODEL)), full((D_MODEL, D_MODEL)), full((D_MODEL, LANES)),
        ],
        out_specs=(x_spec, x_spec, pl.BlockSpec((tm, LANES), lambda i: (i, 0))),
        compiler_params=_cparams("parallel"),
        name="merge_route",
    )(*ys, qkv, qkv, qkv, qkv, qkv, x, mod3, a_norm.reshape(1, BRANCH_W), subln.reshape(1, LANES),
      norm_ffn.reshape(1, D_MODEL), w_branch, w_out, w_route)


def _moe_kernel(h_ref, gate_ref, x_ref, mod_ref, wg_ref, wu_ref, wd_ref, nf_ref, o_ref, acc_ref, *, final_norm):
    e = pl.program_id(1)

    @pl.when(e == 0)
    def _():
        acc_ref[...] = jnp.zeros_like(acc_ref)

    h = h_ref[...]
    gates = gate_ref[...]
    lane = lax.broadcasted_iota(jnp.int32, gates.shape, 1)
    g = jnp.sum(jnp.where(lane == e + N_GROUPS, gates, 0.0), axis=1, keepdims=True)
    hg = _dot(h, wg_ref[0])
    hid = hg * _sigmoid(hg) * _dot(h, wu_ref[0]) * g
    acc_ref[...] += _dot(hid.astype(BF16), wd_ref[0])

    @pl.when(e == N_EXPERTS - 1)
    def _():
        y = x_ref[...] + mod_ref[0, 5:6, :] * acc_ref[...]
        o_ref[...] = _rms(y, nf_ref[...]) if final_norm else y


def _moe(h2, gates, x, mod3, wg, wu, wd, norm_final, final_norm):
    n = x.shape[0]
    tm = 1024
    tiles_per_mod = n // mod3.shape[0] // tm
    row = lambda w: pl.BlockSpec((tm, w), lambda i, e: (i, 0))
    return pl.pallas_call(
        functools.partial(_moe_kernel, final_norm=final_norm),
        out_shape=jax.ShapeDtypeStruct((n, D_MODEL), F32),
        grid=(n // tm, N_EXPERTS),
        in_specs=[
            row(D_MODEL), row(LANES), row(D_MODEL),
            pl.BlockSpec((1, 6, D_MODEL), lambda i, e: (i // tiles_per_mod, 0, 0)),
            pl.BlockSpec((1, D_MODEL, D_EXPERT), lambda i, e: (e, 0, 0)),
            pl.BlockSpec((1, D_MODEL, D_EXPERT), lambda i, e: (e, 0, 0)),
            pl.BlockSpec((1, D_EXPERT, D_MODEL), lambda i, e: (e, 0, 0)),
            pl.BlockSpec((1, D_MODEL), lambda i, e: (0, 0)),
        ],
        out_specs=row(D_MODEL),
        scratch_shapes=[pltpu.VMEM((tm, D_MODEL), F32)],
        compiler_params=_cparams("parallel", "arbitrary"),
        name="moe",
    )(h2, gates, x, mod3, wg, wu, wd, norm_final.reshape(1, D_MODEL))


def _reorder_w_in(w):
    pad = jnp.zeros((D_MODEL, COL_TILE - 2 * LANES - 16), w.dtype)
    cols = [w[:, 0:2048], w[:, 2064:4112], w[:, 4368:5904], w[:, 4112:4368], w[:, 2048:2064], pad, w[:, 5904:]]
    return jnp.concatenate(cols, axis=1).astype(BF16)


def _rope_tables(seq):
    nf = DH_C // 4
    inv = ROPE_BASE ** (-jnp.arange(nf, dtype=F32) / nf)
    t = jnp.arange(seq)
    ang_r = (t // GRID_W).astype(F32)[:, None] * inv[None, :]
    ang_c = (t % GRID_W).astype(F32)[:, None] * inv[None, :]
    cos = jnp.concatenate([jnp.cos(ang_r)] * 2 + [jnp.cos(ang_c)] * 2, axis=1)
    sin = jnp.concatenate([-jnp.sin(ang_r), jnp.sin(ang_r), -jnp.sin(ang_c), jnp.sin(ang_c)], axis=1)
    return jnp.tile(cos, (1, 2)), jnp.tile(sin, (1, 2))


def _layer(l, x, mod3, seq, lat, p):
    batch = x.shape[0] // seq
    lam_init = 0.8 - 0.6 * math.exp(-0.3 * l)
    ctx = lat is None
    proj = _qkv_proj(x, p["norm_mix"][l], mod3, p["w_in"][l], None if ctx else p["rope"], ctx,
                     tm=1024 if ctx else 2048)
    qkv, gates_raw = proj[:2]
    gates = gates_raw[:, :16] + jnp.concatenate([p["i_bias"][l].reshape(-1), p["f_bias"][l].reshape(-1)])[None]
    lam = (jnp.exp(jnp.sum(p["lq1"][l] * p["lk1"][l])) - jnp.exp(jnp.sum(p["lq2"][l] * p["lk2"][l]))
           + lam_init).reshape(1).astype(F32)
    sink = p["sink"][l]
    new = None
    if ctx:
        C0 = jnp.zeros((batch, 2, H_A, DH_A, DH_A), F32)
        n0 = jnp.zeros((batch, 2, H_A, DH_A), F32)
        m0 = jnp.zeros((batch, 2, H_A), F32)
        ya, Cs, ns, ms = _mlstm(qkv, gates, C0, n0, m0, batch, seq)
        yb, yc, yd = _ctx_attention(qkv, lam, sink, batch, seq)
        bk, bv, dk, dv, ckv = proj[2:]
        new = (bk.reshape(batch, seq, H_B, DH_B), bv.reshape(batch, seq, H_B, DH_B),
               ckv[:, :LANES].reshape(batch, seq, KV_C, DH_C), ckv[:, LANES:].reshape(batch, seq, KV_C, DH_C),
               dk.reshape(batch, seq, H_D, 2 * DH_D), dv.reshape(batch, seq, H_D, 2 * DH_D), Cs, ns, ms)
    else:
        nat_k, nat_v, swa_k, swa_v, diff_k, diff_v, C0, n0, m0 = lat
        past = nat_k.shape[1]
        flat = lambda a: a.reshape(batch, past, -1)
        ya, _, _, _ = _mlstm(qkv, gates, C0, n0, m0, batch, seq)
        yb = _nat_latent(qkv, flat(nat_k), flat(nat_v), p["rpb"][l], batch, seq)
        yc = _swa_latent(qkv, flat(swa_k), flat(swa_v), sink, batch, seq)
        yd = _diff_latent(qkv, flat(diff_k), flat(diff_v), lam, batch, seq)
    xn, h2, gate_w = _merge((ya, yb, yc, yd), qkv, x, mod3, p["a_norm"][l], p["subln"][l],
                            p["norm_ffn"][l], p["w_branch"][l], p["w_out"][l], p["w_route"][l], lam_init)
    y = _moe(h2, gate_w, xn, mod3, p["wg"][l], p["wu"][l], p["wd"][l], p["norm_final"],
             final_norm=(l == DEPTH - 1))
    return y, new


def kernel(x_prompt, x_sample, cache_nat_k, cache_nat_v, cache_swa_k, cache_swa_v, cache_diff_k, cache_diff_v, state_mlstm_C, state_mlstm_n, state_mlstm_m, c, c_ctx, norm_mix, norm_ffn, norm_final, w_mod, b_mod, w_in, mlstm_i_bias, mlstm_f_bias, mlstm_norm, nat_rpb, swa_sink, diff_lq1, diff_lk1, diff_lq2, diff_lk2, diff_subln, w_branch, w_out, router_group, router_expert, w_exp_gate, w_exp_up, w_exp_down):
    b_ctx, s_ctx, _ = x_prompt.shape
    b_lat, s_lat, _ = x_sample.shape
    cond8 = jnp.concatenate([c_ctx[None], c, jnp.zeros((8 - 1 - b_lat, D_MODEL), F32)], axis=0)
    mod = _modulation(cond8, w_mod, b_mod).reshape(DEPTH, 8, 6, D_MODEL)
    route_pad = jnp.zeros((DEPTH, D_MODEL, LANES - N_GROUPS - N_EXPERTS), F32)
    p = dict(
        norm_mix=norm_mix, norm_ffn=norm_ffn, norm_final=norm_final,
        w_in=[_reorder_w_in(w_in[l]) for l in range(DEPTH)],
        i_bias=mlstm_i_bias, f_bias=mlstm_f_bias, a_norm=mlstm_norm, rpb=nat_rpb, sink=swa_sink,
        lq1=diff_lq1, lk1=diff_lk1, lq2=diff_lq2, lk2=diff_lk2, subln=diff_subln,
        w_branch=w_branch.astype(BF16), w_out=w_out.astype(BF16),
        w_route=jnp.concatenate([router_group, router_expert, route_pad], axis=-1).astype(BF16),
        wg=w_exp_gate.astype(BF16), wu=w_exp_up.astype(BF16), wd=w_exp_down.astype(BF16),
        rope=_rope_tables(s_lat),
    )
    yp = x_prompt.reshape(b_ctx * s_ctx, D_MODEL)
    ys = x_sample.reshape(b_lat * s_lat, D_MODEL)
    news = []
    for l in range(DEPTH):
        yp, new = _layer(l, yp, mod[l, 0:1], s_ctx, None, p)
        news.append(new)
        lat = (cache_nat_k[:, l], cache_nat_v[:, l], cache_swa_k[:, l], cache_swa_v[:, l],
               cache_diff_k[:, l], cache_diff_v[:, l], state_mlstm_C[:, l], state_mlstm_n[:, l],
               state_mlstm_m[:, l])
        ys, _ = _layer(l, ys, mod[l, 1:1 + b_lat], s_lat, lat, p)
    stacked = tuple(jnp.stack([news[l][k] for l in range(DEPTH)], axis=1) for k in range(9))
    return (yp.reshape(b_ctx, s_ctx, D_MODEL), ys.reshape(b_lat, s_lat, D_MODEL)) + stacked
```

```python
import functools
import math

import numpy as np
import jax
import jax.numpy as jnp
from jax import lax
from jax.experimental import pallas as pl
from jax.experimental.pallas import tpu as pltpu

F32 = jnp.float32
BF16 = jnp.bfloat16

D_MODEL = 1024
DEPTH = 2
GRID_W = 64
BRANCH_W = 512
N_BRANCH = 4
H_A, DH_A, MLSTM_CHUNK = 4, 128, 128
H_B, DH_B = 8, 64
NA_ROWS, NA_COLS = 8, 16
H_C, KV_C, DH_C = 8, 2, 64
SWA_WINDOW = 128
H_D, DH_D = 4, 64
N_GROUPS, EXP_PER_GROUP = 4, 4
N_EXPERTS = N_GROUPS * EXP_PER_GROUP
D_EXPERT = 256
ROPE_BASE = 10000.0
EPS = 1e-6

LANES = 128
HALF = 64
COL_TILE = 512
N_COL_TILES = 20
N_PROJ = COL_TILE * N_COL_TILES
T_AQ, T_AK, T_AV, T_AO, T_BQ, T_BK, T_BV, T_CQ, T_DQ, T_DK, T_DV, T_MISC, T_GT = range(13)
MISC_GATE_OFF = 256
VMEM_LIMIT = 56 * 1024 * 1024
NEG_INF = float("-inf")

_NT = (((1,), (1,)), ((), ()))
_TN = (((0,), (0,)), ((), ()))


def _cparams(*sem):
    return pltpu.CompilerParams(dimension_semantics=sem, vmem_limit_bytes=VMEM_LIMIT)


def _dot(a, b):
    return jnp.dot(a, b, preferred_element_type=F32)


def _dot_nt(a, b):
    return lax.dot_general(a, b, _NT, preferred_element_type=F32)


def _dot_tn(a, b):
    return lax.dot_general(a, b, _TN, preferred_element_type=F32)


def _rms(x, g):
    return x * lax.rsqrt(jnp.mean(x * x, axis=-1, keepdims=True) + EPS) * g


def _sigmoid(x):
    return 0.5 * jnp.tanh(0.5 * x) + 0.5


def _mod_kernel(c_ref, w_ref, b_ref, o_ref):
    c = c_ref[...]
    s = (c * _sigmoid(c)).astype(BF16)
    o_ref[0] = _dot(s, w_ref[0].astype(BF16)) + b_ref[0]


def _modulation(cond8, w_mod, b_mod):
    tn = 1536
    n_out = 6 * D_MODEL
    return pl.pallas_call(
        _mod_kernel,
        out_shape=jax.ShapeDtypeStruct((DEPTH, 8, n_out), F32),
        grid=(DEPTH, n_out // tn),
        in_specs=[
            pl.BlockSpec((8, D_MODEL), lambda l, j: (0, 0)),
            pl.BlockSpec((1, D_MODEL, tn), lambda l, j: (l, 0, j)),
            pl.BlockSpec((1, 1, tn), lambda l, j: (l, 0, j)),
        ],
        out_specs=pl.BlockSpec((1, 8, tn), lambda l, j: (l, 0, j)),
        compiler_params=_cparams("parallel", "parallel"),
        name="modulation",
    )(cond8, w_mod, b_mod.reshape(DEPTH, 1, n_out))


CACHE_TILES = (T_BK, T_BV, T_DK, T_DV)
QKV_ROW_CHUNK = 512


def _rope128(x, cos, sin):
    lane = lax.broadcasted_iota(jnp.int32, x.shape, 1)
    partner = jnp.where((lane % 32) < 16, pltpu.roll(x, LANES - 16, 1), pltpu.roll(x, 16, 1))
    return x * cos + partner * sin


def _qkv_kernel(x_ref, nw_ref, mod_ref, w_ref, *rest, rope, cache):
    rest = list(rest)
    if rope:
        cos_ref, sin_ref = rest[:2]
        rest = rest[2:]
    o_ref, g_ref = rest[:2]
    cache_refs = rest[2:-1]
    h_scr = rest[-1]
    j = pl.program_id(1)

    @pl.when(j == 0)
    def _():
        h = _rms(x_ref[...], nw_ref[...]) * (1.0 + mod_ref[0, 1:2, :]) + mod_ref[0, 0:1, :]
        h_scr[...] = h.astype(BF16)

    def tile_kind(cond, store):
        @pl.when(cond)
        def _():
            for r in range(h_scr.shape[0] // QKV_ROW_CHUNK):
                rows = slice(r * QKV_ROW_CHUNK, (r + 1) * QKV_ROW_CHUNK)
                store(rows, _dot(h_scr[rows, :], w_ref[...]))

    def roped(rows, acc, c):
        sl = slice(c * LANES, (c + 1) * LANES)
        return _rope128(acc[:, sl], cos_ref[rows, :], sin_ref[rows, :]).astype(BF16)

    def store_plain(rows, acc):
        o_ref[rows, :] = acc.astype(BF16)

    def store_ak(rows, acc):
        o_ref[rows, :] = (acc * (DH_A ** -0.5)).astype(BF16)

    def store_misc(rows, acc):
        o_ref[rows, :] = acc.astype(BF16)
        if rope:
            o_ref[rows, :LANES] = roped(rows, acc, 0)
        g_ref[rows, :] = acc[:, MISC_GATE_OFF:MISC_GATE_OFF + LANES]
        if cache:
            cache_refs[-1][rows, :] = acc[:, :2 * LANES]

    def store_rope(rows, acc):
        for c in range(COL_TILE // LANES):
            o_ref[rows, c * LANES:(c + 1) * LANES] = roped(rows, acc, c)

    special = (j == T_AK) | (j == T_MISC)
    tile_kind(j == T_AK, store_ak)
    tile_kind(j == T_MISC, store_misc)
    if rope:
        is_rope = (j == T_CQ) | (j == T_DQ) | (j == T_DK)
        special = special | is_rope
        tile_kind(is_rope, store_rope)
    if cache:
        for t, ref in zip(CACHE_TILES, cache_refs[:-1]):
            def store_cached(rows, acc, ref=ref):
                o_ref[rows, :] = acc.astype(BF16)
                ref[rows, :] = acc
            special = special | (j == t)
            tile_kind(j == t, store_cached)
    tile_kind(jnp.logical_not(special), store_plain)


def _qkv_proj(x, norm_w, mod3, w_in_r, rope_tabs, cache, tm):
    n = x.shape[0]
    tiles_per_mod = n // mod3.shape[0] // tm
    rope = rope_tabs is not None
    in_specs = [
        pl.BlockSpec((tm, D_MODEL), lambda i, j: (i, 0)),
        pl.BlockSpec((1, D_MODEL), lambda i, j: (0, 0)),
        pl.BlockSpec((1, 6, D_MODEL), lambda i, j: (i // tiles_per_mod, 0, 0)),
        pl.BlockSpec((D_MODEL, COL_TILE), lambda i, j: (0, j)),
    ]
    args = [x, norm_w.reshape(1, D_MODEL), mod3, w_in_r]
    if rope:
        tiles_per_seq = rope_tabs[0].shape[0] // tm
        in_specs += [pl.BlockSpec((tm, LANES), lambda i, j: (i % tiles_per_seq, 0))] * 2
        args += list(rope_tabs)
    row = lambda w: pl.BlockSpec((tm, w), lambda i, j: (i, 0))
    out_shape = [jax.ShapeDtypeStruct((n, N_PROJ), BF16), jax.ShapeDtypeStruct((n, LANES), F32)]
    out_specs = [pl.BlockSpec((tm, COL_TILE), lambda i, j: (i, j)), row(LANES)]
    if cache:
        out_shape += [jax.ShapeDtypeStruct((n, COL_TILE), F32)] * len(CACHE_TILES)
        out_shape += [jax.ShapeDtypeStruct((n, 2 * LANES), F32)]
        out_specs += [row(COL_TILE)] * len(CACHE_TILES) + [row(2 * LANES)]
    return pl.pallas_call(
        functools.partial(_qkv_kernel, rope=rope, cache=cache),
        out_shape=tuple(out_shape),
        grid=(n // tm, N_COL_TILES),
        in_specs=in_specs,
        out_specs=tuple(out_specs),
        scratch_shapes=[pltpu.VMEM((tm, D_MODEL), BF16)],
        compiler_params=_cparams("parallel", "arbitrary"),
        name="qkv_rope" if rope else "qkv",
    )(*args)


def _log_sigmoid(x):
    return jnp.minimum(x, 0.0) - jnp.log1p(jnp.exp(-jnp.abs(x)))


def _mlstm_chunk(c, h, d, state, q_ref, k_ref, v_ref, gr_ref, gc_ref, out_ref):
    L = MLSTM_CHUNK
    C, n, m = state
    row = lax.broadcasted_iota(jnp.int32, (L, L), 0)
    col = lax.broadcasted_iota(jnp.int32, (L, L), 1)
    tri = (col <= row) if d == 0 else (col >= row)
    tri_t = (row <= col) if d == 0 else (row >= col)
    off = pl.multiple_of(c * L, L)
    hs = slice(h * DH_A, (h + 1) * DH_A)
    qb = q_ref[pl.ds(off, L), hs]
    kb = k_ref[pl.ds(off, L), hs]
    vb = v_ref[pl.ds(off, L), hs]
    qx, kx, vx = qb.astype(F32), kb.astype(F32), vb.astype(F32)
    g_rows = gr_ref[h, c]
    g_cols = gc_ref[h, c]
    ig_r = g_rows[2 * d:2 * d + 1, :]
    lf_r = _log_sigmoid(g_rows[2 * d + 1:2 * d + 2, :])
    ig_c = g_cols[:, 2 * d:2 * d + 1]
    lf_c = _log_sigmoid(g_cols[:, 2 * d + 1:2 * d + 2])
    b_c = jnp.sum(jnp.where(tri, lf_r, 0.0), axis=1, keepdims=True)
    b_r = jnp.sum(jnp.where(tri_t, lf_c, 0.0), axis=0, keepdims=True)
    qk = _dot_nt(qb, kb)
    qc = _dot_nt(qb, C.astype(BF16))
    yield
    logd = jnp.where(tri, b_c - b_r + ig_r, NEG_INF)
    inter = b_c + m
    m_t = jnp.maximum(inter, jnp.max(logd, axis=1, keepdims=True))
    yield
    w = jnp.exp(logd - m_t)
    a = jnp.exp(inter - m_t)
    s = qk * w
    num = a * qc + _dot(s.astype(BF16), vb)
    den = a * jnp.sum(qx * n, axis=1, keepdims=True) + jnp.sum(s, axis=1, keepdims=True)
    yield
    out_ref[pl.ds(off, L), hs] = num / jnp.maximum(jnp.abs(den), jnp.exp(-m_t))
    b_tot = jnp.sum(lf_r, axis=1, keepdims=True)
    g_c = b_tot - b_c + ig_c
    m_new = jnp.maximum(b_tot + m, jnp.max(g_c, axis=0, keepdims=True))
    yield
    decay = jnp.exp(b_tot + m - m_new)
    wk = jnp.exp(g_c - m_new)
    C_new = decay * C + _dot_tn((wk * vx).astype(BF16), kb)
    n_new = decay * n + jnp.sum(wk * kx, axis=0, keepdims=True)
    return C_new, n_new, m_new


def _run_staged(gens):
    results = [None] * len(gens)
    live = list(range(len(gens)))
    while live:
        for i in list(live):
            try:
                next(gens[i])
            except StopIteration as stop:
                results[i] = stop.value
                live.remove(i)
    return results


def _mlstm_kernel(q_ref, k_ref, v_ref, gr_ref, gc_ref, c0_ref, n0_ref, m0_ref,
                  h_ref, cs_ref, ns_ref, ms_ref, hb_scr, *, nc):
    chains = [(h, d) for h in range(H_A) for d in range(2)]

    def body(ci, states):
        gens = [_mlstm_chunk(ci if d == 0 else nc - 1 - ci, h, d, state, q_ref, k_ref, v_ref, gr_ref, gc_ref,
                             h_ref if d == 0 else hb_scr) for (h, d), state in zip(chains, states)]
        return tuple(_run_staged(gens))

    init = tuple((c0_ref[0, d, h], n0_ref[0, h, d:d + 1, :], m0_ref[0, h, d:d + 1, 0:1]) for h, d in chains)
    final = lax.fori_loop(0, nc, body, init)
    for (h, d), (C, n, m) in zip(chains, final):
        cs_ref[0, d, h] = C
        ns_ref[0, h, d:d + 1, :] = n
        ms_ref[0, h, d:d + 1, :] = jnp.broadcast_to(m, (1, LANES))
    h_ref[...] += hb_scr[...]


def _mlstm(qkv, gates, C0, n0, m0, batch, seq):
    nc = seq // MLSTM_CHUNK
    L = MLSTM_CHUNK
    ig = gates[:, :8].reshape(batch, nc, L, 2, H_A)
    fg = gates[:, 8:].reshape(batch, nc, L, 2, H_A)
    g4 = jnp.stack([ig[..., 0, :], fg[..., 0, :], ig[..., 1, :], fg[..., 1, :]], axis=-1)
    g_cols = jnp.transpose(g4, (3, 0, 1, 2, 4)).reshape(H_A, batch * nc, L, 4)
    g_rows = jnp.transpose(g4, (3, 0, 1, 4, 2)).reshape(H_A, batch * nc, 4, L)
    n0t = jnp.transpose(n0, (0, 2, 1, 3))
    m0t = jnp.broadcast_to(jnp.transpose(m0, (0, 2, 1))[..., None], (batch, H_A, 2, LANES))
    qkv_spec = lambda tile: pl.BlockSpec((seq, COL_TILE), lambda b: (b, tile))
    c_spec = pl.BlockSpec((1, 2, H_A, DH_A, DH_A), lambda b: (b, 0, 0, 0, 0))
    state_spec = pl.BlockSpec((1, H_A, 2, LANES), lambda b: (b, 0, 0, 0))
    h, Cs, ns, ms = pl.pallas_call(
        functools.partial(_mlstm_kernel, nc=nc),
        out_shape=(
            jax.ShapeDtypeStruct((batch * seq, H_A * DH_A), F32),
            jax.ShapeDtypeStruct((batch, 2, H_A, DH_A, DH_A), F32),
            jax.ShapeDtypeStruct((batch, H_A, 2, DH_A), F32),
            jax.ShapeDtypeStruct((batch, H_A, 2, LANES), F32),
        ),
        grid=(batch,),
        in_specs=[
            qkv_spec(T_AQ), qkv_spec(T_AK), qkv_spec(T_AV),
            pl.BlockSpec((H_A, nc, 4, L), lambda b: (0, b, 0, 0)),
            pl.BlockSpec((H_A, nc, L, 4), lambda b: (0, b, 0, 0)),
            c_spec, state_spec, state_spec,
        ],
        out_specs=(pl.BlockSpec((seq, H_A * DH_A), lambda b: (b, 0)), c_spec, state_spec, state_spec),
        scratch_shapes=[pltpu.VMEM((seq, H_A * DH_A), F32)],
        compiler_params=_cparams("parallel"),
        name="mlstm",
    )(qkv, qkv, qkv, g_rows, g_cols, C0, n0t, m0t)
    return h, Cs, jnp.transpose(ns, (0, 2, 1, 3)), jnp.transpose(ms[..., 0], (0, 2, 1))


ATTN_SCALE = DH_B ** -0.5


def _half_mask(shape, half):
    lane = lax.broadcasted_iota(jnp.int32, shape, 1)
    return (lane >= HALF) if half else (lane < HALF)


def _select_half(q, half):
    return jnp.where(_half_mask(q.shape, half), q, jnp.zeros_like(q))


def _swap_halves(x):
    return pltpu.roll(x.astype(F32), HALF, 1).astype(x.dtype)


def _softmax_parts(scores, extra=None):
    m = functools.reduce(jnp.maximum, [jnp.max(s, axis=1, keepdims=True) for s in scores])
    if extra is not None:
        m = jnp.maximum(m, extra)
    es = [jnp.exp(s - m) for s in scores]
    l = functools.reduce(lambda x, y: x + y, [jnp.sum(e, axis=1, keepdims=True) for e in es])
    if extra is not None:
        l = l + jnp.exp(extra - m)
    return es, l


def _pv(es, vs):
    return functools.reduce(lambda x, y: x + y, [_dot(e.astype(BF16), v) for e, v in zip(es, vs)])


def _diff_head(q, ks, vs, lam):
    parts = [_softmax_parts([_dot_nt(_select_half(q, comp), k) for k in ks]) for comp in range(2)]
    (e0, l0), (e1, l1) = parts
    coef = lam * l0 / l1
    pd = [a - coef * b for a, b in zip(e0, e1)]
    return _pv(pd, vs) / l0


def _ctx_attn_kernel(lam_ref, sink_ref, bq, bk, bv, cq, misc, dq, dk, dv, yb_ref, yc_ref, yd_ref):
    for p in range(H_B // 2):
        sl = slice(p * LANES, (p + 1) * LANES)
        q = bq[:, sl] * ATTN_SCALE
        outs = []
        for half in range(2):
            es, l = _softmax_parts([_dot_nt(_select_half(q, half), bk[:, sl])])
            outs.append(_pv(es, [bv[:, sl]]) / l)
        yb_ref[:, sl] = jnp.where(_half_mask(q.shape, 0), outs[0], outs[1]).astype(yb_ref.dtype)

    kc = misc[:, 0:LANES]
    vc = misc[:, LANES:2 * LANES]
    for p in range(H_C // 2):
        sl = slice(p * LANES, (p + 1) * LANES)
        q = cq[:, sl] * ATTN_SCALE
        outs = []
        for half in range(2):
            head = 2 * p + half
            kv = head // (H_C // KV_C)
            qm = _select_half(q, half)
            if kv != half:
                qm = _swap_halves(qm)
            es, l = _softmax_parts([_dot_nt(qm, kc)], extra=sink_ref[head])
            o = _pv(es, [vc]) / l
            outs.append(o if kv == half else _swap_halves(o))
        yc_ref[:, sl] = jnp.where(_half_mask(q.shape, 0), outs[0], outs[1]).astype(yc_ref.dtype)

    lam = lam_ref[0]
    for h in range(H_D):
        sl = slice(h * LANES, (h + 1) * LANES)
        yd_ref[:, sl] = _diff_head(dq[:, sl] * ATTN_SCALE, [dk[:, sl]], [dv[:, sl]], lam)


def _ctx_attention(qkv, lam, sink, batch, seq):
    tile = lambda t: pl.BlockSpec((seq, COL_TILE), lambda b: (b, t))
    smem = pl.BlockSpec(memory_space=pltpu.SMEM)
    out = lambda dt: jax.ShapeDtypeStruct((batch * seq, BRANCH_W), dt)
    out_spec = pl.BlockSpec((seq, BRANCH_W), lambda b: (b, 0))
    return pl.pallas_call(
        _ctx_attn_kernel,
        out_shape=(out(BF16), out(BF16), out(F32)),
        grid=(batch,),
        in_specs=[smem, smem, tile(T_BQ), tile(T_BK), tile(T_BV), tile(T_CQ), tile(T_MISC),
                  tile(T_DQ), tile(T_DK), tile(T_DV)],
        out_specs=(out_spec, out_spec, out_spec),
        compiler_params=_cparams("parallel"),
        name="ctx_attention",
    )(lam, sink, *([qkv] * 8))


NAT_QROWS = 4
NAT_KROWS = 12
NAT_RI = 2 * NA_ROWS - 1


def _nat_kernel(q_ref, k_ref, v_ref, kc_ref, vc_ref, cb_ref, o_ref):
    rows = k_ref.shape[0] // GRID_W
    r0 = pl.program_id(2) * NAT_QROWS
    kb = jnp.clip(r0 - NA_ROWS // 2, 0, rows - NAT_KROWS)
    koff = pl.multiple_of(kb * GRID_W, GRID_W)
    nq = NAT_QROWS * GRID_W
    nk = NAT_KROWS * GRID_W
    q = q_ref[...] * ATTN_SCALE
    k = k_ref[pl.ds(koff, nk), :]
    v = v_ref[pl.ds(koff, nk), :]
    kc = kc_ref[0].astype(BF16)
    vc = vc_ref[0].astype(BF16)
    left = _half_mask((GRID_W, LANES), 0)
    outs = []
    for half in range(2):
        bias_rows = []
        for a in range(NAT_QROWS):
            r = r0 + a
            rs = jnp.clip(r - NA_ROWS // 2, 0, rows - NA_ROWS)
            blocks = []
            for ip in range(NAT_KROWS // 2):
                sides = []
                for side in range(2):
                    kr = kb + 2 * ip + side
                    ok = (kr >= rs) & (kr < rs + NA_ROWS)
                    ri = jnp.clip(kr - r + NA_ROWS - 1, 0, NAT_RI - 1)
                    sides.append(jnp.where(ok, cb_ref[half, ri], NEG_INF))
                blocks.append(jnp.where(left, sides[0], sides[1]))
            bias_rows.append(jnp.concatenate(blocks, axis=1))
        bias = jnp.concatenate(bias_rows, axis=0)
        qm = _select_half(q, half)
        s_nb = _dot_nt(qm, k) + bias
        s_ctx = _dot_nt(qm, kc)
        es, l = _softmax_parts([s_nb, s_ctx])
        outs.append(_pv(es, [v, vc]) / l)
    o_ref[...] = jnp.where(_half_mask((nq, LANES), 0), outs[0], outs[1]).astype(o_ref.dtype)


def _nat_col_table(rpb):
    qc = np.arange(GRID_W)[:, None]
    kc = np.arange(GRID_W)[None, :]
    cs = np.clip(qc - NA_COLS // 2, 0, GRID_W - NA_COLS)
    valid = (kc >= cs) & (kc < cs + NA_COLS)
    n_ci = 2 * NA_COLS - 1
    lo = GRID_W - 1 - (NA_COLS - 1)
    v = jnp.pad(rpb, ((0, 0), (0, 0), (lo, 2 * GRID_W - n_ci - lo)))
    flat = jnp.tile(v, (1, 1, GRID_W))[..., :GRID_W * (2 * GRID_W - 1)]
    toep = flat.reshape(rpb.shape[0], rpb.shape[1], GRID_W, 2 * GRID_W - 1)[..., GRID_W - 1:]
    tab = jnp.where(valid[None, None], toep, NEG_INF)
    return jnp.concatenate([tab, tab], axis=-1)


def _nat_latent(qkv, kc, vc, rpb, batch, seq):
    nq = NAT_QROWS * GRID_W
    tpb = seq // nq
    past = kc.shape[1]
    cb = _nat_col_table(rpb)
    return pl.pallas_call(
        _nat_kernel,
        out_shape=jax.ShapeDtypeStruct((batch * seq, BRANCH_W), BF16),
        grid=(batch, H_B // 2, tpb),
        in_specs=[
            pl.BlockSpec((nq, LANES), lambda b, p, i: (b * tpb + i, T_BQ * 4 + p)),
            pl.BlockSpec((seq, LANES), lambda b, p, i: (b, T_BK * 4 + p)),
            pl.BlockSpec((seq, LANES), lambda b, p, i: (b, T_BV * 4 + p)),
            pl.BlockSpec((1, past, LANES), lambda b, p, i: (b, 0, p)),
            pl.BlockSpec((1, past, LANES), lambda b, p, i: (b, 0, p)),
            pl.BlockSpec((2, NAT_RI, GRID_W, LANES), lambda b, p, i: (p, 0, 0, 0)),
        ],
        out_specs=pl.BlockSpec((nq, LANES), lambda b, p, i: (b * tpb + i, p)),
        compiler_params=_cparams("parallel", "parallel", "parallel"),
        name="nat_latent",
    )(qkv, qkv, qkv, kc, vc, cb)


SWA_QBLK = 256
SWA_KBLK = SWA_QBLK + 2 * SWA_WINDOW


def _swa_kernel(sink_ref, q_ref, kv_ref, kc_ref, vc_ref, o_ref):
    seq = kv_ref.shape[0]
    p = pl.program_id(1)
    t0 = pl.program_id(2) * SWA_QBLK
    k0 = pl.multiple_of(jnp.clip(t0 - SWA_WINDOW, 0, seq - SWA_KBLK), SWA_WINDOW)
    q = q_ref[...] * ATTN_SCALE
    k = kv_ref[pl.ds(k0, SWA_KBLK), 0:LANES]
    v = kv_ref[pl.ds(k0, SWA_KBLK), LANES:2 * LANES]
    kc = kc_ref[0].astype(BF16)
    vc = vc_ref[0].astype(BF16)
    tq = t0 + lax.broadcasted_iota(jnp.int32, (SWA_QBLK, SWA_KBLK), 0)
    tk = k0 + lax.broadcasted_iota(jnp.int32, (SWA_QBLK, SWA_KBLK), 1)
    inside = jnp.abs(tq - tk) <= SWA_WINDOW
    kv_half = p // ((H_C // 2) // KV_C)
    outs = []
    for half in range(2):
        aligned = kv_half == half
        qm = _select_half(q, half)
        qm = jnp.where(aligned, qm, _swap_halves(qm))
        s_band = jnp.where(inside, _dot_nt(qm, k), NEG_INF)
        s_ctx = _dot_nt(qm, kc)
        es, l = _softmax_parts([s_band, s_ctx], extra=sink_ref[2 * p + half])
        o = _pv(es, [v, vc]) / l
        outs.append(jnp.where(aligned, o, _swap_halves(o)))
    o_ref[...] = jnp.where(_half_mask(q.shape, 0), outs[0], outs[1]).astype(o_ref.dtype)


def _swa_latent(qkv, kc, vc, sink, batch, seq):
    tpb = seq // SWA_QBLK
    past = kc.shape[1]
    return pl.pallas_call(
        _swa_kernel,
        out_shape=jax.ShapeDtypeStruct((batch * seq, BRANCH_W), BF16),
        grid=(batch, H_C // 2, tpb),
        in_specs=[
            pl.BlockSpec(memory_space=pltpu.SMEM),
            pl.BlockSpec((SWA_QBLK, LANES), lambda b, p, i: (b * tpb + i, T_CQ * 4 + p)),
            pl.BlockSpec((seq, COL_TILE), lambda b, p, i: (b, T_MISC)),
            pl.BlockSpec((1, past, LANES), lambda b, p, i: (b, 0, 0)),
            pl.BlockSpec((1, past, LANES), lambda b, p, i: (b, 0, 0)),
        ],
        out_specs=pl.BlockSpec((SWA_QBLK, LANES), lambda b, p, i: (b * tpb + i, p)),
        compiler_params=_cparams("parallel", "parallel", "parallel"),
        name="swa_latent",
    )(sink, qkv, qkv, kc, vc)


DIFF_QBLK = 256


def _diff_kernel(lam_ref, q_ref, k_ref, v_ref, kc_ref, vc_ref, o_ref):
    q = q_ref[...] * ATTN_SCALE
    ks = [k_ref[...], kc_ref[0].astype(BF16)]
    vs = [v_ref[...], vc_ref[0].astype(BF16)]
    o_ref[...] = _diff_head(q, ks, vs, lam_ref[0])


def _diff_latent(qkv, kc, vc, lam, batch, seq):
    tpb = seq // DIFF_QBLK
    past = kc.shape[1]
    return pl.pallas_call(
        _diff_kernel,
        out_shape=jax.ShapeDtypeStruct((batch * seq, BRANCH_W), F32),
        grid=(batch, H_D, tpb),
        in_specs=[
            pl.BlockSpec(memory_space=pltpu.SMEM),
            pl.BlockSpec((DIFF_QBLK, LANES), lambda b, h, i: (b * tpb + i, T_DQ * 4 + h)),
            pl.BlockSpec((seq, LANES), lambda b, h, i: (b, T_DK * 4 + h)),
            pl.BlockSpec((seq, LANES), lambda b, h, i: (b, T_DV * 4 + h)),
            pl.BlockSpec((1, past, LANES), lambda b, h, i: (b, 0, h)),
            pl.BlockSpec((1, past, LANES), lambda b, h, i: (b, 0, h)),
        ],
        out_specs=pl.BlockSpec((DIFF_QBLK, LANES), lambda b, h, i: (b * tpb + i, h)),
        compiler_params=_cparams("parallel", "parallel", "parallel"),
        name="diff_latent",
    )(lam, qkv, qkv, qkv, kc, vc)


def _rms_heads(y, g_ref, per_head_gain):
    outs = []
    for c in range(BRANCH_W // LANES):
        sl = slice(c * LANES, (c + 1) * LANES)
        g = g_ref[:, sl] if per_head_gain else g_ref[...]
        outs.append(_rms(y[:, sl], g))
    return jnp.concatenate(outs, axis=1)


def _first_lane(hit, lane_f):
    return jnp.min(jnp.where(hit, lane_f, float(LANES)), axis=1, keepdims=True)


def _top_group(logits):
    lane = lax.broadcasted_iota(jnp.int32, logits.shape, 1)
    gl = jnp.where(lane < N_GROUPS, logits, NEG_INF)
    return _first_lane(gl == jnp.max(gl, axis=1, keepdims=True), lane.astype(F32))


def _gate_weights(logits, group):
    lane = lax.broadcasted_iota(jnp.int32, logits.shape, 1)
    lane_f = lane.astype(F32)
    gl = jnp.where(lane < N_GROUPS, logits, NEG_INF)
    gmax = jnp.max(gl, axis=1, keepdims=True)
    g_logit = jnp.sum(jnp.where(lane_f == group, logits, 0.0), axis=1, keepdims=True)
    gw = jnp.exp(g_logit - gmax) / jnp.sum(jnp.exp(gl - gmax), axis=1, keepdims=True)
    e_lane = lane - N_GROUPS
    e_group = lax.shift_right_arithmetic(e_lane, EXP_PER_GROUP.bit_length() - 1).astype(F32)
    in_group = (e_lane >= 0) & (e_lane < N_EXPERTS) & (e_group == group)
    el = jnp.where(in_group, logits, NEG_INF)
    e1 = jnp.max(el, axis=1, keepdims=True)
    i1 = _first_lane(el == e1, lane_f)
    el2 = jnp.where(lane_f == i1, NEG_INF, el)
    e2 = jnp.max(el2, axis=1, keepdims=True)
    i2 = _first_lane(el2 == e2, lane_f)
    t = jnp.exp(e2 - e1)
    w1 = gw / (1.0 + t)
    return jnp.where(lane_f == i1, w1, jnp.where(lane_f == i2, w1 * t, 0.0))


def _merge_kernel(ya, yb, yc, yd, ao, gt0, gt1, gt2, gt3, x_ref, mod_ref, anorm, subln, nffn,
                  wbr, wout, wr, xo_ref, h2_ref, group_ref, *, yd_scale):
    a = _sigmoid(ao[...].astype(F32)) * _rms_heads(ya[...], anorm, True)
    d = _rms_heads(yd[...], subln, False) * yd_scale
    branches = (a, yb[...], yc[...], d)
    mix = None
    for n, (br, gt) in enumerate(zip(branches, (gt0, gt1, gt2, gt3))):
        term = _sigmoid(gt[...].astype(F32)) * _dot(br.astype(BF16), wbr[n])
        mix = term if mix is None else mix + term
    out = _dot(mix.astype(BF16), wout[...])
    xn = x_ref[...] + mod_ref[0, 2:3, :] * out
    xo_ref[...] = xn
    h2 = _rms(xn, nffn[...]) * (1.0 + mod_ref[0, 4:5, :]) + mod_ref[0, 3:4, :]
    h2_ref[...] = h2
    group_ref[...] = jnp.broadcast_to(_top_group(_dot(h2.astype(BF16), wr[...])), group_ref.shape)


def _merge(ys, qkv, x, mod3, a_norm, subln, norm_ffn, w_branch, w_out, w_route, lam_init):
    n = x.shape[0]
    tm = 512
    tiles_per_mod = n // mod3.shape[0] // tm
    y_spec = pl.BlockSpec((tm, BRANCH_W), lambda i: (i, 0))
    gt_spec = lambda k: pl.BlockSpec((tm, D_MODEL), lambda i: (i, T_GT * COL_TILE // D_MODEL + k))
    full = lambda shape: pl.BlockSpec(shape, lambda i: (0,) * len(shape))
    x_spec = pl.BlockSpec((tm, D_MODEL), lambda i: (i, 0))
    return pl.pallas_call(
        functools.partial(_merge_kernel, yd_scale=1.0 - lam_init),
        out_shape=(
            jax.ShapeDtypeStruct((n, D_MODEL), F32),
            jax.ShapeDtypeStruct((n, D_MODEL), F32),
            jax.ShapeDtypeStruct((n, LANES), F32),
        ),
        grid=(n // tm,),
        in_specs=[
            y_spec, y_spec, y_spec, y_spec,
            pl.BlockSpec((tm, COL_TILE), lambda i: (i, T_AO)),
            gt_spec(0), gt_spec(1), gt_spec(2), gt_spec(3),
            x_spec,
            pl.BlockSpec((1, 6, D_MODEL), lambda i: (i // tiles_per_mod, 0, 0)),
            full((1, BRANCH_W)), full((1, LANES)), full((1, D_MODEL)),
            full((N_BRANCH, BRANCH_W, D_MODEL)), full((D_MODEL, D_MODEL)), full((D_MODEL, LANES)),
        ],
        out_specs=(x_spec, x_spec, pl.BlockSpec((tm, LANES), lambda i: (i, 0))),
        compiler_params=_cparams("parallel"),
        name="merge_route",
    )(*ys, qkv, qkv, qkv, qkv, qkv, x, mod3, a_norm.reshape(1, BRANCH_W), subln.reshape(1, LANES),
      norm_ffn.reshape(1, D_MODEL), w_branch, w_out, w_route)


MOE_TILE = 512


def _row_copies(src_row, dst_row, sem, n):
    def copy(i):
        return pltpu.make_async_copy(src_row(i), dst_row(i), sem)

    def start(i, c):
        copy(i).start()
        return c

    def wait(i, c):
        copy(i).wait()
        return c

    lax.fori_loop(0, n, start, 0)
    lax.fori_loop(0, n, wait, 0)


def _moe_scatter_kernel(slot_ref, h_ref, init_ref, sorted_ref, sem):
    del init_ref
    base = pl.program_id(0) * MOE_TILE
    _row_copies(lambda i: h_ref.at[pl.ds(i, 1), :],
                lambda i: sorted_ref.at[pl.ds(slot_ref[base + i], 1), :], sem, MOE_TILE)


def _moe_group_kernel(tile_group_ref, n_used_ref, h_ref, wr_ref, wg_ref, wu_ref, wd_ref, o_ref):
    t = pl.program_id(0)

    @pl.when(t < n_used_ref[0])
    def _():
        group = tile_group_ref[t]
        h = h_ref[...].astype(BF16)
        gates = _gate_weights(_dot(h, wr_ref[...]), group.astype(F32))
        lane = lax.broadcasted_iota(jnp.int32, gates.shape, 1)
        acc = None
        for e in range(EXP_PER_GROUP):
            g = jnp.sum(jnp.where(lane == N_GROUPS + group * EXP_PER_GROUP + e, gates, 0.0), axis=1, keepdims=True)
            hg = _dot(h, wg_ref[0, e])
            hid = hg * _sigmoid(hg) * _dot(h, wu_ref[0, e]) * g
            term = _dot(hid.astype(BF16), wd_ref[0, e])
            acc = term if acc is None else acc + term
        o_ref[...] = acc

    @pl.when(t >= n_used_ref[0])
    def _():
        o_ref[...] = jnp.zeros_like(o_ref)


def _moe_gather_kernel(slot_ref, y_ref, x_ref, mod_ref, nf_ref, o_ref, buf, sem, *, final_norm):
    t = pl.program_id(0)
    nt = pl.num_programs(0)

    def copy(tile, b, i):
        return pltpu.make_async_copy(y_ref.at[pl.ds(slot_ref[tile * MOE_TILE + i], 1), :],
                                     buf.at[b, pl.ds(i, 1), :], sem.at[b])

    def start_tile(tile, b):
        def start(i, c):
            copy(tile, b, i).start()
            return c

        lax.fori_loop(0, MOE_TILE, start, 0)

    cur = t % 2

    @pl.when(t == 0)
    def _():
        start_tile(0, 0)

    @pl.when(t + 1 < nt)
    def _():
        start_tile(t + 1, 1 - cur)

    def wait(i, c):
        copy(t, cur, i).wait()
        return c

    lax.fori_loop(0, MOE_TILE, wait, 0)
    y = x_ref[...] + mod_ref[0, 5:6, :] * buf[cur]
    o_ref[...] = _rms(y, nf_ref[...]) if final_norm else y


def _moe(h2, group, x, mod3, w_route, wg, wu, wd, norm_final, final_norm):
    n = x.shape[0]
    tm = MOE_TILE
    n_tiles = n // tm + N_GROUPS
    onehot = (group[:, None] == jnp.arange(N_GROUPS, dtype=jnp.int32)[None]).astype(jnp.int32)
    csum = jnp.cumsum(onehot, axis=0)
    counts = csum[-1]
    rank = jnp.sum((csum - onehot) * onehot, axis=1)
    group_tiles = (counts + tm - 1) // tm
    tile_end = jnp.cumsum(group_tiles)
    group_start = (tile_end - group_tiles) * tm
    slot = (jnp.sum(onehot * group_start[None], axis=1) + rank).astype(jnp.int32)
    tile_group = jnp.minimum(jnp.sum(jnp.arange(n_tiles, dtype=jnp.int32)[:, None] >= tile_end[None], axis=1),
                             N_GROUPS - 1).astype(jnp.int32)
    n_used = tile_end[-1:].astype(jnp.int32)

    sorted_h = pl.pallas_call(
        _moe_scatter_kernel,
        out_shape=jax.ShapeDtypeStruct((n_tiles * tm, D_MODEL), F32),
        grid_spec=pltpu.PrefetchScalarGridSpec(
            num_scalar_prefetch=1, grid=(n // tm,),
            in_specs=[pl.BlockSpec((tm, D_MODEL), lambda i, slot: (i, 0)), pl.BlockSpec(memory_space=pl.ANY)],
            out_specs=pl.BlockSpec(memory_space=pl.ANY),
            scratch_shapes=[pltpu.SemaphoreType.DMA],
        ),
        input_output_aliases={2: 0},
        compiler_params=_cparams("arbitrary"),
        name="moe_scatter",
    )(slot, h2, jnp.zeros((n_tiles * tm, D_MODEL), F32))

    grp = lambda shape: pl.BlockSpec((1,) + shape, lambda t, tg, nu: (tg[t], 0, 0, 0))
    sorted_y = pl.pallas_call(
        _moe_group_kernel,
        out_shape=jax.ShapeDtypeStruct((n_tiles * tm, D_MODEL), F32),
        grid_spec=pltpu.PrefetchScalarGridSpec(
            num_scalar_prefetch=2, grid=(n_tiles,),
            in_specs=[
                pl.BlockSpec((tm, D_MODEL), lambda t, tg, nu: (t, 0)),
                pl.BlockSpec((D_MODEL, LANES), lambda t, tg, nu: (0, 0)),
                grp((EXP_PER_GROUP, D_MODEL, D_EXPERT)), grp((EXP_PER_GROUP, D_MODEL, D_EXPERT)),
                grp((EXP_PER_GROUP, D_EXPERT, D_MODEL)),
            ],
            out_specs=pl.BlockSpec((tm, D_MODEL), lambda t, tg, nu: (t, 0)),
        ),
        compiler_params=_cparams("arbitrary"),
        name="moe_experts",
    )(tile_group, n_used, sorted_h, w_route,
      wg.reshape(N_GROUPS, EXP_PER_GROUP, D_MODEL, D_EXPERT), wu.reshape(N_GROUPS, EXP_PER_GROUP, D_MODEL, D_EXPERT),
      wd.reshape(N_GROUPS, EXP_PER_GROUP, D_EXPERT, D_MODEL))

    tiles_per_mod = n // mod3.shape[0] // tm
    row = pl.BlockSpec((tm, D_MODEL), lambda i, slot: (i, 0))
    return pl.pallas_call(
        functools.partial(_moe_gather_kernel, final_norm=final_norm),
        out_shape=jax.ShapeDtypeStruct((n, D_MODEL), F32),
        grid_spec=pltpu.PrefetchScalarGridSpec(
            num_scalar_prefetch=1, grid=(n // tm,),
            in_specs=[
                pl.BlockSpec(memory_space=pl.ANY), row,
                pl.BlockSpec((1, 6, D_MODEL), lambda i, slot: (i // tiles_per_mod, 0, 0)),
                pl.BlockSpec((1, D_MODEL), lambda i, slot: (0, 0)),
            ],
            out_specs=row,
            scratch_shapes=[pltpu.VMEM((2, tm, D_MODEL), F32), pltpu.SemaphoreType.DMA((2,))],
        ),
        compiler_params=_cparams("arbitrary"),
        name="moe_gather",
    )(slot, sorted_y, x, mod3, norm_final.reshape(1, D_MODEL))


def _reorder_w_in(w):
    pad = jnp.zeros((D_MODEL, COL_TILE - 2 * LANES - 16), w.dtype)
    cols = [w[:, 0:2048], w[:, 2064:4112], w[:, 4368:5904], w[:, 4112:4368], w[:, 2048:2064], pad, w[:, 5904:]]
    return jnp.concatenate(cols, axis=1).astype(BF16)


def _rope_tables(seq):
    nf = DH_C // 4
    inv = ROPE_BASE ** (-jnp.arange(nf, dtype=F32) / nf)
    t = jnp.arange(seq)
    ang_r = (t // GRID_W).astype(F32)[:, None] * inv[None, :]
    ang_c = (t % GRID_W).astype(F32)[:, None] * inv[None, :]
    cos = jnp.concatenate([jnp.cos(ang_r)] * 2 + [jnp.cos(ang_c)] * 2, axis=1)
    sin = jnp.concatenate([-jnp.sin(ang_r), jnp.sin(ang_r), -jnp.sin(ang_c), jnp.sin(ang_c)], axis=1)
    return jnp.tile(cos, (1, 2)), jnp.tile(sin, (1, 2))


def _layer(l, x, mod3, seq, lat, p):
    batch = x.shape[0] // seq
    lam_init = 0.8 - 0.6 * math.exp(-0.3 * l)
    ctx = lat is None
    proj = _qkv_proj(x, p["norm_mix"][l], mod3, p["w_in"][l], None if ctx else p["rope"], ctx,
                     tm=1024 if ctx else 2048)
    qkv, gates_raw = proj[:2]
    gates = gates_raw[:, :16] + jnp.concatenate([p["i_bias"][l].reshape(-1), p["f_bias"][l].reshape(-1)])[None]
    lam = (jnp.exp(jnp.sum(p["lq1"][l] * p["lk1"][l])) - jnp.exp(jnp.sum(p["lq2"][l] * p["lk2"][l]))
           + lam_init).reshape(1).astype(F32)
    sink = p["sink"][l]
    new = None
    if ctx:
        C0 = jnp.zeros((batch, 2, H_A, DH_A, DH_A), F32)
        n0 = jnp.zeros((batch, 2, H_A, DH_A), F32)
        m0 = jnp.zeros((batch, 2, H_A), F32)
        ya, Cs, ns, ms = _mlstm(qkv, gates, C0, n0, m0, batch, seq)
        yb, yc, yd = _ctx_attention(qkv, lam, sink, batch, seq)
        bk, bv, dk, dv, ckv = proj[2:]
        new = (bk.reshape(batch, seq, H_B, DH_B), bv.reshape(batch, seq, H_B, DH_B),
               ckv[:, :LANES].reshape(batch, seq, KV_C, DH_C), ckv[:, LANES:].reshape(batch, seq, KV_C, DH_C),
               dk.reshape(batch, seq, H_D, 2 * DH_D), dv.reshape(batch, seq, H_D, 2 * DH_D), Cs, ns, ms)
    else:
        nat_k, nat_v, swa_k, swa_v, diff_k, diff_v, C0, n0, m0 = lat
        past = nat_k.shape[1]
        flat = lambda a: a.reshape(batch, past, -1)
        ya, _, _, _ = _mlstm(qkv, gates, C0, n0, m0, batch, seq)
        yb = _nat_latent(qkv, flat(nat_k), flat(nat_v), p["rpb"][l], batch, seq)
        yc = _swa_latent(qkv, flat(swa_k), flat(swa_v), sink, batch, seq)
        yd = _diff_latent(qkv, flat(diff_k), flat(diff_v), lam, batch, seq)
    xn, h2, group = _merge((ya, yb, yc, yd), qkv, x, mod3, p["a_norm"][l], p["subln"][l],
                           p["norm_ffn"][l], p["w_branch"][l], p["w_out"][l], p["w_route"][l], lam_init)
    y = _moe(h2, group[:, 0].astype(jnp.int32), xn, mod3, p["w_route"][l], p["wg"][l], p["wu"][l], p["wd"][l],
             p["norm_final"], final_norm=(l == DEPTH - 1))
    return y, new


def kernel(x_prompt, x_sample, cache_nat_k, cache_nat_v, cache_swa_k, cache_swa_v, cache_diff_k, cache_diff_v, state_mlstm_C, state_mlstm_n, state_mlstm_m, c, c_ctx, norm_mix, norm_ffn, norm_final, w_mod, b_mod, w_in, mlstm_i_bias, mlstm_f_bias, mlstm_norm, nat_rpb, swa_sink, diff_lq1, diff_lk1, diff_lq2, diff_lk2, diff_subln, w_branch, w_out, router_group, router_expert, w_exp_gate, w_exp_up, w_exp_down):
    b_ctx, s_ctx, _ = x_prompt.shape
    b_lat, s_lat, _ = x_sample.shape
    cond8 = jnp.concatenate([c_ctx[None], c, jnp.zeros((8 - 1 - b_lat, D_MODEL), F32)], axis=0)
    mod = _modulation(cond8, w_mod, b_mod).reshape(DEPTH, 8, 6, D_MODEL)
    route_pad = jnp.zeros((DEPTH, D_MODEL, LANES - N_GROUPS - N_EXPERTS), F32)
    p = dict(
        norm_mix=norm_mix, norm_ffn=norm_ffn, norm_final=norm_final,
        w_in=[_reorder_w_in(w_in[l]) for l in range(DEPTH)],
        i_bias=mlstm_i_bias, f_bias=mlstm_f_bias, a_norm=mlstm_norm, rpb=nat_rpb, sink=swa_sink,
        lq1=diff_lq1, lk1=diff_lk1, lq2=diff_lq2, lk2=diff_lk2, subln=diff_subln,
        w_branch=w_branch.astype(BF16), w_out=w_out.astype(BF16),
        w_route=jnp.concatenate([router_group, router_expert, route_pad], axis=-1).astype(BF16),
        wg=w_exp_gate.astype(BF16), wu=w_exp_up.astype(BF16), wd=w_exp_down.astype(BF16),
        rope=_rope_tables(s_lat),
    )
    yp = x_prompt.reshape(b_ctx * s_ctx, D_MODEL)
    ys = x_sample.reshape(b_lat * s_lat, D_MODEL)
    news = []
    for l in range(DEPTH):
        yp, new = _layer(l, yp, mod[l, 0:1], s_ctx, None, p)
        news.append(new)
        lat = (cache_nat_k[:, l], cache_nat_v[:, l], cache_swa_k[:, l], cache_swa_v[:, l],
               cache_diff_k[:, l], cache_diff_v[:, l], state_mlstm_C[:, l], state_mlstm_n[:, l],
               state_mlstm_m[:, l])
        ys, _ = _layer(l, ys, mod[l, 1:1 + b_lat], s_lat, lat, p)
    stacked = tuple(jnp.stack([news[l][k] for l in range(DEPTH)], axis=1) for k in range(9))
    return (yp.reshape(b_ctx, s_ctx, D_MODEL), ys.reshape(b_lat, s_lat, D_MODEL)) + stacked
```

```python
import functools
import math

import numpy as np
import jax
import jax.numpy as jnp
from jax import lax
from jax.experimental import pallas as pl
from jax.experimental.pallas import tpu as pltpu

F32 = jnp.float32
BF16 = jnp.bfloat16

D_MODEL = 1024
DEPTH = 2
GRID_W = 64
BRANCH_W = 512
N_BRANCH = 4
H_A, DH_A, MLSTM_CHUNK = 4, 128, 128
H_B, DH_B = 8, 64
NA_ROWS, NA_COLS = 8, 16
H_C, KV_C, DH_C = 8, 2, 64
SWA_WINDOW = 128
H_D, DH_D = 4, 64
N_GROUPS, EXP_PER_GROUP = 4, 4
N_EXPERTS = N_GROUPS * EXP_PER_GROUP
D_EXPERT = 256
ROPE_BASE = 10000.0
EPS = 1e-6

LANES = 128
HALF = 64
COL_TILE = 512
N_COL_TILES = 20
N_PROJ = COL_TILE * N_COL_TILES
T_AQ, T_AK, T_AV, T_AO, T_BQ, T_BK, T_BV, T_CQ, T_DQ, T_DK, T_DV, T_MISC, T_GT = range(13)
MISC_GATE_OFF = 256
VMEM_LIMIT = 56 * 1024 * 1024
NEG_INF = float("-inf")

_NT = (((1,), (1,)), ((), ()))
_TN = (((0,), (0,)), ((), ()))


def _cparams(*sem):
    return pltpu.CompilerParams(dimension_semantics=sem, vmem_limit_bytes=VMEM_LIMIT)


def _dot(a, b):
    return jnp.dot(a, b, preferred_element_type=F32)


def _dot_nt(a, b):
    return lax.dot_general(a, b, _NT, preferred_element_type=F32)


def _dot_tn(a, b):
    return lax.dot_general(a, b, _TN, preferred_element_type=F32)


def _rms(x, g):
    return x * lax.rsqrt(jnp.mean(x * x, axis=-1, keepdims=True) + EPS) * g


def _sigmoid(x):
    return 0.5 * jnp.tanh(0.5 * x) + 0.5


def _mod_kernel(c_ref, w_ref, b_ref, o_ref):
    c = c_ref[...]
    s = (c * _sigmoid(c)).astype(BF16)
    o_ref[0] = _dot(s, w_ref[0].astype(BF16)) + b_ref[0]


def _modulation(cond8, w_mod, b_mod):
    tn = 1536
    n_out = 6 * D_MODEL
    return pl.pallas_call(
        _mod_kernel,
        out_shape=jax.ShapeDtypeStruct((DEPTH, 8, n_out), F32),
        grid=(DEPTH, n_out // tn),
        in_specs=[
            pl.BlockSpec((8, D_MODEL), lambda l, j: (0, 0)),
            pl.BlockSpec((1, D_MODEL, tn), lambda l, j: (l, 0, j)),
            pl.BlockSpec((1, 1, tn), lambda l, j: (l, 0, j)),
        ],
        out_specs=pl.BlockSpec((1, 8, tn), lambda l, j: (l, 0, j)),
        compiler_params=_cparams("parallel", "parallel"),
        name="modulation",
    )(cond8, w_mod, b_mod.reshape(DEPTH, 1, n_out))


CACHE_TILES = (T_BK, T_BV, T_DK, T_DV)
QKV_ROW_CHUNK = 512


def _rope128(x, cos, sin):
    lane = lax.broadcasted_iota(jnp.int32, x.shape, 1)
    partner = jnp.where((lane % 32) < 16, pltpu.roll(x, LANES - 16, 1), pltpu.roll(x, 16, 1))
    return x * cos + partner * sin


def _qkv_kernel(x_ref, nw_ref, mod_ref, w_ref, *rest, rope, cache):
    rest = list(rest)
    if rope:
        cos_ref, sin_ref = rest[:2]
        rest = rest[2:]
    o_ref, g_ref = rest[:2]
    cache_refs = rest[2:-1]
    h_scr = rest[-1]
    j = pl.program_id(1)

    @pl.when(j == 0)
    def _():
        h = _rms(x_ref[...], nw_ref[...]) * (1.0 + mod_ref[0, 1:2, :]) + mod_ref[0, 0:1, :]
        h_scr[...] = h.astype(BF16)

    def tile_kind(cond, store):
        @pl.when(cond)
        def _():
            for r in range(h_scr.shape[0] // QKV_ROW_CHUNK):
                rows = slice(r * QKV_ROW_CHUNK, (r + 1) * QKV_ROW_CHUNK)
                store(rows, _dot(h_scr[rows, :], w_ref[...]))

    def roped(rows, acc, c):
        sl = slice(c * LANES, (c + 1) * LANES)
        return _rope128(acc[:, sl], cos_ref[rows, :], sin_ref[rows, :]).astype(BF16)

    def store_plain(rows, acc):
        o_ref[rows, :] = acc.astype(BF16)

    def store_ak(rows, acc):
        o_ref[rows, :] = (acc * (DH_A ** -0.5)).astype(BF16)

    def store_misc(rows, acc):
        o_ref[rows, :] = acc.astype(BF16)
        if rope:
            o_ref[rows, :LANES] = roped(rows, acc, 0)
        g_ref[rows, :] = acc[:, MISC_GATE_OFF:MISC_GATE_OFF + LANES]
        if cache:
            cache_refs[-1][rows, :] = acc[:, :2 * LANES]

    def store_rope(rows, acc):
        for c in range(COL_TILE // LANES):
            o_ref[rows, c * LANES:(c + 1) * LANES] = roped(rows, acc, c)

    special = (j == T_AK) | (j == T_MISC)
    tile_kind(j == T_AK, store_ak)
    tile_kind(j == T_MISC, store_misc)
    if rope:
        is_rope = (j == T_CQ) | (j == T_DQ) | (j == T_DK)
        special = special | is_rope
        tile_kind(is_rope, store_rope)
    if cache:
        for t, ref in zip(CACHE_TILES, cache_refs[:-1]):
            def store_cached(rows, acc, ref=ref):
                o_ref[rows, :] = acc.astype(BF16)
                ref[rows, :] = acc
            special = special | (j == t)
            tile_kind(j == t, store_cached)
    tile_kind(jnp.logical_not(special), store_plain)


def _qkv_proj(x, norm_w, mod3, w_in_r, rope_tabs, cache, tm):
    n = x.shape[0]
    tiles_per_mod = n // mod3.shape[0] // tm
    rope = rope_tabs is not None
    in_specs = [
        pl.BlockSpec((tm, D_MODEL), lambda i, j: (i, 0)),
        pl.BlockSpec((1, D_MODEL), lambda i, j: (0, 0)),
        pl.BlockSpec((1, 6, D_MODEL), lambda i, j: (i // tiles_per_mod, 0, 0)),
        pl.BlockSpec((D_MODEL, COL_TILE), lambda i, j: (0, j)),
    ]
    args = [x, norm_w.reshape(1, D_MODEL), mod3, w_in_r]
    if rope:
        tiles_per_seq = rope_tabs[0].shape[0] // tm
        in_specs += [pl.BlockSpec((tm, LANES), lambda i, j: (i % tiles_per_seq, 0))] * 2
        args += list(rope_tabs)
    row = lambda w: pl.BlockSpec((tm, w), lambda i, j: (i, 0))
    out_shape = [jax.ShapeDtypeStruct((n, N_PROJ), BF16), jax.ShapeDtypeStruct((n, LANES), F32)]
    out_specs = [pl.BlockSpec((tm, COL_TILE), lambda i, j: (i, j)), row(LANES)]
    if cache:
        out_shape += [jax.ShapeDtypeStruct((n, COL_TILE), F32)] * len(CACHE_TILES)
        out_shape += [jax.ShapeDtypeStruct((n, 2 * LANES), F32)]
        out_specs += [row(COL_TILE)] * len(CACHE_TILES) + [row(2 * LANES)]
    return pl.pallas_call(
        functools.partial(_qkv_kernel, rope=rope, cache=cache),
        out_shape=tuple(out_shape),
        grid=(n // tm, N_COL_TILES),
        in_specs=in_specs,
        out_specs=tuple(out_specs),
        scratch_shapes=[pltpu.VMEM((tm, D_MODEL), BF16)],
        compiler_params=_cparams("parallel", "arbitrary"),
        name="qkv_rope" if rope else "qkv",
    )(*args)


def _log_sigmoid(x):
    return jnp.minimum(x, 0.0) - jnp.log1p(jnp.exp(-jnp.abs(x)))


def _mlstm_chunk(c, h, d, state, q_ref, k_ref, v_ref, gr_ref, gc_ref, out_ref):
    L = MLSTM_CHUNK
    C, n, m = state
    row = lax.broadcasted_iota(jnp.int32, (L, L), 0)
    col = lax.broadcasted_iota(jnp.int32, (L, L), 1)
    tri = (col <= row) if d == 0 else (col >= row)
    tri_t = (row <= col) if d == 0 else (row >= col)
    off = pl.multiple_of(c * L, L)
    hs = slice(h * DH_A, (h + 1) * DH_A)
    qb = q_ref[pl.ds(off, L), hs]
    kb = k_ref[pl.ds(off, L), hs]
    vb = v_ref[pl.ds(off, L), hs]
    qx, kx, vx = qb.astype(F32), kb.astype(F32), vb.astype(F32)
    g_rows = gr_ref[h, c]
    g_cols = gc_ref[h, c]
    ig_r = g_rows[2 * d:2 * d + 1, :]
    lf_r = _log_sigmoid(g_rows[2 * d + 1:2 * d + 2, :])
    ig_c = g_cols[:, 2 * d:2 * d + 1]
    lf_c = _log_sigmoid(g_cols[:, 2 * d + 1:2 * d + 2])
    b_c = jnp.sum(jnp.where(tri, lf_r, 0.0), axis=1, keepdims=True)
    b_r = jnp.sum(jnp.where(tri_t, lf_c, 0.0), axis=0, keepdims=True)
    qk = _dot_nt(qb, kb)
    qc = _dot_nt(qb, C.astype(BF16))
    yield
    logd = jnp.where(tri, b_c - b_r + ig_r, NEG_INF)
    inter = b_c + m
    m_t = jnp.maximum(inter, jnp.max(logd, axis=1, keepdims=True))
    yield
    w = jnp.exp(logd - m_t)
    a = jnp.exp(inter - m_t)
    s = qk * w
    num = a * qc + _dot(s.astype(BF16), vb)
    den = a * jnp.sum(qx * n, axis=1, keepdims=True) + jnp.sum(s, axis=1, keepdims=True)
    yield
    out_ref[pl.ds(off, L), hs] = num / jnp.maximum(jnp.abs(den), jnp.exp(-m_t))
    b_tot = jnp.sum(lf_r, axis=1, keepdims=True)
    g_c = b_tot - b_c + ig_c
    m_new = jnp.maximum(b_tot + m, jnp.max(g_c, axis=0, keepdims=True))
    yield
    decay = jnp.exp(b_tot + m - m_new)
    wk = jnp.exp(g_c - m_new)
    C_new = decay * C + _dot_tn((wk * vx).astype(BF16), kb)
    n_new = decay * n + jnp.sum(wk * kx, axis=0, keepdims=True)
    return C_new, n_new, m_new


def _run_staged(gens):
    results = [None] * len(gens)
    live = list(range(len(gens)))
    while live:
        for i in list(live):
            try:
                next(gens[i])
            except StopIteration as stop:
                results[i] = stop.value
                live.remove(i)
    return results


def _mlstm_kernel(q_ref, k_ref, v_ref, gr_ref, gc_ref, c0_ref, n0_ref, m0_ref,
                  h_ref, cs_ref, ns_ref, ms_ref, hb_scr, *, nc):
    chains = [(h, d) for h in range(H_A) for d in range(2)]

    def body(ci, states):
        gens = [_mlstm_chunk(ci if d == 0 else nc - 1 - ci, h, d, state, q_ref, k_ref, v_ref, gr_ref, gc_ref,
                             h_ref if d == 0 else hb_scr) for (h, d), state in zip(chains, states)]
        return tuple(_run_staged(gens))

    init = tuple((c0_ref[0, d, h], n0_ref[0, h, d:d + 1, :], m0_ref[0, h, d:d + 1, 0:1]) for h, d in chains)
    final = lax.fori_loop(0, nc, body, init)
    for (h, d), (C, n, m) in zip(chains, final):
        cs_ref[0, d, h] = C
        ns_ref[0, h, d:d + 1, :] = n
        ms_ref[0, h, d:d + 1, :] = jnp.broadcast_to(m, (1, LANES))
    h_ref[...] += hb_scr[...]


def _mlstm(qkv, gates, C0, n0, m0, batch, seq):
    nc = seq // MLSTM_CHUNK
    L = MLSTM_CHUNK
    ig = gates[:, :8].reshape(batch, nc, L, 2, H_A)
    fg = gates[:, 8:].reshape(batch, nc, L, 2, H_A)
    g4 = jnp.stack([ig[..., 0, :], fg[..., 0, :], ig[..., 1, :], fg[..., 1, :]], axis=-1)
    g_cols = jnp.transpose(g4, (3, 0, 1, 2, 4)).reshape(H_A, batch * nc, L, 4)
    g_rows = jnp.transpose(g4, (3, 0, 1, 4, 2)).reshape(H_A, batch * nc, 4, L)
    n0t = jnp.transpose(n0, (0, 2, 1, 3))
    m0t = jnp.broadcast_to(jnp.transpose(m0, (0, 2, 1))[..., None], (batch, H_A, 2, LANES))
    qkv_spec = lambda tile: pl.BlockSpec((seq, COL_TILE), lambda b: (b, tile))
    c_spec = pl.BlockSpec((1, 2, H_A, DH_A, DH_A), lambda b: (b, 0, 0, 0, 0))
    state_spec = pl.BlockSpec((1, H_A, 2, LANES), lambda b: (b, 0, 0, 0))
    h, Cs, ns, ms = pl.pallas_call(
        functools.partial(_mlstm_kernel, nc=nc),
        out_shape=(
            jax.ShapeDtypeStruct((batch * seq, H_A * DH_A), F32),
            jax.ShapeDtypeStruct((batch, 2, H_A, DH_A, DH_A), F32),
            jax.ShapeDtypeStruct((batch, H_A, 2, DH_A), F32),
            jax.ShapeDtypeStruct((batch, H_A, 2, LANES), F32),
        ),
        grid=(batch,),
        in_specs=[
            qkv_spec(T_AQ), qkv_spec(T_AK), qkv_spec(T_AV),
            pl.BlockSpec((H_A, nc, 4, L), lambda b: (0, b, 0, 0)),
            pl.BlockSpec((H_A, nc, L, 4), lambda b: (0, b, 0, 0)),
            c_spec, state_spec, state_spec,
        ],
        out_specs=(pl.BlockSpec((seq, H_A * DH_A), lambda b: (b, 0)), c_spec, state_spec, state_spec),
        scratch_shapes=[pltpu.VMEM((seq, H_A * DH_A), F32)],
        compiler_params=_cparams("parallel"),
        name="mlstm",
    )(qkv, qkv, qkv, g_rows, g_cols, C0, n0t, m0t)
    return h, Cs, jnp.transpose(ns, (0, 2, 1, 3)), jnp.transpose(ms[..., 0], (0, 2, 1))


ATTN_SCALE = DH_B ** -0.5


def _half_mask(shape, half):
    lane = lax.broadcasted_iota(jnp.int32, shape, 1)
    return (lane >= HALF) if half else (lane < HALF)


def _select_half(q, half):
    return jnp.where(_half_mask(q.shape, half), q, jnp.zeros_like(q))


def _swap_halves(x):
    return pltpu.roll(x.astype(F32), HALF, 1).astype(x.dtype)


def _softmax_parts(scores, extra=None):
    m = functools.reduce(jnp.maximum, [jnp.max(s, axis=1, keepdims=True) for s in scores])
    if extra is not None:
        m = jnp.maximum(m, extra)
    es = [jnp.exp(s - m) for s in scores]
    l = functools.reduce(lambda x, y: x + y, [jnp.sum(e, axis=1, keepdims=True) for e in es])
    if extra is not None:
        l = l + jnp.exp(extra - m)
    return es, l


def _pv(es, vs):
    return functools.reduce(lambda x, y: x + y, [_dot(e.astype(BF16), v) for e, v in zip(es, vs)])


def _diff_head(q, ks, vs, lam):
    parts = [_softmax_parts([_dot_nt(_select_half(q, comp), k) for k in ks]) for comp in range(2)]
    (e0, l0), (e1, l1) = parts
    coef = lam * l0 / l1
    pd = [a - coef * b for a, b in zip(e0, e1)]
    return _pv(pd, vs) / l0


def _ctx_attn_kernel(lam_ref, sink_ref, bq, bk, bv, cq, misc, dq, dk, dv, yb_ref, yc_ref, yd_ref):
    for p in range(H_B // 2):
        sl = slice(p * LANES, (p + 1) * LANES)
        q = bq[:, sl] * ATTN_SCALE
        outs = []
        for half in range(2):
            es, l = _softmax_parts([_dot_nt(_select_half(q, half), bk[:, sl])])
            outs.append(_pv(es, [bv[:, sl]]) / l)
        yb_ref[:, sl] = jnp.where(_half_mask(q.shape, 0), outs[0], outs[1]).astype(yb_ref.dtype)

    kc = misc[:, 0:LANES]
    vc = misc[:, LANES:2 * LANES]
    for p in range(H_C // 2):
        sl = slice(p * LANES, (p + 1) * LANES)
        q = cq[:, sl] * ATTN_SCALE
        outs = []
        for half in range(2):
            head = 2 * p + half
            kv = head // (H_C // KV_C)
            qm = _select_half(q, half)
            if kv != half:
                qm = _swap_halves(qm)
            es, l = _softmax_parts([_dot_nt(qm, kc)], extra=sink_ref[head])
            o = _pv(es, [vc]) / l
            outs.append(o if kv == half else _swap_halves(o))
        yc_ref[:, sl] = jnp.where(_half_mask(q.shape, 0), outs[0], outs[1]).astype(yc_ref.dtype)

    lam = lam_ref[0]
    for h in range(H_D):
        sl = slice(h * LANES, (h + 1) * LANES)
        yd_ref[:, sl] = _diff_head(dq[:, sl] * ATTN_SCALE, [dk[:, sl]], [dv[:, sl]], lam)


def _ctx_attention(qkv, lam, sink, batch, seq):
    tile = lambda t: pl.BlockSpec((seq, COL_TILE), lambda b: (b, t))
    smem = pl.BlockSpec(memory_space=pltpu.SMEM)
    out = lambda dt: jax.ShapeDtypeStruct((batch * seq, BRANCH_W), dt)
    out_spec = pl.BlockSpec((seq, BRANCH_W), lambda b: (b, 0))
    return pl.pallas_call(
        _ctx_attn_kernel,
        out_shape=(out(BF16), out(BF16), out(F32)),
        grid=(batch,),
        in_specs=[smem, smem, tile(T_BQ), tile(T_BK), tile(T_BV), tile(T_CQ), tile(T_MISC),
                  tile(T_DQ), tile(T_DK), tile(T_DV)],
        out_specs=(out_spec, out_spec, out_spec),
        compiler_params=_cparams("parallel"),
        name="ctx_attention",
    )(lam, sink, *([qkv] * 8))


NAT_QROWS = 4
NAT_KROWS = 12
NAT_RI = 2 * NA_ROWS - 1


def _nat_kernel(q_ref, k_ref, v_ref, kc_ref, vc_ref, cb_ref, o_ref):
    rows = k_ref.shape[0] // GRID_W
    r0 = pl.program_id(2) * NAT_QROWS
    kb = jnp.clip(r0 - NA_ROWS // 2, 0, rows - NAT_KROWS)
    koff = pl.multiple_of(kb * GRID_W, GRID_W)
    nq = NAT_QROWS * GRID_W
    nk = NAT_KROWS * GRID_W
    q = q_ref[...] * ATTN_SCALE
    k = k_ref[pl.ds(koff, nk), :]
    v = v_ref[pl.ds(koff, nk), :]
    kc = kc_ref[0].astype(BF16)
    vc = vc_ref[0].astype(BF16)
    left = _half_mask((GRID_W, LANES), 0)
    outs = []
    for half in range(2):
        bias_rows = []
        for a in range(NAT_QROWS):
            r = r0 + a
            rs = jnp.clip(r - NA_ROWS // 2, 0, rows - NA_ROWS)
            blocks = []
            for ip in range(NAT_KROWS // 2):
                sides = []
                for side in range(2):
                    kr = kb + 2 * ip + side
                    ok = (kr >= rs) & (kr < rs + NA_ROWS)
                    ri = jnp.clip(kr - r + NA_ROWS - 1, 0, NAT_RI - 1)
                    sides.append(jnp.where(ok, cb_ref[half, ri], NEG_INF))
                blocks.append(jnp.where(left, sides[0], sides[1]))
            bias_rows.append(jnp.concatenate(blocks, axis=1))
        bias = jnp.concatenate(bias_rows, axis=0)
        qm = _select_half(q, half)
        s_nb = _dot_nt(qm, k) + bias
        s_ctx = _dot_nt(qm, kc)
        es, l = _softmax_parts([s_nb, s_ctx])
        outs.append(_pv(es, [v, vc]) / l)
    o_ref[...] = jnp.where(_half_mask((nq, LANES), 0), outs[0], outs[1]).astype(o_ref.dtype)


def _nat_col_table(rpb):
    qc = np.arange(GRID_W)[:, None]
    kc = np.arange(GRID_W)[None, :]
    cs = np.clip(qc - NA_COLS // 2, 0, GRID_W - NA_COLS)
    valid = (kc >= cs) & (kc < cs + NA_COLS)
    n_ci = 2 * NA_COLS - 1
    lo = GRID_W - 1 - (NA_COLS - 1)
    v = jnp.pad(rpb, ((0, 0), (0, 0), (lo, 2 * GRID_W - n_ci - lo)))
    flat = jnp.tile(v, (1, 1, GRID_W))[..., :GRID_W * (2 * GRID_W - 1)]
    toep = flat.reshape(rpb.shape[0], rpb.shape[1], GRID_W, 2 * GRID_W - 1)[..., GRID_W - 1:]
    tab = jnp.where(valid[None, None], toep, NEG_INF)
    return jnp.concatenate([tab, tab], axis=-1)


def _nat_latent(qkv, kc, vc, rpb, batch, seq):
    nq = NAT_QROWS * GRID_W
    tpb = seq // nq
    past = kc.shape[1]
    cb = _nat_col_table(rpb)
    return pl.pallas_call(
        _nat_kernel,
        out_shape=jax.ShapeDtypeStruct((batch * seq, BRANCH_W), BF16),
        grid=(batch, H_B // 2, tpb),
        in_specs=[
            pl.BlockSpec((nq, LANES), lambda b, p, i: (b * tpb + i, T_BQ * 4 + p)),
            pl.BlockSpec((seq, LANES), lambda b, p, i: (b, T_BK * 4 + p)),
            pl.BlockSpec((seq, LANES), lambda b, p, i: (b, T_BV * 4 + p)),
            pl.BlockSpec((1, past, LANES), lambda b, p, i: (b, 0, p)),
            pl.BlockSpec((1, past, LANES), lambda b, p, i: (b, 0, p)),
            pl.BlockSpec((2, NAT_RI, GRID_W, LANES), lambda b, p, i: (p, 0, 0, 0)),
        ],
        out_specs=pl.BlockSpec((nq, LANES), lambda b, p, i: (b * tpb + i, p)),
        compiler_params=_cparams("parallel", "parallel", "parallel"),
        name="nat_latent",
    )(qkv, qkv, qkv, kc, vc, cb)


SWA_QBLK = 256
SWA_KBLK = SWA_QBLK + 2 * SWA_WINDOW


def _swa_kernel(sink_ref, q_ref, kv_ref, kc_ref, vc_ref, o_ref):
    seq = kv_ref.shape[0]
    p = pl.program_id(1)
    t0 = pl.program_id(2) * SWA_QBLK
    k0 = pl.multiple_of(jnp.clip(t0 - SWA_WINDOW, 0, seq - SWA_KBLK), SWA_WINDOW)
    q = q_ref[...] * ATTN_SCALE
    k = kv_ref[pl.ds(k0, SWA_KBLK), 0:LANES]
    v = kv_ref[pl.ds(k0, SWA_KBLK), LANES:2 * LANES]
    kc = kc_ref[0].astype(BF16)
    vc = vc_ref[0].astype(BF16)
    tq = t0 + lax.broadcasted_iota(jnp.int32, (SWA_QBLK, SWA_KBLK), 0)
    tk = k0 + lax.broadcasted_iota(jnp.int32, (SWA_QBLK, SWA_KBLK), 1)
    inside = jnp.abs(tq - tk) <= SWA_WINDOW
    kv_half = p // ((H_C // 2) // KV_C)
    outs = []
    for half in range(2):
        aligned = kv_half == half
        qm = _select_half(q, half)
        qm = jnp.where(aligned, qm, _swap_halves(qm))
        s_band = jnp.where(inside, _dot_nt(qm, k), NEG_INF)
        s_ctx = _dot_nt(qm, kc)
        es, l = _softmax_parts([s_band, s_ctx], extra=sink_ref[2 * p + half])
        o = _pv(es, [v, vc]) / l
        outs.append(jnp.where(aligned, o, _swap_halves(o)))
    o_ref[...] = jnp.where(_half_mask(q.shape, 0), outs[0], outs[1]).astype(o_ref.dtype)


def _swa_latent(qkv, kc, vc, sink, batch, seq):
    tpb = seq // SWA_QBLK
    past = kc.shape[1]
    return pl.pallas_call(
        _swa_kernel,
        out_shape=jax.ShapeDtypeStruct((batch * seq, BRANCH_W), BF16),
        grid=(batch, H_C // 2, tpb),
        in_specs=[
            pl.BlockSpec(memory_space=pltpu.SMEM),
            pl.BlockSpec((SWA_QBLK, LANES), lambda b, p, i: (b * tpb + i, T_CQ * 4 + p)),
            pl.BlockSpec((seq, COL_TILE), lambda b, p, i: (b, T_MISC)),
            pl.BlockSpec((1, past, LANES), lambda b, p, i: (b, 0, 0)),
            pl.BlockSpec((1, past, LANES), lambda b, p, i: (b, 0, 0)),
        ],
        out_specs=pl.BlockSpec((SWA_QBLK, LANES), lambda b, p, i: (b * tpb + i, p)),
        compiler_params=_cparams("parallel", "parallel", "parallel"),
        name="swa_latent",
    )(sink, qkv, qkv, kc, vc)


DIFF_QBLK = 256


def _diff_kernel(lam_ref, q_ref, k_ref, v_ref, kc_ref, vc_ref, o_ref):
    q = q_ref[...] * ATTN_SCALE
    ks = [k_ref[...], kc_ref[0].astype(BF16)]
    vs = [v_ref[...], vc_ref[0].astype(BF16)]
    o_ref[...] = _diff_head(q, ks, vs, lam_ref[0])


def _diff_latent(qkv, kc, vc, lam, batch, seq):
    tpb = seq // DIFF_QBLK
    past = kc.shape[1]
    return pl.pallas_call(
        _diff_kernel,
        out_shape=jax.ShapeDtypeStruct((batch * seq, BRANCH_W), F32),
        grid=(batch, H_D, tpb),
        in_specs=[
            pl.BlockSpec(memory_space=pltpu.SMEM),
            pl.BlockSpec((DIFF_QBLK, LANES), lambda b, h, i: (b * tpb + i, T_DQ * 4 + h)),
            pl.BlockSpec((seq, LANES), lambda b, h, i: (b, T_DK * 4 + h)),
            pl.BlockSpec((seq, LANES), lambda b, h, i: (b, T_DV * 4 + h)),
            pl.BlockSpec((1, past, LANES), lambda b, h, i: (b, 0, h)),
            pl.BlockSpec((1, past, LANES), lambda b, h, i: (b, 0, h)),
        ],
        out_specs=pl.BlockSpec((DIFF_QBLK, LANES), lambda b, h, i: (b * tpb + i, h)),
        compiler_params=_cparams("parallel", "parallel", "parallel"),
        name="diff_latent",
    )(lam, qkv, qkv, qkv, kc, vc)


def _rms_heads(y, g_ref, per_head_gain):
    outs = []
    for c in range(BRANCH_W // LANES):
        sl = slice(c * LANES, (c + 1) * LANES)
        g = g_ref[:, sl] if per_head_gain else g_ref[...]
        outs.append(_rms(y[:, sl], g))
    return jnp.concatenate(outs, axis=1)


def _first_lane(hit, lane_f):
    return jnp.min(jnp.where(hit, lane_f, float(LANES)), axis=1, keepdims=True)


def _top_group(logits):
    lane = lax.broadcasted_iota(jnp.int32, logits.shape, 1)
    gl = jnp.where(lane < N_GROUPS, logits, NEG_INF)
    return _first_lane(gl == jnp.max(gl, axis=1, keepdims=True), lane.astype(F32))


def _gate_weights(logits, group):
    lane = lax.broadcasted_iota(jnp.int32, logits.shape, 1)
    lane_f = lane.astype(F32)
    gl = jnp.where(lane < N_GROUPS, logits, NEG_INF)
    gmax = jnp.max(gl, axis=1, keepdims=True)
    g_logit = jnp.sum(jnp.where(lane_f == group, logits, 0.0), axis=1, keepdims=True)
    gw = jnp.exp(g_logit - gmax) / jnp.sum(jnp.exp(gl - gmax), axis=1, keepdims=True)
    e_lane = lane - N_GROUPS
    e_group = lax.shift_right_arithmetic(e_lane, EXP_PER_GROUP.bit_length() - 1).astype(F32)
    in_group = (e_lane >= 0) & (e_lane < N_EXPERTS) & (e_group == group)
    el = jnp.where(in_group, logits, NEG_INF)
    e1 = jnp.max(el, axis=1, keepdims=True)
    i1 = _first_lane(el == e1, lane_f)
    el2 = jnp.where(lane_f == i1, NEG_INF, el)
    e2 = jnp.max(el2, axis=1, keepdims=True)
    i2 = _first_lane(el2 == e2, lane_f)
    t = jnp.exp(e2 - e1)
    w1 = gw / (1.0 + t)
    return jnp.where(lane_f == i1, w1, jnp.where(lane_f == i2, w1 * t, 0.0))


def _merge_kernel(ya, yb, yc, yd, ao, gt0, gt1, gt2, gt3, x_ref, mod_ref, anorm, subln, nffn,
                  wbr, wout, wr, xo_ref, h2_ref, group_ref, *, yd_scale):
    a = _sigmoid(ao[...].astype(F32)) * _rms_heads(ya[...], anorm, True)
    d = _rms_heads(yd[...], subln, False) * yd_scale
    branches = (a, yb[...], yc[...], d)
    mix = None
    for n, (br, gt) in enumerate(zip(branches, (gt0, gt1, gt2, gt3))):
        term = _sigmoid(gt[...].astype(F32)) * _dot(br.astype(BF16), wbr[n])
        mix = term if mix is None else mix + term
    out = _dot(mix.astype(BF16), wout[...])
    xn = x_ref[...] + mod_ref[0, 2:3, :] * out
    xo_ref[...] = xn
    h2 = _rms(xn, nffn[...]) * (1.0 + mod_ref[0, 4:5, :]) + mod_ref[0, 3:4, :]
    h2_ref[...] = h2
    group_ref[...] = jnp.broadcast_to(_top_group(_dot(h2.astype(BF16), wr[...])), group_ref.shape)


def _merge(ys, qkv, x, mod3, a_norm, subln, norm_ffn, w_branch, w_out, w_route, lam_init):
    n = x.shape[0]
    tm = 512
    tiles_per_mod = n // mod3.shape[0] // tm
    y_spec = pl.BlockSpec((tm, BRANCH_W), lambda i: (i, 0))
    gt_spec = lambda k: pl.BlockSpec((tm, D_MODEL), lambda i: (i, T_GT * COL_TILE // D_MODEL + k))
    full = lambda shape: pl.BlockSpec(shape, lambda i: (0,) * len(shape))
    x_spec = pl.BlockSpec((tm, D_MODEL), lambda i: (i, 0))
    return pl.pallas_call(
        functools.partial(_merge_kernel, yd_scale=1.0 - lam_init),
        out_shape=(
            jax.ShapeDtypeStruct((n, D_MODEL), F32),
            jax.ShapeDtypeStruct((n, D_MODEL), F32),
            jax.ShapeDtypeStruct((n, LANES), F32),
        ),
        grid=(n // tm,),
        in_specs=[
            y_spec, y_spec, y_spec, y_spec,
            pl.BlockSpec((tm, COL_TILE), lambda i: (i, T_AO)),
            gt_spec(0), gt_spec(1), gt_spec(2), gt_spec(3),
            x_spec,
            pl.BlockSpec((1, 6, D_MODEL), lambda i: (i // tiles_per_mod, 0, 0)),
            full((1, BRANCH_W)), full((1, LANES)), full((1, D_MODEL)),
            full((N_BRANCH, BRANCH_W, D_MODEL)), full((D_MODEL, D_MODEL)), full((D_MODEL, LANES)),
        ],
        out_specs=(x_spec, x_spec, pl.BlockSpec((tm, LANES), lambda i: (i, 0))),
        compiler_params=_cparams("parallel"),
        name="merge_route",
    )(*ys, qkv, qkv, qkv, qkv, qkv, x, mod3, a_norm.reshape(1, BRANCH_W), subln.reshape(1, LANES),
      norm_ffn.reshape(1, D_MODEL), w_branch, w_out, w_route)


MOE_TILE = 512
ROW_DMA_UNROLL = 16


def _row_copies(src_row, dst_row, sem, n):
    def copy(i):
        return pltpu.make_async_copy(src_row(i), dst_row(i), sem)

    def start(i, c):
        copy(i).start()
        return c

    def wait(i, c):
        copy(i).wait()
        return c

    lax.fori_loop(0, n, start, 0, unroll=ROW_DMA_UNROLL)
    lax.fori_loop(0, n, wait, 0, unroll=ROW_DMA_UNROLL)


def _moe_scatter_kernel(slot_ref, h_ref, init_ref, sorted_ref, sem):
    del init_ref
    base = pl.program_id(0) * MOE_TILE
    _row_copies(lambda i: h_ref.at[pl.ds(i, 1), :],
                lambda i: sorted_ref.at[pl.ds(slot_ref[base + i], 1), :], sem, MOE_TILE)


def _moe_group_kernel(tile_group_ref, n_used_ref, h_ref, wr_ref, wg_ref, wu_ref, wd_ref, o_ref):
    t = pl.program_id(0)

    @pl.when(t < n_used_ref[0])
    def _():
        group = tile_group_ref[t]
        h = h_ref[...].astype(BF16)
        gates = _gate_weights(_dot(h, wr_ref[...]), group.astype(F32))
        lane = lax.broadcasted_iota(jnp.int32, gates.shape, 1)
        acc = None
        for e in range(EXP_PER_GROUP):
            g = jnp.sum(jnp.where(lane == N_GROUPS + group * EXP_PER_GROUP + e, gates, 0.0), axis=1, keepdims=True)
            hg = _dot(h, wg_ref[0, e])
            hid = hg * _sigmoid(hg) * _dot(h, wu_ref[0, e]) * g
            term = _dot(hid.astype(BF16), wd_ref[0, e])
            acc = term if acc is None else acc + term
        o_ref[...] = acc

    @pl.when(t >= n_used_ref[0])
    def _():
        o_ref[...] = jnp.zeros_like(o_ref)


def _moe_gather_kernel(slot_ref, y_ref, x_ref, mod_ref, nf_ref, o_ref, buf, sem, *, final_norm):
    t = pl.program_id(0)
    nt = pl.num_programs(0)

    def copy(tile, b, i):
        return pltpu.make_async_copy(y_ref.at[pl.ds(slot_ref[tile * MOE_TILE + i], 1), :],
                                     buf.at[b, pl.ds(i, 1), :], sem.at[b])

    def start_tile(tile, b):
        def start(i, c):
            copy(tile, b, i).start()
            return c

        lax.fori_loop(0, MOE_TILE, start, 0, unroll=ROW_DMA_UNROLL)

    cur = t % 2

    @pl.when(t == 0)
    def _():
        start_tile(0, 0)

    @pl.when(t + 1 < nt)
    def _():
        start_tile(t + 1, 1 - cur)

    def wait(i, c):
        copy(t, cur, i).wait()
        return c

    lax.fori_loop(0, MOE_TILE, wait, 0, unroll=ROW_DMA_UNROLL)
    y = x_ref[...] + mod_ref[0, 5:6, :] * buf[cur]
    o_ref[...] = _rms(y, nf_ref[...]) if final_norm else y


def _moe(h2, group, x, mod3, w_route, wg, wu, wd, norm_final, final_norm):
    n = x.shape[0]
    tm = MOE_TILE
    n_tiles = n // tm + N_GROUPS
    onehot = (group[:, None] == jnp.arange(N_GROUPS, dtype=jnp.int32)[None]).astype(jnp.int32)
    csum = jnp.cumsum(onehot, axis=0)
    counts = csum[-1]
    rank = jnp.sum((csum - onehot) * onehot, axis=1)
    group_tiles = (counts + tm - 1) // tm
    tile_end = jnp.cumsum(group_tiles)
    group_start = (tile_end - group_tiles) * tm
    slot = (jnp.sum(onehot * group_start[None], axis=1) + rank).astype(jnp.int32)
    tile_group = jnp.minimum(jnp.sum(jnp.arange(n_tiles, dtype=jnp.int32)[:, None] >= tile_end[None], axis=1),
                             N_GROUPS - 1).astype(jnp.int32)
    n_used = tile_end[-1:].astype(jnp.int32)

    sorted_h = pl.pallas_call(
        _moe_scatter_kernel,
        out_shape=jax.ShapeDtypeStruct((n_tiles * tm, D_MODEL), F32),
        grid_spec=pltpu.PrefetchScalarGridSpec(
            num_scalar_prefetch=1, grid=(n // tm,),
            in_specs=[pl.BlockSpec((tm, D_MODEL), lambda i, slot: (i, 0)), pl.BlockSpec(memory_space=pl.ANY)],
            out_specs=pl.BlockSpec(memory_space=pl.ANY),
            scratch_shapes=[pltpu.SemaphoreType.DMA],
        ),
        input_output_aliases={2: 0},
        compiler_params=_cparams("arbitrary"),
        name="moe_scatter",
    )(slot, h2, jnp.zeros((n_tiles * tm, D_MODEL), F32))

    grp = lambda shape: pl.BlockSpec((1,) + shape, lambda t, tg, nu: (tg[t], 0, 0, 0))
    sorted_y = pl.pallas_call(
        _moe_group_kernel,
        out_shape=jax.ShapeDtypeStruct((n_tiles * tm, D_MODEL), F32),
        grid_spec=pltpu.PrefetchScalarGridSpec(
            num_scalar_prefetch=2, grid=(n_tiles,),
            in_specs=[
                pl.BlockSpec((tm, D_MODEL), lambda t, tg, nu: (t, 0)),
                pl.BlockSpec((D_MODEL, LANES), lambda t, tg, nu: (0, 0)),
                grp((EXP_PER_GROUP, D_MODEL, D_EXPERT)), grp((EXP_PER_GROUP, D_MODEL, D_EXPERT)),
                grp((EXP_PER_GROUP, D_EXPERT, D_MODEL)),
            ],
            out_specs=pl.BlockSpec((tm, D_MODEL), lambda t, tg, nu: (t, 0)),
        ),
        compiler_params=_cparams("arbitrary"),
        name="moe_experts",
    )(tile_group, n_used, sorted_h, w_route,
      wg.reshape(N_GROUPS, EXP_PER_GROUP, D_MODEL, D_EXPERT), wu.reshape(N_GROUPS, EXP_PER_GROUP, D_MODEL, D_EXPERT),
      wd.reshape(N_GROUPS, EXP_PER_GROUP, D_EXPERT, D_MODEL))

    tiles_per_mod = n // mod3.shape[0] // tm
    row = pl.BlockSpec((tm, D_MODEL), lambda i, slot: (i, 0))
    return pl.pallas_call(
        functools.partial(_moe_gather_kernel, final_norm=final_norm),
        out_shape=jax.ShapeDtypeStruct((n, D_MODEL), F32),
        grid_spec=pltpu.PrefetchScalarGridSpec(
            num_scalar_prefetch=1, grid=(n // tm,),
            in_specs=[
                pl.BlockSpec(memory_space=pl.ANY), row,
                pl.BlockSpec((1, 6, D_MODEL), lambda i, slot: (i // tiles_per_mod, 0, 0)),
                pl.BlockSpec((1, D_MODEL), lambda i, slot: (0, 0)),
            ],
            out_specs=row,
            scratch_shapes=[pltpu.VMEM((2, tm, D_MODEL), F32), pltpu.SemaphoreType.DMA((2,))],
        ),
        compiler_params=_cparams("arbitrary"),
        name="moe_gather",
    )(slot, sorted_y, x, mod3, norm_final.reshape(1, D_MODEL))


def _reorder_w_in(w):
    pad = jnp.zeros((D_MODEL, COL_TILE - 2 * LANES - 16), w.dtype)
    cols = [w[:, 0:2048], w[:, 2064:4112], w[:, 4368:5904], w[:, 4112:4368], w[:, 2048:2064], pad, w[:, 5904:]]
    return jnp.concatenate(cols, axis=1).astype(BF16)


def _rope_tables(seq):
    nf = DH_C // 4
    inv = ROPE_BASE ** (-jnp.arange(nf, dtype=F32) / nf)
    t = jnp.arange(seq)
    ang_r = (t // GRID_W).astype(F32)[:, None] * inv[None, :]
    ang_c = (t % GRID_W).astype(F32)[:, None] * inv[None, :]
    cos = jnp.concatenate([jnp.cos(ang_r)] * 2 + [jnp.cos(ang_c)] * 2, axis=1)
    sin = jnp.concatenate([-jnp.sin(ang_r), jnp.sin(ang_r), -jnp.sin(ang_c), jnp.sin(ang_c)], axis=1)
    return jnp.tile(cos, (1, 2)), jnp.tile(sin, (1, 2))


def _layer(l, x, mod3, seq, lat, p):
    batch = x.shape[0] // seq
    lam_init = 0.8 - 0.6 * math.exp(-0.3 * l)
    ctx = lat is None
    proj = _qkv_proj(x, p["norm_mix"][l], mod3, p["w_in"][l], None if ctx else p["rope"], ctx,
                     tm=1024 if ctx else 2048)
    qkv, gates_raw = proj[:2]
    gates = gates_raw[:, :16] + jnp.concatenate([p["i_bias"][l].reshape(-1), p["f_bias"][l].reshape(-1)])[None]
    lam = (jnp.exp(jnp.sum(p["lq1"][l] * p["lk1"][l])) - jnp.exp(jnp.sum(p["lq2"][l] * p["lk2"][l]))
           + lam_init).reshape(1).astype(F32)
    sink = p["sink"][l]
    new = None
    if ctx:
        C0 = jnp.zeros((batch, 2, H_A, DH_A, DH_A), F32)
        n0 = jnp.zeros((batch, 2, H_A, DH_A), F32)
        m0 = jnp.zeros((batch, 2, H_A), F32)
        ya, Cs, ns, ms = _mlstm(qkv, gates, C0, n0, m0, batch, seq)
        yb, yc, yd = _ctx_attention(qkv, lam, sink, batch, seq)
        bk, bv, dk, dv, ckv = proj[2:]
        new = (bk.reshape(batch, seq, H_B, DH_B), bv.reshape(batch, seq, H_B, DH_B),
               ckv[:, :LANES].reshape(batch, seq, KV_C, DH_C), ckv[:, LANES:].reshape(batch, seq, KV_C, DH_C),
               dk.reshape(batch, seq, H_D, 2 * DH_D), dv.reshape(batch, seq, H_D, 2 * DH_D), Cs, ns, ms)
    else:
        nat_k, nat_v, swa_k, swa_v, diff_k, diff_v, C0, n0, m0 = lat
        past = nat_k.shape[1]
        flat = lambda a: a.reshape(batch, past, -1)
        ya, _, _, _ = _mlstm(qkv, gates, C0, n0, m0, batch, seq)
        yb = _nat_latent(qkv, flat(nat_k), flat(nat_v), p["rpb"][l], batch, seq)
        yc = _swa_latent(qkv, flat(swa_k), flat(swa_v), sink, batch, seq)
        yd = _diff_latent(qkv, flat(diff_k), flat(diff_v), lam, batch, seq)
    xn, h2, group = _merge((ya, yb, yc, yd), qkv, x, mod3, p["a_norm"][l], p["subln"][l],
                           p["norm_ffn"][l], p["w_branch"][l], p["w_out"][l], p["w_route"][l], lam_init)
    y = _moe(h2, group[:, 0].astype(jnp.int32), xn, mod3, p["w_route"][l], p["wg"][l], p["wu"][l], p["wd"][l],
             p["norm_final"], final_norm=(l == DEPTH - 1))
    return y, new


def kernel(x_prompt, x_sample, cache_nat_k, cache_nat_v, cache_swa_k, cache_swa_v, cache_diff_k, cache_diff_v, state_mlstm_C, state_mlstm_n, state_mlstm_m, c, c_ctx, norm_mix, norm_ffn, norm_final, w_mod, b_mod, w_in, mlstm_i_bias, mlstm_f_bias, mlstm_norm, nat_rpb, swa_sink, diff_lq1, diff_lk1, diff_lq2, diff_lk2, diff_subln, w_branch, w_out, router_group, router_expert, w_exp_gate, w_exp_up, w_exp_down):
    b_ctx, s_ctx, _ = x_prompt.shape
    b_lat, s_lat, _ = x_sample.shape
    cond8 = jnp.concatenate([c_ctx[None], c, jnp.zeros((8 - 1 - b_lat, D_MODEL), F32)], axis=0)
    mod = _modulation(cond8, w_mod, b_mod).reshape(DEPTH, 8, 6, D_MODEL)
    route_pad = jnp.zeros((DEPTH, D_MODEL, LANES - N_GROUPS - N_EXPERTS), F32)
    p = dict(
        norm_mix=norm_mix, norm_ffn=norm_ffn, norm_final=norm_final,
        w_in=[_reorder_w_in(w_in[l]) for l in range(DEPTH)],
        i_bias=mlstm_i_bias, f_bias=mlstm_f_bias, a_norm=mlstm_norm, rpb=nat_rpb, sink=swa_sink,
        lq1=diff_lq1, lk1=diff_lk1, lq2=diff_lq2, lk2=diff_lk2, subln=diff_subln,
        w_branch=w_branch.astype(BF16), w_out=w_out.astype(BF16),
        w_route=jnp.concatenate([router_group, router_expert, route_pad], axis=-1).astype(BF16),
        wg=w_exp_gate.astype(BF16), wu=w_exp_up.astype(BF16), wd=w_exp_down.astype(BF16),
        rope=_rope_tables(s_lat),
    )
    yp = x_prompt.reshape(b_ctx * s_ctx, D_MODEL)
    ys = x_sample.reshape(b_lat * s_lat, D_MODEL)
    news = []
    for l in range(DEPTH):
        yp, new = _layer(l, yp, mod[l, 0:1], s_ctx, None, p)
        news.append(new)
        lat = (cache_nat_k[:, l], cache_nat_v[:, l], cache_swa_k[:, l], cache_swa_v[:, l],
               cache_diff_k[:, l], cache_diff_v[:, l], state_mlstm_C[:, l], state_mlstm_n[:, l],
               state_mlstm_m[:, l])
        ys, _ = _layer(l, ys, mod[l, 1:1 + b_lat], s_lat, lat, p)
    stacked = tuple(jnp.stack([news[l][k] for l in range(DEPTH)], axis=1) for k in range(9))
    return (yp.reshape(b_ctx, s_ctx, D_MODEL), ys.reshape(b_lat, s_lat, D_MODEL)) + stacked
```

```python
import functools
import math

import numpy as np
import jax
import jax.numpy as jnp
from jax import lax
from jax.experimental import pallas as pl
from jax.experimental.pallas import tpu as pltpu

F32 = jnp.float32
BF16 = jnp.bfloat16

D_MODEL = 1024
DEPTH = 2
GRID_W = 64
BRANCH_W = 512
N_BRANCH = 4
H_A, DH_A, MLSTM_CHUNK = 4, 128, 128
H_B, DH_B = 8, 64
NA_ROWS, NA_COLS = 8, 16
H_C, KV_C, DH_C = 8, 2, 64
SWA_WINDOW = 128
H_D, DH_D = 4, 64
N_GROUPS, EXP_PER_GROUP = 4, 4
N_EXPERTS = N_GROUPS * EXP_PER_GROUP
D_EXPERT = 256
ROPE_BASE = 10000.0
EPS = 1e-6

LANES = 128
HALF = 64
COL_TILE = 512
N_COL_TILES = 20
N_PROJ = COL_TILE * N_COL_TILES
T_AQ, T_AK, T_AV, T_AO, T_BQ, T_BK, T_BV, T_CQ, T_DQ, T_DK, T_DV, T_MISC, T_GT = range(13)
MISC_GATE_OFF = 256
VMEM_LIMIT = 56 * 1024 * 1024
NEG_INF = float("-inf")

_NT = (((1,), (1,)), ((), ()))
_TN = (((0,), (0,)), ((), ()))


def _cparams(*sem):
    return pltpu.CompilerParams(dimension_semantics=sem, vmem_limit_bytes=VMEM_LIMIT)


def _dot(a, b):
    return jnp.dot(a, b, preferred_element_type=F32)


def _dot_nt(a, b):
    return lax.dot_general(a, b, _NT, preferred_element_type=F32)


def _dot_tn(a, b):
    return lax.dot_general(a, b, _TN, preferred_element_type=F32)


def _rms(x, g):
    return x * lax.rsqrt(jnp.mean(x * x, axis=-1, keepdims=True) + EPS) * g


def _sigmoid(x):
    return 0.5 * jnp.tanh(0.5 * x) + 0.5


def _mod_kernel(c_ref, w_ref, b_ref, o_ref):
    c = c_ref[...]
    s = (c * _sigmoid(c)).astype(BF16)
    o_ref[0] = _dot(s, w_ref[0].astype(BF16)) + b_ref[0]


def _modulation(cond8, w_mod, b_mod):
    tn = 1536
    n_out = 6 * D_MODEL
    return pl.pallas_call(
        _mod_kernel,
        out_shape=jax.ShapeDtypeStruct((DEPTH, 8, n_out), F32),
        grid=(DEPTH, n_out // tn),
        in_specs=[
            pl.BlockSpec((8, D_MODEL), lambda l, j: (0, 0)),
            pl.BlockSpec((1, D_MODEL, tn), lambda l, j: (l, 0, j)),
            pl.BlockSpec((1, 1, tn), lambda l, j: (l, 0, j)),
        ],
        out_specs=pl.BlockSpec((1, 8, tn), lambda l, j: (l, 0, j)),
        compiler_params=_cparams("parallel", "parallel"),
        name="modulation",
    )(cond8, w_mod, b_mod.reshape(DEPTH, 1, n_out))


CACHE_TILES = (T_BK, T_BV, T_DK, T_DV)
QKV_ROW_CHUNK = 512


def _rope128(x, cos, sin):
    lane = lax.broadcasted_iota(jnp.int32, x.shape, 1)
    partner = jnp.where((lane % 32) < 16, pltpu.roll(x, LANES - 16, 1), pltpu.roll(x, 16, 1))
    return x * cos + partner * sin


def _qkv_kernel(x_ref, nw_ref, mod_ref, w_ref, *rest, rope, cache):
    rest = list(rest)
    if rope:
        cos_ref, sin_ref = rest[:2]
        rest = rest[2:]
    o_ref, g_ref = rest[:2]
    cache_refs = rest[2:-1]
    h_scr = rest[-1]
    j = pl.program_id(1)

    @pl.when(j == 0)
    def _():
        h = _rms(x_ref[...], nw_ref[...]) * (1.0 + mod_ref[0, 1:2, :]) + mod_ref[0, 0:1, :]
        h_scr[...] = h.astype(BF16)

    def tile_kind(cond, store):
        @pl.when(cond)
        def _():
            for r in range(h_scr.shape[0] // QKV_ROW_CHUNK):
                rows = slice(r * QKV_ROW_CHUNK, (r + 1) * QKV_ROW_CHUNK)
                store(rows, _dot(h_scr[rows, :], w_ref[...]))

    def roped(rows, acc, c):
        sl = slice(c * LANES, (c + 1) * LANES)
        return _rope128(acc[:, sl], cos_ref[rows, :], sin_ref[rows, :]).astype(BF16)

    def store_plain(rows, acc):
        o_ref[rows, :] = acc.astype(BF16)

    def store_ak(rows, acc):
        o_ref[rows, :] = (acc * (DH_A ** -0.5)).astype(BF16)

    def store_misc(rows, acc):
        o_ref[rows, :] = acc.astype(BF16)
        if rope:
            o_ref[rows, :LANES] = roped(rows, acc, 0)
        g_ref[rows, :] = acc[:, MISC_GATE_OFF:MISC_GATE_OFF + LANES]
        if cache:
            cache_refs[-1][rows, :] = acc[:, :2 * LANES]

    def store_rope(rows, acc):
        for c in range(COL_TILE // LANES):
            o_ref[rows, c * LANES:(c + 1) * LANES] = roped(rows, acc, c)

    special = (j == T_AK) | (j == T_MISC)
    tile_kind(j == T_AK, store_ak)
    tile_kind(j == T_MISC, store_misc)
    if rope:
        is_rope = (j == T_CQ) | (j == T_DQ) | (j == T_DK)
        special = special | is_rope
        tile_kind(is_rope, store_rope)
    if cache:
        for t, ref in zip(CACHE_TILES, cache_refs[:-1]):
            def store_cached(rows, acc, ref=ref):
                o_ref[rows, :] = acc.astype(BF16)
                ref[rows, :] = acc
            special = special | (j == t)
            tile_kind(j == t, store_cached)
    tile_kind(jnp.logical_not(special), store_plain)


def _qkv_proj(x, norm_w, mod3, w_in_r, rope_tabs, cache, tm):
    n = x.shape[0]
    tiles_per_mod = n // mod3.shape[0] // tm
    rope = rope_tabs is not None
    in_specs = [
        pl.BlockSpec((tm, D_MODEL), lambda i, j: (i, 0)),
        pl.BlockSpec((1, D_MODEL), lambda i, j: (0, 0)),
        pl.BlockSpec((1, 6, D_MODEL), lambda i, j: (i // tiles_per_mod, 0, 0)),
        pl.BlockSpec((D_MODEL, COL_TILE), lambda i, j: (0, j)),
    ]
    args = [x, norm_w.reshape(1, D_MODEL), mod3, w_in_r]
    if rope:
        tiles_per_seq = rope_tabs[0].shape[0] // tm
        in_specs += [pl.BlockSpec((tm, LANES), lambda i, j: (i % tiles_per_seq, 0))] * 2
        args += list(rope_tabs)
    row = lambda w: pl.BlockSpec((tm, w), lambda i, j: (i, 0))
    out_shape = [jax.ShapeDtypeStruct((n, N_PROJ), BF16), jax.ShapeDtypeStruct((n, LANES), F32)]
    out_specs = [pl.BlockSpec((tm, COL_TILE), lambda i, j: (i, j)), row(LANES)]
    if cache:
        out_shape += [jax.ShapeDtypeStruct((n, COL_TILE), F32)] * len(CACHE_TILES)
        out_shape += [jax.ShapeDtypeStruct((n, 2 * LANES), F32)]
        out_specs += [row(COL_TILE)] * len(CACHE_TILES) + [row(2 * LANES)]
    return pl.pallas_call(
        functools.partial(_qkv_kernel, rope=rope, cache=cache),
        out_shape=tuple(out_shape),
        grid=(n // tm, N_COL_TILES),
        in_specs=in_specs,
        out_specs=tuple(out_specs),
        scratch_shapes=[pltpu.VMEM((tm, D_MODEL), BF16)],
        compiler_params=_cparams("parallel", "arbitrary"),
        name="qkv_rope" if rope else "qkv",
    )(*args)


def _log_sigmoid(x):
    return jnp.minimum(x, 0.0) - jnp.log1p(jnp.exp(-jnp.abs(x)))


def _mlstm_chunk(c, h, d, state, q_ref, k_ref, v_ref, gr_ref, gc_ref, out_ref):
    L = MLSTM_CHUNK
    C, n, m = state
    row = lax.broadcasted_iota(jnp.int32, (L, L), 0)
    col = lax.broadcasted_iota(jnp.int32, (L, L), 1)
    tri = (col <= row) if d == 0 else (col >= row)
    tri_t = (row <= col) if d == 0 else (row >= col)
    off = pl.multiple_of(c * L, L)
    hs = slice(h * DH_A, (h + 1) * DH_A)
    qb = q_ref[pl.ds(off, L), hs]
    kb = k_ref[pl.ds(off, L), hs]
    vb = v_ref[pl.ds(off, L), hs]
    qx, kx, vx = qb.astype(F32), kb.astype(F32), vb.astype(F32)
    g_rows = gr_ref[h, c]
    g_cols = gc_ref[h, c]
    ig_r = g_rows[2 * d:2 * d + 1, :]
    lf_r = _log_sigmoid(g_rows[2 * d + 1:2 * d + 2, :])
    ig_c = g_cols[:, 2 * d:2 * d + 1]
    lf_c = _log_sigmoid(g_cols[:, 2 * d + 1:2 * d + 2])
    b_c = jnp.sum(jnp.where(tri, lf_r, 0.0), axis=1, keepdims=True)
    b_r = jnp.sum(jnp.where(tri_t, lf_c, 0.0), axis=0, keepdims=True)
    qk = _dot_nt(qb, kb)
    qc = _dot_nt(qb, C.astype(BF16))
    yield
    logd = jnp.where(tri, b_c - b_r + ig_r, NEG_INF)
    inter = b_c + m
    m_t = jnp.maximum(inter, jnp.max(logd, axis=1, keepdims=True))
    yield
    w = jnp.exp(logd - m_t)
    a = jnp.exp(inter - m_t)
    s = qk * w
    num = a * qc + _dot(s.astype(BF16), vb)
    den = a * jnp.sum(qx * n, axis=1, keepdims=True) + jnp.sum(s, axis=1, keepdims=True)
    yield
    out_ref[pl.ds(off, L), hs] = num / jnp.maximum(jnp.abs(den), jnp.exp(-m_t))
    b_tot = jnp.sum(lf_r, axis=1, keepdims=True)
    g_c = b_tot - b_c + ig_c
    m_new = jnp.maximum(b_tot + m, jnp.max(g_c, axis=0, keepdims=True))
    yield
    decay = jnp.exp(b_tot + m - m_new)
    wk = jnp.exp(g_c - m_new)
    C_new = decay * C + _dot_tn((wk * vx).astype(BF16), kb)
    n_new = decay * n + jnp.sum(wk * kx, axis=0, keepdims=True)
    return C_new, n_new, m_new


def _run_staged(gens):
    results = [None] * len(gens)
    live = list(range(len(gens)))
    while live:
        for i in list(live):
            try:
                next(gens[i])
            except StopIteration as stop:
                results[i] = stop.value
                live.remove(i)
    return results


def _mlstm_kernel(q_ref, k_ref, v_ref, gr_ref, gc_ref, c0_ref, n0_ref, m0_ref,
                  h_ref, cs_ref, ns_ref, ms_ref, hb_scr, *, nc):
    chains = [(h, d) for h in range(H_A) for d in range(2)]

    def body(ci, states):
        gens = [_mlstm_chunk(ci if d == 0 else nc - 1 - ci, h, d, state, q_ref, k_ref, v_ref, gr_ref, gc_ref,
                             h_ref if d == 0 else hb_scr) for (h, d), state in zip(chains, states)]
        return tuple(_run_staged(gens))

    init = tuple((c0_ref[0, d, h], n0_ref[0, h, d:d + 1, :], m0_ref[0, h, d:d + 1, 0:1]) for h, d in chains)
    final = lax.fori_loop(0, nc, body, init)
    for (h, d), (C, n, m) in zip(chains, final):
        cs_ref[0, d, h] = C
        ns_ref[0, h, d:d + 1, :] = n
        ms_ref[0, h, d:d + 1, :] = jnp.broadcast_to(m, (1, LANES))
    h_ref[...] += hb_scr[...]


def _mlstm(qkv, gates, C0, l, n0, m0, batch, seq):
    nc = seq // MLSTM_CHUNK
    L = MLSTM_CHUNK
    ig = gates[:, :8].reshape(batch, nc, L, 2, H_A)
    fg = gates[:, 8:].reshape(batch, nc, L, 2, H_A)
    g4 = jnp.stack([ig[..., 0, :], fg[..., 0, :], ig[..., 1, :], fg[..., 1, :]], axis=-1)
    g_cols = jnp.transpose(g4, (3, 0, 1, 2, 4)).reshape(H_A, batch * nc, L, 4)
    g_rows = jnp.transpose(g4, (3, 0, 1, 4, 2)).reshape(H_A, batch * nc, 4, L)
    n0t = jnp.transpose(n0, (0, 2, 1, 3))
    m0t = jnp.broadcast_to(jnp.transpose(m0, (0, 2, 1))[..., None], (batch, H_A, 2, LANES))
    qkv_spec = lambda tile: pl.BlockSpec((seq, COL_TILE), lambda b: (b, tile))
    c_spec = pl.BlockSpec((1, 2, H_A, DH_A, DH_A), lambda b: (b, 0, 0, 0, 0))
    c0_spec = pl.BlockSpec((1, None, 2, H_A, DH_A, DH_A), lambda b: (b, l, 0, 0, 0, 0))
    state_spec = pl.BlockSpec((1, H_A, 2, LANES), lambda b: (b, 0, 0, 0))
    h, Cs, ns, ms = pl.pallas_call(
        functools.partial(_mlstm_kernel, nc=nc),
        out_shape=(
            jax.ShapeDtypeStruct((batch * seq, H_A * DH_A), F32),
            jax.ShapeDtypeStruct((batch, 2, H_A, DH_A, DH_A), F32),
            jax.ShapeDtypeStruct((batch, H_A, 2, DH_A), F32),
            jax.ShapeDtypeStruct((batch, H_A, 2, LANES), F32),
        ),
        grid=(batch,),
        in_specs=[
            qkv_spec(T_AQ), qkv_spec(T_AK), qkv_spec(T_AV),
            pl.BlockSpec((H_A, nc, 4, L), lambda b: (0, b, 0, 0)),
            pl.BlockSpec((H_A, nc, L, 4), lambda b: (0, b, 0, 0)),
            c0_spec, state_spec, state_spec,
        ],
        out_specs=(pl.BlockSpec((seq, H_A * DH_A), lambda b: (b, 0)), c_spec, state_spec, state_spec),
        scratch_shapes=[pltpu.VMEM((seq, H_A * DH_A), F32)],
        compiler_params=_cparams("parallel"),
        name="mlstm",
    )(qkv, qkv, qkv, g_rows, g_cols, C0, n0t, m0t)
    return h, Cs, jnp.transpose(ns, (0, 2, 1, 3)), jnp.transpose(ms[..., 0], (0, 2, 1))


ATTN_SCALE = DH_B ** -0.5


def _half_mask(shape, half):
    lane = lax.broadcasted_iota(jnp.int32, shape, 1)
    return (lane >= HALF) if half else (lane < HALF)


def _select_half(q, half):
    return jnp.where(_half_mask(q.shape, half), q, jnp.zeros_like(q))


def _swap_halves(x):
    return pltpu.roll(x.astype(F32), HALF, 1).astype(x.dtype)


def _softmax_parts(scores, extra=None):
    m = functools.reduce(jnp.maximum, [jnp.max(s, axis=1, keepdims=True) for s in scores])
    if extra is not None:
        m = jnp.maximum(m, extra)
    es = [jnp.exp(s - m) for s in scores]
    l = functools.reduce(lambda x, y: x + y, [jnp.sum(e, axis=1, keepdims=True) for e in es])
    if extra is not None:
        l = l + jnp.exp(extra - m)
    return es, l


def _pv(es, vs):
    return functools.reduce(lambda x, y: x + y, [_dot(e.astype(BF16), v) for e, v in zip(es, vs)])


def _diff_head(q, ks, vs, lam):
    parts = [_softmax_parts([_dot_nt(_select_half(q, comp), k) for k in ks]) for comp in range(2)]
    (e0, l0), (e1, l1) = parts
    coef = lam * l0 / l1
    pd = [a - coef * b for a, b in zip(e0, e1)]
    return _pv(pd, vs) / l0


def _ctx_attn_kernel(lam_ref, sink_ref, bq, bk, bv, cq, misc, dq, dk, dv, yb_ref, yc_ref, yd_ref):
    for p in range(H_B // 2):
        sl = slice(p * LANES, (p + 1) * LANES)
        q = bq[:, sl] * ATTN_SCALE
        outs = []
        for half in range(2):
            es, l = _softmax_parts([_dot_nt(_select_half(q, half), bk[:, sl])])
            outs.append(_pv(es, [bv[:, sl]]) / l)
        yb_ref[:, sl] = jnp.where(_half_mask(q.shape, 0), outs[0], outs[1]).astype(yb_ref.dtype)

    kc = misc[:, 0:LANES]
    vc = misc[:, LANES:2 * LANES]
    for p in range(H_C // 2):
        sl = slice(p * LANES, (p + 1) * LANES)
        q = cq[:, sl] * ATTN_SCALE
        outs = []
        for half in range(2):
            head = 2 * p + half
            kv = head // (H_C // KV_C)
            qm = _select_half(q, half)
            if kv != half:
                qm = _swap_halves(qm)
            es, l = _softmax_parts([_dot_nt(qm, kc)], extra=sink_ref[head])
            o = _pv(es, [vc]) / l
            outs.append(o if kv == half else _swap_halves(o))
        yc_ref[:, sl] = jnp.where(_half_mask(q.shape, 0), outs[0], outs[1]).astype(yc_ref.dtype)

    lam = lam_ref[0]
    for h in range(H_D):
        sl = slice(h * LANES, (h + 1) * LANES)
        yd_ref[:, sl] = _diff_head(dq[:, sl] * ATTN_SCALE, [dk[:, sl]], [dv[:, sl]], lam)


def _ctx_attention(qkv, lam, sink, batch, seq):
    tile = lambda t: pl.BlockSpec((seq, COL_TILE), lambda b: (b, t))
    smem = pl.BlockSpec(memory_space=pltpu.SMEM)
    out = lambda dt: jax.ShapeDtypeStruct((batch * seq, BRANCH_W), dt)
    out_spec = pl.BlockSpec((seq, BRANCH_W), lambda b: (b, 0))
    return pl.pallas_call(
        _ctx_attn_kernel,
        out_shape=(out(BF16), out(BF16), out(F32)),
        grid=(batch,),
        in_specs=[smem, smem, tile(T_BQ), tile(T_BK), tile(T_BV), tile(T_CQ), tile(T_MISC),
                  tile(T_DQ), tile(T_DK), tile(T_DV)],
        out_specs=(out_spec, out_spec, out_spec),
        compiler_params=_cparams("parallel"),
        name="ctx_attention",
    )(lam, sink, *([qkv] * 8))


NAT_QROWS = 4
NAT_KROWS = 12
NAT_RI = 2 * NA_ROWS - 1


def _nat_kernel(q_ref, k_ref, v_ref, kc_ref, vc_ref, cb_ref, o_ref):
    rows = k_ref.shape[0] // GRID_W
    r0 = pl.program_id(2) * NAT_QROWS
    kb = jnp.clip(r0 - NA_ROWS // 2, 0, rows - NAT_KROWS)
    koff = pl.multiple_of(kb * GRID_W, GRID_W)
    nq = NAT_QROWS * GRID_W
    nk = NAT_KROWS * GRID_W
    q = q_ref[...] * ATTN_SCALE
    k = k_ref[pl.ds(koff, nk), :]
    v = v_ref[pl.ds(koff, nk), :]
    kc = kc_ref[0].astype(BF16)
    vc = vc_ref[0].astype(BF16)
    left = _half_mask((GRID_W, LANES), 0)
    outs = []
    for half in range(2):
        bias_rows = []
        for a in range(NAT_QROWS):
            r = r0 + a
            rs = jnp.clip(r - NA_ROWS // 2, 0, rows - NA_ROWS)
            blocks = []
            for ip in range(NAT_KROWS // 2):
                sides = []
                for side in range(2):
                    kr = kb + 2 * ip + side
                    ok = (kr >= rs) & (kr < rs + NA_ROWS)
                    ri = jnp.clip(kr - r + NA_ROWS - 1, 0, NAT_RI - 1)
                    sides.append(jnp.where(ok, cb_ref[half, ri], NEG_INF))
                blocks.append(jnp.where(left, sides[0], sides[1]))
            bias_rows.append(jnp.concatenate(blocks, axis=1))
        bias = jnp.concatenate(bias_rows, axis=0)
        qm = _select_half(q, half)
        s_nb = _dot_nt(qm, k) + bias
        s_ctx = _dot_nt(qm, kc)
        es, l = _softmax_parts([s_nb, s_ctx])
        outs.append(_pv(es, [v, vc]) / l)
    o_ref[...] = jnp.where(_half_mask((nq, LANES), 0), outs[0], outs[1]).astype(o_ref.dtype)


def _nat_col_table(rpb):
    qc = np.arange(GRID_W)[:, None]
    kc = np.arange(GRID_W)[None, :]
    cs = np.clip(qc - NA_COLS // 2, 0, GRID_W - NA_COLS)
    valid = (kc >= cs) & (kc < cs + NA_COLS)
    n_ci = 2 * NA_COLS - 1
    lo = GRID_W - 1 - (NA_COLS - 1)
    v = jnp.pad(rpb, ((0, 0), (0, 0), (lo, 2 * GRID_W - n_ci - lo)))
    flat = jnp.tile(v, (1, 1, GRID_W))[..., :GRID_W * (2 * GRID_W - 1)]
    toep = flat.reshape(rpb.shape[0], rpb.shape[1], GRID_W, 2 * GRID_W - 1)[..., GRID_W - 1:]
    tab = jnp.where(valid[None, None], toep, NEG_INF)
    return jnp.concatenate([tab, tab], axis=-1)


def _nat_latent(qkv, kc, vc, cb, l, batch, seq):
    nq = NAT_QROWS * GRID_W
    tpb = seq // nq
    past = kc.shape[2]
    return pl.pallas_call(
        _nat_kernel,
        out_shape=jax.ShapeDtypeStruct((batch * seq, BRANCH_W), BF16),
        grid=(batch, H_B // 2, tpb),
        in_specs=[
            pl.BlockSpec((nq, LANES), lambda b, p, i: (b * tpb + i, T_BQ * 4 + p)),
            pl.BlockSpec((seq, LANES), lambda b, p, i: (b, T_BK * 4 + p)),
            pl.BlockSpec((seq, LANES), lambda b, p, i: (b, T_BV * 4 + p)),
            pl.BlockSpec((1, None, past, LANES), lambda b, p, i: (b, l, 0, p)),
            pl.BlockSpec((1, None, past, LANES), lambda b, p, i: (b, l, 0, p)),
            pl.BlockSpec((2, NAT_RI, GRID_W, LANES), lambda b, p, i: (l * (H_B // 2) + p, 0, 0, 0)),
        ],
        out_specs=pl.BlockSpec((nq, LANES), lambda b, p, i: (b * tpb + i, p)),
        compiler_params=_cparams("parallel", "parallel", "parallel"),
        name="nat_latent",
    )(qkv, qkv, qkv, kc, vc, cb)


SWA_QBLK = 256
SWA_KBLK = SWA_QBLK + 2 * SWA_WINDOW


def _swa_kernel(sink_ref, q_ref, kv_ref, kc_ref, vc_ref, o_ref):
    seq = kv_ref.shape[0]
    p = pl.program_id(1)
    t0 = pl.program_id(2) * SWA_QBLK
    k0 = pl.multiple_of(jnp.clip(t0 - SWA_WINDOW, 0, seq - SWA_KBLK), SWA_WINDOW)
    q = q_ref[...] * ATTN_SCALE
    k = kv_ref[pl.ds(k0, SWA_KBLK), 0:LANES]
    v = kv_ref[pl.ds(k0, SWA_KBLK), LANES:2 * LANES]
    kc = kc_ref[0].astype(BF16)
    vc = vc_ref[0].astype(BF16)
    tq = t0 + lax.broadcasted_iota(jnp.int32, (SWA_QBLK, SWA_KBLK), 0)
    tk = k0 + lax.broadcasted_iota(jnp.int32, (SWA_QBLK, SWA_KBLK), 1)
    inside = jnp.abs(tq - tk) <= SWA_WINDOW
    kv_half = p // ((H_C // 2) // KV_C)
    outs = []
    for half in range(2):
        aligned = kv_half == half
        qm = _select_half(q, half)
        qm = jnp.where(aligned, qm, _swap_halves(qm))
        s_band = jnp.where(inside, _dot_nt(qm, k), NEG_INF)
        s_ctx = _dot_nt(qm, kc)
        es, l = _softmax_parts([s_band, s_ctx], extra=sink_ref[2 * p + half])
        o = _pv(es, [v, vc]) / l
        outs.append(jnp.where(aligned, o, _swap_halves(o)))
    o_ref[...] = jnp.where(_half_mask(q.shape, 0), outs[0], outs[1]).astype(o_ref.dtype)


def _swa_latent(qkv, kc, vc, sink, l, batch, seq):
    tpb = seq // SWA_QBLK
    past = kc.shape[2]
    return pl.pallas_call(
        _swa_kernel,
        out_shape=jax.ShapeDtypeStruct((batch * seq, BRANCH_W), BF16),
        grid=(batch, H_C // 2, tpb),
        in_specs=[
            pl.BlockSpec(memory_space=pltpu.SMEM),
            pl.BlockSpec((SWA_QBLK, LANES), lambda b, p, i: (b * tpb + i, T_CQ * 4 + p)),
            pl.BlockSpec((seq, COL_TILE), lambda b, p, i: (b, T_MISC)),
            pl.BlockSpec((1, None, past, LANES), lambda b, p, i: (b, l, 0, 0)),
            pl.BlockSpec((1, None, past, LANES), lambda b, p, i: (b, l, 0, 0)),
        ],
        out_specs=pl.BlockSpec((SWA_QBLK, LANES), lambda b, p, i: (b * tpb + i, p)),
        compiler_params=_cparams("parallel", "parallel", "parallel"),
        name="swa_latent",
    )(sink, qkv, qkv, kc, vc)


DIFF_QBLK = 256


def _diff_kernel(lam_ref, q_ref, k_ref, v_ref, kc_ref, vc_ref, o_ref):
    q = q_ref[...] * ATTN_SCALE
    ks = [k_ref[...], kc_ref[0].astype(BF16)]
    vs = [v_ref[...], vc_ref[0].astype(BF16)]
    o_ref[...] = _diff_head(q, ks, vs, lam_ref[0])


def _diff_latent(qkv, kc, vc, lam, l, batch, seq):
    tpb = seq // DIFF_QBLK
    past = kc.shape[2]
    return pl.pallas_call(
        _diff_kernel,
        out_shape=jax.ShapeDtypeStruct((batch * seq, BRANCH_W), F32),
        grid=(batch, H_D, tpb),
        in_specs=[
            pl.BlockSpec(memory_space=pltpu.SMEM),
            pl.BlockSpec((DIFF_QBLK, LANES), lambda b, h, i: (b * tpb + i, T_DQ * 4 + h)),
            pl.BlockSpec((seq, LANES), lambda b, h, i: (b, T_DK * 4 + h)),
            pl.BlockSpec((seq, LANES), lambda b, h, i: (b, T_DV * 4 + h)),
            pl.BlockSpec((1, None, past, LANES), lambda b, h, i: (b, l, 0, h)),
            pl.BlockSpec((1, None, past, LANES), lambda b, h, i: (b, l, 0, h)),
        ],
        out_specs=pl.BlockSpec((DIFF_QBLK, LANES), lambda b, h, i: (b * tpb + i, h)),
        compiler_params=_cparams("parallel", "parallel", "parallel"),
        name="diff_latent",
    )(lam, qkv, qkv, qkv, kc, vc)


def _rms_heads(y, g_ref, per_head_gain):
    outs = []
    for c in range(BRANCH_W // LANES):
        sl = slice(c * LANES, (c + 1) * LANES)
        g = g_ref[:, sl] if per_head_gain else g_ref[...]
        outs.append(_rms(y[:, sl], g))
    return jnp.concatenate(outs, axis=1)


def _first_lane(hit, lane_f):
    return jnp.min(jnp.where(hit, lane_f, float(LANES)), axis=1, keepdims=True)


def _top_group(logits):
    lane = lax.broadcasted_iota(jnp.int32, logits.shape, 1)
    gl = jnp.where(lane < N_GROUPS, logits, NEG_INF)
    return _first_lane(gl == jnp.max(gl, axis=1, keepdims=True), lane.astype(F32))


def _gate_weights(logits, group):
    lane = lax.broadcasted_iota(jnp.int32, logits.shape, 1)
    lane_f = lane.astype(F32)
    gl = jnp.where(lane < N_GROUPS, logits, NEG_INF)
    gmax = jnp.max(gl, axis=1, keepdims=True)
    g_logit = jnp.sum(jnp.where(lane_f == group, logits, 0.0), axis=1, keepdims=True)
    gw = jnp.exp(g_logit - gmax) / jnp.sum(jnp.exp(gl - gmax), axis=1, keepdims=True)
    e_lane = lane - N_GROUPS
    e_group = lax.shift_right_arithmetic(e_lane, EXP_PER_GROUP.bit_length() - 1).astype(F32)
    in_group = (e_lane >= 0) & (e_lane < N_EXPERTS) & (e_group == group)
    el = jnp.where(in_group, logits, NEG_INF)
    e1 = jnp.max(el, axis=1, keepdims=True)
    i1 = _first_lane(el == e1, lane_f)
    el2 = jnp.where(lane_f == i1, NEG_INF, el)
    e2 = jnp.max(el2, axis=1, keepdims=True)
    i2 = _first_lane(el2 == e2, lane_f)
    t = jnp.exp(e2 - e1)
    w1 = gw / (1.0 + t)
    return jnp.where(lane_f == i1, w1, jnp.where(lane_f == i2, w1 * t, 0.0))


def _merge_kernel(ya, yb, yc, yd, ao, gt0, gt1, gt2, gt3, x_ref, mod_ref, anorm, subln, nffn,
                  wbr, wout, wr, xo_ref, h2_ref, group_ref, *, yd_scale):
    a = _sigmoid(ao[...].astype(F32)) * _rms_heads(ya[...], anorm, True)
    d = _rms_heads(yd[...], subln, False) * yd_scale
    branches = (a, yb[...], yc[...], d)
    mix = None
    for n, (br, gt) in enumerate(zip(branches, (gt0, gt1, gt2, gt3))):
        term = _sigmoid(gt[...].astype(F32)) * _dot(br.astype(BF16), wbr[n])
        mix = term if mix is None else mix + term
    out = _dot(mix.astype(BF16), wout[...])
    xn = x_ref[...] + mod_ref[0, 2:3, :] * out
    xo_ref[...] = xn
    h2 = _rms(xn, nffn[...]) * (1.0 + mod_ref[0, 4:5, :]) + mod_ref[0, 3:4, :]
    h2_ref[...] = h2
    group_ref[...] = jnp.broadcast_to(_top_group(_dot(h2.astype(BF16), wr[...])), group_ref.shape)


def _merge(ys, qkv, x, mod3, a_norm, subln, norm_ffn, w_branch, w_out, w_route, lam_init, l):
    n = x.shape[0]
    tm = 512
    tiles_per_mod = n // mod3.shape[0] // tm
    y_spec = pl.BlockSpec((tm, BRANCH_W), lambda i: (i, 0))
    gt_spec = lambda k: pl.BlockSpec((tm, D_MODEL), lambda i: (i, T_GT * COL_TILE // D_MODEL + k))
    full = lambda shape: pl.BlockSpec(shape, lambda i: (0,) * len(shape))
    layer = lambda shape: pl.BlockSpec((None,) + shape, lambda i: (l,) + (0,) * len(shape))
    x_spec = pl.BlockSpec((tm, D_MODEL), lambda i: (i, 0))
    return pl.pallas_call(
        functools.partial(_merge_kernel, yd_scale=1.0 - lam_init),
        out_shape=(
            jax.ShapeDtypeStruct((n, D_MODEL), F32),
            jax.ShapeDtypeStruct((n, D_MODEL), F32),
            jax.ShapeDtypeStruct((n, LANES), F32),
        ),
        grid=(n // tm,),
        in_specs=[
            y_spec, y_spec, y_spec, y_spec,
            pl.BlockSpec((tm, COL_TILE), lambda i: (i, T_AO)),
            gt_spec(0), gt_spec(1), gt_spec(2), gt_spec(3),
            x_spec,
            pl.BlockSpec((1, 6, D_MODEL), lambda i: (i // tiles_per_mod, 0, 0)),
            full((1, BRANCH_W)), full((1, LANES)), full((1, D_MODEL)),
            layer((N_BRANCH, BRANCH_W, D_MODEL)), layer((D_MODEL, D_MODEL)), layer((D_MODEL, LANES)),
        ],
        out_specs=(x_spec, x_spec, pl.BlockSpec((tm, LANES), lambda i: (i, 0))),
        compiler_params=_cparams("parallel"),
        name="merge_route",
    )(*ys, qkv, qkv, qkv, qkv, qkv, x, mod3, a_norm.reshape(1, BRANCH_W), subln.reshape(1, LANES),
      norm_ffn.reshape(1, D_MODEL), w_branch, w_out, w_route)


MOE_TILE = 512
ROW_DMA_UNROLL = 16


def _row_copies(src_row, dst_row, sem, n):
    def copy(i):
        return pltpu.make_async_copy(src_row(i), dst_row(i), sem)

    def start(i, c):
        copy(i).start()
        return c

    def wait(i, c):
        copy(i).wait()
        return c

    lax.fori_loop(0, n, start, 0, unroll=ROW_DMA_UNROLL)
    lax.fori_loop(0, n, wait, 0, unroll=ROW_DMA_UNROLL)


def _moe_scatter_kernel(slot_ref, h_ref, init_ref, sorted_ref, sem):
    del init_ref
    base = pl.program_id(0) * MOE_TILE
    _row_copies(lambda i: h_ref.at[pl.ds(i, 1), :],
                lambda i: sorted_ref.at[pl.ds(slot_ref[base + i], 1), :], sem, MOE_TILE)


def _moe_group_kernel(tile_group_ref, n_used_ref, h_ref, wr_ref, wg_ref, wu_ref, wd_ref, o_ref):
    t = pl.program_id(0)

    @pl.when(t < n_used_ref[0])
    def _():
        group = tile_group_ref[t]
        h = h_ref[...].astype(BF16)
        gates = _gate_weights(_dot(h, wr_ref[...]), group.astype(F32))
        lane = lax.broadcasted_iota(jnp.int32, gates.shape, 1)
        acc = None
        for e in range(EXP_PER_GROUP):
            g = jnp.sum(jnp.where(lane == N_GROUPS + group * EXP_PER_GROUP + e, gates, 0.0), axis=1, keepdims=True)
            hg = _dot(h, wg_ref[0, e])
            hid = hg * _sigmoid(hg) * _dot(h, wu_ref[0, e]) * g
            term = _dot(hid.astype(BF16), wd_ref[0, e])
            acc = term if acc is None else acc + term
        o_ref[...] = acc

    @pl.when(t >= n_used_ref[0])
    def _():
        o_ref[...] = jnp.zeros_like(o_ref)


def _moe_gather_kernel(slot_ref, y_ref, x_ref, mod_ref, nf_ref, o_ref, buf, sem, *, final_norm):
    t = pl.program_id(0)
    nt = pl.num_programs(0)

    def copy(tile, b, i):
        return pltpu.make_async_copy(y_ref.at[pl.ds(slot_ref[tile * MOE_TILE + i], 1), :],
                                     buf.at[b, pl.ds(i, 1), :], sem.at[b])

    def start_tile(tile, b):
        def start(i, c):
            copy(tile, b, i).start()
            return c

        lax.fori_loop(0, MOE_TILE, start, 0, unroll=ROW_DMA_UNROLL)

    cur = t % 2

    @pl.when(t == 0)
    def _():
        start_tile(0, 0)

    @pl.when(t + 1 < nt)
    def _():
        start_tile(t + 1, 1 - cur)

    def wait(i, c):
        copy(t, cur, i).wait()
        return c

    lax.fori_loop(0, MOE_TILE, wait, 0, unroll=ROW_DMA_UNROLL)
    y = x_ref[...] + mod_ref[0, 5:6, :] * buf[cur]
    o_ref[...] = _rms(y, nf_ref[...]) if final_norm else y


def _moe(h2, group, x, mod3, w_route, wg, wu, wd, norm_final, final_norm, l):
    n = x.shape[0]
    tm = MOE_TILE
    n_tiles = n // tm + N_GROUPS
    onehot = (group[:, None] == jnp.arange(N_GROUPS, dtype=jnp.int32)[None]).astype(jnp.int32)
    csum = jnp.cumsum(onehot, axis=0)
    counts = csum[-1]
    rank = jnp.sum((csum - onehot) * onehot, axis=1)
    group_tiles = (counts + tm - 1) // tm
    tile_end = jnp.cumsum(group_tiles)
    group_start = (tile_end - group_tiles) * tm
    slot = (jnp.sum(onehot * group_start[None], axis=1) + rank).astype(jnp.int32)
    tile_group = jnp.minimum(jnp.sum(jnp.arange(n_tiles, dtype=jnp.int32)[:, None] >= tile_end[None], axis=1),
                             N_GROUPS - 1).astype(jnp.int32)
    n_used = tile_end[-1:].astype(jnp.int32)

    sorted_h = pl.pallas_call(
        _moe_scatter_kernel,
        out_shape=jax.ShapeDtypeStruct((n_tiles * tm, D_MODEL), F32),
        grid_spec=pltpu.PrefetchScalarGridSpec(
            num_scalar_prefetch=1, grid=(n // tm,),
            in_specs=[pl.BlockSpec((tm, D_MODEL), lambda i, slot: (i, 0)), pl.BlockSpec(memory_space=pl.ANY)],
            out_specs=pl.BlockSpec(memory_space=pl.ANY),
            scratch_shapes=[pltpu.SemaphoreType.DMA],
        ),
        input_output_aliases={2: 0},
        compiler_params=_cparams("arbitrary"),
        name="moe_scatter",
    )(slot, h2, jnp.zeros((n_tiles * tm, D_MODEL), F32))

    grp = lambda shape: pl.BlockSpec((None, 1) + shape, lambda t, tg, nu: (l, tg[t], 0, 0, 0))
    sorted_y = pl.pallas_call(
        _moe_group_kernel,
        out_shape=jax.ShapeDtypeStruct((n_tiles * tm, D_MODEL), F32),
        grid_spec=pltpu.PrefetchScalarGridSpec(
            num_scalar_prefetch=2, grid=(n_tiles,),
            in_specs=[
                pl.BlockSpec((tm, D_MODEL), lambda t, tg, nu: (t, 0)),
                pl.BlockSpec((None, D_MODEL, LANES), lambda t, tg, nu: (l, 0, 0)),
                grp((EXP_PER_GROUP, D_MODEL, D_EXPERT)), grp((EXP_PER_GROUP, D_MODEL, D_EXPERT)),
                grp((EXP_PER_GROUP, D_EXPERT, D_MODEL)),
            ],
            out_specs=pl.BlockSpec((tm, D_MODEL), lambda t, tg, nu: (t, 0)),
        ),
        compiler_params=_cparams("arbitrary"),
        name="moe_experts",
    )(tile_group, n_used, sorted_h, w_route, wg, wu, wd)

    tiles_per_mod = n // mod3.shape[0] // tm
    row = pl.BlockSpec((tm, D_MODEL), lambda i, slot: (i, 0))
    return pl.pallas_call(
        functools.partial(_moe_gather_kernel, final_norm=final_norm),
        out_shape=jax.ShapeDtypeStruct((n, D_MODEL), F32),
        grid_spec=pltpu.PrefetchScalarGridSpec(
            num_scalar_prefetch=1, grid=(n // tm,),
            in_specs=[
                pl.BlockSpec(memory_space=pl.ANY), row,
                pl.BlockSpec((1, 6, D_MODEL), lambda i, slot: (i // tiles_per_mod, 0, 0)),
                pl.BlockSpec((1, D_MODEL), lambda i, slot: (0, 0)),
            ],
            out_specs=row,
            scratch_shapes=[pltpu.VMEM((2, tm, D_MODEL), F32), pltpu.SemaphoreType.DMA((2,))],
        ),
        compiler_params=_cparams("arbitrary"),
        name="moe_gather",
    )(slot, sorted_y, x, mod3, norm_final.reshape(1, D_MODEL))


def _reorder_w_in(w):
    pad = jnp.zeros((D_MODEL, COL_TILE - 2 * LANES - 16), w.dtype)
    cols = [w[:, 0:2048], w[:, 2064:4112], w[:, 4368:5904], w[:, 4112:4368], w[:, 2048:2064], pad, w[:, 5904:]]
    return jnp.concatenate(cols, axis=1).astype(BF16)


def _rope_tables(seq):
    nf = DH_C // 4
    inv = ROPE_BASE ** (-jnp.arange(nf, dtype=F32) / nf)
    t = jnp.arange(seq)
    ang_r = (t // GRID_W).astype(F32)[:, None] * inv[None, :]
    ang_c = (t % GRID_W).astype(F32)[:, None] * inv[None, :]
    cos = jnp.concatenate([jnp.cos(ang_r)] * 2 + [jnp.cos(ang_c)] * 2, axis=1)
    sin = jnp.concatenate([-jnp.sin(ang_r), jnp.sin(ang_r), -jnp.sin(ang_c), jnp.sin(ang_c)], axis=1)
    return jnp.tile(cos, (1, 2)), jnp.tile(sin, (1, 2))


def _layer(l, x, mod3, seq, lat, p):
    batch = x.shape[0] // seq
    lam_init = 0.8 - 0.6 * math.exp(-0.3 * l)
    ctx = lat is None
    proj = _qkv_proj(x, p["norm_mix"][l], mod3, p["w_in"][l], None if ctx else p["rope"], ctx,
                     tm=1024 if ctx else 2048)
    qkv, gates_raw = proj[:2]
    gates = gates_raw[:, :16] + jnp.concatenate([p["i_bias"][l].reshape(-1), p["f_bias"][l].reshape(-1)])[None]
    lam = (jnp.exp(jnp.sum(p["lq1"][l] * p["lk1"][l])) - jnp.exp(jnp.sum(p["lq2"][l] * p["lk2"][l]))
           + lam_init).reshape(1).astype(F32)
    sink = p["sink"][l]
    new = None
    if ctx:
        C0 = jnp.zeros((batch, 1, 2, H_A, DH_A, DH_A), F32)
        n0 = jnp.zeros((batch, 2, H_A, DH_A), F32)
        m0 = jnp.zeros((batch, 2, H_A), F32)
        ya, Cs, ns, ms = _mlstm(qkv, gates, C0, 0, n0, m0, batch, seq)
        yb, yc, yd = _ctx_attention(qkv, lam, sink, batch, seq)
        bk, bv, dk, dv, ckv = proj[2:]
        new = (bk.reshape(batch, seq, H_B, DH_B), bv.reshape(batch, seq, H_B, DH_B),
               ckv[:, :LANES].reshape(batch, seq, KV_C, DH_C), ckv[:, LANES:].reshape(batch, seq, KV_C, DH_C),
               dk.reshape(batch, seq, H_D, 2 * DH_D), dv.reshape(batch, seq, H_D, 2 * DH_D), Cs, ns, ms)
    else:
        nat_k, nat_v, swa_k, swa_v, diff_k, diff_v, C0, n0, m0 = lat
        flat = lambda a: a.reshape(a.shape[:3] + (-1,))
        ya, _, _, _ = _mlstm(qkv, gates, C0, l, n0[:, l], m0[:, l], batch, seq)
        yb = _nat_latent(qkv, flat(nat_k), flat(nat_v), p["nat_table"], l, batch, seq)
        yc = _swa_latent(qkv, flat(swa_k), flat(swa_v), sink, l, batch, seq)
        yd = _diff_latent(qkv, flat(diff_k), flat(diff_v), lam, l, batch, seq)
    xn, h2, group = _merge((ya, yb, yc, yd), qkv, x, mod3, p["a_norm"][l], p["subln"][l],
                           p["norm_ffn"][l], p["w_branch"], p["w_out"], p["w_route"], lam_init, l)
    y = _moe(h2, group[:, 0].astype(jnp.int32), xn, mod3, p["w_route"], p["wg"], p["wu"], p["wd"],
             p["norm_final"], final_norm=(l == DEPTH - 1), l=l)
    return y, new


def kernel(x_prompt, x_sample, cache_nat_k, cache_nat_v, cache_swa_k, cache_swa_v, cache_diff_k, cache_diff_v, state_mlstm_C, state_mlstm_n, state_mlstm_m, c, c_ctx, norm_mix, norm_ffn, norm_final, w_mod, b_mod, w_in, mlstm_i_bias, mlstm_f_bias, mlstm_norm, nat_rpb, swa_sink, diff_lq1, diff_lk1, diff_lq2, diff_lk2, diff_subln, w_branch, w_out, router_group, router_expert, w_exp_gate, w_exp_up, w_exp_down):
    b_ctx, s_ctx, _ = x_prompt.shape
    b_lat, s_lat, _ = x_sample.shape
    cond8 = jnp.concatenate([c_ctx[None], c, jnp.zeros((8 - 1 - b_lat, D_MODEL), F32)], axis=0)
    mod = _modulation(cond8, w_mod, b_mod).reshape(DEPTH, 8, 6, D_MODEL)
    route_pad = jnp.zeros((DEPTH, D_MODEL, LANES - N_GROUPS - N_EXPERTS), F32)
    p = dict(
        norm_mix=norm_mix, norm_ffn=norm_ffn, norm_final=norm_final,
        w_in=[_reorder_w_in(w_in[l]) for l in range(DEPTH)],
        i_bias=mlstm_i_bias, f_bias=mlstm_f_bias, a_norm=mlstm_norm, sink=swa_sink,
        nat_table=_nat_col_table(nat_rpb.reshape((DEPTH * H_B,) + nat_rpb.shape[2:])),
        lq1=diff_lq1, lk1=diff_lk1, lq2=diff_lq2, lk2=diff_lk2, subln=diff_subln,
        w_branch=w_branch.astype(BF16), w_out=w_out.astype(BF16),
        w_route=jnp.concatenate([router_group, router_expert, route_pad], axis=-1).astype(BF16),
        wg=w_exp_gate.astype(BF16).reshape(DEPTH, N_GROUPS, EXP_PER_GROUP, D_MODEL, D_EXPERT),
        wu=w_exp_up.astype(BF16).reshape(DEPTH, N_GROUPS, EXP_PER_GROUP, D_MODEL, D_EXPERT),
        wd=w_exp_down.astype(BF16).reshape(DEPTH, N_GROUPS, EXP_PER_GROUP, D_EXPERT, D_MODEL),
        rope=_rope_tables(s_lat),
    )
    yp = x_prompt.reshape(b_ctx * s_ctx, D_MODEL)
    ys = x_sample.reshape(b_lat * s_lat, D_MODEL)
    news = []
    for l in range(DEPTH):
        yp, new = _layer(l, yp, mod[l, 0:1], s_ctx, None, p)
        news.append(new)
        lat = (cache_nat_k, cache_nat_v, cache_swa_k, cache_swa_v, cache_diff_k, cache_diff_v,
               state_mlstm_C, state_mlstm_n, state_mlstm_m)
        ys, _ = _layer(l, ys, mod[l, 1:1 + b_lat], s_lat, lat, p)
    def stack_layers(parts):
        flat = [a.reshape(a.shape[0], 1, -1) for a in parts]
        return jnp.concatenate(flat, axis=1).reshape((parts[0].shape[0], DEPTH) + parts[0].shape[1:])

    stacked = tuple(stack_layers([news[l][k] for l in range(DEPTH)]) for k in range(9))
    return (yp.reshape(b_ctx, s_ctx, D_MODEL), ys.reshape(b_lat, s_lat, D_MODEL)) + stacked
```

```python
import functools
import math

import numpy as np
import jax
import jax.numpy as jnp
from jax import lax
from jax.experimental import pallas as pl
from jax.experimental.pallas import tpu as pltpu

F32 = jnp.float32
BF16 = jnp.bfloat16

D_MODEL = 1024
DEPTH = 2
GRID_W = 64
BRANCH_W = 512
N_BRANCH = 4
H_A, DH_A, MLSTM_CHUNK = 4, 128, 128
H_B, DH_B = 8, 64
NA_ROWS, NA_COLS = 8, 16
H_C, KV_C, DH_C = 8, 2, 64
SWA_WINDOW = 128
H_D, DH_D = 4, 64
N_GROUPS, EXP_PER_GROUP = 4, 4
N_EXPERTS = N_GROUPS * EXP_PER_GROUP
D_EXPERT = 256
ROPE_BASE = 10000.0
EPS = 1e-6

LANES = 128
HALF = 64
COL_TILE = 512
N_COL_TILES = 20
N_PROJ = COL_TILE * N_COL_TILES
T_AQ, T_AK, T_AV, T_AO, T_BQ, T_BK, T_BV, T_CQ, T_DQ, T_DK, T_DV, T_MISC, T_GT = range(13)
MISC_GATE_OFF = 256
VMEM_LIMIT = 56 * 1024 * 1024
NEG_INF = float("-inf")

_NT = (((1,), (1,)), ((), ()))
_TN = (((0,), (0,)), ((), ()))


def _cparams(*sem):
    return pltpu.CompilerParams(dimension_semantics=sem, vmem_limit_bytes=VMEM_LIMIT)


def _dot(a, b):
    return jnp.dot(a, b, preferred_element_type=F32)


def _dot_nt(a, b):
    return lax.dot_general(a, b, _NT, preferred_element_type=F32)


def _dot_tn(a, b):
    return lax.dot_general(a, b, _TN, preferred_element_type=F32)


def _rms(x, g):
    return x * lax.rsqrt(jnp.mean(x * x, axis=-1, keepdims=True) + EPS) * g


def _sigmoid(x):
    return 0.5 * jnp.tanh(0.5 * x) + 0.5


def _mod_kernel(c_ref, w_ref, b_ref, o_ref):
    c = c_ref[...]
    s = (c * _sigmoid(c)).astype(BF16)
    o_ref[0] = _dot(s, w_ref[0].astype(BF16)) + b_ref[0]


def _modulation(cond8, w_mod, b_mod):
    tn = 1536
    n_out = 6 * D_MODEL
    return pl.pallas_call(
        _mod_kernel,
        out_shape=jax.ShapeDtypeStruct((DEPTH, 8, n_out), F32),
        grid=(DEPTH, n_out // tn),
        in_specs=[
            pl.BlockSpec((8, D_MODEL), lambda l, j: (0, 0)),
            pl.BlockSpec((1, D_MODEL, tn), lambda l, j: (l, 0, j)),
            pl.BlockSpec((1, 1, tn), lambda l, j: (l, 0, j)),
        ],
        out_specs=pl.BlockSpec((1, 8, tn), lambda l, j: (l, 0, j)),
        compiler_params=_cparams("parallel", "parallel"),
        name="modulation",
    )(cond8, w_mod, b_mod.reshape(DEPTH, 1, n_out))


CACHE_TILES = (T_BK, T_BV, T_DK, T_DV)
QKV_ROW_CHUNK = 512


def _rope128(x, cos, sin):
    lane = lax.broadcasted_iota(jnp.int32, x.shape, 1)
    partner = jnp.where((lane % 32) < 16, pltpu.roll(x, LANES - 16, 1), pltpu.roll(x, 16, 1))
    return x * cos + partner * sin


def _qkv_kernel(x_ref, nw_ref, mod_ref, w_ref, *rest, rope, cache):
    rest = list(rest)
    if rope:
        cos_ref, sin_ref = rest[:2]
        rest = rest[2:]
    o_ref, g_ref = rest[:2]
    cache_refs = rest[2:-1]
    h_scr = rest[-1]
    j = pl.program_id(1)

    @pl.when(j == 0)
    def _():
        h = _rms(x_ref[...], nw_ref[...]) * (1.0 + mod_ref[0, 1:2, :]) + mod_ref[0, 0:1, :]
        h_scr[...] = h.astype(BF16)

    def tile_kind(cond, store):
        @pl.when(cond)
        def _():
            for r in range(h_scr.shape[0] // QKV_ROW_CHUNK):
                rows = slice(r * QKV_ROW_CHUNK, (r + 1) * QKV_ROW_CHUNK)
                store(rows, _dot(h_scr[rows, :], w_ref[...]))

    def roped(rows, acc, c):
        sl = slice(c * LANES, (c + 1) * LANES)
        return _rope128(acc[:, sl], cos_ref[rows, :], sin_ref[rows, :]).astype(BF16)

    def store_plain(rows, acc):
        o_ref[rows, :] = acc.astype(BF16)

    def store_ak(rows, acc):
        o_ref[rows, :] = (acc * (DH_A ** -0.5)).astype(BF16)

    def store_misc(rows, acc):
        o_ref[rows, :] = acc.astype(BF16)
        if rope:
            o_ref[rows, :LANES] = roped(rows, acc, 0)
        g_ref[rows, :] = acc[:, MISC_GATE_OFF:MISC_GATE_OFF + LANES]
        if cache:
            cache_refs[-1][rows, :] = acc[:, :2 * LANES]

    def store_rope(rows, acc):
        for c in range(COL_TILE // LANES):
            o_ref[rows, c * LANES:(c + 1) * LANES] = roped(rows, acc, c)

    special = (j == T_AK) | (j == T_MISC)
    tile_kind(j == T_AK, store_ak)
    tile_kind(j == T_MISC, store_misc)
    if rope:
        is_rope = (j == T_CQ) | (j == T_DQ) | (j == T_DK)
        special = special | is_rope
        tile_kind(is_rope, store_rope)
    if cache:
        for t, ref in zip(CACHE_TILES, cache_refs[:-1]):
            def store_cached(rows, acc, ref=ref):
                o_ref[rows, :] = acc.astype(BF16)
                ref[rows, :] = acc
            special = special | (j == t)
            tile_kind(j == t, store_cached)
    tile_kind(jnp.logical_not(special), store_plain)


def _qkv_proj(x, norm_w, mod3, w_in_r, rope_tabs, cache, tm):
    n = x.shape[0]
    tiles_per_mod = n // mod3.shape[0] // tm
    rope = rope_tabs is not None
    in_specs = [
        pl.BlockSpec((tm, D_MODEL), lambda i, j: (i, 0)),
        pl.BlockSpec((1, D_MODEL), lambda i, j: (0, 0)),
        pl.BlockSpec((1, 6, D_MODEL), lambda i, j: (i // tiles_per_mod, 0, 0)),
        pl.BlockSpec((D_MODEL, COL_TILE), lambda i, j: (0, j)),
    ]
    args = [x, norm_w.reshape(1, D_MODEL), mod3, w_in_r]
    if rope:
        tiles_per_seq = rope_tabs[0].shape[0] // tm
        in_specs += [pl.BlockSpec((tm, LANES), lambda i, j: (i % tiles_per_seq, 0))] * 2
        args += list(rope_tabs)
    row = lambda w: pl.BlockSpec((tm, w), lambda i, j: (i, 0))
    out_shape = [jax.ShapeDtypeStruct((n, N_PROJ), BF16), jax.ShapeDtypeStruct((n, LANES), F32)]
    out_specs = [pl.BlockSpec((tm, COL_TILE), lambda i, j: (i, j)), row(LANES)]
    if cache:
        out_shape += [jax.ShapeDtypeStruct((n, COL_TILE), F32)] * len(CACHE_TILES)
        out_shape += [jax.ShapeDtypeStruct((n, 2 * LANES), F32)]
        out_specs += [row(COL_TILE)] * len(CACHE_TILES) + [row(2 * LANES)]
    return pl.pallas_call(
        functools.partial(_qkv_kernel, rope=rope, cache=cache),
        out_shape=tuple(out_shape),
        grid=(n // tm, N_COL_TILES),
        in_specs=in_specs,
        out_specs=tuple(out_specs),
        scratch_shapes=[pltpu.VMEM((tm, D_MODEL), BF16)],
        compiler_params=_cparams("parallel", "arbitrary"),
        name="qkv_rope" if rope else "qkv",
    )(*args)


def _log_sigmoid(x):
    return jnp.minimum(x, 0.0) - jnp.log1p(jnp.exp(-jnp.abs(x)))


def _mlstm_chunk(c, h, d, state, q_ref, k_ref, v_ref, gr_ref, gc_ref, out_ref):
    L = MLSTM_CHUNK
    C, n, m = state
    row = lax.broadcasted_iota(jnp.int32, (L, L), 0)
    col = lax.broadcasted_iota(jnp.int32, (L, L), 1)
    tri = (col <= row) if d == 0 else (col >= row)
    tri_t = (row <= col) if d == 0 else (row >= col)
    off = pl.multiple_of(c * L, L)
    hs = slice(h * DH_A, (h + 1) * DH_A)
    qb = q_ref[pl.ds(off, L), hs]
    kb = k_ref[pl.ds(off, L), hs]
    vb = v_ref[pl.ds(off, L), hs]
    qx, kx, vx = qb.astype(F32), kb.astype(F32), vb.astype(F32)
    g_rows = gr_ref[h, c]
    g_cols = gc_ref[h, c]
    ig_r = g_rows[2 * d:2 * d + 1, :]
    lf_r = _log_sigmoid(g_rows[2 * d + 1:2 * d + 2, :])
    ig_c = g_cols[:, 2 * d:2 * d + 1]
    lf_c = _log_sigmoid(g_cols[:, 2 * d + 1:2 * d + 2])
    b_c = jnp.sum(jnp.where(tri, lf_r, 0.0), axis=1, keepdims=True)
    b_r = jnp.sum(jnp.where(tri_t, lf_c, 0.0), axis=0, keepdims=True)
    qk = _dot_nt(qb, kb)
    qc = _dot_nt(qb, C.astype(BF16))
    yield
    logd = jnp.where(tri, b_c - b_r + ig_r, NEG_INF)
    inter = b_c + m
    m_t = jnp.maximum(inter, jnp.max(logd, axis=1, keepdims=True))
    yield
    w = jnp.exp(logd - m_t)
    a = jnp.exp(inter - m_t)
    s = qk * w
    num = a * qc + _dot(s.astype(BF16), vb)
    den = a * jnp.sum(qx * n, axis=1, keepdims=True) + jnp.sum(s, axis=1, keepdims=True)
    yield
    out_ref[pl.ds(off, L), hs] = num / jnp.maximum(jnp.abs(den), jnp.exp(-m_t))
    b_tot = jnp.sum(lf_r, axis=1, keepdims=True)
    g_c = b_tot - b_c + ig_c
    m_new = jnp.maximum(b_tot + m, jnp.max(g_c, axis=0, keepdims=True))
    yield
    decay = jnp.exp(b_tot + m - m_new)
    wk = jnp.exp(g_c - m_new)
    C_new = decay * C + _dot_tn((wk * vx).astype(BF16), kb)
    n_new = decay * n + jnp.sum(wk * kx, axis=0, keepdims=True)
    return C_new, n_new, m_new


def _run_staged(gens):
    results = [None] * len(gens)
    live = list(range(len(gens)))
    while live:
        for i in list(live):
            try:
                next(gens[i])
            except StopIteration as stop:
                results[i] = stop.value
                live.remove(i)
    return results


def _mlstm_kernel(q_ref, k_ref, v_ref, gr_ref, gc_ref, c0_ref, n0_ref, m0_ref,
                  h_ref, cs_ref, ns_ref, ms_ref, hb_scr, *, nc):
    chains = [(h, d) for h in range(H_A) for d in range(2)]

    def body(ci, states):
        gens = [_mlstm_chunk(ci if d == 0 else nc - 1 - ci, h, d, state, q_ref, k_ref, v_ref, gr_ref, gc_ref,
                             h_ref if d == 0 else hb_scr) for (h, d), state in zip(chains, states)]
        return tuple(_run_staged(gens))

    init = tuple((c0_ref[0, d, h], n0_ref[0, h, d:d + 1, :], m0_ref[0, h, d:d + 1, 0:1]) for h, d in chains)
    final = lax.fori_loop(0, nc, body, init)
    for (h, d), (C, n, m) in zip(chains, final):
        cs_ref[0, d, h] = C
        ns_ref[0, h, d:d + 1, :] = n
        ms_ref[0, h, d:d + 1, :] = jnp.broadcast_to(m, (1, LANES))
    h_ref[...] += hb_scr[...]


def _mlstm(qkv, gates, C0, l, n0, m0, batch, seq):
    nc = seq // MLSTM_CHUNK
    L = MLSTM_CHUNK
    ig = gates[:, :8].reshape(batch, nc, L, 2, H_A)
    fg = gates[:, 8:].reshape(batch, nc, L, 2, H_A)
    g4 = jnp.stack([ig[..., 0, :], fg[..., 0, :], ig[..., 1, :], fg[..., 1, :]], axis=-1)
    g_cols = jnp.transpose(g4, (3, 0, 1, 2, 4)).reshape(H_A, batch * nc, L, 4)
    g_rows = jnp.transpose(g4, (3, 0, 1, 4, 2)).reshape(H_A, batch * nc, 4, L)
    n0t = jnp.transpose(n0, (0, 2, 1, 3))
    m0t = jnp.broadcast_to(jnp.transpose(m0, (0, 2, 1))[..., None], (batch, H_A, 2, LANES))
    qkv_spec = lambda tile: pl.BlockSpec((seq, COL_TILE), lambda b: (b, tile))
    c_spec = pl.BlockSpec((1, 2, H_A, DH_A, DH_A), lambda b: (b, 0, 0, 0, 0))
    c0_spec = pl.BlockSpec((1, None, 2, H_A, DH_A, DH_A), lambda b: (b, l, 0, 0, 0, 0))
    state_spec = pl.BlockSpec((1, H_A, 2, LANES), lambda b: (b, 0, 0, 0))
    h, Cs, ns, ms = pl.pallas_call(
        functools.partial(_mlstm_kernel, nc=nc),
        out_shape=(
            jax.ShapeDtypeStruct((batch * seq, H_A * DH_A), F32),
            jax.ShapeDtypeStruct((batch, 2, H_A, DH_A, DH_A), F32),
            jax.ShapeDtypeStruct((batch, H_A, 2, DH_A), F32),
            jax.ShapeDtypeStruct((batch, H_A, 2, LANES), F32),
        ),
        grid=(batch,),
        in_specs=[
            qkv_spec(T_AQ), qkv_spec(T_AK), qkv_spec(T_AV),
            pl.BlockSpec((H_A, nc, 4, L), lambda b: (0, b, 0, 0)),
            pl.BlockSpec((H_A, nc, L, 4), lambda b: (0, b, 0, 0)),
            c0_spec, state_spec, state_spec,
        ],
        out_specs=(pl.BlockSpec((seq, H_A * DH_A), lambda b: (b, 0)), c_spec, state_spec, state_spec),
        scratch_shapes=[pltpu.VMEM((seq, H_A * DH_A), F32)],
        compiler_params=_cparams("parallel"),
        name="mlstm",
    )(qkv, qkv, qkv, g_rows, g_cols, C0, n0t, m0t)
    return h, Cs, jnp.transpose(ns, (0, 2, 1, 3)), jnp.transpose(ms[..., 0], (0, 2, 1))


ATTN_SCALE = DH_B ** -0.5


def _half_mask(shape, half):
    lane = lax.broadcasted_iota(jnp.int32, shape, 1)
    return (lane >= HALF) if half else (lane < HALF)


def _select_half(q, half):
    return jnp.where(_half_mask(q.shape, half), q, jnp.zeros_like(q))


def _swap_halves(x):
    return pltpu.roll(x.astype(F32), HALF, 1).astype(x.dtype)


def _with_ones(v):
    return jnp.concatenate([v, jnp.ones_like(v)], axis=1)


def _softmax_pv(scores, vs, extra=None):
    m = functools.reduce(jnp.maximum, [jnp.max(s, axis=1, keepdims=True) for s in scores])
    if extra is not None:
        m = jnp.maximum(m, extra)
    acc = functools.reduce(lambda x, y: x + y,
                           [_dot(jnp.exp((s - m).astype(BF16)), v) for s, v in zip(scores, vs)])
    denom = acc[:, LANES:]
    if extra is not None:
        denom = denom + jnp.exp(extra - m)
    return acc[:, :LANES] / denom


def _diff_head(q, ks, vs, lam):
    scores = [[_dot_nt(_select_half(q, comp), k) for k in ks] for comp in range(2)]
    o1, o2 = [_softmax_pv(s, vs) for s in scores]
    return o1 - lam * o2


def _ctx_attn_kernel(lam_ref, sink_ref, bq, bk, bv, cq, misc, dq, dk, dv, yb_ref, yc_ref, yd_ref):
    pairs = [slice(p * LANES, (p + 1) * LANES) for p in range(H_B // 2)]
    scores = [[[_dot_nt(_select_half(bq[:, sl] * ATTN_SCALE, half), bk[:, sl])] for half in range(2)] for sl in pairs]
    for sl, pair_scores in zip(pairs, scores):
        v = _with_ones(bv[:, sl])
        outs = [_softmax_pv(s, [v]) for s in pair_scores]
        yb_ref[:, sl] = jnp.where(_half_mask(outs[0].shape, 0), outs[0], outs[1]).astype(yb_ref.dtype)

    kc = misc[:, 0:LANES]
    vc = _with_ones(misc[:, LANES:2 * LANES])
    kv_of = lambda head: head // (H_C // KV_C)
    scores = []
    for p, sl in enumerate(pairs):
        pair_scores = []
        for half in range(2):
            qm = _select_half(cq[:, sl] * ATTN_SCALE, half)
            if kv_of(2 * p + half) != half:
                qm = _swap_halves(qm)
            pair_scores.append([_dot_nt(qm, kc)])
        scores.append(pair_scores)
    for p, (sl, pair_scores) in enumerate(zip(pairs, scores)):
        outs = []
        for half, s in enumerate(pair_scores):
            o = _softmax_pv(s, [vc], extra=sink_ref[2 * p + half])
            outs.append(o if kv_of(2 * p + half) == half else _swap_halves(o))
        yc_ref[:, sl] = jnp.where(_half_mask(outs[0].shape, 0), outs[0], outs[1]).astype(yc_ref.dtype)

    lam = lam_ref[0]
    for h in range(H_D):
        sl = slice(h * LANES, (h + 1) * LANES)
        yd_ref[:, sl] = _diff_head(dq[:, sl] * ATTN_SCALE, [dk[:, sl]], [_with_ones(dv[:, sl])], lam)


def _ctx_attention(qkv, lam, sink, batch, seq):
    tile = lambda t: pl.BlockSpec((seq, COL_TILE), lambda b: (b, t))
    smem = pl.BlockSpec(memory_space=pltpu.SMEM)
    out = lambda dt: jax.ShapeDtypeStruct((batch * seq, BRANCH_W), dt)
    out_spec = pl.BlockSpec((seq, BRANCH_W), lambda b: (b, 0))
    return pl.pallas_call(
        _ctx_attn_kernel,
        out_shape=(out(BF16), out(BF16), out(F32)),
        grid=(batch,),
        in_specs=[smem, smem, tile(T_BQ), tile(T_BK), tile(T_BV), tile(T_CQ), tile(T_MISC),
                  tile(T_DQ), tile(T_DK), tile(T_DV)],
        out_specs=(out_spec, out_spec, out_spec),
        compiler_params=_cparams("parallel"),
        name="ctx_attention",
    )(lam, sink, *([qkv] * 8))


NAT_QROWS = 4
NAT_KROWS = 12
NAT_RI = 2 * NA_ROWS - 1


def _nat_kernel(q_ref, k_ref, v_ref, kc_ref, vc_ref, cb_ref, o_ref):
    rows = k_ref.shape[0] // GRID_W
    r0 = pl.program_id(2) * NAT_QROWS
    kb = jnp.clip(r0 - NA_ROWS // 2, 0, rows - NAT_KROWS)
    koff = pl.multiple_of(kb * GRID_W, GRID_W)
    nq = NAT_QROWS * GRID_W
    nk = NAT_KROWS * GRID_W
    q = q_ref[...] * ATTN_SCALE
    k = k_ref[pl.ds(koff, nk), :]
    v = _with_ones(v_ref[pl.ds(koff, nk), :])
    kc = kc_ref[0].astype(BF16)
    vc = _with_ones(vc_ref[0].astype(BF16))
    left = _half_mask((GRID_W, LANES), 0)
    scores = []
    for half in range(2):
        bias_rows = []
        for a in range(NAT_QROWS):
            r = r0 + a
            rs = jnp.clip(r - NA_ROWS // 2, 0, rows - NA_ROWS)
            blocks = []
            for ip in range(NAT_KROWS // 2):
                sides = []
                for side in range(2):
                    kr = kb + 2 * ip + side
                    ok = (kr >= rs) & (kr < rs + NA_ROWS)
                    ri = jnp.clip(kr - r + NA_ROWS - 1, 0, NAT_RI - 1)
                    sides.append(jnp.where(ok, cb_ref[half, ri], NEG_INF))
                blocks.append(jnp.where(left, sides[0], sides[1]))
            bias_rows.append(jnp.concatenate(blocks, axis=1))
        bias = jnp.concatenate(bias_rows, axis=0)
        qm = _select_half(q, half)
        scores.append([_dot_nt(qm, k) + bias, _dot_nt(qm, kc)])
    outs = [_softmax_pv(s, [v, vc]) for s in scores]
    o_ref[...] = jnp.where(_half_mask((nq, LANES), 0), outs[0], outs[1]).astype(o_ref.dtype)


def _nat_col_table(rpb):
    qc = np.arange(GRID_W)[:, None]
    kc = np.arange(GRID_W)[None, :]
    cs = np.clip(qc - NA_COLS // 2, 0, GRID_W - NA_COLS)
    valid = (kc >= cs) & (kc < cs + NA_COLS)
    n_ci = 2 * NA_COLS - 1
    lo = GRID_W - 1 - (NA_COLS - 1)
    v = jnp.pad(rpb, ((0, 0), (0, 0), (lo, 2 * GRID_W - n_ci - lo)))
    flat = jnp.tile(v, (1, 1, GRID_W))[..., :GRID_W * (2 * GRID_W - 1)]
    toep = flat.reshape(rpb.shape[0], rpb.shape[1], GRID_W, 2 * GRID_W - 1)[..., GRID_W - 1:]
    tab = jnp.where(valid[None, None], toep, NEG_INF)
    return jnp.concatenate([tab, tab], axis=-1)


def _nat_latent(qkv, kc, vc, cb, l, batch, seq):
    nq = NAT_QROWS * GRID_W
    tpb = seq // nq
    past = kc.shape[2]
    return pl.pallas_call(
        _nat_kernel,
        out_shape=jax.ShapeDtypeStruct((batch * seq, BRANCH_W), BF16),
        grid=(batch, H_B // 2, tpb),
        in_specs=[
            pl.BlockSpec((nq, LANES), lambda b, p, i: (b * tpb + i, T_BQ * 4 + p)),
            pl.BlockSpec((seq, LANES), lambda b, p, i: (b, T_BK * 4 + p)),
            pl.BlockSpec((seq, LANES), lambda b, p, i: (b, T_BV * 4 + p)),
            pl.BlockSpec((1, None, past, LANES), lambda b, p, i: (b, l, 0, p)),
            pl.BlockSpec((1, None, past, LANES), lambda b, p, i: (b, l, 0, p)),
            pl.BlockSpec((2, NAT_RI, GRID_W, LANES), lambda b, p, i: (l * (H_B // 2) + p, 0, 0, 0)),
        ],
        out_specs=pl.BlockSpec((nq, LANES), lambda b, p, i: (b * tpb + i, p)),
        compiler_params=_cparams("parallel", "parallel", "parallel"),
        name="nat_latent",
    )(qkv, qkv, qkv, kc, vc, cb)


SWA_QBLK = 512
SWA_KBLK = SWA_QBLK + 2 * SWA_WINDOW


def _swa_kernel(sink_ref, q_ref, kv_ref, kc_ref, vc_ref, o_ref):
    seq = kv_ref.shape[0]
    p = pl.program_id(1)
    t0 = pl.program_id(2) * SWA_QBLK
    k0 = pl.multiple_of(jnp.clip(t0 - SWA_WINDOW, 0, seq - SWA_KBLK), SWA_WINDOW)
    q = q_ref[...] * ATTN_SCALE
    k = kv_ref[pl.ds(k0, SWA_KBLK), 0:LANES]
    v = _with_ones(kv_ref[pl.ds(k0, SWA_KBLK), LANES:2 * LANES])
    kc = kc_ref[0].astype(BF16)
    vc = _with_ones(vc_ref[0].astype(BF16))
    tq = t0 + lax.broadcasted_iota(jnp.int32, (SWA_QBLK, SWA_KBLK), 0)
    tk = k0 + lax.broadcasted_iota(jnp.int32, (SWA_QBLK, SWA_KBLK), 1)
    inside = jnp.abs(tq - tk) <= SWA_WINDOW
    kv_half = p // ((H_C // 2) // KV_C)
    scores = []
    for half in range(2):
        qm = _select_half(q, half)
        qm = jnp.where(kv_half == half, qm, _swap_halves(qm))
        scores.append([jnp.where(inside, _dot_nt(qm, k), NEG_INF), _dot_nt(qm, kc)])
    outs = []
    for half in range(2):
        o = _softmax_pv(scores[half], [v, vc], extra=sink_ref[2 * p + half])
        outs.append(jnp.where(kv_half == half, o, _swap_halves(o)))
    o_ref[...] = jnp.where(_half_mask(q.shape, 0), outs[0], outs[1]).astype(o_ref.dtype)


def _swa_latent(qkv, kc, vc, sink, l, batch, seq):
    tpb = seq // SWA_QBLK
    past = kc.shape[2]
    return pl.pallas_call(
        _swa_kernel,
        out_shape=jax.ShapeDtypeStruct((batch * seq, BRANCH_W), BF16),
        grid=(batch, H_C // 2, tpb),
        in_specs=[
            pl.BlockSpec(memory_space=pltpu.SMEM),
            pl.BlockSpec((SWA_QBLK, LANES), lambda b, p, i: (b * tpb + i, T_CQ * 4 + p)),
            pl.BlockSpec((seq, COL_TILE), lambda b, p, i: (b, T_MISC)),
            pl.BlockSpec((1, None, past, LANES), lambda b, p, i: (b, l, 0, 0)),
            pl.BlockSpec((1, None, past, LANES), lambda b, p, i: (b, l, 0, 0)),
        ],
        out_specs=pl.BlockSpec((SWA_QBLK, LANES), lambda b, p, i: (b * tpb + i, p)),
        compiler_params=_cparams("parallel", "parallel", "parallel"),
        name="swa_latent",
    )(sink, qkv, qkv, kc, vc)


DIFF_QBLK = 512


def _diff_kernel(lam_ref, q_ref, k_ref, v_ref, kc_ref, vc_ref, o_ref):
    q = q_ref[...] * ATTN_SCALE
    ks = [k_ref[...], kc_ref[0].astype(BF16)]
    vs = [_with_ones(v_ref[...]), _with_ones(vc_ref[0].astype(BF16))]
    o_ref[...] = _diff_head(q, ks, vs, lam_ref[0])


def _diff_latent(qkv, kc, vc, lam, l, batch, seq):
    tpb = seq // DIFF_QBLK
    past = kc.shape[2]
    return pl.pallas_call(
        _diff_kernel,
        out_shape=jax.ShapeDtypeStruct((batch * seq, BRANCH_W), F32),
        grid=(batch, H_D, tpb),
        in_specs=[
            pl.BlockSpec(memory_space=pltpu.SMEM),
            pl.BlockSpec((DIFF_QBLK, LANES), lambda b, h, i: (b * tpb + i, T_DQ * 4 + h)),
            pl.BlockSpec((seq, LANES), lambda b, h, i: (b, T_DK * 4 + h)),
            pl.BlockSpec((seq, LANES), lambda b, h, i: (b, T_DV * 4 + h)),
            pl.BlockSpec((1, None, past, LANES), lambda b, h, i: (b, l, 0, h)),
            pl.BlockSpec((1, None, past, LANES), lambda b, h, i: (b, l, 0, h)),
        ],
        out_specs=pl.BlockSpec((DIFF_QBLK, LANES), lambda b, h, i: (b * tpb + i, h)),
        compiler_params=_cparams("parallel", "parallel", "parallel"),
        name="diff_latent",
    )(lam, qkv, qkv, qkv, kc, vc)


def _rms_heads(y, g_ref, per_head_gain):
    outs = []
    for c in range(BRANCH_W // LANES):
        sl = slice(c * LANES, (c + 1) * LANES)
        g = g_ref[:, sl] if per_head_gain else g_ref[...]
        outs.append(_rms(y[:, sl], g))
    return jnp.concatenate(outs, axis=1)


def _first_lane(hit, lane_f):
    return jnp.min(jnp.where(hit, lane_f, float(LANES)), axis=1, keepdims=True)


def _top_group(logits):
    lane = lax.broadcasted_iota(jnp.int32, logits.shape, 1)
    gl = jnp.where(lane < N_GROUPS, logits, NEG_INF)
    return _first_lane(gl == jnp.max(gl, axis=1, keepdims=True), lane.astype(F32))


def _gate_weights(logits, group):
    lane = lax.broadcasted_iota(jnp.int32, logits.shape, 1)
    lane_f = lane.astype(F32)
    gl = jnp.where(lane < N_GROUPS, logits, NEG_INF)
    gmax = jnp.max(gl, axis=1, keepdims=True)
    g_logit = jnp.sum(jnp.where(lane_f == group, logits, 0.0), axis=1, keepdims=True)
    gw = jnp.exp(g_logit - gmax) / jnp.sum(jnp.exp(gl - gmax), axis=1, keepdims=True)
    e_lane = lane - N_GROUPS
    e_group = lax.shift_right_arithmetic(e_lane, EXP_PER_GROUP.bit_length() - 1).astype(F32)
    in_group = (e_lane >= 0) & (e_lane < N_EXPERTS) & (e_group == group)
    el = jnp.where(in_group, logits, NEG_INF)
    e1 = jnp.max(el, axis=1, keepdims=True)
    i1 = _first_lane(el == e1, lane_f)
    el2 = jnp.where(lane_f == i1, NEG_INF, el)
    e2 = jnp.max(el2, axis=1, keepdims=True)
    i2 = _first_lane(el2 == e2, lane_f)
    t = jnp.exp(e2 - e1)
    w1 = gw / (1.0 + t)
    return jnp.where(lane_f == i1, w1, jnp.where(lane_f == i2, w1 * t, 0.0))


def _merge_kernel(ya, yb, yc, yd, ao, gt0, gt1, gt2, gt3, x_ref, mod_ref, anorm, subln, nffn,
                  wbr, wout, wr, xo_ref, h2_ref, group_ref, *, yd_scale):
    a = _sigmoid(ao[...].astype(F32)) * _rms_heads(ya[...], anorm, True)
    d = _rms_heads(yd[...], subln, False) * yd_scale
    branches = (a, yb[...], yc[...], d)
    mix = None
    for n, (br, gt) in enumerate(zip(branches, (gt0, gt1, gt2, gt3))):
        term = _sigmoid(gt[...].astype(F32)) * _dot(br.astype(BF16), wbr[n])
        mix = term if mix is None else mix + term
    out = _dot(mix.astype(BF16), wout[...])
    xn = x_ref[...] + mod_ref[0, 2:3, :] * out
    xo_ref[...] = xn
    h2 = _rms(xn, nffn[...]) * (1.0 + mod_ref[0, 4:5, :]) + mod_ref[0, 3:4, :]
    h2_ref[...] = h2
    group_ref[...] = jnp.broadcast_to(_top_group(_dot(h2.astype(BF16), wr[...])), group_ref.shape)


def _merge(ys, qkv, x, mod3, a_norm, subln, norm_ffn, w_branch, w_out, w_route, lam_init, l):
    n = x.shape[0]
    tm = 512
    tiles_per_mod = n // mod3.shape[0] // tm
    y_spec = pl.BlockSpec((tm, BRANCH_W), lambda i: (i, 0))
    gt_spec = lambda k: pl.BlockSpec((tm, D_MODEL), lambda i: (i, T_GT * COL_TILE // D_MODEL + k))
    full = lambda shape: pl.BlockSpec(shape, lambda i: (0,) * len(shape))
    layer = lambda shape: pl.BlockSpec((None,) + shape, lambda i: (l,) + (0,) * len(shape))
    x_spec = pl.BlockSpec((tm, D_MODEL), lambda i: (i, 0))
    return pl.pallas_call(
        functools.partial(_merge_kernel, yd_scale=1.0 - lam_init),
        out_shape=(
            jax.ShapeDtypeStruct((n, D_MODEL), F32),
            jax.ShapeDtypeStruct((n, D_MODEL), F32),
            jax.ShapeDtypeStruct((n, LANES), F32),
        ),
        grid=(n // tm,),
        in_specs=[
            y_spec, y_spec, y_spec, y_spec,
            pl.BlockSpec((tm, COL_TILE), lambda i: (i, T_AO)),
            gt_spec(0), gt_spec(1), gt_spec(2), gt_spec(3),
            x_spec,
            pl.BlockSpec((1, 6, D_MODEL), lambda i: (i // tiles_per_mod, 0, 0)),
            full((1, BRANCH_W)), full((1, LANES)), full((1, D_MODEL)),
            layer((N_BRANCH, BRANCH_W, D_MODEL)), layer((D_MODEL, D_MODEL)), layer((D_MODEL, LANES)),
        ],
        out_specs=(x_spec, x_spec, pl.BlockSpec((tm, LANES), lambda i: (i, 0))),
        compiler_params=_cparams("parallel"),
        name="merge_route",
    )(*ys, qkv, qkv, qkv, qkv, qkv, x, mod3, a_norm.reshape(1, BRANCH_W), subln.reshape(1, LANES),
      norm_ffn.reshape(1, D_MODEL), w_branch, w_out, w_route)


MOE_TILE = 512
ROW_DMA_UNROLL = 16


def _row_copies(src_row, dst_row, sem, n):
    def copy(i):
        return pltpu.make_async_copy(src_row(i), dst_row(i), sem)

    def start(i, c):
        copy(i).start()
        return c

    def wait(i, c):
        copy(i).wait()
        return c

    lax.fori_loop(0, n, start, 0, unroll=ROW_DMA_UNROLL)
    lax.fori_loop(0, n, wait, 0, unroll=ROW_DMA_UNROLL)


def _moe_scatter_kernel(slot_ref, h_ref, init_ref, sorted_ref, sem):
    del init_ref
    base = pl.program_id(0) * MOE_TILE
    _row_copies(lambda i: h_ref.at[pl.ds(i, 1), :],
                lambda i: sorted_ref.at[pl.ds(slot_ref[base + i], 1), :], sem, MOE_TILE)


def _moe_group_kernel(tile_group_ref, n_used_ref, h_ref, wr_ref, wg_ref, wu_ref, wd_ref, o_ref):
    t = pl.program_id(0)

    @pl.when(t < n_used_ref[0])
    def _():
        group = tile_group_ref[t]
        h = h_ref[...].astype(BF16)
        gates = _gate_weights(_dot(h, wr_ref[...]), group.astype(F32))
        lane = lax.broadcasted_iota(jnp.int32, gates.shape, 1)
        acc = None
        for e in range(EXP_PER_GROUP):
            g = jnp.sum(jnp.where(lane == N_GROUPS + group * EXP_PER_GROUP + e, gates, 0.0), axis=1, keepdims=True)
            hg = _dot(h, wg_ref[0, e])
            hid = hg * _sigmoid(hg) * _dot(h, wu_ref[0, e]) * g
            term = _dot(hid.astype(BF16), wd_ref[0, e])
            acc = term if acc is None else acc + term
        o_ref[...] = acc

    @pl.when(t >= n_used_ref[0])
    def _():
        o_ref[...] = jnp.zeros_like(o_ref)


def _moe_gather_kernel(slot_ref, y_ref, x_ref, mod_ref, nf_ref, o_ref, buf, sem, *, final_norm):
    t = pl.program_id(0)
    nt = pl.num_programs(0)

    def copy(tile, b, i):
        return pltpu.make_async_copy(y_ref.at[pl.ds(slot_ref[tile * MOE_TILE + i], 1), :],
                                     buf.at[b, pl.ds(i, 1), :], sem.at[b])

    def start_tile(tile, b):
        def start(i, c):
            copy(tile, b, i).start()
            return c

        lax.fori_loop(0, MOE_TILE, start, 0, unroll=ROW_DMA_UNROLL)

    cur = t % 2

    @pl.when(t == 0)
    def _():
        start_tile(0, 0)

    @pl.when(t + 1 < nt)
    def _():
        start_tile(t + 1, 1 - cur)

    def wait(i, c):
        copy(t, cur, i).wait()
        return c

    lax.fori_loop(0, MOE_TILE, wait, 0, unroll=ROW_DMA_UNROLL)
    y = x_ref[...] + mod_ref[0, 5:6, :] * buf[cur]
    o_ref[...] = _rms(y, nf_ref[...]) if final_norm else y


def _moe(h2, group, x, mod3, w_route, wg, wu, wd, norm_final, final_norm, l):
    n = x.shape[0]
    tm = MOE_TILE
    n_tiles = n // tm + N_GROUPS
    onehot = (group[:, None] == jnp.arange(N_GROUPS, dtype=jnp.int32)[None]).astype(jnp.int32)
    csum = jnp.cumsum(onehot, axis=0)
    counts = csum[-1]
    rank = jnp.sum((csum - onehot) * onehot, axis=1)
    group_tiles = (counts + tm - 1) // tm
    tile_end = jnp.cumsum(group_tiles)
    group_start = (tile_end - group_tiles) * tm
    slot = (jnp.sum(onehot * group_start[None], axis=1) + rank).astype(jnp.int32)
    tile_group = jnp.minimum(jnp.sum(jnp.arange(n_tiles, dtype=jnp.int32)[:, None] >= tile_end[None], axis=1),
                             N_GROUPS - 1).astype(jnp.int32)
    n_used = tile_end[-1:].astype(jnp.int32)

    sorted_h = pl.pallas_call(
        _moe_scatter_kernel,
        out_shape=jax.ShapeDtypeStruct((n_tiles * tm, D_MODEL), F32),
        grid_spec=pltpu.PrefetchScalarGridSpec(
            num_scalar_prefetch=1, grid=(n // tm,),
            in_specs=[pl.BlockSpec((tm, D_MODEL), lambda i, slot: (i, 0)), pl.BlockSpec(memory_space=pl.ANY)],
            out_specs=pl.BlockSpec(memory_space=pl.ANY),
            scratch_shapes=[pltpu.SemaphoreType.DMA],
        ),
        input_output_aliases={2: 0},
        compiler_params=_cparams("arbitrary"),
        name="moe_scatter",
    )(slot, h2, jnp.zeros((n_tiles * tm, D_MODEL), F32))

    grp = lambda shape: pl.BlockSpec((None, 1) + shape, lambda t, tg, nu: (l, tg[t], 0, 0, 0))
    sorted_y = pl.pallas_call(
        _moe_group_kernel,
        out_shape=jax.ShapeDtypeStruct((n_tiles * tm, D_MODEL), F32),
        grid_spec=pltpu.PrefetchScalarGridSpec(
            num_scalar_prefetch=2, grid=(n_tiles,),
            in_specs=[
                pl.BlockSpec((tm, D_MODEL), lambda t, tg, nu: (t, 0)),
                pl.BlockSpec((None, D_MODEL, LANES), lambda t, tg, nu: (l, 0, 0)),
                grp((EXP_PER_GROUP, D_MODEL, D_EXPERT)), grp((EXP_PER_GROUP, D_MODEL, D_EXPERT)),
                grp((EXP_PER_GROUP, D_EXPERT, D_MODEL)),
            ],
            out_specs=pl.BlockSpec((tm, D_MODEL), lambda t, tg, nu: (t, 0)),
        ),
        compiler_params=_cparams("arbitrary"),
        name="moe_experts",
    )(tile_group, n_used, sorted_h, w_route, wg, wu, wd)

    tiles_per_mod = n // mod3.shape[0] // tm
    row = pl.BlockSpec((tm, D_MODEL), lambda i, slot: (i, 0))
    return pl.pallas_call(
        functools.partial(_moe_gather_kernel, final_norm=final_norm),
        out_shape=jax.ShapeDtypeStruct((n, D_MODEL), F32),
        grid_spec=pltpu.PrefetchScalarGridSpec(
            num_scalar_prefetch=1, grid=(n // tm,),
            in_specs=[
                pl.BlockSpec(memory_space=pl.ANY), row,
                pl.BlockSpec((1, 6, D_MODEL), lambda i, slot: (i // tiles_per_mod, 0, 0)),
                pl.BlockSpec((1, D_MODEL), lambda i, slot: (0, 0)),
            ],
            out_specs=row,
            scratch_shapes=[pltpu.VMEM((2, tm, D_MODEL), F32), pltpu.SemaphoreType.DMA((2,))],
        ),
        compiler_params=_cparams("arbitrary"),
        name="moe_gather",
    )(slot, sorted_y, x, mod3, norm_final.reshape(1, D_MODEL))


def _reorder_w_in(w):
    pad = jnp.zeros((D_MODEL, COL_TILE - 2 * LANES - 16), w.dtype)
    cols = [w[:, 0:2048], w[:, 2064:4112], w[:, 4368:5904], w[:, 4112:4368], w[:, 2048:2064], pad, w[:, 5904:]]
    return jnp.concatenate(cols, axis=1).astype(BF16)


def _rope_tables(seq):
    nf = DH_C // 4
    inv = ROPE_BASE ** (-jnp.arange(nf, dtype=F32) / nf)
    t = jnp.arange(seq)
    ang_r = (t // GRID_W).astype(F32)[:, None] * inv[None, :]
    ang_c = (t % GRID_W).astype(F32)[:, None] * inv[None, :]
    cos = jnp.concatenate([jnp.cos(ang_r)] * 2 + [jnp.cos(ang_c)] * 2, axis=1)
    sin = jnp.concatenate([-jnp.sin(ang_r), jnp.sin(ang_r), -jnp.sin(ang_c), jnp.sin(ang_c)], axis=1)
    return jnp.tile(cos, (1, 2)), jnp.tile(sin, (1, 2))


def _layer(l, x, mod3, seq, lat, p):
    batch = x.shape[0] // seq
    lam_init = 0.8 - 0.6 * math.exp(-0.3 * l)
    ctx = lat is None
    proj = _qkv_proj(x, p["norm_mix"][l], mod3, p["w_in"][l], None if ctx else p["rope"], ctx,
                     tm=1024 if ctx else 2048)
    qkv, gates_raw = proj[:2]
    gates = gates_raw[:, :16] + jnp.concatenate([p["i_bias"][l].reshape(-1), p["f_bias"][l].reshape(-1)])[None]
    lam = (jnp.exp(jnp.sum(p["lq1"][l] * p["lk1"][l])) - jnp.exp(jnp.sum(p["lq2"][l] * p["lk2"][l]))
           + lam_init).reshape(1).astype(F32)
    sink = p["sink"][l]
    new = None
    if ctx:
        C0 = jnp.zeros((batch, 1, 2, H_A, DH_A, DH_A), F32)
        n0 = jnp.zeros((batch, 2, H_A, DH_A), F32)
        m0 = jnp.zeros((batch, 2, H_A), F32)
        ya, Cs, ns, ms = _mlstm(qkv, gates, C0, 0, n0, m0, batch, seq)
        yb, yc, yd = _ctx_attention(qkv, lam, sink, batch, seq)
        bk, bv, dk, dv, ckv = proj[2:]
        new = (bk.reshape(batch, seq, H_B, DH_B), bv.reshape(batch, seq, H_B, DH_B),
               ckv[:, :LANES].reshape(batch, seq, KV_C, DH_C), ckv[:, LANES:].reshape(batch, seq, KV_C, DH_C),
               dk.reshape(batch, seq, H_D, 2 * DH_D), dv.reshape(batch, seq, H_D, 2 * DH_D), Cs, ns, ms)
    else:
        nat_k, nat_v, swa_k, swa_v, diff_k, diff_v, C0, n0, m0 = lat
        flat = lambda a: a.reshape(a.shape[:3] + (-1,))
        ya, _, _, _ = _mlstm(qkv, gates, C0, l, n0[:, l], m0[:, l], batch, seq)
        yb = _nat_latent(qkv, flat(nat_k), flat(nat_v), p["nat_table"], l, batch, seq)
        yc = _swa_latent(qkv, flat(swa_k), flat(swa_v), sink, l, batch, seq)
        yd = _diff_latent(qkv, flat(diff_k), flat(diff_v), lam, l, batch, seq)
    xn, h2, group = _merge((ya, yb, yc, yd), qkv, x, mod3, p["a_norm"][l], p["subln"][l],
                           p["norm_ffn"][l], p["w_branch"], p["w_out"], p["w_route"], lam_init, l)
    y = _moe(h2, group[:, 0].astype(jnp.int32), xn, mod3, p["w_route"], p["wg"], p["wu"], p["wd"],
             p["norm_final"], final_norm=(l == DEPTH - 1), l=l)
    return y, new


def kernel(x_prompt, x_sample, cache_nat_k, cache_nat_v, cache_swa_k, cache_swa_v, cache_diff_k, cache_diff_v, state_mlstm_C, state_mlstm_n, state_mlstm_m, c, c_ctx, norm_mix, norm_ffn, norm_final, w_mod, b_mod, w_in, mlstm_i_bias, mlstm_f_bias, mlstm_norm, nat_rpb, swa_sink, diff_lq1, diff_lk1, diff_lq2, diff_lk2, diff_subln, w_branch, w_out, router_group, router_expert, w_exp_gate, w_exp_up, w_exp_down):
    b_ctx, s_ctx, _ = x_prompt.shape
    b_lat, s_lat, _ = x_sample.shape
    cond8 = jnp.concatenate([c_ctx[None], c, jnp.zeros((8 - 1 - b_lat, D_MODEL), F32)], axis=0)
    mod = _modulation(cond8, w_mod, b_mod).reshape(DEPTH, 8, 6, D_MODEL)
    route_pad = jnp.zeros((DEPTH, D_MODEL, LANES - N_GROUPS - N_EXPERTS), F32)
    p = dict(
        norm_mix=norm_mix, norm_ffn=norm_ffn, norm_final=norm_final,
        w_in=[_reorder_w_in(w_in[l]) for l in range(DEPTH)],
        i_bias=mlstm_i_bias, f_bias=mlstm_f_bias, a_norm=mlstm_norm, sink=swa_sink,
        nat_table=_nat_col_table(nat_rpb.reshape((DEPTH * H_B,) + nat_rpb.shape[2:])),
        lq1=diff_lq1, lk1=diff_lk1, lq2=diff_lq2, lk2=diff_lk2, subln=diff_subln,
        w_branch=w_branch.astype(BF16), w_out=w_out.astype(BF16),
        w_route=jnp.concatenate([router_group, router_expert, route_pad], axis=-1).astype(BF16),
        wg=w_exp_gate.astype(BF16).reshape(DEPTH, N_GROUPS, EXP_PER_GROUP, D_MODEL, D_EXPERT),
        wu=w_exp_up.astype(BF16).reshape(DEPTH, N_GROUPS, EXP_PER_GROUP, D_MODEL, D_EXPERT),
        wd=w_exp_down.astype(BF16).reshape(DEPTH, N_GROUPS, EXP_PER_GROUP, D_EXPERT, D_MODEL),
        rope=_rope_tables(s_lat),
    )
    yp = x_prompt.reshape(b_ctx * s_ctx, D_MODEL)
    ys = x_sample.reshape(b_lat * s_lat, D_MODEL)
    news = []
    for l in range(DEPTH):
        yp, new = _layer(l, yp, mod[l, 0:1], s_ctx, None, p)
        news.append(new)
        lat = (cache_nat_k, cache_nat_v, cache_swa_k, cache_swa_v, cache_diff_k, cache_diff_v,
               state_mlstm_C, state_mlstm_n, state_mlstm_m)
        ys, _ = _layer(l, ys, mod[l, 1:1 + b_lat], s_lat, lat, p)
    def stack_layers(parts):
        flat = [a.reshape(a.shape[0], 1, -1) for a in parts]
        return jnp.concatenate(flat, axis=1).reshape((parts[0].shape[0], DEPTH) + parts[0].shape[1:])

    stacked = tuple(stack_layers([news[l][k] for l in range(DEPTH)]) for k in range(9))
    return (yp.reshape(b_ctx, s_ctx, D_MODEL), ys.reshape(b_lat, s_lat, D_MODEL)) + stacked
```

```python
import functools
import math

import numpy as np
import jax
import jax.numpy as jnp
from jax import lax
from jax.experimental import pallas as pl
from jax.experimental.pallas import tpu as pltpu

F32 = jnp.float32
BF16 = jnp.bfloat16

D_MODEL = 1024
DEPTH = 2
GRID_W = 64
BRANCH_W = 512
N_BRANCH = 4
H_A, DH_A, MLSTM_CHUNK = 4, 128, 128
H_B, DH_B = 8, 64
NA_ROWS, NA_COLS = 8, 16
H_C, KV_C, DH_C = 8, 2, 64
SWA_WINDOW = 128
H_D, DH_D = 4, 64
N_GROUPS, EXP_PER_GROUP = 4, 4
N_EXPERTS = N_GROUPS * EXP_PER_GROUP
D_EXPERT = 256
ROPE_BASE = 10000.0
EPS = 1e-6

LANES = 128
HALF = 64
COL_TILE = 512
N_COL_TILES = 20
N_PROJ = COL_TILE * N_COL_TILES
T_AQ, T_AK, T_AV, T_AO, T_BQ, T_BK, T_BV, T_CQ, T_DQ, T_DK, T_DV, T_MISC, T_GT = range(13)
MISC_GATE_OFF = 256
VMEM_LIMIT = 56 * 1024 * 1024
NEG_INF = float("-inf")

_NT = (((1,), (1,)), ((), ()))
_TN = (((0,), (0,)), ((), ()))


def _cparams(*sem):
    return pltpu.CompilerParams(dimension_semantics=sem, vmem_limit_bytes=VMEM_LIMIT)


def _dot(a, b):
    return jnp.dot(a, b, preferred_element_type=F32)


def _dot_nt(a, b):
    return lax.dot_general(a, b, _NT, preferred_element_type=F32)


def _dot_tn(a, b):
    return lax.dot_general(a, b, _TN, preferred_element_type=F32)


def _rms(x, g):
    return x * lax.rsqrt(jnp.mean(x * x, axis=-1, keepdims=True) + EPS) * g


def _sigmoid(x):
    return 0.5 * jnp.tanh(0.5 * x) + 0.5


def _mod_kernel(c_ref, w_ref, b_ref, o_ref):
    c = c_ref[...]
    s = (c * _sigmoid(c)).astype(BF16)
    o_ref[0] = _dot(s, w_ref[0].astype(BF16)) + b_ref[0]


def _modulation(cond8, w_mod, b_mod):
    tn = 1536
    n_out = 6 * D_MODEL
    return pl.pallas_call(
        _mod_kernel,
        out_shape=jax.ShapeDtypeStruct((DEPTH, 8, n_out), F32),
        grid=(DEPTH, n_out // tn),
        in_specs=[
            pl.BlockSpec((8, D_MODEL), lambda l, j: (0, 0)),
            pl.BlockSpec((1, D_MODEL, tn), lambda l, j: (l, 0, j)),
            pl.BlockSpec((1, 1, tn), lambda l, j: (l, 0, j)),
        ],
        out_specs=pl.BlockSpec((1, 8, tn), lambda l, j: (l, 0, j)),
        compiler_params=_cparams("parallel", "parallel"),
        name="modulation",
    )(cond8, w_mod, b_mod.reshape(DEPTH, 1, n_out))


CACHE_TILES = (T_BK, T_BV, T_DK, T_DV)
QKV_ROW_CHUNK = 512


def _rope128(x, cos, sin):
    lane = lax.broadcasted_iota(jnp.int32, x.shape, 1)
    partner = jnp.where((lane % 32) < 16, pltpu.roll(x, LANES - 16, 1), pltpu.roll(x, 16, 1))
    return x * cos + partner * sin


def _qkv_kernel(x_ref, nw_ref, mod_ref, w_ref, *rest, rope, cache):
    rest = list(rest)
    if rope:
        cos_ref, sin_ref = rest[:2]
        rest = rest[2:]
    o_ref, g_ref = rest[:2]
    cache_refs = rest[2:-1]
    h_scr = rest[-1]
    j = pl.program_id(1)

    @pl.when(j == 0)
    def _():
        h = _rms(x_ref[...], nw_ref[...]) * (1.0 + mod_ref[0, 1:2, :]) + mod_ref[0, 0:1, :]
        h_scr[...] = h.astype(BF16)

    def tile_kind(cond, store):
        @pl.when(cond)
        def _():
            for r in range(h_scr.shape[0] // QKV_ROW_CHUNK):
                rows = slice(r * QKV_ROW_CHUNK, (r + 1) * QKV_ROW_CHUNK)
                store(rows, _dot(h_scr[rows, :], w_ref[...]))

    def roped(rows, acc, c):
        sl = slice(c * LANES, (c + 1) * LANES)
        return _rope128(acc[:, sl], cos_ref[rows, :], sin_ref[rows, :]).astype(BF16)

    def store_plain(rows, acc):
        o_ref[rows, :] = acc.astype(BF16)

    def store_ak(rows, acc):
        o_ref[rows, :] = (acc * (DH_A ** -0.5)).astype(BF16)

    def store_misc(rows, acc):
        o_ref[rows, :] = acc.astype(BF16)
        if rope:
            o_ref[rows, :LANES] = roped(rows, acc, 0)
        g_ref[rows, :] = acc[:, MISC_GATE_OFF:MISC_GATE_OFF + LANES]
        if cache:
            cache_refs[-1][rows, :] = acc[:, :2 * LANES]

    def store_rope(rows, acc):
        for c in range(COL_TILE // LANES):
            o_ref[rows, c * LANES:(c + 1) * LANES] = roped(rows, acc, c)

    special = (j == T_AK) | (j == T_MISC)
    tile_kind(j == T_AK, store_ak)
    tile_kind(j == T_MISC, store_misc)
    if rope:
        is_rope = (j == T_CQ) | (j == T_DQ) | (j == T_DK)
        special = special | is_rope
        tile_kind(is_rope, store_rope)
    if cache:
        for t, ref in zip(CACHE_TILES, cache_refs[:-1]):
            def store_cached(rows, acc, ref=ref):
                o_ref[rows, :] = acc.astype(BF16)
                ref[rows, :] = acc
            special = special | (j == t)
            tile_kind(j == t, store_cached)
    tile_kind(jnp.logical_not(special), store_plain)


def _qkv_proj(x, norm_w, mod3, w_in_r, rope_tabs, cache, tm):
    n = x.shape[0]
    tiles_per_mod = n // mod3.shape[0] // tm
    rope = rope_tabs is not None
    in_specs = [
        pl.BlockSpec((tm, D_MODEL), lambda i, j: (i, 0)),
        pl.BlockSpec((1, D_MODEL), lambda i, j: (0, 0)),
        pl.BlockSpec((1, 6, D_MODEL), lambda i, j: (i // tiles_per_mod, 0, 0)),
        pl.BlockSpec((D_MODEL, COL_TILE), lambda i, j: (0, j)),
    ]
    args = [x, norm_w.reshape(1, D_MODEL), mod3, w_in_r]
    if rope:
        tiles_per_seq = rope_tabs[0].shape[0] // tm
        in_specs += [pl.BlockSpec((tm, LANES), lambda i, j: (i % tiles_per_seq, 0))] * 2
        args += list(rope_tabs)
    row = lambda w: pl.BlockSpec((tm, w), lambda i, j: (i, 0))
    out_shape = [jax.ShapeDtypeStruct((n, N_PROJ), BF16), jax.ShapeDtypeStruct((n, LANES), F32)]
    out_specs = [pl.BlockSpec((tm, COL_TILE), lambda i, j: (i, j)), row(LANES)]
    if cache:
        out_shape += [jax.ShapeDtypeStruct((n, COL_TILE), F32)] * len(CACHE_TILES)
        out_shape += [jax.ShapeDtypeStruct((n, 2 * LANES), F32)]
        out_specs += [row(COL_TILE)] * len(CACHE_TILES) + [row(2 * LANES)]
    return pl.pallas_call(
        functools.partial(_qkv_kernel, rope=rope, cache=cache),
        out_shape=tuple(out_shape),
        grid=(n // tm, N_COL_TILES),
        in_specs=in_specs,
        out_specs=tuple(out_specs),
        scratch_shapes=[pltpu.VMEM((tm, D_MODEL), BF16)],
        compiler_params=_cparams("parallel", "arbitrary"),
        name="qkv_rope" if rope else "qkv",
    )(*args)


def _log_sigmoid(x):
    return jnp.minimum(x, 0.0) - jnp.log1p(jnp.exp(-jnp.abs(x)))


def _mlstm_chunk(c, h, d, state, q_ref, k_ref, v_ref, gr_ref, gc_ref, out_ref):
    L = MLSTM_CHUNK
    C, n, m = state
    row = lax.broadcasted_iota(jnp.int32, (L, L), 0)
    col = lax.broadcasted_iota(jnp.int32, (L, L), 1)
    tri = (col <= row) if d == 0 else (col >= row)
    tri_t = (row <= col) if d == 0 else (row >= col)
    off = pl.multiple_of(c * L, L)
    hs = slice(h * DH_A, (h + 1) * DH_A)
    qb = q_ref[pl.ds(off, L), hs]
    kb = k_ref[pl.ds(off, L), hs]
    vb = v_ref[pl.ds(off, L), hs]
    kx, vx = kb.astype(F32), vb.astype(F32)
    g_rows = gr_ref[h, c]
    g_cols = gc_ref[h, c]
    ig_r = g_rows[2 * d:2 * d + 1, :]
    lf_r = _log_sigmoid(g_rows[2 * d + 1:2 * d + 2, :])
    ig_c = g_cols[:, 2 * d:2 * d + 1]
    lf_c = _log_sigmoid(g_cols[:, 2 * d + 1:2 * d + 2])
    b_c = jnp.sum(jnp.where(tri, lf_r, 0.0), axis=1, keepdims=True)
    b_r = jnp.sum(jnp.where(tri_t, lf_c, 0.0), axis=0, keepdims=True)
    qk = _dot_nt(qb, kb)
    c_ext = jnp.concatenate([C.astype(BF16), jnp.broadcast_to(n.astype(BF16), (DH_A, DH_A))], axis=0)
    qc = _dot_nt(qb, c_ext)
    yield
    logd = jnp.where(tri, b_c - b_r + ig_r, NEG_INF)
    inter = jnp.broadcast_to(b_c + m, (L, L))
    m_t = jnp.maximum(inter, jnp.broadcast_to(jnp.max(logd, axis=1, keepdims=True), (L, L)))
    yield
    a = jnp.exp(inter - m_t)
    s = qk * jnp.exp(logd - m_t)
    sv = _dot(s.astype(BF16), _with_ones(vb))
    num = a * qc[:, :DH_A] + sv[:, :DH_A]
    den = a * qc[:, DH_A:] + sv[:, DH_A:]
    yield
    out_ref[pl.ds(off, L), hs] = num / jnp.maximum(jnp.abs(den), jnp.exp(-m_t))
    b_tot = jnp.sum(lf_r, axis=1, keepdims=True)
    g_c = b_tot - b_c + ig_c
    m_new = jnp.maximum(b_tot + m, jnp.max(g_c, axis=0, keepdims=True))
    yield
    decay = jnp.exp(b_tot + m - m_new)
    wk = jnp.exp(g_c - m_new)
    C_new = decay * C + _dot_tn((wk * vx).astype(BF16), kb)
    n_new = decay * n + jnp.sum(wk * kx, axis=0, keepdims=True)
    return C_new, n_new, m_new


def _run_staged(gens):
    results = [None] * len(gens)
    live = list(range(len(gens)))
    while live:
        for i in list(live):
            try:
                next(gens[i])
            except StopIteration as stop:
                results[i] = stop.value
                live.remove(i)
    return results


def _mlstm_kernel(q_ref, k_ref, v_ref, gr_ref, gc_ref, c0_ref, n0_ref, m0_ref,
                  h_ref, cs_ref, ns_ref, ms_ref, hb_scr, *, nc):
    chains = [(h, d) for h in range(H_A) for d in range(2)]

    def body(ci, states):
        gens = [_mlstm_chunk(ci if d == 0 else nc - 1 - ci, h, d, state, q_ref, k_ref, v_ref, gr_ref, gc_ref,
                             h_ref if d == 0 else hb_scr) for (h, d), state in zip(chains, states)]
        return tuple(_run_staged(gens))

    init = tuple((c0_ref[0, d, h], n0_ref[0, h, d:d + 1, :], m0_ref[0, h, d:d + 1, 0:1]) for h, d in chains)
    final = lax.fori_loop(0, nc, body, init)
    for (h, d), (C, n, m) in zip(chains, final):
        cs_ref[0, d, h] = C
        ns_ref[0, h, d:d + 1, :] = n
        ms_ref[0, h, d:d + 1, :] = jnp.broadcast_to(m, (1, LANES))
    h_ref[...] += hb_scr[...]


def _mlstm(qkv, gates, C0, l, n0, m0, batch, seq):
    nc = seq // MLSTM_CHUNK
    L = MLSTM_CHUNK
    ig = gates[:, :8].reshape(batch, nc, L, 2, H_A)
    fg = gates[:, 8:].reshape(batch, nc, L, 2, H_A)
    g4 = jnp.stack([ig[..., 0, :], fg[..., 0, :], ig[..., 1, :], fg[..., 1, :]], axis=-1)
    g_cols = jnp.transpose(g4, (3, 0, 1, 2, 4)).reshape(H_A, batch * nc, L, 4)
    g_rows = jnp.transpose(g4, (3, 0, 1, 4, 2)).reshape(H_A, batch * nc, 4, L)
    n0t = jnp.transpose(n0, (0, 2, 1, 3))
    m0t = jnp.broadcast_to(jnp.transpose(m0, (0, 2, 1))[..., None], (batch, H_A, 2, LANES))
    qkv_spec = lambda tile: pl.BlockSpec((seq, COL_TILE), lambda b: (b, tile))
    c_spec = pl.BlockSpec((1, 2, H_A, DH_A, DH_A), lambda b: (b, 0, 0, 0, 0))
    c0_spec = pl.BlockSpec((1, None, 2, H_A, DH_A, DH_A), lambda b: (b, l, 0, 0, 0, 0))
    state_spec = pl.BlockSpec((1, H_A, 2, LANES), lambda b: (b, 0, 0, 0))
    h, Cs, ns, ms = pl.pallas_call(
        functools.partial(_mlstm_kernel, nc=nc),
        out_shape=(
            jax.ShapeDtypeStruct((batch * seq, H_A * DH_A), F32),
            jax.ShapeDtypeStruct((batch, 2, H_A, DH_A, DH_A), F32),
            jax.ShapeDtypeStruct((batch, H_A, 2, DH_A), F32),
            jax.ShapeDtypeStruct((batch, H_A, 2, LANES), F32),
        ),
        grid=(batch,),
        in_specs=[
            qkv_spec(T_AQ), qkv_spec(T_AK), qkv_spec(T_AV),
            pl.BlockSpec((H_A, nc, 4, L), lambda b: (0, b, 0, 0)),
            pl.BlockSpec((H_A, nc, L, 4), lambda b: (0, b, 0, 0)),
            c0_spec, state_spec, state_spec,
        ],
        out_specs=(pl.BlockSpec((seq, H_A * DH_A), lambda b: (b, 0)), c_spec, state_spec, state_spec),
        scratch_shapes=[pltpu.VMEM((seq, H_A * DH_A), F32)],
        compiler_params=_cparams("parallel"),
        name="mlstm",
    )(qkv, qkv, qkv, g_rows, g_cols, C0, n0t, m0t)
    return h, Cs, jnp.transpose(ns, (0, 2, 1, 3)), jnp.transpose(ms[..., 0], (0, 2, 1))


ATTN_SCALE = DH_B ** -0.5


def _half_mask(shape, half):
    lane = lax.broadcasted_iota(jnp.int32, shape, 1)
    return (lane >= HALF) if half else (lane < HALF)


def _select_half(q, half):
    return jnp.where(_half_mask(q.shape, half), q, jnp.zeros_like(q))


def _swap_halves(x):
    return pltpu.roll(x.astype(F32), HALF, 1).astype(x.dtype)


def _with_ones(v):
    return jnp.concatenate([v, jnp.ones_like(v)], axis=1)


def _softmax_pv(scores, vs, extra=None):
    m = functools.reduce(jnp.maximum, [jnp.max(s, axis=1, keepdims=True) for s in scores])
    if extra is not None:
        m = jnp.maximum(m, extra)
    acc = functools.reduce(lambda x, y: x + y,
                           [_dot(jnp.exp((s - m).astype(BF16)), v) for s, v in zip(scores, vs)])
    denom = acc[:, LANES:]
    if extra is not None:
        denom = denom + jnp.exp(extra - m)
    return acc[:, :LANES] / denom


def _diff_head(q, ks, vs, lam):
    scores = [[_dot_nt(_select_half(q, comp), k) for k in ks] for comp in range(2)]
    o1, o2 = [_softmax_pv(s, vs) for s in scores]
    return o1 - lam * o2


def _ctx_attn_kernel(lam_ref, sink_ref, bq, bk, bv, cq, misc, dq, dk, dv, yb_ref, yc_ref, yd_ref):
    pairs = [slice(p * LANES, (p + 1) * LANES) for p in range(H_B // 2)]
    scores = [[[_dot_nt(_select_half(bq[:, sl] * ATTN_SCALE, half), bk[:, sl])] for half in range(2)] for sl in pairs]
    for sl, pair_scores in zip(pairs, scores):
        v = _with_ones(bv[:, sl])
        outs = [_softmax_pv(s, [v]) for s in pair_scores]
        yb_ref[:, sl] = jnp.where(_half_mask(outs[0].shape, 0), outs[0], outs[1]).astype(yb_ref.dtype)

    kc = misc[:, 0:LANES]
    vc = _with_ones(misc[:, LANES:2 * LANES])
    kv_of = lambda head: head // (H_C // KV_C)
    scores = []
    for p, sl in enumerate(pairs):
        pair_scores = []
        for half in range(2):
            qm = _select_half(cq[:, sl] * ATTN_SCALE, half)
            if kv_of(2 * p + half) != half:
                qm = _swap_halves(qm)
            pair_scores.append([_dot_nt(qm, kc)])
        scores.append(pair_scores)
    for p, (sl, pair_scores) in enumerate(zip(pairs, scores)):
        outs = []
        for half, s in enumerate(pair_scores):
            o = _softmax_pv(s, [vc], extra=sink_ref[2 * p + half])
            outs.append(o if kv_of(2 * p + half) == half else _swap_halves(o))
        yc_ref[:, sl] = jnp.where(_half_mask(outs[0].shape, 0), outs[0], outs[1]).astype(yc_ref.dtype)

    lam = lam_ref[0]
    for h in range(H_D):
        sl = slice(h * LANES, (h + 1) * LANES)
        yd_ref[:, sl] = _diff_head(dq[:, sl] * ATTN_SCALE, [dk[:, sl]], [_with_ones(dv[:, sl])], lam)


def _ctx_attention(qkv, lam, sink, batch, seq):
    tile = lambda t: pl.BlockSpec((seq, COL_TILE), lambda b: (b, t))
    smem = pl.BlockSpec(memory_space=pltpu.SMEM)
    out = lambda dt: jax.ShapeDtypeStruct((batch * seq, BRANCH_W), dt)
    out_spec = pl.BlockSpec((seq, BRANCH_W), lambda b: (b, 0))
    return pl.pallas_call(
        _ctx_attn_kernel,
        out_shape=(out(BF16), out(BF16), out(F32)),
        grid=(batch,),
        in_specs=[smem, smem, tile(T_BQ), tile(T_BK), tile(T_BV), tile(T_CQ), tile(T_MISC),
                  tile(T_DQ), tile(T_DK), tile(T_DV)],
        out_specs=(out_spec, out_spec, out_spec),
        compiler_params=_cparams("parallel"),
        name="ctx_attention",
    )(lam, sink, *([qkv] * 8))


NAT_QROWS = 4
NAT_KROWS = 12
NAT_RI = 2 * NA_ROWS - 1


def _nat_kernel(q_ref, k_ref, v_ref, kc_ref, vc_ref, cb_ref, o_ref):
    rows = k_ref.shape[0] // GRID_W
    r0 = pl.program_id(2) * NAT_QROWS
    kb = jnp.clip(r0 - NA_ROWS // 2, 0, rows - NAT_KROWS)
    koff = pl.multiple_of(kb * GRID_W, GRID_W)
    nq = NAT_QROWS * GRID_W
    nk = NAT_KROWS * GRID_W
    q = q_ref[...] * ATTN_SCALE
    k = k_ref[pl.ds(koff, nk), :]
    v = _with_ones(v_ref[pl.ds(koff, nk), :])
    kc = kc_ref[0].astype(BF16)
    vc = _with_ones(vc_ref[0].astype(BF16))
    left = _half_mask((GRID_W, LANES), 0)
    scores = []
    for half in range(2):
        bias_rows = []
        for a in range(NAT_QROWS):
            r = r0 + a
            rs = jnp.clip(r - NA_ROWS // 2, 0, rows - NA_ROWS)
            blocks = []
            for ip in range(NAT_KROWS // 2):
                sides = []
                for side in range(2):
                    kr = kb + 2 * ip + side
                    ok = (kr >= rs) & (kr < rs + NA_ROWS)
                    ri = jnp.clip(kr - r + NA_ROWS - 1, 0, NAT_RI - 1)
                    sides.append(jnp.where(ok, cb_ref[half, ri], NEG_INF))
                blocks.append(jnp.where(left, sides[0], sides[1]))
            bias_rows.append(jnp.concatenate(blocks, axis=1))
        bias = jnp.concatenate(bias_rows, axis=0)
        qm = _select_half(q, half)
        scores.append([_dot_nt(qm, k) + bias, _dot_nt(qm, kc)])
    outs = [_softmax_pv(s, [v, vc]) for s in scores]
    o_ref[...] = jnp.where(_half_mask((nq, LANES), 0), outs[0], outs[1]).astype(o_ref.dtype)


def _nat_col_table(rpb):
    qc = np.arange(GRID_W)[:, None]
    kc = np.arange(GRID_W)[None, :]
    cs = np.clip(qc - NA_COLS // 2, 0, GRID_W - NA_COLS)
    valid = (kc >= cs) & (kc < cs + NA_COLS)
    n_ci = 2 * NA_COLS - 1
    lo = GRID_W - 1 - (NA_COLS - 1)
    v = jnp.pad(rpb, ((0, 0), (0, 0), (lo, 2 * GRID_W - n_ci - lo)))
    flat = jnp.tile(v, (1, 1, GRID_W))[..., :GRID_W * (2 * GRID_W - 1)]
    toep = flat.reshape(rpb.shape[0], rpb.shape[1], GRID_W, 2 * GRID_W - 1)[..., GRID_W - 1:]
    tab = jnp.where(valid[None, None], toep, NEG_INF)
    return jnp.concatenate([tab, tab], axis=-1)


def _nat_latent(qkv, kc, vc, cb, l, batch, seq):
    nq = NAT_QROWS * GRID_W
    tpb = seq // nq
    past = kc.shape[2]
    return pl.pallas_call(
        _nat_kernel,
        out_shape=jax.ShapeDtypeStruct((batch * seq, BRANCH_W), BF16),
        grid=(batch, H_B // 2, tpb),
        in_specs=[
            pl.BlockSpec((nq, LANES), lambda b, p, i: (b * tpb + i, T_BQ * 4 + p)),
            pl.BlockSpec((seq, LANES), lambda b, p, i: (b, T_BK * 4 + p)),
            pl.BlockSpec((seq, LANES), lambda b, p, i: (b, T_BV * 4 + p)),
            pl.BlockSpec((1, None, past, LANES), lambda b, p, i: (b, l, 0, p)),
            pl.BlockSpec((1, None, past, LANES), lambda b, p, i: (b, l, 0, p)),
            pl.BlockSpec((2, NAT_RI, GRID_W, LANES), lambda b, p, i: (l * (H_B // 2) + p, 0, 0, 0)),
        ],
        out_specs=pl.BlockSpec((nq, LANES), lambda b, p, i: (b * tpb + i, p)),
        compiler_params=_cparams("parallel", "parallel", "parallel"),
        name="nat_latent",
    )(qkv, qkv, qkv, kc, vc, cb)


SWA_QBLK = 512
SWA_KBLK = SWA_QBLK + 2 * SWA_WINDOW


def _swa_kernel(sink_ref, q_ref, kv_ref, kc_ref, vc_ref, o_ref):
    seq = kv_ref.shape[0]
    p = pl.program_id(1)
    t0 = pl.program_id(2) * SWA_QBLK
    k0 = pl.multiple_of(jnp.clip(t0 - SWA_WINDOW, 0, seq - SWA_KBLK), SWA_WINDOW)
    q = q_ref[...] * ATTN_SCALE
    k = kv_ref[pl.ds(k0, SWA_KBLK), 0:LANES]
    v = _with_ones(kv_ref[pl.ds(k0, SWA_KBLK), LANES:2 * LANES])
    kc = kc_ref[0].astype(BF16)
    vc = _with_ones(vc_ref[0].astype(BF16))
    tq = t0 + lax.broadcasted_iota(jnp.int32, (SWA_QBLK, SWA_KBLK), 0)
    tk = k0 + lax.broadcasted_iota(jnp.int32, (SWA_QBLK, SWA_KBLK), 1)
    inside = jnp.abs(tq - tk) <= SWA_WINDOW
    kv_half = p // ((H_C // 2) // KV_C)
    scores = []
    for half in range(2):
        qm = _select_half(q, half)
        qm = jnp.where(kv_half == half, qm, _swap_halves(qm))
        scores.append([jnp.where(inside, _dot_nt(qm, k), NEG_INF), _dot_nt(qm, kc)])
    outs = []
    for half in range(2):
        o = _softmax_pv(scores[half], [v, vc], extra=sink_ref[2 * p + half])
        outs.append(jnp.where(kv_half == half, o, _swap_halves(o)))
    o_ref[...] = jnp.where(_half_mask(q.shape, 0), outs[0], outs[1]).astype(o_ref.dtype)


def _swa_latent(qkv, kc, vc, sink, l, batch, seq):
    tpb = seq // SWA_QBLK
    past = kc.shape[2]
    return pl.pallas_call(
        _swa_kernel,
        out_shape=jax.ShapeDtypeStruct((batch * seq, BRANCH_W), BF16),
        grid=(batch, H_C // 2, tpb),
        in_specs=[
            pl.BlockSpec(memory_space=pltpu.SMEM),
            pl.BlockSpec((SWA_QBLK, LANES), lambda b, p, i: (b * tpb + i, T_CQ * 4 + p)),
            pl.BlockSpec((seq, COL_TILE), lambda b, p, i: (b, T_MISC)),
            pl.BlockSpec((1, None, past, LANES), lambda b, p, i: (b, l, 0, 0)),
            pl.BlockSpec((1, None, past, LANES), lambda b, p, i: (b, l, 0, 0)),
        ],
        out_specs=pl.BlockSpec((SWA_QBLK, LANES), lambda b, p, i: (b * tpb + i, p)),
        compiler_params=_cparams("parallel", "parallel", "parallel"),
        name="swa_latent",
    )(sink, qkv, qkv, kc, vc)


DIFF_QBLK = 512


def _diff_kernel(lam_ref, q_ref, k_ref, v_ref, kc_ref, vc_ref, o_ref):
    q = q_ref[...] * ATTN_SCALE
    ks = [k_ref[...], kc_ref[0].astype(BF16)]
    vs = [_with_ones(v_ref[...]), _with_ones(vc_ref[0].astype(BF16))]
    o_ref[...] = _diff_head(q, ks, vs, lam_ref[0])


def _diff_latent(qkv, kc, vc, lam, l, batch, seq):
    tpb = seq // DIFF_QBLK
    past = kc.shape[2]
    return pl.pallas_call(
        _diff_kernel,
        out_shape=jax.ShapeDtypeStruct((batch * seq, BRANCH_W), F32),
        grid=(batch, H_D, tpb),
        in_specs=[
            pl.BlockSpec(memory_space=pltpu.SMEM),
            pl.BlockSpec((DIFF_QBLK, LANES), lambda b, h, i: (b * tpb + i, T_DQ * 4 + h)),
            pl.BlockSpec((seq, LANES), lambda b, h, i: (b, T_DK * 4 + h)),
            pl.BlockSpec((seq, LANES), lambda b, h, i: (b, T_DV * 4 + h)),
            pl.BlockSpec((1, None, past, LANES), lambda b, h, i: (b, l, 0, h)),
            pl.BlockSpec((1, None, past, LANES), lambda b, h, i: (b, l, 0, h)),
        ],
        out_specs=pl.BlockSpec((DIFF_QBLK, LANES), lambda b, h, i: (b * tpb + i, h)),
        compiler_params=_cparams("parallel", "parallel", "parallel"),
        name="diff_latent",
    )(lam, qkv, qkv, qkv, kc, vc)


def _rms_heads(y, g_ref, per_head_gain):
    outs = []
    for c in range(BRANCH_W // LANES):
        sl = slice(c * LANES, (c + 1) * LANES)
        g = g_ref[:, sl] if per_head_gain else g_ref[...]
        outs.append(_rms(y[:, sl], g))
    return jnp.concatenate(outs, axis=1)


def _first_lane(hit, lane_f):
    return jnp.min(jnp.where(hit, lane_f, float(LANES)), axis=1, keepdims=True)


def _top_group(logits):
    lane = lax.broadcasted_iota(jnp.int32, logits.shape, 1)
    gl = jnp.where(lane < N_GROUPS, logits, NEG_INF)
    return _first_lane(gl == jnp.max(gl, axis=1, keepdims=True), lane.astype(F32))


def _gate_weights(logits, group):
    lane = lax.broadcasted_iota(jnp.int32, logits.shape, 1)
    lane_f = lane.astype(F32)
    gl = jnp.where(lane < N_GROUPS, logits, NEG_INF)
    gmax = jnp.max(gl, axis=1, keepdims=True)
    g_logit = jnp.sum(jnp.where(lane_f == group, logits, 0.0), axis=1, keepdims=True)
    gw = jnp.exp(g_logit - gmax) / jnp.sum(jnp.exp(gl - gmax), axis=1, keepdims=True)
    e_lane = lane - N_GROUPS
    e_group = lax.shift_right_arithmetic(e_lane, EXP_PER_GROUP.bit_length() - 1).astype(F32)
    in_group = (e_lane >= 0) & (e_lane < N_EXPERTS) & (e_group == group)
    el = jnp.where(in_group, logits, NEG_INF)
    e1 = jnp.max(el, axis=1, keepdims=True)
    i1 = _first_lane(el == e1, lane_f)
    el2 = jnp.where(lane_f == i1, NEG_INF, el)
    e2 = jnp.max(el2, axis=1, keepdims=True)
    i2 = _first_lane(el2 == e2, lane_f)
    t = jnp.exp(e2 - e1)
    w1 = gw / (1.0 + t)
    return jnp.where(lane_f == i1, w1, jnp.where(lane_f == i2, w1 * t, 0.0))


def _merge_kernel(ya, yb, yc, yd, ao, gt0, gt1, gt2, gt3, x_ref, mod_ref, anorm, subln, nffn,
                  wbr, wout, wr, xo_ref, h2_ref, group_ref, *, yd_scale):
    a = _sigmoid(ao[...].astype(F32)) * _rms_heads(ya[...], anorm, True)
    d = _rms_heads(yd[...], subln, False) * yd_scale
    branches = (a, yb[...], yc[...], d)
    mix = None
    for n, (br, gt) in enumerate(zip(branches, (gt0, gt1, gt2, gt3))):
        term = _sigmoid(gt[...].astype(F32)) * _dot(br.astype(BF16), wbr[n])
        mix = term if mix is None else mix + term
    out = _dot(mix.astype(BF16), wout[...])
    xn = x_ref[...] + mod_ref[0, 2:3, :] * out
    xo_ref[...] = xn
    h2 = _rms(xn, nffn[...]) * (1.0 + mod_ref[0, 4:5, :]) + mod_ref[0, 3:4, :]
    h2_ref[...] = h2
    group_ref[...] = jnp.broadcast_to(_top_group(_dot(h2.astype(BF16), wr[...])), group_ref.shape)


def _merge(ys, qkv, x, mod3, a_norm, subln, norm_ffn, w_branch, w_out, w_route, lam_init, l):
    n = x.shape[0]
    tm = 512
    tiles_per_mod = n // mod3.shape[0] // tm
    y_spec = pl.BlockSpec((tm, BRANCH_W), lambda i: (i, 0))
    gt_spec = lambda k: pl.BlockSpec((tm, D_MODEL), lambda i: (i, T_GT * COL_TILE // D_MODEL + k))
    full = lambda shape: pl.BlockSpec(shape, lambda i: (0,) * len(shape))
    layer = lambda shape: pl.BlockSpec((None,) + shape, lambda i: (l,) + (0,) * len(shape))
    x_spec = pl.BlockSpec((tm, D_MODEL), lambda i: (i, 0))
    return pl.pallas_call(
        functools.partial(_merge_kernel, yd_scale=1.0 - lam_init),
        out_shape=(
            jax.ShapeDtypeStruct((n, D_MODEL), F32),
            jax.ShapeDtypeStruct((n, D_MODEL), F32),
            jax.ShapeDtypeStruct((n, LANES), F32),
        ),
        grid=(n // tm,),
        in_specs=[
            y_spec, y_spec, y_spec, y_spec,
            pl.BlockSpec((tm, COL_TILE), lambda i: (i, T_AO)),
            gt_spec(0), gt_spec(1), gt_spec(2), gt_spec(3),
            x_spec,
            pl.BlockSpec((1, 6, D_MODEL), lambda i: (i // tiles_per_mod, 0, 0)),
            full((1, BRANCH_W)), full((1, LANES)), full((1, D_MODEL)),
            layer((N_BRANCH, BRANCH_W, D_MODEL)), layer((D_MODEL, D_MODEL)), layer((D_MODEL, LANES)),
        ],
        out_specs=(x_spec, x_spec, pl.BlockSpec((tm, LANES), lambda i: (i, 0))),
        compiler_params=_cparams("parallel"),
        name="merge_route",
    )(*ys, qkv, qkv, qkv, qkv, qkv, x, mod3, a_norm.reshape(1, BRANCH_W), subln.reshape(1, LANES),
      norm_ffn.reshape(1, D_MODEL), w_branch, w_out, w_route)


MOE_TILE = 512
ROW_DMA_UNROLL = 16


def _row_copies(src_row, dst_row, sem, n):
    def copy(i):
        return pltpu.make_async_copy(src_row(i), dst_row(i), sem)

    def start(i, c):
        copy(i).start()
        return c

    def wait(i, c):
        copy(i).wait()
        return c

    lax.fori_loop(0, n, start, 0, unroll=ROW_DMA_UNROLL)
    lax.fori_loop(0, n, wait, 0, unroll=ROW_DMA_UNROLL)


def _moe_scatter_kernel(slot_ref, h_ref, init_ref, sorted_ref, sem):
    del init_ref
    base = pl.program_id(0) * MOE_TILE
    _row_copies(lambda i: h_ref.at[pl.ds(i, 1), :],
                lambda i: sorted_ref.at[pl.ds(slot_ref[base + i], 1), :], sem, MOE_TILE)


def _moe_group_kernel(tile_group_ref, n_used_ref, h_ref, wr_ref, wg_ref, wu_ref, wd_ref, o_ref):
    t = pl.program_id(0)

    @pl.when(t < n_used_ref[0])
    def _():
        group = tile_group_ref[t]
        h = h_ref[...].astype(BF16)
        gates = _gate_weights(_dot(h, wr_ref[...]), group.astype(F32))
        lane = lax.broadcasted_iota(jnp.int32, gates.shape, 1)
        acc = None
        for e in range(EXP_PER_GROUP):
            g = jnp.sum(jnp.where(lane == N_GROUPS + group * EXP_PER_GROUP + e, gates, 0.0), axis=1, keepdims=True)
            hg = _dot(h, wg_ref[0, e])
            hid = hg * _sigmoid(hg) * _dot(h, wu_ref[0, e]) * g
            term = _dot(hid.astype(BF16), wd_ref[0, e])
            acc = term if acc is None else acc + term
        o_ref[...] = acc

    @pl.when(t >= n_used_ref[0])
    def _():
        o_ref[...] = jnp.zeros_like(o_ref)


def _moe_gather_kernel(slot_ref, y_ref, x_ref, mod_ref, nf_ref, o_ref, buf, sem, *, final_norm):
    t = pl.program_id(0)
    nt = pl.num_programs(0)

    def copy(tile, b, i):
        return pltpu.make_async_copy(y_ref.at[pl.ds(slot_ref[tile * MOE_TILE + i], 1), :],
                                     buf.at[b, pl.ds(i, 1), :], sem.at[b])

    def start_tile(tile, b):
        def start(i, c):
            copy(tile, b, i).start()
            return c

        lax.fori_loop(0, MOE_TILE, start, 0, unroll=ROW_DMA_UNROLL)

    cur = t % 2

    @pl.when(t == 0)
    def _():
        start_tile(0, 0)

    @pl.when(t + 1 < nt)
    def _():
        start_tile(t + 1, 1 - cur)

    def wait(i, c):
        copy(t, cur, i).wait()
        return c

    lax.fori_loop(0, MOE_TILE, wait, 0, unroll=ROW_DMA_UNROLL)
    y = x_ref[...] + mod_ref[0, 5:6, :] * buf[cur]
    o_ref[...] = _rms(y, nf_ref[...]) if final_norm else y


def _moe(h2, group, x, mod3, w_route, wg, wu, wd, norm_final, final_norm, l):
    n = x.shape[0]
    tm = MOE_TILE
    n_tiles = n // tm + N_GROUPS
    onehot = (group[:, None] == jnp.arange(N_GROUPS, dtype=jnp.int32)[None]).astype(jnp.int32)
    csum = jnp.cumsum(onehot, axis=0)
    counts = csum[-1]
    rank = jnp.sum((csum - onehot) * onehot, axis=1)
    group_tiles = (counts + tm - 1) // tm
    tile_end = jnp.cumsum(group_tiles)
    group_start = (tile_end - group_tiles) * tm
    slot = (jnp.sum(onehot * group_start[None], axis=1) + rank).astype(jnp.int32)
    tile_group = jnp.minimum(jnp.sum(jnp.arange(n_tiles, dtype=jnp.int32)[:, None] >= tile_end[None], axis=1),
                             N_GROUPS - 1).astype(jnp.int32)
    n_used = tile_end[-1:].astype(jnp.int32)

    sorted_h = pl.pallas_call(
        _moe_scatter_kernel,
        out_shape=jax.ShapeDtypeStruct((n_tiles * tm, D_MODEL), F32),
        grid_spec=pltpu.PrefetchScalarGridSpec(
            num_scalar_prefetch=1, grid=(n // tm,),
            in_specs=[pl.BlockSpec((tm, D_MODEL), lambda i, slot: (i, 0)), pl.BlockSpec(memory_space=pl.ANY)],
            out_specs=pl.BlockSpec(memory_space=pl.ANY),
            scratch_shapes=[pltpu.SemaphoreType.DMA],
        ),
        input_output_aliases={2: 0},
        compiler_params=_cparams("arbitrary"),
        name="moe_scatter",
    )(slot, h2, jnp.zeros((n_tiles * tm, D_MODEL), F32))

    grp = lambda shape: pl.BlockSpec((None, 1) + shape, lambda t, tg, nu: (l, tg[t], 0, 0, 0))
    sorted_y = pl.pallas_call(
        _moe_group_kernel,
        out_shape=jax.ShapeDtypeStruct((n_tiles * tm, D_MODEL), F32),
        grid_spec=pltpu.PrefetchScalarGridSpec(
            num_scalar_prefetch=2, grid=(n_tiles,),
            in_specs=[
                pl.BlockSpec((tm, D_MODEL), lambda t, tg, nu: (t, 0)),
                pl.BlockSpec((None, D_MODEL, LANES), lambda t, tg, nu: (l, 0, 0)),
                grp((EXP_PER_GROUP, D_MODEL, D_EXPERT)), grp((EXP_PER_GROUP, D_MODEL, D_EXPERT)),
                grp((EXP_PER_GROUP, D_EXPERT, D_MODEL)),
            ],
            out_specs=pl.BlockSpec((tm, D_MODEL), lambda t, tg, nu: (t, 0)),
        ),
        compiler_params=_cparams("arbitrary"),
        name="moe_experts",
    )(tile_group, n_used, sorted_h, w_route, wg, wu, wd)

    tiles_per_mod = n // mod3.shape[0] // tm
    row = pl.BlockSpec((tm, D_MODEL), lambda i, slot: (i, 0))
    return pl.pallas_call(
        functools.partial(_moe_gather_kernel, final_norm=final_norm),
        out_shape=jax.ShapeDtypeStruct((n, D_MODEL), F32),
        grid_spec=pltpu.PrefetchScalarGridSpec(
            num_scalar_prefetch=1, grid=(n // tm,),
            in_specs=[
                pl.BlockSpec(memory_space=pl.ANY), row,
                pl.BlockSpec((1, 6, D_MODEL), lambda i, slot: (i // tiles_per_mod, 0, 0)),
                pl.BlockSpec((1, D_MODEL), lambda i, slot: (0, 0)),
            ],
            out_specs=row,
            scratch_shapes=[pltpu.VMEM((2, tm, D_MODEL), F32), pltpu.SemaphoreType.DMA((2,))],
        ),
        compiler_params=_cparams("arbitrary"),
        name="moe_gather",
    )(slot, sorted_y, x, mod3, norm_final.reshape(1, D_MODEL))


def _reorder_w_in(w):
    pad = jnp.zeros((D_MODEL, COL_TILE - 2 * LANES - 16), w.dtype)
    cols = [w[:, 0:2048], w[:, 2064:4112], w[:, 4368:5904], w[:, 4112:4368], w[:, 2048:2064], pad, w[:, 5904:]]
    return jnp.concatenate(cols, axis=1).astype(BF16)


def _rope_tables(seq):
    nf = DH_C // 4
    inv = ROPE_BASE ** (-jnp.arange(nf, dtype=F32) / nf)
    t = jnp.arange(seq)
    ang_r = (t // GRID_W).astype(F32)[:, None] * inv[None, :]
    ang_c = (t % GRID_W).astype(F32)[:, None] * inv[None, :]
    cos = jnp.concatenate([jnp.cos(ang_r)] * 2 + [jnp.cos(ang_c)] * 2, axis=1)
    sin = jnp.concatenate([-jnp.sin(ang_r), jnp.sin(ang_r), -jnp.sin(ang_c), jnp.sin(ang_c)], axis=1)
    return jnp.tile(cos, (1, 2)), jnp.tile(sin, (1, 2))


def _layer(l, x, mod3, seq, lat, p):
    batch = x.shape[0] // seq
    lam_init = 0.8 - 0.6 * math.exp(-0.3 * l)
    ctx = lat is None
    proj = _qkv_proj(x, p["norm_mix"][l], mod3, p["w_in"][l], None if ctx else p["rope"], ctx,
                     tm=1024 if ctx else 2048)
    qkv, gates_raw = proj[:2]
    gates = gates_raw[:, :16] + jnp.concatenate([p["i_bias"][l].reshape(-1), p["f_bias"][l].reshape(-1)])[None]
    lam = (jnp.exp(jnp.sum(p["lq1"][l] * p["lk1"][l])) - jnp.exp(jnp.sum(p["lq2"][l] * p["lk2"][l]))
           + lam_init).reshape(1).astype(F32)
    sink = p["sink"][l]
    new = None
    if ctx:
        C0 = jnp.zeros((batch, 1, 2, H_A, DH_A, DH_A), F32)
        n0 = jnp.zeros((batch, 2, H_A, DH_A), F32)
        m0 = jnp.zeros((batch, 2, H_A), F32)
        ya, Cs, ns, ms = _mlstm(qkv, gates, C0, 0, n0, m0, batch, seq)
        yb, yc, yd = _ctx_attention(qkv, lam, sink, batch, seq)
        bk, bv, dk, dv, ckv = proj[2:]
        new = (bk.reshape(batch, seq, H_B, DH_B), bv.reshape(batch, seq, H_B, DH_B),
               ckv[:, :LANES].reshape(batch, seq, KV_C, DH_C), ckv[:, LANES:].reshape(batch, seq, KV_C, DH_C),
               dk.reshape(batch, seq, H_D, 2 * DH_D), dv.reshape(batch, seq, H_D, 2 * DH_D), Cs, ns, ms)
    else:
        nat_k, nat_v, swa_k, swa_v, diff_k, diff_v, C0, n0, m0 = lat
        flat = lambda a: a.reshape(a.shape[:3] + (-1,))
        ya, _, _, _ = _mlstm(qkv, gates, C0, l, n0[:, l], m0[:, l], batch, seq)
        yb = _nat_latent(qkv, flat(nat_k), flat(nat_v), p["nat_table"], l, batch, seq)
        yc = _swa_latent(qkv, flat(swa_k), flat(swa_v), sink, l, batch, seq)
        yd = _diff_latent(qkv, flat(diff_k), flat(diff_v), lam, l, batch, seq)
    xn, h2, group = _merge((ya, yb, yc, yd), qkv, x, mod3, p["a_norm"][l], p["subln"][l],
                           p["norm_ffn"][l], p["w_branch"], p["w_out"], p["w_route"], lam_init, l)
    y = _moe(h2, group[:, 0].astype(jnp.int32), xn, mod3, p["w_route"], p["wg"], p["wu"], p["wd"],
             p["norm_final"], final_norm=(l == DEPTH - 1), l=l)
    return y, new


def kernel(x_prompt, x_sample, cache_nat_k, cache_nat_v, cache_swa_k, cache_swa_v, cache_diff_k, cache_diff_v, state_mlstm_C, state_mlstm_n, state_mlstm_m, c, c_ctx, norm_mix, norm_ffn, norm_final, w_mod, b_mod, w_in, mlstm_i_bias, mlstm_f_bias, mlstm_norm, nat_rpb, swa_sink, diff_lq1, diff_lk1, diff_lq2, diff_lk2, diff_subln, w_branch, w_out, router_group, router_expert, w_exp_gate, w_exp_up, w_exp_down):
    b_ctx, s_ctx, _ = x_prompt.shape
    b_lat, s_lat, _ = x_sample.shape
    cond8 = jnp.concatenate([c_ctx[None], c, jnp.zeros((8 - 1 - b_lat, D_MODEL), F32)], axis=0)
    mod = _modulation(cond8, w_mod, b_mod).reshape(DEPTH, 8, 6, D_MODEL)
    route_pad = jnp.zeros((DEPTH, D_MODEL, LANES - N_GROUPS - N_EXPERTS), F32)
    p = dict(
        norm_mix=norm_mix, norm_ffn=norm_ffn, norm_final=norm_final,
        w_in=[_reorder_w_in(w_in[l]) for l in range(DEPTH)],
        i_bias=mlstm_i_bias, f_bias=mlstm_f_bias, a_norm=mlstm_norm, sink=swa_sink,
        nat_table=_nat_col_table(nat_rpb.reshape((DEPTH * H_B,) + nat_rpb.shape[2:])),
        lq1=diff_lq1, lk1=diff_lk1, lq2=diff_lq2, lk2=diff_lk2, subln=diff_subln,
        w_branch=w_branch.astype(BF16), w_out=w_out.astype(BF16),
        w_route=jnp.concatenate([router_group, router_expert, route_pad], axis=-1).astype(BF16),
        wg=w_exp_gate.astype(BF16).reshape(DEPTH, N_GROUPS, EXP_PER_GROUP, D_MODEL, D_EXPERT),
        wu=w_exp_up.astype(BF16).reshape(DEPTH, N_GROUPS, EXP_PER_GROUP, D_MODEL, D_EXPERT),
        wd=w_exp_down.astype(BF16).reshape(DEPTH, N_GROUPS, EXP_PER_GROUP, D_EXPERT, D_MODEL),
        rope=_rope_tables(s_lat),
    )
    yp = x_prompt.reshape(b_ctx * s_ctx, D_MODEL)
    ys = x_sample.reshape(b_lat * s_lat, D_MODEL)
    news = []
    for l in range(DEPTH):
        yp, new = _layer(l, yp, mod[l, 0:1], s_ctx, None, p)
        news.append(new)
        lat = (cache_nat_k, cache_nat_v, cache_swa_k, cache_swa_v, cache_diff_k, cache_diff_v,
               state_mlstm_C, state_mlstm_n, state_mlstm_m)
        ys, _ = _layer(l, ys, mod[l, 1:1 + b_lat], s_lat, lat, p)
    def stack_layers(parts):
        flat = [a.reshape(a.shape[0], 1, -1) for a in parts]
        return jnp.concatenate(flat, axis=1).reshape((parts[0].shape[0], DEPTH) + parts[0].shape[1:])

    stacked = tuple(stack_layers([news[l][k] for l in range(DEPTH)]) for k in range(9))
    return (yp.reshape(b_ctx, s_ctx, D_MODEL), ys.reshape(b_lat, s_lat, D_MODEL)) + stacked
```

```python
import functools
import math

import numpy as np
import jax
import jax.numpy as jnp
from jax import lax
from jax.experimental import pallas as pl
from jax.experimental.pallas import tpu as pltpu

F32 = jnp.float32
BF16 = jnp.bfloat16

D_MODEL = 1024
DEPTH = 2
GRID_W = 64
BRANCH_W = 512
N_BRANCH = 4
H_A, DH_A, MLSTM_CHUNK = 4, 128, 128
H_B, DH_B = 8, 64
NA_ROWS, NA_COLS = 8, 16
H_C, KV_C, DH_C = 8, 2, 64
SWA_WINDOW = 128
H_D, DH_D = 4, 64
N_GROUPS, EXP_PER_GROUP = 4, 4
N_EXPERTS = N_GROUPS * EXP_PER_GROUP
D_EXPERT = 256
ROPE_BASE = 10000.0
EPS = 1e-6

LANES = 128
HALF = 64
COL_TILE = 512
N_COL_TILES = 20
N_PROJ = COL_TILE * N_COL_TILES
T_AQ, T_AK, T_AV, T_AO, T_BQ, T_BK, T_BV, T_CQ, T_DQ, T_DK, T_DV, T_MISC, T_GT = range(13)
MISC_GATE_OFF = 256
VMEM_LIMIT = 56 * 1024 * 1024
NEG_INF = float("-inf")

_NT = (((1,), (1,)), ((), ()))
_TN = (((0,), (0,)), ((), ()))


def _cparams(*sem):
    return pltpu.CompilerParams(dimension_semantics=sem, vmem_limit_bytes=VMEM_LIMIT)


def _dot(a, b):
    return jnp.dot(a, b, preferred_element_type=F32)


def _dot_nt(a, b):
    return lax.dot_general(a, b, _NT, preferred_element_type=F32)


def _dot_tn(a, b):
    return lax.dot_general(a, b, _TN, preferred_element_type=F32)


def _rms(x, g):
    return x * lax.rsqrt(jnp.mean(x * x, axis=-1, keepdims=True) + EPS) * g


def _sigmoid(x):
    return 0.5 * jnp.tanh(0.5 * x) + 0.5


def _mod_kernel(c_ref, w_ref, b_ref, o_ref):
    c = c_ref[...]
    s = (c * _sigmoid(c)).astype(BF16)
    o_ref[0] = _dot(s, w_ref[0].astype(BF16)) + b_ref[0]


def _modulation(cond8, w_mod, b_mod):
    tn = 1536
    n_out = 6 * D_MODEL
    return pl.pallas_call(
        _mod_kernel,
        out_shape=jax.ShapeDtypeStruct((DEPTH, 8, n_out), F32),
        grid=(DEPTH, n_out // tn),
        in_specs=[
            pl.BlockSpec((8, D_MODEL), lambda l, j: (0, 0)),
            pl.BlockSpec((1, D_MODEL, tn), lambda l, j: (l, 0, j)),
            pl.BlockSpec((1, 1, tn), lambda l, j: (l, 0, j)),
        ],
        out_specs=pl.BlockSpec((1, 8, tn), lambda l, j: (l, 0, j)),
        compiler_params=_cparams("parallel", "parallel"),
        name="modulation",
    )(cond8, w_mod, b_mod.reshape(DEPTH, 1, n_out))


CACHE_TILES = (T_BK, T_BV, T_DK, T_DV)
QKV_ROW_CHUNK = 512


def _rope128(x, cos, sin):
    lane = lax.broadcasted_iota(jnp.int32, x.shape, 1)
    partner = jnp.where((lane % 32) < 16, pltpu.roll(x, LANES - 16, 1), pltpu.roll(x, 16, 1))
    return x * cos + partner * sin


def _qkv_kernel(x_ref, nw_ref, mod_ref, w_ref, *rest, rope, cache, n_aliased):
    rest = list(rest)
    if rope:
        cos_ref, sin_ref = rest[:2]
        rest = rest[2:]
    rest = rest[n_aliased:]
    o_ref, g_ref = rest[:2]
    cache_refs = rest[2:-1]
    h_scr = rest[-1]
    j = pl.program_id(1)

    @pl.when(j == 0)
    def _():
        h = _rms(x_ref[...], nw_ref[...]) * (1.0 + mod_ref[0, 1:2, :]) + mod_ref[0, 0:1, :]
        h_scr[...] = h.astype(BF16)

    def tile_kind(cond, store):
        @pl.when(cond)
        def _():
            for r in range(h_scr.shape[0] // QKV_ROW_CHUNK):
                rows = slice(r * QKV_ROW_CHUNK, (r + 1) * QKV_ROW_CHUNK)
                store(rows, _dot(h_scr[rows, :], w_ref[...]))

    def store_batches(ref, rows, val):
        seq = ref.shape[1]
        for bb in range(QKV_ROW_CHUNK // seq):
            ref[rows.start // seq + bb] = val[bb * seq:(bb + 1) * seq]

    def roped(rows, acc, c):
        sl = slice(c * LANES, (c + 1) * LANES)
        return _rope128(acc[:, sl], cos_ref[rows, :], sin_ref[rows, :]).astype(BF16)

    def store_plain(rows, acc):
        o_ref[rows, :] = acc.astype(BF16)

    def store_ak(rows, acc):
        o_ref[rows, :] = (acc * (DH_A ** -0.5)).astype(BF16)

    def store_misc(rows, acc):
        o_ref[rows, :] = acc.astype(BF16)
        if rope:
            o_ref[rows, :LANES] = roped(rows, acc, 0)
        g_ref[rows, :] = acc[:, MISC_GATE_OFF:MISC_GATE_OFF + LANES]
        if cache:
            store_batches(cache_refs[-2], rows, acc[:, :LANES])
            store_batches(cache_refs[-1], rows, acc[:, LANES:2 * LANES])

    def store_rope(rows, acc):
        for c in range(COL_TILE // LANES):
            o_ref[rows, c * LANES:(c + 1) * LANES] = roped(rows, acc, c)

    special = (j == T_AK) | (j == T_MISC)
    tile_kind(j == T_AK, store_ak)
    tile_kind(j == T_MISC, store_misc)
    if rope:
        is_rope = (j == T_CQ) | (j == T_DQ) | (j == T_DK)
        special = special | is_rope
        tile_kind(is_rope, store_rope)
    if cache:
        for t, ref in zip(CACHE_TILES, cache_refs[:-2]):
            def store_cached(rows, acc, ref=ref):
                o_ref[rows, :] = acc.astype(BF16)
                store_batches(ref, rows, acc)
            special = special | (j == t)
            tile_kind(j == t, store_cached)
    tile_kind(jnp.logical_not(special), store_plain)


def _qkv_proj(x, norm_w, mod3, w_in_r, rope_tabs, cache, tm):
    n = x.shape[0]
    tiles_per_mod = n // mod3.shape[0] // tm
    rope = rope_tabs is not None
    in_specs = [
        pl.BlockSpec((tm, D_MODEL), lambda i, j: (i, 0)),
        pl.BlockSpec((1, D_MODEL), lambda i, j: (0, 0)),
        pl.BlockSpec((1, 6, D_MODEL), lambda i, j: (i // tiles_per_mod, 0, 0)),
        pl.BlockSpec((D_MODEL, COL_TILE), lambda i, j: (0, j)),
    ]
    args = [x, norm_w.reshape(1, D_MODEL), mod3, w_in_r]
    if rope:
        tiles_per_seq = rope_tabs[0].shape[0] // tm
        in_specs += [pl.BlockSpec((tm, LANES), lambda i, j: (i % tiles_per_seq, 0))] * 2
        args += list(rope_tabs)
    row = lambda w: pl.BlockSpec((tm, w), lambda i, j: (i, 0))
    out_shape = [jax.ShapeDtypeStruct((n, N_PROJ), BF16), jax.ShapeDtypeStruct((n, LANES), F32)]
    out_specs = [pl.BlockSpec((tm, COL_TILE), lambda i, j: (i, j)), row(LANES)]
    aliases = {}
    n_aliased = 0
    if cache:
        seq, layer = cache["seq"], cache["layer"]
        widths = [COL_TILE] * len(CACHE_TILES) + [LANES, LANES]
        out_shape += [jax.ShapeDtypeStruct((n // seq, DEPTH, seq, w), F32) for w in widths]
        out_specs += [pl.BlockSpec((tm // seq, None, seq, w), lambda i, j: (i, layer, 0, 0)) for w in widths]
        prev = cache["prev"] or [jnp.zeros(s.shape, s.dtype) for s in out_shape[2:]]
        n_aliased = len(widths)
        aliases = {len(args) + k: 2 + k for k in range(n_aliased)}
        in_specs += [pl.BlockSpec(memory_space=pl.ANY)] * n_aliased
        args += list(prev)
    return pl.pallas_call(
        functools.partial(_qkv_kernel, rope=rope, cache=bool(cache), n_aliased=n_aliased),
        input_output_aliases=aliases,
        out_shape=tuple(out_shape),
        grid=(n // tm, N_COL_TILES),
        in_specs=in_specs,
        out_specs=tuple(out_specs),
        scratch_shapes=[pltpu.VMEM((tm, D_MODEL), BF16)],
        compiler_params=_cparams("parallel", "arbitrary"),
        name="qkv_rope" if rope else "qkv",
    )(*args)


def _log_sigmoid(x):
    return jnp.minimum(x, 0.0) - jnp.log1p(jnp.exp(-jnp.abs(x)))


def _mlstm_chunk(c, h, d, state, q_ref, k_ref, v_ref, gr_ref, gc_ref, out_ref):
    L = MLSTM_CHUNK
    C, n, m = state
    row = lax.broadcasted_iota(jnp.int32, (L, L), 0)
    col = lax.broadcasted_iota(jnp.int32, (L, L), 1)
    tri = (col <= row) if d == 0 else (col >= row)
    tri_t = (row <= col) if d == 0 else (row >= col)
    off = pl.multiple_of(c * L, L)
    hs = slice(h * DH_A, (h + 1) * DH_A)
    qb = q_ref[pl.ds(off, L), hs]
    kb = k_ref[pl.ds(off, L), hs]
    vb = v_ref[pl.ds(off, L), hs]
    kx, vx = kb.astype(F32), vb.astype(F32)
    g_rows = gr_ref[h, c]
    g_cols = gc_ref[h, c]
    ig_r = g_rows[2 * d:2 * d + 1, :]
    lf_r = _log_sigmoid(g_rows[2 * d + 1:2 * d + 2, :])
    ig_c = g_cols[:, 2 * d:2 * d + 1]
    lf_c = _log_sigmoid(g_cols[:, 2 * d + 1:2 * d + 2])
    b_c = jnp.sum(jnp.where(tri, lf_r, 0.0), axis=1, keepdims=True)
    b_r = jnp.sum(jnp.where(tri_t, lf_c, 0.0), axis=0, keepdims=True)
    qk = _dot_nt(qb, kb)
    c_ext = jnp.concatenate([C.astype(BF16), jnp.broadcast_to(n.astype(BF16), (DH_A, DH_A))], axis=0)
    qc = _dot_nt(qb, c_ext)
    yield
    logd = jnp.where(tri, b_c - b_r + ig_r, NEG_INF)
    inter = jnp.broadcast_to(b_c + m, (L, L))
    m_t = jnp.maximum(inter, jnp.broadcast_to(jnp.max(logd, axis=1, keepdims=True), (L, L)))
    yield
    a = jnp.exp(inter - m_t)
    s = qk * jnp.exp(logd - m_t)
    sv = _dot(s.astype(BF16), _with_ones(vb))
    num = a * qc[:, :DH_A] + sv[:, :DH_A]
    den = a * qc[:, DH_A:] + sv[:, DH_A:]
    yield
    out_ref[pl.ds(off, L), hs] = num / jnp.maximum(jnp.abs(den), jnp.exp(-m_t))
    b_tot = jnp.sum(lf_r, axis=1, keepdims=True)
    g_c = b_tot - b_c + ig_c
    m_new = jnp.maximum(b_tot + m, jnp.max(g_c, axis=0, keepdims=True))
    yield
    decay = jnp.exp(b_tot + m - m_new)
    wk = jnp.exp(g_c - m_new)
    C_new = decay * C + _dot_tn((wk * vx).astype(BF16), kb)
    n_new = decay * n + jnp.sum(wk * kx, axis=0, keepdims=True)
    return C_new, n_new, m_new


def _run_staged(gens):
    results = [None] * len(gens)
    live = list(range(len(gens)))
    while live:
        for i in list(live):
            try:
                next(gens[i])
            except StopIteration as stop:
                results[i] = stop.value
                live.remove(i)
    return results


def _mlstm_kernel(q_ref, k_ref, v_ref, gr_ref, gc_ref, c0_ref, n0_ref, m0_ref,
                  h_ref, cs_ref, ns_ref, ms_ref, hb_scr, *, nc):
    chains = [(h, d) for h in range(H_A) for d in range(2)]

    def body(ci, states):
        gens = [_mlstm_chunk(ci if d == 0 else nc - 1 - ci, h, d, state, q_ref, k_ref, v_ref, gr_ref, gc_ref,
                             h_ref if d == 0 else hb_scr) for (h, d), state in zip(chains, states)]
        return tuple(_run_staged(gens))

    init = tuple((c0_ref[0, d, h], n0_ref[0, h, d:d + 1, :], m0_ref[0, h, d:d + 1, 0:1]) for h, d in chains)
    final = lax.fori_loop(0, nc, body, init)
    for (h, d), (C, n, m) in zip(chains, final):
        cs_ref[0, d, h] = C
        ns_ref[0, h, d:d + 1, :] = n
        ms_ref[0, h, d:d + 1, :] = jnp.broadcast_to(m, (1, LANES))
    h_ref[...] += hb_scr[...]


def _mlstm(qkv, gates, C0, l, n0, m0, batch, seq):
    nc = seq // MLSTM_CHUNK
    L = MLSTM_CHUNK
    ig = gates[:, :8].reshape(batch, nc, L, 2, H_A)
    fg = gates[:, 8:].reshape(batch, nc, L, 2, H_A)
    g4 = jnp.stack([ig[..., 0, :], fg[..., 0, :], ig[..., 1, :], fg[..., 1, :]], axis=-1)
    g_cols = jnp.transpose(g4, (3, 0, 1, 2, 4)).reshape(H_A, batch * nc, L, 4)
    g_rows = jnp.transpose(g4, (3, 0, 1, 4, 2)).reshape(H_A, batch * nc, 4, L)
    n0t = jnp.transpose(n0, (0, 2, 1, 3))
    m0t = jnp.broadcast_to(jnp.transpose(m0, (0, 2, 1))[..., None], (batch, H_A, 2, LANES))
    qkv_spec = lambda tile: pl.BlockSpec((seq, COL_TILE), lambda b: (b, tile))
    c_spec = pl.BlockSpec((1, 2, H_A, DH_A, DH_A), lambda b: (b, 0, 0, 0, 0))
    c0_spec = pl.BlockSpec((1, None, 2, H_A, DH_A, DH_A), lambda b: (b, l, 0, 0, 0, 0))
    state_spec = pl.BlockSpec((1, H_A, 2, LANES), lambda b: (b, 0, 0, 0))
    h, Cs, ns, ms = pl.pallas_call(
        functools.partial(_mlstm_kernel, nc=nc),
        out_shape=(
            jax.ShapeDtypeStruct((batch * seq, H_A * DH_A), F32),
            jax.ShapeDtypeStruct((batch, 2, H_A, DH_A, DH_A), F32),
            jax.ShapeDtypeStruct((batch, H_A, 2, DH_A), F32),
            jax.ShapeDtypeStruct((batch, H_A, 2, LANES), F32),
        ),
        grid=(batch,),
        in_specs=[
            qkv_spec(T_AQ), qkv_spec(T_AK), qkv_spec(T_AV),
            pl.BlockSpec((H_A, nc, 4, L), lambda b: (0, b, 0, 0)),
            pl.BlockSpec((H_A, nc, L, 4), lambda b: (0, b, 0, 0)),
            c0_spec, state_spec, state_spec,
        ],
        out_specs=(pl.BlockSpec((seq, H_A * DH_A), lambda b: (b, 0)), c_spec, state_spec, state_spec),
        scratch_shapes=[pltpu.VMEM((seq, H_A * DH_A), F32)],
        compiler_params=_cparams("parallel"),
        name="mlstm",
    )(qkv, qkv, qkv, g_rows, g_cols, C0, n0t, m0t)
    return h, Cs, jnp.transpose(ns, (0, 2, 1, 3)), jnp.transpose(ms[..., 0], (0, 2, 1))


ATTN_SCALE = DH_B ** -0.5


def _half_mask(shape, half):
    lane = lax.broadcasted_iota(jnp.int32, shape, 1)
    return (lane >= HALF) if half else (lane < HALF)


def _select_half(q, half):
    return jnp.where(_half_mask(q.shape, half), q, jnp.zeros_like(q))


def _swap_halves(x):
    return pltpu.roll(x.astype(F32), HALF, 1).astype(x.dtype)


def _with_ones(v):
    return jnp.concatenate([v, jnp.ones_like(v)], axis=1)


def _softmax_pv(scores, vs, extra=None):
    m = functools.reduce(jnp.maximum, [jnp.max(s, axis=1, keepdims=True) for s in scores])
    if extra is not None:
        m = jnp.maximum(m, extra)
    acc = functools.reduce(lambda x, y: x + y,
                           [_dot(jnp.exp((s - m).astype(BF16)), v) for s, v in zip(scores, vs)])
    denom = acc[:, LANES:]
    if extra is not None:
        denom = denom + jnp.exp(extra - m)
    return acc[:, :LANES] / denom


def _diff_head(q, ks, vs, lam):
    scores = [[_dot_nt(_select_half(q, comp), k) for k in ks] for comp in range(2)]
    o1, o2 = [_softmax_pv(s, vs) for s in scores]
    return o1 - lam * o2


def _ctx_attn_kernel(lam_ref, sink_ref, bq, bk, bv, cq, misc, dq, dk, dv, yb_ref, yc_ref, yd_ref):
    pairs = [slice(p * LANES, (p + 1) * LANES) for p in range(H_B // 2)]
    scores = [[[_dot_nt(_select_half(bq[:, sl] * ATTN_SCALE, half), bk[:, sl])] for half in range(2)] for sl in pairs]
    for sl, pair_scores in zip(pairs, scores):
        v = _with_ones(bv[:, sl])
        outs = [_softmax_pv(s, [v]) for s in pair_scores]
        yb_ref[:, sl] = jnp.where(_half_mask(outs[0].shape, 0), outs[0], outs[1]).astype(yb_ref.dtype)

    kc = misc[:, 0:LANES]
    vc = _with_ones(misc[:, LANES:2 * LANES])
    kv_of = lambda head: head // (H_C // KV_C)
    scores = []
    for p, sl in enumerate(pairs):
        pair_scores = []
        for half in range(2):
            qm = _select_half(cq[:, sl] * ATTN_SCALE, half)
            if kv_of(2 * p + half) != half:
                qm = _swap_halves(qm)
            pair_scores.append([_dot_nt(qm, kc)])
        scores.append(pair_scores)
    for p, (sl, pair_scores) in enumerate(zip(pairs, scores)):
        outs = []
        for half, s in enumerate(pair_scores):
            o = _softmax_pv(s, [vc], extra=sink_ref[2 * p + half])
            outs.append(o if kv_of(2 * p + half) == half else _swap_halves(o))
        yc_ref[:, sl] = jnp.where(_half_mask(outs[0].shape, 0), outs[0], outs[1]).astype(yc_ref.dtype)

    lam = lam_ref[0]
    for h in range(H_D):
        sl = slice(h * LANES, (h + 1) * LANES)
        yd_ref[:, sl] = _diff_head(dq[:, sl] * ATTN_SCALE, [dk[:, sl]], [_with_ones(dv[:, sl])], lam)


def _ctx_attention(qkv, lam, sink, batch, seq):
    tile = lambda t: pl.BlockSpec((seq, COL_TILE), lambda b: (b, t))
    smem = pl.BlockSpec(memory_space=pltpu.SMEM)
    out = lambda dt: jax.ShapeDtypeStruct((batch * seq, BRANCH_W), dt)
    out_spec = pl.BlockSpec((seq, BRANCH_W), lambda b: (b, 0))
    return pl.pallas_call(
        _ctx_attn_kernel,
        out_shape=(out(BF16), out(BF16), out(F32)),
        grid=(batch,),
        in_specs=[smem, smem, tile(T_BQ), tile(T_BK), tile(T_BV), tile(T_CQ), tile(T_MISC),
                  tile(T_DQ), tile(T_DK), tile(T_DV)],
        out_specs=(out_spec, out_spec, out_spec),
        compiler_params=_cparams("parallel"),
        name="ctx_attention",
    )(lam, sink, *([qkv] * 8))


NAT_QROWS = 4
NAT_KROWS = 12
NAT_RI = 2 * NA_ROWS - 1


def _nat_kernel(q_ref, k_ref, v_ref, kc_ref, vc_ref, cb_ref, o_ref):
    rows = k_ref.shape[0] // GRID_W
    r0 = pl.program_id(2) * NAT_QROWS
    kb = jnp.clip(r0 - NA_ROWS // 2, 0, rows - NAT_KROWS)
    koff = pl.multiple_of(kb * GRID_W, GRID_W)
    nq = NAT_QROWS * GRID_W
    nk = NAT_KROWS * GRID_W
    q = q_ref[...] * ATTN_SCALE
    k = k_ref[pl.ds(koff, nk), :]
    v = _with_ones(v_ref[pl.ds(koff, nk), :])
    kc = kc_ref[0].astype(BF16)
    vc = _with_ones(vc_ref[0].astype(BF16))
    left = _half_mask((GRID_W, LANES), 0)
    scores = []
    for half in range(2):
        bias_rows = []
        for a in range(NAT_QROWS):
            r = r0 + a
            rs = jnp.clip(r - NA_ROWS // 2, 0, rows - NA_ROWS)
            blocks = []
            for ip in range(NAT_KROWS // 2):
                sides = []
                for side in range(2):
                    kr = kb + 2 * ip + side
                    ok = (kr >= rs) & (kr < rs + NA_ROWS)
                    ri = jnp.clip(kr - r + NA_ROWS - 1, 0, NAT_RI - 1)
                    sides.append(jnp.where(ok, cb_ref[half, ri], NEG_INF))
                blocks.append(jnp.where(left, sides[0], sides[1]))
            bias_rows.append(jnp.concatenate(blocks, axis=1))
        bias = jnp.concatenate(bias_rows, axis=0)
        qm = _select_half(q, half)
        scores.append([_dot_nt(qm, k) + bias, _dot_nt(qm, kc)])
    outs = [_softmax_pv(s, [v, vc]) for s in scores]
    o_ref[...] = jnp.where(_half_mask((nq, LANES), 0), outs[0], outs[1]).astype(o_ref.dtype)


def _nat_col_table(rpb):
    qc = np.arange(GRID_W)[:, None]
    kc = np.arange(GRID_W)[None, :]
    cs = np.clip(qc - NA_COLS // 2, 0, GRID_W - NA_COLS)
    valid = (kc >= cs) & (kc < cs + NA_COLS)
    n_ci = 2 * NA_COLS - 1
    lo = GRID_W - 1 - (NA_COLS - 1)
    v = jnp.pad(rpb, ((0, 0), (0, 0), (lo, 2 * GRID_W - n_ci - lo)))
    flat = jnp.tile(v, (1, 1, GRID_W))[..., :GRID_W * (2 * GRID_W - 1)]
    toep = flat.reshape(rpb.shape[0], rpb.shape[1], GRID_W, 2 * GRID_W - 1)[..., GRID_W - 1:]
    tab = jnp.where(valid[None, None], toep, NEG_INF)
    return jnp.concatenate([tab, tab], axis=-1)


def _nat_latent(qkv, kc, vc, cb, l, batch, seq):
    nq = NAT_QROWS * GRID_W
    tpb = seq // nq
    past = kc.shape[2]
    return pl.pallas_call(
        _nat_kernel,
        out_shape=jax.ShapeDtypeStruct((batch * seq, BRANCH_W), BF16),
        grid=(batch, H_B // 2, tpb),
        in_specs=[
            pl.BlockSpec((nq, LANES), lambda b, p, i: (b * tpb + i, T_BQ * 4 + p)),
            pl.BlockSpec((seq, LANES), lambda b, p, i: (b, T_BK * 4 + p)),
            pl.BlockSpec((seq, LANES), lambda b, p, i: (b, T_BV * 4 + p)),
            pl.BlockSpec((1, None, past, LANES), lambda b, p, i: (b, l, 0, p)),
            pl.BlockSpec((1, None, past, LANES), lambda b, p, i: (b, l, 0, p)),
            pl.BlockSpec((2, NAT_RI, GRID_W, LANES), lambda b, p, i: (l * (H_B // 2) + p, 0, 0, 0)),
        ],
        out_specs=pl.BlockSpec((nq, LANES), lambda b, p, i: (b * tpb + i, p)),
        compiler_params=_cparams("parallel", "parallel", "parallel"),
        name="nat_latent",
    )(qkv, qkv, qkv, kc, vc, cb)


SWA_QBLK = 512
SWA_KBLK = SWA_QBLK + 2 * SWA_WINDOW


def _swa_kernel(sink_ref, q_ref, kv_ref, kc_ref, vc_ref, o_ref):
    seq = kv_ref.shape[0]
    p = pl.program_id(1)
    t0 = pl.program_id(2) * SWA_QBLK
    k0 = pl.multiple_of(jnp.clip(t0 - SWA_WINDOW, 0, seq - SWA_KBLK), SWA_WINDOW)
    q = q_ref[...] * ATTN_SCALE
    k = kv_ref[pl.ds(k0, SWA_KBLK), 0:LANES]
    v = _with_ones(kv_ref[pl.ds(k0, SWA_KBLK), LANES:2 * LANES])
    kc = kc_ref[0].astype(BF16)
    vc = _with_ones(vc_ref[0].astype(BF16))
    tq = t0 + lax.broadcasted_iota(jnp.int32, (SWA_QBLK, SWA_KBLK), 0)
    tk = k0 + lax.broadcasted_iota(jnp.int32, (SWA_QBLK, SWA_KBLK), 1)
    inside = jnp.abs(tq - tk) <= SWA_WINDOW
    kv_half = p // ((H_C // 2) // KV_C)
    scores = []
    for half in range(2):
        qm = _select_half(q, half)
        qm = jnp.where(kv_half == half, qm, _swap_halves(qm))
        scores.append([jnp.where(inside, _dot_nt(qm, k), NEG_INF), _dot_nt(qm, kc)])
    outs = []
    for half in range(2):
        o = _softmax_pv(scores[half], [v, vc], extra=sink_ref[2 * p + half])
        outs.append(jnp.where(kv_half == half, o, _swap_halves(o)))
    o_ref[...] = jnp.where(_half_mask(q.shape, 0), outs[0], outs[1]).astype(o_ref.dtype)


def _swa_latent(qkv, kc, vc, sink, l, batch, seq):
    tpb = seq // SWA_QBLK
    past = kc.shape[2]
    return pl.pallas_call(
        _swa_kernel,
        out_shape=jax.ShapeDtypeStruct((batch * seq, BRANCH_W), BF16),
        grid=(batch, H_C // 2, tpb),
        in_specs=[
            pl.BlockSpec(memory_space=pltpu.SMEM),
            pl.BlockSpec((SWA_QBLK, LANES), lambda b, p, i: (b * tpb + i, T_CQ * 4 + p)),
            pl.BlockSpec((seq, COL_TILE), lambda b, p, i: (b, T_MISC)),
            pl.BlockSpec((1, None, past, LANES), lambda b, p, i: (b, l, 0, 0)),
            pl.BlockSpec((1, None, past, LANES), lambda b, p, i: (b, l, 0, 0)),
        ],
        out_specs=pl.BlockSpec((SWA_QBLK, LANES), lambda b, p, i: (b * tpb + i, p)),
        compiler_params=_cparams("parallel", "parallel", "parallel"),
        name="swa_latent",
    )(sink, qkv, qkv, kc, vc)


DIFF_QBLK = 512


def _diff_kernel(lam_ref, q_ref, k_ref, v_ref, kc_ref, vc_ref, o_ref):
    q = q_ref[...] * ATTN_SCALE
    ks = [k_ref[...], kc_ref[0].astype(BF16)]
    vs = [_with_ones(v_ref[...]), _with_ones(vc_ref[0].astype(BF16))]
    o_ref[...] = _diff_head(q, ks, vs, lam_ref[0])


def _diff_latent(qkv, kc, vc, lam, l, batch, seq):
    tpb = seq // DIFF_QBLK
    past = kc.shape[2]
    return pl.pallas_call(
        _diff_kernel,
        out_shape=jax.ShapeDtypeStruct((batch * seq, BRANCH_W), F32),
        grid=(batch, H_D, tpb),
        in_specs=[
            pl.BlockSpec(memory_space=pltpu.SMEM),
            pl.BlockSpec((DIFF_QBLK, LANES), lambda b, h, i: (b * tpb + i, T_DQ * 4 + h)),
            pl.BlockSpec((seq, LANES), lambda b, h, i: (b, T_DK * 4 + h)),
            pl.BlockSpec((seq, LANES), lambda b, h, i: (b, T_DV * 4 + h)),
            pl.BlockSpec((1, None, past, LANES), lambda b, h, i: (b, l, 0, h)),
            pl.BlockSpec((1, None, past, LANES), lambda b, h, i: (b, l, 0, h)),
        ],
        out_specs=pl.BlockSpec((DIFF_QBLK, LANES), lambda b, h, i: (b * tpb + i, h)),
        compiler_params=_cparams("parallel", "parallel", "parallel"),
        name="diff_latent",
    )(lam, qkv, qkv, qkv, kc, vc)


def _rms_heads(y, g_ref, per_head_gain):
    outs = []
    for c in range(BRANCH_W // LANES):
        sl = slice(c * LANES, (c + 1) * LANES)
        g = g_ref[:, sl] if per_head_gain else g_ref[...]
        outs.append(_rms(y[:, sl], g))
    return jnp.concatenate(outs, axis=1)


def _first_lane(hit, lane_f):
    return jnp.min(jnp.where(hit, lane_f, float(LANES)), axis=1, keepdims=True)


def _top_group(logits):
    lane = lax.broadcasted_iota(jnp.int32, logits.shape, 1)
    gl = jnp.where(lane < N_GROUPS, logits, NEG_INF)
    return _first_lane(gl == jnp.max(gl, axis=1, keepdims=True), lane.astype(F32))


def _gate_weights(logits, group):
    lane = lax.broadcasted_iota(jnp.int32, logits.shape, 1)
    lane_f = lane.astype(F32)
    gl = jnp.where(lane < N_GROUPS, logits, NEG_INF)
    gmax = jnp.max(gl, axis=1, keepdims=True)
    g_logit = jnp.sum(jnp.where(lane_f == group, logits, 0.0), axis=1, keepdims=True)
    gw = jnp.exp(g_logit - gmax) / jnp.sum(jnp.exp(gl - gmax), axis=1, keepdims=True)
    e_lane = lane - N_GROUPS
    e_group = lax.shift_right_arithmetic(e_lane, EXP_PER_GROUP.bit_length() - 1).astype(F32)
    in_group = (e_lane >= 0) & (e_lane < N_EXPERTS) & (e_group == group)
    el = jnp.where(in_group, logits, NEG_INF)
    e1 = jnp.max(el, axis=1, keepdims=True)
    i1 = _first_lane(el == e1, lane_f)
    el2 = jnp.where(lane_f == i1, NEG_INF, el)
    e2 = jnp.max(el2, axis=1, keepdims=True)
    i2 = _first_lane(el2 == e2, lane_f)
    t = jnp.exp(e2 - e1)
    w1 = gw / (1.0 + t)
    return jnp.where(lane_f == i1, w1, jnp.where(lane_f == i2, w1 * t, 0.0))


def _merge_kernel(ya, yb, yc, yd, ao, gt0, gt1, gt2, gt3, x_ref, mod_ref, anorm, subln, nffn,
                  wbr, wout, wr, xo_ref, h2_ref, group_ref, *, yd_scale):
    a = _sigmoid(ao[...].astype(F32)) * _rms_heads(ya[...], anorm, True)
    d = _rms_heads(yd[...], subln, False) * yd_scale
    branches = (a, yb[...], yc[...], d)
    mix = None
    for n, (br, gt) in enumerate(zip(branches, (gt0, gt1, gt2, gt3))):
        term = _sigmoid(gt[...].astype(F32)) * _dot(br.astype(BF16), wbr[n])
        mix = term if mix is None else mix + term
    out = _dot(mix.astype(BF16), wout[...])
    xn = x_ref[...] + mod_ref[0, 2:3, :] * out
    xo_ref[...] = xn
    h2 = _rms(xn, nffn[...]) * (1.0 + mod_ref[0, 4:5, :]) + mod_ref[0, 3:4, :]
    h2_ref[...] = h2
    group_ref[...] = jnp.broadcast_to(_top_group(_dot(h2.astype(BF16), wr[...])), group_ref.shape)


def _merge(ys, qkv, x, mod3, a_norm, subln, norm_ffn, w_branch, w_out, w_route, lam_init, l):
    n = x.shape[0]
    tm = 512
    tiles_per_mod = n // mod3.shape[0] // tm
    y_spec = pl.BlockSpec((tm, BRANCH_W), lambda i: (i, 0))
    gt_spec = lambda k: pl.BlockSpec((tm, D_MODEL), lambda i: (i, T_GT * COL_TILE // D_MODEL + k))
    full = lambda shape: pl.BlockSpec(shape, lambda i: (0,) * len(shape))
    layer = lambda shape: pl.BlockSpec((None,) + shape, lambda i: (l,) + (0,) * len(shape))
    x_spec = pl.BlockSpec((tm, D_MODEL), lambda i: (i, 0))
    return pl.pallas_call(
        functools.partial(_merge_kernel, yd_scale=1.0 - lam_init),
        out_shape=(
            jax.ShapeDtypeStruct((n, D_MODEL), F32),
            jax.ShapeDtypeStruct((n, D_MODEL), F32),
            jax.ShapeDtypeStruct((n, LANES), F32),
        ),
        grid=(n // tm,),
        in_specs=[
            y_spec, y_spec, y_spec, y_spec,
            pl.BlockSpec((tm, COL_TILE), lambda i: (i, T_AO)),
            gt_spec(0), gt_spec(1), gt_spec(2), gt_spec(3),
            x_spec,
            pl.BlockSpec((1, 6, D_MODEL), lambda i: (i // tiles_per_mod, 0, 0)),
            full((1, BRANCH_W)), full((1, LANES)), full((1, D_MODEL)),
            layer((N_BRANCH, BRANCH_W, D_MODEL)), layer((D_MODEL, D_MODEL)), layer((D_MODEL, LANES)),
        ],
        out_specs=(x_spec, x_spec, pl.BlockSpec((tm, LANES), lambda i: (i, 0))),
        compiler_params=_cparams("parallel"),
        name="merge_route",
    )(*ys, qkv, qkv, qkv, qkv, qkv, x, mod3, a_norm.reshape(1, BRANCH_W), subln.reshape(1, LANES),
      norm_ffn.reshape(1, D_MODEL), w_branch, w_out, w_route)


MOE_TILE = 512
ROW_DMA_UNROLL = 16


def _row_copies(src_row, dst_row, sem, n):
    def copy(i):
        return pltpu.make_async_copy(src_row(i), dst_row(i), sem)

    def start(i, c):
        copy(i).start()
        return c

    def wait(i, c):
        copy(i).wait()
        return c

    lax.fori_loop(0, n, start, 0, unroll=ROW_DMA_UNROLL)
    lax.fori_loop(0, n, wait, 0, unroll=ROW_DMA_UNROLL)


def _moe_scatter_kernel(slot_ref, h_ref, init_ref, sorted_ref, sem):
    del init_ref
    base = pl.program_id(0) * MOE_TILE
    _row_copies(lambda i: h_ref.at[pl.ds(i, 1), :],
                lambda i: sorted_ref.at[pl.ds(slot_ref[base + i], 1), :], sem, MOE_TILE)


def _moe_group_kernel(tile_group_ref, n_used_ref, h_ref, wr_ref, wg_f32, wu_f32, wd_f32, o_ref,
                      wg_ref, wu_ref, wd_ref):
    t = pl.program_id(0)
    group = tile_group_ref[t]

    @pl.when((t == 0) | (group != tile_group_ref[jnp.maximum(t - 1, 0)]))
    def _():
        for e in range(EXP_PER_GROUP):
            wg_ref[e] = wg_f32[0, e].astype(BF16)
            wu_ref[e] = wu_f32[0, e].astype(BF16)
            wd_ref[e] = wd_f32[0, e].astype(BF16)

    @pl.when(t < n_used_ref[0])
    def _():
        h = h_ref[...].astype(BF16)
        gates = _gate_weights(_dot(h, wr_ref[...]), group.astype(F32))
        lane = lax.broadcasted_iota(jnp.int32, gates.shape, 1)
        acc = None
        for e in range(EXP_PER_GROUP):
            g = jnp.sum(jnp.where(lane == N_GROUPS + group * EXP_PER_GROUP + e, gates, 0.0), axis=1, keepdims=True)
            hg = _dot(h, wg_ref[e])
            hid = hg * _sigmoid(hg) * _dot(h, wu_ref[e]) * g
            term = _dot(hid.astype(BF16), wd_ref[e])
            acc = term if acc is None else acc + term
        o_ref[...] = acc

    @pl.when(t >= n_used_ref[0])
    def _():
        o_ref[...] = jnp.zeros_like(o_ref)


def _moe_gather_kernel(slot_ref, y_ref, x_ref, mod_ref, nf_ref, o_ref, buf, sem, *, final_norm):
    t = pl.program_id(0)
    nt = pl.num_programs(0)

    def copy(tile, b, i):
        return pltpu.make_async_copy(y_ref.at[pl.ds(slot_ref[tile * MOE_TILE + i], 1), :],
                                     buf.at[b, pl.ds(i, 1), :], sem.at[b])

    def start_tile(tile, b):
        def start(i, c):
            copy(tile, b, i).start()
            return c

        lax.fori_loop(0, MOE_TILE, start, 0, unroll=ROW_DMA_UNROLL)

    cur = t % 2

    @pl.when(t == 0)
    def _():
        start_tile(0, 0)

    @pl.when(t + 1 < nt)
    def _():
        start_tile(t + 1, 1 - cur)

    def wait(i, c):
        copy(t, cur, i).wait()
        return c

    lax.fori_loop(0, MOE_TILE, wait, 0, unroll=ROW_DMA_UNROLL)
    y = x_ref[...] + mod_ref[0, 5:6, :] * buf[cur]
    o_ref[...] = _rms(y, nf_ref[...]) if final_norm else y


def _moe(h2, group, x, mod3, w_route, wg, wu, wd, norm_final, final_norm, l):
    n = x.shape[0]
    tm = MOE_TILE
    n_tiles = n // tm + N_GROUPS
    onehot = (group[:, None] == jnp.arange(N_GROUPS, dtype=jnp.int32)[None]).astype(jnp.int32)
    csum = jnp.cumsum(onehot, axis=0)
    counts = csum[-1]
    rank = jnp.sum((csum - onehot) * onehot, axis=1)
    group_tiles = (counts + tm - 1) // tm
    tile_end = jnp.cumsum(group_tiles)
    group_start = (tile_end - group_tiles) * tm
    slot = (jnp.sum(onehot * group_start[None], axis=1) + rank).astype(jnp.int32)
    tile_group = jnp.minimum(jnp.sum(jnp.arange(n_tiles, dtype=jnp.int32)[:, None] >= tile_end[None], axis=1),
                             N_GROUPS - 1).astype(jnp.int32)
    n_used = tile_end[-1:].astype(jnp.int32)

    sorted_h = pl.pallas_call(
        _moe_scatter_kernel,
        out_shape=jax.ShapeDtypeStruct((n_tiles * tm, D_MODEL), F32),
        grid_spec=pltpu.PrefetchScalarGridSpec(
            num_scalar_prefetch=1, grid=(n // tm,),
            in_specs=[pl.BlockSpec((tm, D_MODEL), lambda i, slot: (i, 0)), pl.BlockSpec(memory_space=pl.ANY)],
            out_specs=pl.BlockSpec(memory_space=pl.ANY),
            scratch_shapes=[pltpu.SemaphoreType.DMA],
        ),
        input_output_aliases={2: 0},
        compiler_params=_cparams("arbitrary"),
        name="moe_scatter",
    )(slot, h2, jnp.zeros((n_tiles * tm, D_MODEL), F32))

    grp = lambda shape: pl.BlockSpec((None, 1) + shape, lambda t, tg, nu: (l, tg[t], 0, 0, 0))
    sorted_y = pl.pallas_call(
        _moe_group_kernel,
        out_shape=jax.ShapeDtypeStruct((n_tiles * tm, D_MODEL), F32),
        grid_spec=pltpu.PrefetchScalarGridSpec(
            num_scalar_prefetch=2, grid=(n_tiles,),
            in_specs=[
                pl.BlockSpec((tm, D_MODEL), lambda t, tg, nu: (t, 0)),
                pl.BlockSpec((None, D_MODEL, LANES), lambda t, tg, nu: (l, 0, 0)),
                grp((EXP_PER_GROUP, D_MODEL, D_EXPERT)), grp((EXP_PER_GROUP, D_MODEL, D_EXPERT)),
                grp((EXP_PER_GROUP, D_EXPERT, D_MODEL)),
            ],
            out_specs=pl.BlockSpec((tm, D_MODEL), lambda t, tg, nu: (t, 0)),
            scratch_shapes=[pltpu.VMEM((EXP_PER_GROUP, D_MODEL, D_EXPERT), BF16),
                            pltpu.VMEM((EXP_PER_GROUP, D_MODEL, D_EXPERT), BF16),
                            pltpu.VMEM((EXP_PER_GROUP, D_EXPERT, D_MODEL), BF16)],
        ),
        compiler_params=_cparams("arbitrary"),
        name="moe_experts",
    )(tile_group, n_used, sorted_h, w_route, wg, wu, wd)

    tiles_per_mod = n // mod3.shape[0] // tm
    row = pl.BlockSpec((tm, D_MODEL), lambda i, slot: (i, 0))
    return pl.pallas_call(
        functools.partial(_moe_gather_kernel, final_norm=final_norm),
        out_shape=jax.ShapeDtypeStruct((n, D_MODEL), F32),
        grid_spec=pltpu.PrefetchScalarGridSpec(
            num_scalar_prefetch=1, grid=(n // tm,),
            in_specs=[
                pl.BlockSpec(memory_space=pl.ANY), row,
                pl.BlockSpec((1, 6, D_MODEL), lambda i, slot: (i // tiles_per_mod, 0, 0)),
                pl.BlockSpec((1, D_MODEL), lambda i, slot: (0, 0)),
            ],
            out_specs=row,
            scratch_shapes=[pltpu.VMEM((2, tm, D_MODEL), F32), pltpu.SemaphoreType.DMA((2,))],
        ),
        compiler_params=_cparams("arbitrary"),
        name="moe_gather",
    )(slot, sorted_y, x, mod3, norm_final.reshape(1, D_MODEL))


def _reorder_w_in(w):
    pad = jnp.zeros((D_MODEL, COL_TILE - 2 * LANES - 16), BF16)
    cols = [w[:, 0:2048], w[:, 2064:4112], w[:, 4368:5904], w[:, 4112:4368], w[:, 2048:2064], pad, w[:, 5904:]]
    return jnp.concatenate([c.astype(BF16) for c in cols], axis=1)


def _rope_tables(seq):
    nf = DH_C // 4
    inv = ROPE_BASE ** (-jnp.arange(nf, dtype=F32) / nf)
    t = jnp.arange(seq)
    ang_r = (t // GRID_W).astype(F32)[:, None] * inv[None, :]
    ang_c = (t % GRID_W).astype(F32)[:, None] * inv[None, :]
    cos = jnp.concatenate([jnp.cos(ang_r)] * 2 + [jnp.cos(ang_c)] * 2, axis=1)
    sin = jnp.concatenate([-jnp.sin(ang_r), jnp.sin(ang_r), -jnp.sin(ang_c), jnp.sin(ang_c)], axis=1)
    return jnp.tile(cos, (1, 2)), jnp.tile(sin, (1, 2))


def _layer(l, x, mod3, seq, lat, p, prev_cache=None):
    batch = x.shape[0] // seq
    lam_init = 0.8 - 0.6 * math.exp(-0.3 * l)
    ctx = lat is None
    proj = _qkv_proj(x, p["norm_mix"][l], mod3, p["w_in"][l], None if ctx else p["rope"],
                     dict(layer=l, seq=seq, prev=prev_cache) if ctx else None, tm=1024 if ctx else 2048)
    qkv, gates_raw = proj[:2]
    gates = gates_raw[:, :16] + jnp.concatenate([p["i_bias"][l].reshape(-1), p["f_bias"][l].reshape(-1)])[None]
    lam = (jnp.exp(jnp.sum(p["lq1"][l] * p["lk1"][l])) - jnp.exp(jnp.sum(p["lq2"][l] * p["lk2"][l]))
           + lam_init).reshape(1).astype(F32)
    sink = p["sink"][l]
    new = None
    if ctx:
        C0 = jnp.zeros((batch, 1, 2, H_A, DH_A, DH_A), F32)
        n0 = jnp.zeros((batch, 2, H_A, DH_A), F32)
        m0 = jnp.zeros((batch, 2, H_A), F32)
        ya, Cs, ns, ms = _mlstm(qkv, gates, C0, 0, n0, m0, batch, seq)
        yb, yc, yd = _ctx_attention(qkv, lam, sink, batch, seq)
        new = (proj[2:], (Cs, ns, ms))
    else:
        nat_k, nat_v, swa_k, swa_v, diff_k, diff_v, C0, n0, m0 = lat
        flat = lambda a: a.reshape(a.shape[:3] + (-1,))
        ya, _, _, _ = _mlstm(qkv, gates, C0, l, n0[:, l], m0[:, l], batch, seq)
        yb = _nat_latent(qkv, flat(nat_k), flat(nat_v), p["nat_table"], l, batch, seq)
        yc = _swa_latent(qkv, flat(swa_k), flat(swa_v), sink, l, batch, seq)
        yd = _diff_latent(qkv, flat(diff_k), flat(diff_v), lam, l, batch, seq)
    xn, h2, group = _merge((ya, yb, yc, yd), qkv, x, mod3, p["a_norm"][l], p["subln"][l],
                           p["norm_ffn"][l], p["w_branch"], p["w_out"], p["w_route"], lam_init, l)
    y = _moe(h2, group[:, 0].astype(jnp.int32), xn, mod3, p["w_route"], p["wg"], p["wu"], p["wd"],
             p["norm_final"], final_norm=(l == DEPTH - 1), l=l)
    return y, new


def kernel(x_prompt, x_sample, cache_nat_k, cache_nat_v, cache_swa_k, cache_swa_v, cache_diff_k, cache_diff_v, state_mlstm_C, state_mlstm_n, state_mlstm_m, c, c_ctx, norm_mix, norm_ffn, norm_final, w_mod, b_mod, w_in, mlstm_i_bias, mlstm_f_bias, mlstm_norm, nat_rpb, swa_sink, diff_lq1, diff_lk1, diff_lq2, diff_lk2, diff_subln, w_branch, w_out, router_group, router_expert, w_exp_gate, w_exp_up, w_exp_down):
    b_ctx, s_ctx, _ = x_prompt.shape
    b_lat, s_lat, _ = x_sample.shape
    cond8 = jnp.concatenate([c_ctx[None], c, jnp.zeros((8 - 1 - b_lat, D_MODEL), F32)], axis=0)
    mod = _modulation(cond8, w_mod, b_mod).reshape(DEPTH, 8, 6, D_MODEL)
    route_pad = jnp.zeros((DEPTH, D_MODEL, LANES - N_GROUPS - N_EXPERTS), F32)
    p = dict(
        norm_mix=norm_mix, norm_ffn=norm_ffn, norm_final=norm_final,
        w_in=[_reorder_w_in(w_in[l]) for l in range(DEPTH)],
        i_bias=mlstm_i_bias, f_bias=mlstm_f_bias, a_norm=mlstm_norm, sink=swa_sink,
        nat_table=_nat_col_table(nat_rpb.reshape((DEPTH * H_B,) + nat_rpb.shape[2:])),
        lq1=diff_lq1, lk1=diff_lk1, lq2=diff_lq2, lk2=diff_lk2, subln=diff_subln,
        w_branch=w_branch.astype(BF16), w_out=w_out.astype(BF16),
        w_route=jnp.concatenate([router_group, router_expert, route_pad], axis=-1).astype(BF16),
        wg=w_exp_gate.reshape(DEPTH, N_GROUPS, EXP_PER_GROUP, D_MODEL, D_EXPERT),
        wu=w_exp_up.reshape(DEPTH, N_GROUPS, EXP_PER_GROUP, D_MODEL, D_EXPERT),
        wd=w_exp_down.reshape(DEPTH, N_GROUPS, EXP_PER_GROUP, D_EXPERT, D_MODEL),
        rope=_rope_tables(s_lat),
    )
    yp = x_prompt.reshape(b_ctx * s_ctx, D_MODEL)
    ys = x_sample.reshape(b_lat * s_lat, D_MODEL)
    states = []
    cache = None
    for l in range(DEPTH):
        yp, (cache, state) = _layer(l, yp, mod[l, 0:1], s_ctx, None, p, cache)
        states.append(state)
        lat = (cache_nat_k, cache_nat_v, cache_swa_k, cache_swa_v, cache_diff_k, cache_diff_v,
               state_mlstm_C, state_mlstm_n, state_mlstm_m)
        ys, _ = _layer(l, ys, mod[l, 1:1 + b_lat], s_lat, lat, p)
    def stack_layers(parts):
        flat = [a.reshape(a.shape[0], 1, -1) for a in parts]
        return jnp.concatenate(flat, axis=1).reshape((parts[0].shape[0], DEPTH) + parts[0].shape[1:])

    bk, bv, dk, dv, ck, cv = cache
    heads = lambda a, h: a.reshape(a.shape[:3] + (h, a.shape[3] // h))
    caches = (heads(bk, H_B), heads(bv, H_B), heads(ck, KV_C), heads(cv, KV_C), heads(dk, H_D), heads(dv, H_D))
    stacked = tuple(stack_layers([states[l][k] for l in range(DEPTH)]) for k in range(3))
    return (yp.reshape(b_ctx, s_ctx, D_MODEL), ys.reshape(b_lat, s_lat, D_MODEL)) + caches + stacked
```

```python
import functools
import math

import numpy as np
import jax
import jax.numpy as jnp
from jax import lax
from jax.experimental import pallas as pl
from jax.experimental.pallas import tpu as pltpu

F32 = jnp.float32
BF16 = jnp.bfloat16

D_MODEL = 1024
DEPTH = 2
GRID_W = 64
BRANCH_W = 512
N_BRANCH = 4
H_A, DH_A, MLSTM_CHUNK = 4, 128, 128
H_B, DH_B = 8, 64
NA_ROWS, NA_COLS = 8, 16
H_C, KV_C, DH_C = 8, 2, 64
SWA_WINDOW = 128
H_D, DH_D = 4, 64
N_GROUPS, EXP_PER_GROUP = 4, 4
N_EXPERTS = N_GROUPS * EXP_PER_GROUP
D_EXPERT = 256
ROPE_BASE = 10000.0
EPS = 1e-6

LANES = 128
HALF = 64
COL_TILE = 512
N_COL_TILES = 20
N_PROJ = COL_TILE * N_COL_TILES
T_AQ, T_AK, T_AV, T_AO, T_BQ, T_BK, T_BV, T_CQ, T_DQ, T_DK, T_DV, T_MISC, T_GT = range(13)
MISC_GATE_OFF = 256
VMEM_LIMIT = 56 * 1024 * 1024
NEG_INF = float("-inf")

_NT = (((1,), (1,)), ((), ()))
_TN = (((0,), (0,)), ((), ()))


def _cparams(*sem):
    return pltpu.CompilerParams(dimension_semantics=sem, vmem_limit_bytes=VMEM_LIMIT)


def _dot(a, b):
    return jnp.dot(a, b, preferred_element_type=F32)


def _dot_nt(a, b):
    return lax.dot_general(a, b, _NT, preferred_element_type=F32)


def _dot_tn(a, b):
    return lax.dot_general(a, b, _TN, preferred_element_type=F32)


def _rms(x, g):
    return x * lax.rsqrt(jnp.mean(x * x, axis=-1, keepdims=True) + EPS) * g


def _sigmoid(x):
    return 0.5 * jnp.tanh(0.5 * x) + 0.5


def _mod_kernel(c_ref, w_ref, b_ref, o_ref):
    c = c_ref[...]
    s = (c * _sigmoid(c)).astype(BF16)
    o_ref[0] = _dot(s, w_ref[0].astype(BF16)) + b_ref[0]


def _modulation(cond8, w_mod, b_mod):
    tn = 1536
    n_out = 6 * D_MODEL
    return pl.pallas_call(
        _mod_kernel,
        out_shape=jax.ShapeDtypeStruct((DEPTH, 8, n_out), F32),
        grid=(DEPTH, n_out // tn),
        in_specs=[
            pl.BlockSpec((8, D_MODEL), lambda l, j: (0, 0)),
            pl.BlockSpec((1, D_MODEL, tn), lambda l, j: (l, 0, j)),
            pl.BlockSpec((1, 1, tn), lambda l, j: (l, 0, j)),
        ],
        out_specs=pl.BlockSpec((1, 8, tn), lambda l, j: (l, 0, j)),
        compiler_params=_cparams("parallel", "parallel"),
        name="modulation",
    )(cond8, w_mod, b_mod.reshape(DEPTH, 1, n_out))


CACHE_TILES = (T_BK, T_BV, T_DK, T_DV)
QKV_ROW_CHUNK = 512


def _rope128(x, cos, sin):
    lane = lax.broadcasted_iota(jnp.int32, x.shape, 1)
    partner = jnp.where((lane % 32) < 16, pltpu.roll(x, LANES - 16, 1), pltpu.roll(x, 16, 1))
    return x * cos + partner * sin


def _qkv_kernel(x_ref, nw_ref, mod_ref, w_ref, *rest, rope, cache, n_aliased):
    rest = list(rest)
    if rope:
        cos_ref, sin_ref = rest[:2]
        rest = rest[2:]
    rest = rest[n_aliased:]
    o_ref, g_ref = rest[:2]
    cache_refs = rest[2:-1]
    h_scr = rest[-1]
    j = pl.program_id(1)

    @pl.when(j == 0)
    def _():
        h = _rms(x_ref[...], nw_ref[...]) * (1.0 + mod_ref[0, 1:2, :]) + mod_ref[0, 0:1, :]
        h_scr[...] = h.astype(BF16)

    def tile_kind(cond, store):
        @pl.when(cond)
        def _():
            for r in range(h_scr.shape[0] // QKV_ROW_CHUNK):
                rows = slice(r * QKV_ROW_CHUNK, (r + 1) * QKV_ROW_CHUNK)
                store(rows, _dot(h_scr[rows, :], w_ref[...]))

    def store_batches(ref, rows, val):
        seq = ref.shape[1]
        for bb in range(QKV_ROW_CHUNK // seq):
            ref[rows.start // seq + bb] = val[bb * seq:(bb + 1) * seq]

    def roped(rows, acc, c):
        sl = slice(c * LANES, (c + 1) * LANES)
        return _rope128(acc[:, sl], cos_ref[rows, :], sin_ref[rows, :]).astype(BF16)

    def store_plain(rows, acc):
        o_ref[rows, :] = acc.astype(BF16)

    def store_ak(rows, acc):
        o_ref[rows, :] = (acc * (DH_A ** -0.5)).astype(BF16)

    def store_misc(rows, acc):
        o_ref[rows, :] = acc.astype(BF16)
        if rope:
            o_ref[rows, :LANES] = roped(rows, acc, 0)
        g_ref[rows, :] = acc[:, MISC_GATE_OFF:MISC_GATE_OFF + LANES]
        if cache:
            store_batches(cache_refs[-2], rows, acc[:, :LANES])
            store_batches(cache_refs[-1], rows, acc[:, LANES:2 * LANES])

    def store_rope(rows, acc):
        for c in range(COL_TILE // LANES):
            o_ref[rows, c * LANES:(c + 1) * LANES] = roped(rows, acc, c)

    special = (j == T_AK) | (j == T_MISC)
    tile_kind(j == T_AK, store_ak)
    tile_kind(j == T_MISC, store_misc)
    if rope:
        is_rope = (j == T_CQ) | (j == T_DQ) | (j == T_DK)
        special = special | is_rope
        tile_kind(is_rope, store_rope)
    if cache:
        for t, ref in zip(CACHE_TILES, cache_refs[:-2]):
            def store_cached(rows, acc, ref=ref):
                o_ref[rows, :] = acc.astype(BF16)
                store_batches(ref, rows, acc)
            special = special | (j == t)
            tile_kind(j == t, store_cached)
    tile_kind(jnp.logical_not(special), store_plain)


def _qkv_proj(x, norm_w, mod3, w_in_r, l, rope_tabs, cache, tm):
    n = x.shape[0]
    tiles_per_mod = n // mod3.shape[0] // tm
    rope = rope_tabs is not None
    in_specs = [
        pl.BlockSpec((tm, D_MODEL), lambda i, j: (i, 0)),
        pl.BlockSpec((1, D_MODEL), lambda i, j: (0, 0)),
        pl.BlockSpec((1, 6, D_MODEL), lambda i, j: (i // tiles_per_mod, 0, 0)),
        pl.BlockSpec((None, D_MODEL, COL_TILE), lambda i, j: (l, 0, j)),
    ]
    args = [x, norm_w.reshape(1, D_MODEL), mod3, w_in_r]
    if rope:
        tiles_per_seq = rope_tabs[0].shape[0] // tm
        in_specs += [pl.BlockSpec((tm, LANES), lambda i, j: (i % tiles_per_seq, 0))] * 2
        args += list(rope_tabs)
    row = lambda w: pl.BlockSpec((tm, w), lambda i, j: (i, 0))
    out_shape = [jax.ShapeDtypeStruct((n, N_PROJ), BF16), jax.ShapeDtypeStruct((n, LANES), F32)]
    out_specs = [pl.BlockSpec((tm, COL_TILE), lambda i, j: (i, j)), row(LANES)]
    aliases = {}
    n_aliased = 0
    if cache:
        seq, layer = cache["seq"], cache["layer"]
        widths = [COL_TILE] * len(CACHE_TILES) + [LANES, LANES]
        out_shape += [jax.ShapeDtypeStruct((n // seq, DEPTH, seq, w), F32) for w in widths]
        out_specs += [pl.BlockSpec((tm // seq, None, seq, w), lambda i, j: (i, layer, 0, 0)) for w in widths]
        prev = cache["prev"] or [jnp.zeros(s.shape, s.dtype) for s in out_shape[2:]]
        n_aliased = len(widths)
        aliases = {len(args) + k: 2 + k for k in range(n_aliased)}
        in_specs += [pl.BlockSpec(memory_space=pl.ANY)] * n_aliased
        args += list(prev)
    return pl.pallas_call(
        functools.partial(_qkv_kernel, rope=rope, cache=bool(cache), n_aliased=n_aliased),
        input_output_aliases=aliases,
        out_shape=tuple(out_shape),
        grid=(n // tm, N_COL_TILES),
        in_specs=in_specs,
        out_specs=tuple(out_specs),
        scratch_shapes=[pltpu.VMEM((tm, D_MODEL), BF16)],
        compiler_params=_cparams("parallel", "arbitrary"),
        name="qkv_rope" if rope else "qkv",
    )(*args)


def _log_sigmoid(x):
    return jnp.minimum(x, 0.0) - jnp.log1p(jnp.exp(-jnp.abs(x)))


def _mlstm_chunk(c, h, d, state, q_ref, k_ref, v_ref, gr_ref, gc_ref, out_ref):
    L = MLSTM_CHUNK
    C, n, m = state
    row = lax.broadcasted_iota(jnp.int32, (L, L), 0)
    col = lax.broadcasted_iota(jnp.int32, (L, L), 1)
    tri = (col <= row) if d == 0 else (col >= row)
    tri_t = (row <= col) if d == 0 else (row >= col)
    off = pl.multiple_of(c * L, L)
    hs = slice(h * DH_A, (h + 1) * DH_A)
    qb = q_ref[pl.ds(off, L), hs]
    kb = k_ref[pl.ds(off, L), hs]
    vb = v_ref[pl.ds(off, L), hs]
    kx, vx = kb.astype(F32), vb.astype(F32)
    g_rows = gr_ref[h, c]
    g_cols = gc_ref[h, c]
    ig_r = g_rows[2 * d:2 * d + 1, :]
    lf_r = _log_sigmoid(g_rows[2 * d + 1:2 * d + 2, :])
    ig_c = g_cols[:, 2 * d:2 * d + 1]
    lf_c = _log_sigmoid(g_cols[:, 2 * d + 1:2 * d + 2])
    b_c = jnp.sum(jnp.where(tri, lf_r, 0.0), axis=1, keepdims=True)
    b_r = jnp.sum(jnp.where(tri_t, lf_c, 0.0), axis=0, keepdims=True)
    qk = _dot_nt(qb, kb)
    c_ext = jnp.concatenate([C.astype(BF16), jnp.broadcast_to(n.astype(BF16), (DH_A, DH_A))], axis=0)
    qc = _dot_nt(qb, c_ext)
    yield
    logd = jnp.where(tri, b_c - b_r + ig_r, NEG_INF)
    inter = jnp.broadcast_to(b_c + m, (L, L))
    m_t = jnp.maximum(inter, jnp.broadcast_to(jnp.max(logd, axis=1, keepdims=True), (L, L)))
    yield
    a = jnp.exp(inter - m_t)
    s = qk * jnp.exp(logd - m_t)
    sv = _dot(s.astype(BF16), _with_ones(vb))
    num = a * qc[:, :DH_A] + sv[:, :DH_A]
    den = a * qc[:, DH_A:] + sv[:, DH_A:]
    yield
    out_ref[pl.ds(off, L), hs] = num / jnp.maximum(jnp.abs(den), jnp.exp(-m_t))
    b_tot = jnp.sum(lf_r, axis=1, keepdims=True)
    g_c = b_tot - b_c + ig_c
    m_new = jnp.maximum(b_tot + m, jnp.max(g_c, axis=0, keepdims=True))
    yield
    decay = jnp.exp(b_tot + m - m_new)
    wk = jnp.exp(g_c - m_new)
    C_new = decay * C + _dot_tn((wk * vx).astype(BF16), kb)
    n_new = decay * n + jnp.sum(wk * kx, axis=0, keepdims=True)
    return C_new, n_new, m_new


def _run_staged(gens):
    results = [None] * len(gens)
    live = list(range(len(gens)))
    while live:
        for i in list(live):
            try:
                next(gens[i])
            except StopIteration as stop:
                results[i] = stop.value
                live.remove(i)
    return results


def _mlstm_kernel(q_ref, k_ref, v_ref, gr_ref, gc_ref, c0_ref, n0_ref, m0_ref,
                  h_ref, cs_ref, ns_ref, ms_ref, hb_scr, *, nc):
    chains = [(h, d) for h in range(H_A) for d in range(2)]

    def body(ci, states):
        gens = [_mlstm_chunk(ci if d == 0 else nc - 1 - ci, h, d, state, q_ref, k_ref, v_ref, gr_ref, gc_ref,
                             h_ref if d == 0 else hb_scr) for (h, d), state in zip(chains, states)]
        return tuple(_run_staged(gens))

    init = tuple((c0_ref[0, d, h], n0_ref[0, h, d:d + 1, :], m0_ref[0, h, d:d + 1, 0:1]) for h, d in chains)
    final = lax.fori_loop(0, nc, body, init)
    for (h, d), (C, n, m) in zip(chains, final):
        cs_ref[0, d, h] = C
        ns_ref[0, h, d:d + 1, :] = n
        ms_ref[0, h, d:d + 1, :] = jnp.broadcast_to(m, (1, LANES))
    h_ref[...] += hb_scr[...]


def _mlstm(qkv, gates, C0, l, n0, m0, batch, seq):
    nc = seq // MLSTM_CHUNK
    L = MLSTM_CHUNK
    ig = gates[:, :8].reshape(batch, nc, L, 2, H_A)
    fg = gates[:, 8:].reshape(batch, nc, L, 2, H_A)
    g4 = jnp.stack([ig[..., 0, :], fg[..., 0, :], ig[..., 1, :], fg[..., 1, :]], axis=-1)
    g_cols = jnp.transpose(g4, (3, 0, 1, 2, 4)).reshape(H_A, batch * nc, L, 4)
    g_rows = jnp.transpose(g4, (3, 0, 1, 4, 2)).reshape(H_A, batch * nc, 4, L)
    n0t = jnp.transpose(n0, (0, 2, 1, 3))
    m0t = jnp.broadcast_to(jnp.transpose(m0, (0, 2, 1))[..., None], (batch, H_A, 2, LANES))
    qkv_spec = lambda tile: pl.BlockSpec((seq, COL_TILE), lambda b: (b, tile))
    c_spec = pl.BlockSpec((1, 2, H_A, DH_A, DH_A), lambda b: (b, 0, 0, 0, 0))
    c0_spec = pl.BlockSpec((1, None, 2, H_A, DH_A, DH_A), lambda b: (b, l, 0, 0, 0, 0))
    state_spec = pl.BlockSpec((1, H_A, 2, LANES), lambda b: (b, 0, 0, 0))
    h, Cs, ns, ms = pl.pallas_call(
        functools.partial(_mlstm_kernel, nc=nc),
        out_shape=(
            jax.ShapeDtypeStruct((batch * seq, H_A * DH_A), F32),
            jax.ShapeDtypeStruct((batch, 2, H_A, DH_A, DH_A), F32),
            jax.ShapeDtypeStruct((batch, H_A, 2, DH_A), F32),
            jax.ShapeDtypeStruct((batch, H_A, 2, LANES), F32),
        ),
        grid=(batch,),
        in_specs=[
            qkv_spec(T_AQ), qkv_spec(T_AK), qkv_spec(T_AV),
            pl.BlockSpec((H_A, nc, 4, L), lambda b: (0, b, 0, 0)),
            pl.BlockSpec((H_A, nc, L, 4), lambda b: (0, b, 0, 0)),
            c0_spec, state_spec, state_spec,
        ],
        out_specs=(pl.BlockSpec((seq, H_A * DH_A), lambda b: (b, 0)), c_spec, state_spec, state_spec),
        scratch_shapes=[pltpu.VMEM((seq, H_A * DH_A), F32)],
        compiler_params=_cparams("parallel"),
        name="mlstm",
    )(qkv, qkv, qkv, g_rows, g_cols, C0, n0t, m0t)
    return h, Cs, jnp.transpose(ns, (0, 2, 1, 3)), jnp.transpose(ms[..., 0], (0, 2, 1))


ATTN_SCALE = DH_B ** -0.5


def _half_mask(shape, half):
    lane = lax.broadcasted_iota(jnp.int32, shape, 1)
    return (lane >= HALF) if half else (lane < HALF)


def _select_half(q, half):
    return jnp.where(_half_mask(q.shape, half), q, jnp.zeros_like(q))


def _swap_halves(x):
    return pltpu.roll(x.astype(F32), HALF, 1).astype(x.dtype)


def _with_ones(v):
    return jnp.concatenate([v, jnp.ones_like(v)], axis=1)


def _softmax_pv(scores, vs, extra=None):
    m = functools.reduce(jnp.maximum, [jnp.max(s, axis=1, keepdims=True) for s in scores])
    if extra is not None:
        m = jnp.maximum(m, extra)
    acc = functools.reduce(lambda x, y: x + y,
                           [_dot(jnp.exp((s - m).astype(BF16)), v) for s, v in zip(scores, vs)])
    denom = acc[:, LANES:]
    if extra is not None:
        denom = denom + jnp.exp(extra - m)
    return acc[:, :LANES] / denom


def _diff_head(q, ks, vs, lam):
    scores = [[_dot_nt(_select_half(q, comp), k) for k in ks] for comp in range(2)]
    o1, o2 = [_softmax_pv(s, vs) for s in scores]
    return o1 - lam * o2


def _ctx_attn_kernel(lam_ref, sink_ref, bq, bk, bv, cq, misc, dq, dk, dv, yb_ref, yc_ref, yd_ref):
    pairs = [slice(p * LANES, (p + 1) * LANES) for p in range(H_B // 2)]
    scores = [[[_dot_nt(_select_half(bq[:, sl] * ATTN_SCALE, half), bk[:, sl])] for half in range(2)] for sl in pairs]
    for sl, pair_scores in zip(pairs, scores):
        v = _with_ones(bv[:, sl])
        outs = [_softmax_pv(s, [v]) for s in pair_scores]
        yb_ref[:, sl] = jnp.where(_half_mask(outs[0].shape, 0), outs[0], outs[1]).astype(yb_ref.dtype)

    kc = misc[:, 0:LANES]
    vc = _with_ones(misc[:, LANES:2 * LANES])
    kv_of = lambda head: head // (H_C // KV_C)
    scores = []
    for p, sl in enumerate(pairs):
        pair_scores = []
        for half in range(2):
            qm = _select_half(cq[:, sl] * ATTN_SCALE, half)
            if kv_of(2 * p + half) != half:
                qm = _swap_halves(qm)
            pair_scores.append([_dot_nt(qm, kc)])
        scores.append(pair_scores)
    for p, (sl, pair_scores) in enumerate(zip(pairs, scores)):
        outs = []
        for half, s in enumerate(pair_scores):
            o = _softmax_pv(s, [vc], extra=sink_ref[2 * p + half])
            outs.append(o if kv_of(2 * p + half) == half else _swap_halves(o))
        yc_ref[:, sl] = jnp.where(_half_mask(outs[0].shape, 0), outs[0], outs[1]).astype(yc_ref.dtype)

    lam = lam_ref[0]
    for h in range(H_D):
        sl = slice(h * LANES, (h + 1) * LANES)
        yd_ref[:, sl] = _diff_head(dq[:, sl] * ATTN_SCALE, [dk[:, sl]], [_with_ones(dv[:, sl])], lam)


def _ctx_attention(qkv, lam, sink, batch, seq):
    tile = lambda t: pl.BlockSpec((seq, COL_TILE), lambda b: (b, t))
    smem = pl.BlockSpec(memory_space=pltpu.SMEM)
    out = lambda dt: jax.ShapeDtypeStruct((batch * seq, BRANCH_W), dt)
    out_spec = pl.BlockSpec((seq, BRANCH_W), lambda b: (b, 0))
    return pl.pallas_call(
        _ctx_attn_kernel,
        out_shape=(out(BF16), out(BF16), out(F32)),
        grid=(batch,),
        in_specs=[smem, smem, tile(T_BQ), tile(T_BK), tile(T_BV), tile(T_CQ), tile(T_MISC),
                  tile(T_DQ), tile(T_DK), tile(T_DV)],
        out_specs=(out_spec, out_spec, out_spec),
        compiler_params=_cparams("parallel"),
        name="ctx_attention",
    )(lam, sink, *([qkv] * 8))


NAT_QROWS = 4
NAT_KROWS = 12
NAT_RI = 2 * NA_ROWS - 1


def _nat_kernel(q_ref, k_ref, v_ref, kc_ref, vc_ref, cb_ref, o_ref):
    rows = k_ref.shape[0] // GRID_W
    r0 = pl.program_id(2) * NAT_QROWS
    kb = jnp.clip(r0 - NA_ROWS // 2, 0, rows - NAT_KROWS)
    koff = pl.multiple_of(kb * GRID_W, GRID_W)
    nq = NAT_QROWS * GRID_W
    nk = NAT_KROWS * GRID_W
    q = q_ref[...] * ATTN_SCALE
    k = k_ref[pl.ds(koff, nk), :]
    v = _with_ones(v_ref[pl.ds(koff, nk), :])
    kc = kc_ref[0].astype(BF16)
    vc = _with_ones(vc_ref[0].astype(BF16))
    left = _half_mask((GRID_W, LANES), 0)
    scores = []
    for half in range(2):
        bias_rows = []
        for a in range(NAT_QROWS):
            r = r0 + a
            rs = jnp.clip(r - NA_ROWS // 2, 0, rows - NA_ROWS)
            blocks = []
            for ip in range(NAT_KROWS // 2):
                sides = []
                for side in range(2):
                    kr = kb + 2 * ip + side
                    ok = (kr >= rs) & (kr < rs + NA_ROWS)
                    ri = jnp.clip(kr - r + NA_ROWS - 1, 0, NAT_RI - 1)
                    sides.append(jnp.where(ok, cb_ref[half, ri], NEG_INF))
                blocks.append(jnp.where(left, sides[0], sides[1]))
            bias_rows.append(jnp.concatenate(blocks, axis=1))
        bias = jnp.concatenate(bias_rows, axis=0)
        qm = _select_half(q, half)
        scores.append([_dot_nt(qm, k) + bias, _dot_nt(qm, kc)])
    outs = [_softmax_pv(s, [v, vc]) for s in scores]
    o_ref[...] = jnp.where(_half_mask((nq, LANES), 0), outs[0], outs[1]).astype(o_ref.dtype)


def _nat_col_table(rpb):
    qc = np.arange(GRID_W)[:, None]
    kc = np.arange(GRID_W)[None, :]
    cs = np.clip(qc - NA_COLS // 2, 0, GRID_W - NA_COLS)
    valid = (kc >= cs) & (kc < cs + NA_COLS)
    n_ci = 2 * NA_COLS - 1
    lo = GRID_W - 1 - (NA_COLS - 1)
    v = jnp.pad(rpb, ((0, 0), (0, 0), (lo, 2 * GRID_W - n_ci - lo)))
    flat = jnp.tile(v, (1, 1, GRID_W))[..., :GRID_W * (2 * GRID_W - 1)]
    toep = flat.reshape(rpb.shape[0], rpb.shape[1], GRID_W, 2 * GRID_W - 1)[..., GRID_W - 1:]
    tab = jnp.where(valid[None, None], toep, NEG_INF)
    return jnp.concatenate([tab, tab], axis=-1)


def _nat_latent(qkv, kc, vc, cb, l, batch, seq):
    nq = NAT_QROWS * GRID_W
    tpb = seq // nq
    past = kc.shape[2]
    return pl.pallas_call(
        _nat_kernel,
        out_shape=jax.ShapeDtypeStruct((batch * seq, BRANCH_W), BF16),
        grid=(batch, H_B // 2, tpb),
        in_specs=[
            pl.BlockSpec((nq, LANES), lambda b, p, i: (b * tpb + i, T_BQ * 4 + p)),
            pl.BlockSpec((seq, LANES), lambda b, p, i: (b, T_BK * 4 + p)),
            pl.BlockSpec((seq, LANES), lambda b, p, i: (b, T_BV * 4 + p)),
            pl.BlockSpec((1, None, past, LANES), lambda b, p, i: (b, l, 0, p)),
            pl.BlockSpec((1, None, past, LANES), lambda b, p, i: (b, l, 0, p)),
            pl.BlockSpec((2, NAT_RI, GRID_W, LANES), lambda b, p, i: (l * (H_B // 2) + p, 0, 0, 0)),
        ],
        out_specs=pl.BlockSpec((nq, LANES), lambda b, p, i: (b * tpb + i, p)),
        compiler_params=_cparams("parallel", "parallel", "parallel"),
        name="nat_latent",
    )(qkv, qkv, qkv, kc, vc, cb)


SWA_QBLK = 512
SWA_KBLK = SWA_QBLK + 2 * SWA_WINDOW


def _swa_kernel(sink_ref, q_ref, kv_ref, kc_ref, vc_ref, o_ref):
    seq = kv_ref.shape[0]
    p = pl.program_id(1)
    t0 = pl.program_id(2) * SWA_QBLK
    k0 = pl.multiple_of(jnp.clip(t0 - SWA_WINDOW, 0, seq - SWA_KBLK), SWA_WINDOW)
    q = q_ref[...] * ATTN_SCALE
    k = kv_ref[pl.ds(k0, SWA_KBLK), 0:LANES]
    v = _with_ones(kv_ref[pl.ds(k0, SWA_KBLK), LANES:2 * LANES])
    kc = kc_ref[0].astype(BF16)
    vc = _with_ones(vc_ref[0].astype(BF16))
    tq = t0 + lax.broadcasted_iota(jnp.int32, (SWA_QBLK, SWA_KBLK), 0)
    tk = k0 + lax.broadcasted_iota(jnp.int32, (SWA_QBLK, SWA_KBLK), 1)
    inside = jnp.abs(tq - tk) <= SWA_WINDOW
    kv_half = p // ((H_C // 2) // KV_C)
    scores = []
    for half in range(2):
        qm = _select_half(q, half)
        qm = jnp.where(kv_half == half, qm, _swap_halves(qm))
        scores.append([jnp.where(inside, _dot_nt(qm, k), NEG_INF), _dot_nt(qm, kc)])
    outs = []
    for half in range(2):
        o = _softmax_pv(scores[half], [v, vc], extra=sink_ref[2 * p + half])
        outs.append(jnp.where(kv_half == half, o, _swap_halves(o)))
    o_ref[...] = jnp.where(_half_mask(q.shape, 0), outs[0], outs[1]).astype(o_ref.dtype)


def _swa_latent(qkv, kc, vc, sink, l, batch, seq):
    tpb = seq // SWA_QBLK
    past = kc.shape[2]
    return pl.pallas_call(
        _swa_kernel,
        out_shape=jax.ShapeDtypeStruct((batch * seq, BRANCH_W), BF16),
        grid=(batch, H_C // 2, tpb),
        in_specs=[
            pl.BlockSpec(memory_space=pltpu.SMEM),
            pl.BlockSpec((SWA_QBLK, LANES), lambda b, p, i: (b * tpb + i, T_CQ * 4 + p)),
            pl.BlockSpec((seq, COL_TILE), lambda b, p, i: (b, T_MISC)),
            pl.BlockSpec((1, None, past, LANES), lambda b, p, i: (b, l, 0, 0)),
            pl.BlockSpec((1, None, past, LANES), lambda b, p, i: (b, l, 0, 0)),
        ],
        out_specs=pl.BlockSpec((SWA_QBLK, LANES), lambda b, p, i: (b * tpb + i, p)),
        compiler_params=_cparams("parallel", "parallel", "parallel"),
        name="swa_latent",
    )(sink, qkv, qkv, kc, vc)


DIFF_QBLK = 512


def _diff_kernel(lam_ref, q_ref, k_ref, v_ref, kc_ref, vc_ref, o_ref):
    q = q_ref[...] * ATTN_SCALE
    ks = [k_ref[...], kc_ref[0].astype(BF16)]
    vs = [_with_ones(v_ref[...]), _with_ones(vc_ref[0].astype(BF16))]
    o_ref[...] = _diff_head(q, ks, vs, lam_ref[0])


def _diff_latent(qkv, kc, vc, lam, l, batch, seq):
    tpb = seq // DIFF_QBLK
    past = kc.shape[2]
    return pl.pallas_call(
        _diff_kernel,
        out_shape=jax.ShapeDtypeStruct((batch * seq, BRANCH_W), F32),
        grid=(batch, H_D, tpb),
        in_specs=[
            pl.BlockSpec(memory_space=pltpu.SMEM),
            pl.BlockSpec((DIFF_QBLK, LANES), lambda b, h, i: (b * tpb + i, T_DQ * 4 + h)),
            pl.BlockSpec((seq, LANES), lambda b, h, i: (b, T_DK * 4 + h)),
            pl.BlockSpec((seq, LANES), lambda b, h, i: (b, T_DV * 4 + h)),
            pl.BlockSpec((1, None, past, LANES), lambda b, h, i: (b, l, 0, h)),
            pl.BlockSpec((1, None, past, LANES), lambda b, h, i: (b, l, 0, h)),
        ],
        out_specs=pl.BlockSpec((DIFF_QBLK, LANES), lambda b, h, i: (b * tpb + i, h)),
        compiler_params=_cparams("parallel", "parallel", "parallel"),
        name="diff_latent",
    )(lam, qkv, qkv, qkv, kc, vc)


def _rms_heads(y, g_ref, per_head_gain):
    outs = []
    for c in range(BRANCH_W // LANES):
        sl = slice(c * LANES, (c + 1) * LANES)
        g = g_ref[:, sl] if per_head_gain else g_ref[...]
        outs.append(_rms(y[:, sl], g))
    return jnp.concatenate(outs, axis=1)


def _first_lane(hit, lane_f):
    return jnp.min(jnp.where(hit, lane_f, float(LANES)), axis=1, keepdims=True)


def _top_group(logits):
    lane = lax.broadcasted_iota(jnp.int32, logits.shape, 1)
    gl = jnp.where(lane < N_GROUPS, logits, NEG_INF)
    return _first_lane(gl == jnp.max(gl, axis=1, keepdims=True), lane.astype(F32))


def _gate_weights(logits, group):
    lane = lax.broadcasted_iota(jnp.int32, logits.shape, 1)
    lane_f = lane.astype(F32)
    gl = jnp.where(lane < N_GROUPS, logits, NEG_INF)
    gmax = jnp.max(gl, axis=1, keepdims=True)
    g_logit = jnp.sum(jnp.where(lane_f == group, logits, 0.0), axis=1, keepdims=True)
    gw = jnp.exp(g_logit - gmax) / jnp.sum(jnp.exp(gl - gmax), axis=1, keepdims=True)
    e_lane = lane - N_GROUPS
    e_group = lax.shift_right_arithmetic(e_lane, EXP_PER_GROUP.bit_length() - 1).astype(F32)
    in_group = (e_lane >= 0) & (e_lane < N_EXPERTS) & (e_group == group)
    el = jnp.where(in_group, logits, NEG_INF)
    e1 = jnp.max(el, axis=1, keepdims=True)
    i1 = _first_lane(el == e1, lane_f)
    el2 = jnp.where(lane_f == i1, NEG_INF, el)
    e2 = jnp.max(el2, axis=1, keepdims=True)
    i2 = _first_lane(el2 == e2, lane_f)
    t = jnp.exp(e2 - e1)
    w1 = gw / (1.0 + t)
    return jnp.where(lane_f == i1, w1, jnp.where(lane_f == i2, w1 * t, 0.0))


def _merge_kernel(ya, yb, yc, yd, ao, gt0, gt1, gt2, gt3, x_ref, mod_ref, anorm, subln, nffn,
                  wbr, wout, wr, xo_ref, h2_ref, group_ref, *, yd_scale):
    a = _sigmoid(ao[...].astype(F32)) * _rms_heads(ya[...], anorm, True)
    d = _rms_heads(yd[...], subln, False) * yd_scale
    branches = (a, yb[...], yc[...], d)
    mix = None
    for n, (br, gt) in enumerate(zip(branches, (gt0, gt1, gt2, gt3))):
        term = _sigmoid(gt[...].astype(F32)) * _dot(br.astype(BF16), wbr[n])
        mix = term if mix is None else mix + term
    out = _dot(mix.astype(BF16), wout[...])
    xn = x_ref[...] + mod_ref[0, 2:3, :] * out
    xo_ref[...] = xn
    h2 = _rms(xn, nffn[...]) * (1.0 + mod_ref[0, 4:5, :]) + mod_ref[0, 3:4, :]
    h2_ref[...] = h2
    group_ref[...] = jnp.broadcast_to(_top_group(_dot(h2.astype(BF16), wr[...])), group_ref.shape)


def _merge(ys, qkv, x, mod3, a_norm, subln, norm_ffn, w_branch, w_out, w_route, lam_init, l):
    n = x.shape[0]
    tm = 512
    tiles_per_mod = n // mod3.shape[0] // tm
    y_spec = pl.BlockSpec((tm, BRANCH_W), lambda i: (i, 0))
    gt_spec = lambda k: pl.BlockSpec((tm, D_MODEL), lambda i: (i, T_GT * COL_TILE // D_MODEL + k))
    full = lambda shape: pl.BlockSpec(shape, lambda i: (0,) * len(shape))
    layer = lambda shape: pl.BlockSpec((None,) + shape, lambda i: (l,) + (0,) * len(shape))
    x_spec = pl.BlockSpec((tm, D_MODEL), lambda i: (i, 0))
    return pl.pallas_call(
        functools.partial(_merge_kernel, yd_scale=1.0 - lam_init),
        out_shape=(
            jax.ShapeDtypeStruct((n, D_MODEL), F32),
            jax.ShapeDtypeStruct((n, D_MODEL), F32),
            jax.ShapeDtypeStruct((n, LANES), F32),
        ),
        grid=(n // tm,),
        in_specs=[
            y_spec, y_spec, y_spec, y_spec,
            pl.BlockSpec((tm, COL_TILE), lambda i: (i, T_AO)),
            gt_spec(0), gt_spec(1), gt_spec(2), gt_spec(3),
            x_spec,
            pl.BlockSpec((1, 6, D_MODEL), lambda i: (i // tiles_per_mod, 0, 0)),
            full((1, BRANCH_W)), full((1, LANES)), full((1, D_MODEL)),
            layer((N_BRANCH, BRANCH_W, D_MODEL)), layer((D_MODEL, D_MODEL)), layer((D_MODEL, LANES)),
        ],
        out_specs=(x_spec, x_spec, pl.BlockSpec((tm, LANES), lambda i: (i, 0))),
        compiler_params=_cparams("parallel"),
        name="merge_route",
    )(*ys, qkv, qkv, qkv, qkv, qkv, x, mod3, a_norm.reshape(1, BRANCH_W), subln.reshape(1, LANES),
      norm_ffn.reshape(1, D_MODEL), w_branch, w_out, w_route)


MOE_TILE = 512
ROW_DMA_UNROLL = 16


def _row_copies(src_row, dst_row, sem, n):
    def copy(i):
        return pltpu.make_async_copy(src_row(i), dst_row(i), sem)

    def start(i, c):
        copy(i).start()
        return c

    def wait(i, c):
        copy(i).wait()
        return c

    lax.fori_loop(0, n, start, 0, unroll=ROW_DMA_UNROLL)
    lax.fori_loop(0, n, wait, 0, unroll=ROW_DMA_UNROLL)


def _moe_scatter_kernel(slot_ref, h_ref, init_ref, sorted_ref, sem):
    del init_ref
    base = pl.program_id(0) * MOE_TILE
    _row_copies(lambda i: h_ref.at[pl.ds(i, 1), :],
                lambda i: sorted_ref.at[pl.ds(slot_ref[base + i], 1), :], sem, MOE_TILE)


def _moe_group_kernel(tile_group_ref, n_used_ref, h_ref, wr_ref, wg_f32, wu_f32, wd_f32, o_ref,
                      wg_ref, wu_ref, wd_ref):
    t = pl.program_id(0)
    group = tile_group_ref[t]

    @pl.when((t == 0) | (group != tile_group_ref[jnp.maximum(t - 1, 0)]))
    def _():
        for e in range(EXP_PER_GROUP):
            wg_ref[e] = wg_f32[0, e].astype(BF16)
            wu_ref[e] = wu_f32[0, e].astype(BF16)
            wd_ref[e] = wd_f32[0, e].astype(BF16)

    @pl.when(t < n_used_ref[0])
    def _():
        h = h_ref[...].astype(BF16)
        gates = _gate_weights(_dot(h, wr_ref[...]), group.astype(F32))
        lane = lax.broadcasted_iota(jnp.int32, gates.shape, 1)
        acc = None
        for e in range(EXP_PER_GROUP):
            g = jnp.sum(jnp.where(lane == N_GROUPS + group * EXP_PER_GROUP + e, gates, 0.0), axis=1, keepdims=True)
            hg = _dot(h, wg_ref[e])
            hid = hg * _sigmoid(hg) * _dot(h, wu_ref[e]) * g
            term = _dot(hid.astype(BF16), wd_ref[e])
            acc = term if acc is None else acc + term
        o_ref[...] = acc

    @pl.when(t >= n_used_ref[0])
    def _():
        o_ref[...] = jnp.zeros_like(o_ref)


def _moe_gather_kernel(slot_ref, y_ref, x_ref, mod_ref, nf_ref, o_ref, buf, sem, *, final_norm):
    t = pl.program_id(0)
    nt = pl.num_programs(0)

    def copy(tile, b, i):
        return pltpu.make_async_copy(y_ref.at[pl.ds(slot_ref[tile * MOE_TILE + i], 1), :],
                                     buf.at[b, pl.ds(i, 1), :], sem.at[b])

    def start_tile(tile, b):
        def start(i, c):
            copy(tile, b, i).start()
            return c

        lax.fori_loop(0, MOE_TILE, start, 0, unroll=ROW_DMA_UNROLL)

    cur = t % 2

    @pl.when(t == 0)
    def _():
        start_tile(0, 0)

    @pl.when(t + 1 < nt)
    def _():
        start_tile(t + 1, 1 - cur)

    def wait(i, c):
        copy(t, cur, i).wait()
        return c

    lax.fori_loop(0, MOE_TILE, wait, 0, unroll=ROW_DMA_UNROLL)
    y = x_ref[...] + mod_ref[0, 5:6, :] * buf[cur]
    o_ref[...] = _rms(y, nf_ref[...]) if final_norm else y


def _moe(h2, group, x, mod3, w_route, wg, wu, wd, norm_final, final_norm, l):
    n = x.shape[0]
    tm = MOE_TILE
    n_tiles = n // tm + N_GROUPS
    onehot = (group[:, None] == jnp.arange(N_GROUPS, dtype=jnp.int32)[None]).astype(jnp.int32)
    csum = jnp.cumsum(onehot, axis=0)
    counts = csum[-1]
    rank = jnp.sum((csum - onehot) * onehot, axis=1)
    group_tiles = (counts + tm - 1) // tm
    tile_end = jnp.cumsum(group_tiles)
    group_start = (tile_end - group_tiles) * tm
    slot = (jnp.sum(onehot * group_start[None], axis=1) + rank).astype(jnp.int32)
    tile_group = jnp.minimum(jnp.sum(jnp.arange(n_tiles, dtype=jnp.int32)[:, None] >= tile_end[None], axis=1),
                             N_GROUPS - 1).astype(jnp.int32)
    n_used = tile_end[-1:].astype(jnp.int32)

    sorted_h = pl.pallas_call(
        _moe_scatter_kernel,
        out_shape=jax.ShapeDtypeStruct((n_tiles * tm, D_MODEL), F32),
        grid_spec=pltpu.PrefetchScalarGridSpec(
            num_scalar_prefetch=1, grid=(n // tm,),
            in_specs=[pl.BlockSpec((tm, D_MODEL), lambda i, slot: (i, 0)), pl.BlockSpec(memory_space=pl.ANY)],
            out_specs=pl.BlockSpec(memory_space=pl.ANY),
            scratch_shapes=[pltpu.SemaphoreType.DMA],
        ),
        input_output_aliases={2: 0},
        compiler_params=_cparams("arbitrary"),
        name="moe_scatter",
    )(slot, h2, jnp.zeros((n_tiles * tm, D_MODEL), F32))

    grp = lambda shape: pl.BlockSpec((None, 1) + shape, lambda t, tg, nu: (l, tg[t], 0, 0, 0))
    sorted_y = pl.pallas_call(
        _moe_group_kernel,
        out_shape=jax.ShapeDtypeStruct((n_tiles * tm, D_MODEL), F32),
        grid_spec=pltpu.PrefetchScalarGridSpec(
            num_scalar_prefetch=2, grid=(n_tiles,),
            in_specs=[
                pl.BlockSpec((tm, D_MODEL), lambda t, tg, nu: (t, 0)),
                pl.BlockSpec((None, D_MODEL, LANES), lambda t, tg, nu: (l, 0, 0)),
                grp((EXP_PER_GROUP, D_MODEL, D_EXPERT)), grp((EXP_PER_GROUP, D_MODEL, D_EXPERT)),
                grp((EXP_PER_GROUP, D_EXPERT, D_MODEL)),
            ],
            out_specs=pl.BlockSpec((tm, D_MODEL), lambda t, tg, nu: (t, 0)),
            scratch_shapes=[pltpu.VMEM((EXP_PER_GROUP, D_MODEL, D_EXPERT), BF16),
                            pltpu.VMEM((EXP_PER_GROUP, D_MODEL, D_EXPERT), BF16),
                            pltpu.VMEM((EXP_PER_GROUP, D_EXPERT, D_MODEL), BF16)],
        ),
        compiler_params=_cparams("arbitrary"),
        name="moe_experts",
    )(tile_group, n_used, sorted_h, w_route, wg, wu, wd)

    tiles_per_mod = n // mod3.shape[0] // tm
    row = pl.BlockSpec((tm, D_MODEL), lambda i, slot: (i, 0))
    return pl.pallas_call(
        functools.partial(_moe_gather_kernel, final_norm=final_norm),
        out_shape=jax.ShapeDtypeStruct((n, D_MODEL), F32),
        grid_spec=pltpu.PrefetchScalarGridSpec(
            num_scalar_prefetch=1, grid=(n // tm,),
            in_specs=[
                pl.BlockSpec(memory_space=pl.ANY), row,
                pl.BlockSpec((1, 6, D_MODEL), lambda i, slot: (i // tiles_per_mod, 0, 0)),
                pl.BlockSpec((1, D_MODEL), lambda i, slot: (0, 0)),
            ],
            out_specs=row,
            scratch_shapes=[pltpu.VMEM((2, tm, D_MODEL), F32), pltpu.SemaphoreType.DMA((2,))],
        ),
        compiler_params=_cparams("arbitrary"),
        name="moe_gather",
    )(slot, sorted_y, x, mod3, norm_final.reshape(1, D_MODEL))


W_IN_GATES = 2048
W_IN_SHIFT = 16
REORDER_BLOCKS = 5
W_IN_TILE_SRC = {T_AQ: 0, T_AK: 512, T_AV: 1024, T_AO: 1536, T_BQ: 2064, T_BK: 2576, T_BV: 3088, T_CQ: 3600,
                 T_DQ: 4368, T_DK: 4880, T_DV: 5392, **{T_GT + k: 5904 + COL_TILE * k for k in range(8)}}
W_IN_CKV = 4112


def _reorder_kernel(blk_ref, *refs):
    del blk_ref
    o_ref = refs[-1]
    j = pl.program_id(1)
    win = jnp.concatenate([r[...] for r in refs[:-1]], axis=1)
    aligned = j < T_BQ
    misc = j == T_MISC

    @pl.when(aligned)
    def _():
        o_ref[...] = win[:, :COL_TILE].astype(BF16)

    @pl.when(misc)
    def _():
        gate_lo = (REORDER_BLOCKS - 2) * LANES
        pad = jnp.zeros((win.shape[0], COL_TILE - 2 * LANES - W_IN_SHIFT), F32)
        tile = jnp.concatenate([win[:, W_IN_SHIFT:W_IN_SHIFT + 2 * LANES],
                                win[:, gate_lo:gate_lo + W_IN_SHIFT], pad], axis=1)
        o_ref[...] = tile.astype(BF16)

    @pl.when(jnp.logical_not(aligned | misc))
    def _():
        o_ref[...] = win[:, W_IN_SHIFT:W_IN_SHIFT + COL_TILE].astype(BF16)


def _reorder_w_in(w_in):
    table = np.zeros((N_COL_TILES, REORDER_BLOCKS), np.int32)
    for t, src in W_IN_TILE_SRC.items():
        table[t] = src // LANES + np.arange(REORDER_BLOCKS)
    ckv = W_IN_CKV // LANES
    table[T_MISC] = [ckv, ckv + 1, ckv + 2, W_IN_GATES // LANES, W_IN_GATES // LANES]
    block = lambda c: pl.BlockSpec((None, D_MODEL, LANES),
                                   lambda l, j, blk: (l, 0, blk[j * REORDER_BLOCKS + c]))
    return pl.pallas_call(
        _reorder_kernel,
        out_shape=jax.ShapeDtypeStruct((DEPTH, D_MODEL, N_PROJ), BF16),
        grid_spec=pltpu.PrefetchScalarGridSpec(
            num_scalar_prefetch=1, grid=(DEPTH, N_COL_TILES),
            in_specs=[block(c) for c in range(REORDER_BLOCKS)],
            out_specs=pl.BlockSpec((None, D_MODEL, COL_TILE), lambda l, j, blk: (l, 0, j)),
        ),
        compiler_params=_cparams("parallel", "parallel"),
        name="reorder_w_in",
    )(jnp.asarray(table.reshape(-1)), *([w_in] * REORDER_BLOCKS))


def _rope_tables(seq):
    nf = DH_C // 4
    inv = ROPE_BASE ** (-jnp.arange(nf, dtype=F32) / nf)
    t = jnp.arange(seq)
    ang_r = (t // GRID_W).astype(F32)[:, None] * inv[None, :]
    ang_c = (t % GRID_W).astype(F32)[:, None] * inv[None, :]
    cos = jnp.concatenate([jnp.cos(ang_r)] * 2 + [jnp.cos(ang_c)] * 2, axis=1)
    sin = jnp.concatenate([-jnp.sin(ang_r), jnp.sin(ang_r), -jnp.sin(ang_c), jnp.sin(ang_c)], axis=1)
    return jnp.tile(cos, (1, 2)), jnp.tile(sin, (1, 2))


def _layer(l, x, mod3, seq, lat, p, prev_cache=None):
    batch = x.shape[0] // seq
    lam_init = 0.8 - 0.6 * math.exp(-0.3 * l)
    ctx = lat is None
    proj = _qkv_proj(x, p["norm_mix"][l], mod3, p["w_in"], l, None if ctx else p["rope"],
                     dict(layer=l, seq=seq, prev=prev_cache) if ctx else None, tm=1024 if ctx else 2048)
    qkv, gates_raw = proj[:2]
    gates = gates_raw[:, :16] + jnp.concatenate([p["i_bias"][l].reshape(-1), p["f_bias"][l].reshape(-1)])[None]
    lam = (jnp.exp(jnp.sum(p["lq1"][l] * p["lk1"][l])) - jnp.exp(jnp.sum(p["lq2"][l] * p["lk2"][l]))
           + lam_init).reshape(1).astype(F32)
    sink = p["sink"][l]
    new = None
    if ctx:
        C0 = jnp.zeros((batch, 1, 2, H_A, DH_A, DH_A), F32)
        n0 = jnp.zeros((batch, 2, H_A, DH_A), F32)
        m0 = jnp.zeros((batch, 2, H_A), F32)
        ya, Cs, ns, ms = _mlstm(qkv, gates, C0, 0, n0, m0, batch, seq)
        yb, yc, yd = _ctx_attention(qkv, lam, sink, batch, seq)
        new = (proj[2:], (Cs, ns, ms))
    else:
        nat_k, nat_v, swa_k, swa_v, diff_k, diff_v, C0, n0, m0 = lat
        flat = lambda a: a.reshape(a.shape[:3] + (-1,))
        ya, _, _, _ = _mlstm(qkv, gates, C0, l, n0[:, l], m0[:, l], batch, seq)
        yb = _nat_latent(qkv, flat(nat_k), flat(nat_v), p["nat_table"], l, batch, seq)
        yc = _swa_latent(qkv, flat(swa_k), flat(swa_v), sink, l, batch, seq)
        yd = _diff_latent(qkv, flat(diff_k), flat(diff_v), lam, l, batch, seq)
    xn, h2, group = _merge((ya, yb, yc, yd), qkv, x, mod3, p["a_norm"][l], p["subln"][l],
                           p["norm_ffn"][l], p["w_branch"], p["w_out"], p["w_route"], lam_init, l)
    y = _moe(h2, group[:, 0].astype(jnp.int32), xn, mod3, p["w_route"], p["wg"], p["wu"], p["wd"],
             p["norm_final"], final_norm=(l == DEPTH - 1), l=l)
    return y, new


def kernel(x_prompt, x_sample, cache_nat_k, cache_nat_v, cache_swa_k, cache_swa_v, cache_diff_k, cache_diff_v, state_mlstm_C, state_mlstm_n, state_mlstm_m, c, c_ctx, norm_mix, norm_ffn, norm_final, w_mod, b_mod, w_in, mlstm_i_bias, mlstm_f_bias, mlstm_norm, nat_rpb, swa_sink, diff_lq1, diff_lk1, diff_lq2, diff_lk2, diff_subln, w_branch, w_out, router_group, router_expert, w_exp_gate, w_exp_up, w_exp_down):
    b_ctx, s_ctx, _ = x_prompt.shape
    b_lat, s_lat, _ = x_sample.shape
    cond8 = jnp.concatenate([c_ctx[None], c, jnp.zeros((8 - 1 - b_lat, D_MODEL), F32)], axis=0)
    mod = _modulation(cond8, w_mod, b_mod).reshape(DEPTH, 8, 6, D_MODEL)
    route_pad = jnp.zeros((DEPTH, D_MODEL, LANES - N_GROUPS - N_EXPERTS), F32)
    p = dict(
        norm_mix=norm_mix, norm_ffn=norm_ffn, norm_final=norm_final,
        w_in=_reorder_w_in(w_in),
        i_bias=mlstm_i_bias, f_bias=mlstm_f_bias, a_norm=mlstm_norm, sink=swa_sink,
        nat_table=_nat_col_table(nat_rpb.reshape((DEPTH * H_B,) + nat_rpb.shape[2:])),
        lq1=diff_lq1, lk1=diff_lk1, lq2=diff_lq2, lk2=diff_lk2, subln=diff_subln,
        w_branch=w_branch.astype(BF16), w_out=w_out.astype(BF16),
        w_route=jnp.concatenate([router_group, router_expert, route_pad], axis=-1).astype(BF16),
        wg=w_exp_gate.reshape(DEPTH, N_GROUPS, EXP_PER_GROUP, D_MODEL, D_EXPERT),
        wu=w_exp_up.reshape(DEPTH, N_GROUPS, EXP_PER_GROUP, D_MODEL, D_EXPERT),
        wd=w_exp_down.reshape(DEPTH, N_GROUPS, EXP_PER_GROUP, D_EXPERT, D_MODEL),
        rope=_rope_tables(s_lat),
    )
    yp = x_prompt.reshape(b_ctx * s_ctx, D_MODEL)
    ys = x_sample.reshape(b_lat * s_lat, D_MODEL)
    states = []
    cache = None
    for l in range(DEPTH):
        yp, (cache, state) = _layer(l, yp, mod[l, 0:1], s_ctx, None, p, cache)
        states.append(state)
        lat = (cache_nat_k, cache_nat_v, cache_swa_k, cache_swa_v, cache_diff_k, cache_diff_v,
               state_mlstm_C, state_mlstm_n, state_mlstm_m)
        ys, _ = _layer(l, ys, mod[l, 1:1 + b_lat], s_lat, lat, p)
    def stack_layers(parts):
        flat = [a.reshape(a.shape[0], 1, -1) for a in parts]
        return jnp.concatenate(flat, axis=1).reshape((parts[0].shape[0], DEPTH) + parts[0].shape[1:])

    bk, bv, dk, dv, ck, cv = cache
    heads = lambda a, h: a.reshape(a.shape[:3] + (h, a.shape[3] // h))
    caches = (heads(bk, H_B), heads(bv, H_B), heads(ck, KV_C), heads(cv, KV_C), heads(dk, H_D), heads(dv, H_D))
    stacked = tuple(stack_layers([states[l][k] for l in range(DEPTH)]) for k in range(3))
    return (yp.reshape(b_ctx, s_ctx, D_MODEL), ys.reshape(b_lat, s_lat, D_MODEL)) + caches + stacked
```

```python
import functools
import math

import numpy as np
import jax
import jax.numpy as jnp
from jax import lax
from jax.experimental import pallas as pl
from jax.experimental.pallas import tpu as pltpu

F32 = jnp.float32
BF16 = jnp.bfloat16

D_MODEL = 1024
DEPTH = 2
GRID_W = 64
BRANCH_W = 512
N_BRANCH = 4
H_A, DH_A, MLSTM_CHUNK = 4, 128, 128
H_B, DH_B = 8, 64
NA_ROWS, NA_COLS = 8, 16
H_C, KV_C, DH_C = 8, 2, 64
SWA_WINDOW = 128
H_D, DH_D = 4, 64
N_GROUPS, EXP_PER_GROUP = 4, 4
N_EXPERTS = N_GROUPS * EXP_PER_GROUP
D_EXPERT = 256
ROPE_BASE = 10000.0
EPS = 1e-6

LANES = 128
HALF = 64
COL_TILE = 512
N_COL_TILES = 20
N_PROJ = COL_TILE * N_COL_TILES
T_AQ, T_AK, T_AV, T_AO, T_BQ, T_BK, T_BV, T_CQ, T_DQ, T_DK, T_DV, T_MISC, T_GT = range(13)
MISC_GATE_OFF = 256
VMEM_LIMIT = 56 * 1024 * 1024
NEG_INF = float("-inf")

_NT = (((1,), (1,)), ((), ()))
_TN = (((0,), (0,)), ((), ()))


def _cparams(*sem):
    return pltpu.CompilerParams(dimension_semantics=sem, vmem_limit_bytes=VMEM_LIMIT)


def _dot(a, b):
    return jnp.dot(a, b, preferred_element_type=F32)


def _dot_nt(a, b):
    return lax.dot_general(a, b, _NT, preferred_element_type=F32)


def _dot_tn(a, b):
    return lax.dot_general(a, b, _TN, preferred_element_type=F32)


def _rms(x, g):
    return x * lax.rsqrt(jnp.mean(x * x, axis=-1, keepdims=True) + EPS) * g


def _sigmoid(x):
    return 0.5 * jnp.tanh(0.5 * x) + 0.5


def _mod_kernel(c_ref, w_ref, b_ref, o_ref):
    c = c_ref[...]
    s = (c * _sigmoid(c)).astype(BF16)
    o_ref[0] = _dot(s, w_ref[0].astype(BF16)) + b_ref[0]


def _modulation(cond8, w_mod, b_mod):
    tn = 1536
    n_out = 6 * D_MODEL
    return pl.pallas_call(
        _mod_kernel,
        out_shape=jax.ShapeDtypeStruct((DEPTH, 8, n_out), F32),
        grid=(DEPTH, n_out // tn),
        in_specs=[
            pl.BlockSpec((8, D_MODEL), lambda l, j: (0, 0)),
            pl.BlockSpec((1, D_MODEL, tn), lambda l, j: (l, 0, j)),
            pl.BlockSpec((1, 1, tn), lambda l, j: (l, 0, j)),
        ],
        out_specs=pl.BlockSpec((1, 8, tn), lambda l, j: (l, 0, j)),
        compiler_params=_cparams("parallel", "parallel"),
        name="modulation",
    )(cond8, w_mod, b_mod.reshape(DEPTH, 1, n_out))


CACHE_TILES = (T_BK, T_BV, T_DK, T_DV)
QKV_ROW_CHUNK = 512


def _rope128(x, cos, sin):
    lane = lax.broadcasted_iota(jnp.int32, x.shape, 1)
    partner = jnp.where((lane % 32) < 16, pltpu.roll(x, LANES - 16, 1), pltpu.roll(x, 16, 1))
    return x * cos + partner * sin


def _qkv_kernel(x_ref, nw_ref, mod_ref, w_ref, *rest, rope, cache, n_aliased):
    rest = list(rest)
    if rope:
        cos_ref, sin_ref = rest[:2]
        rest = rest[2:]
    rest = rest[n_aliased:]
    o_ref, g_ref = rest[:2]
    cache_refs = rest[2:-1]
    h_scr = rest[-1]
    j = pl.program_id(1)

    @pl.when(j == 0)
    def _():
        h = _rms(x_ref[...], nw_ref[...]) * (1.0 + mod_ref[0, 1:2, :]) + mod_ref[0, 0:1, :]
        h_scr[...] = h.astype(BF16)

    def tile_kind(cond, store):
        @pl.when(cond)
        def _():
            for r in range(h_scr.shape[0] // QKV_ROW_CHUNK):
                rows = slice(r * QKV_ROW_CHUNK, (r + 1) * QKV_ROW_CHUNK)
                store(rows, _dot(h_scr[rows, :], w_ref[...]))

    def store_batches(ref, rows, val):
        seq = ref.shape[1]
        for bb in range(QKV_ROW_CHUNK // seq):
            ref[rows.start // seq + bb] = val[bb * seq:(bb + 1) * seq]

    def roped(rows, acc, c):
        sl = slice(c * LANES, (c + 1) * LANES)
        return _rope128(acc[:, sl], cos_ref[rows, :], sin_ref[rows, :]).astype(BF16)

    def store_plain(rows, acc):
        o_ref[rows, :] = acc.astype(BF16)

    def store_ak(rows, acc):
        o_ref[rows, :] = (acc * (DH_A ** -0.5)).astype(BF16)

    def store_misc(rows, acc):
        o_ref[rows, :] = acc.astype(BF16)
        if rope:
            o_ref[rows, :LANES] = roped(rows, acc, 0)
        g_ref[rows, :] = acc[:, MISC_GATE_OFF:MISC_GATE_OFF + LANES]
        if cache:
            store_batches(cache_refs[-2], rows, acc[:, :LANES])
            store_batches(cache_refs[-1], rows, acc[:, LANES:2 * LANES])

    def store_rope(rows, acc):
        for c in range(COL_TILE // LANES):
            o_ref[rows, c * LANES:(c + 1) * LANES] = roped(rows, acc, c)

    special = (j == T_AK) | (j == T_MISC)
    tile_kind(j == T_AK, store_ak)
    tile_kind(j == T_MISC, store_misc)
    if rope:
        is_rope = (j == T_CQ) | (j == T_DQ) | (j == T_DK)
        special = special | is_rope
        tile_kind(is_rope, store_rope)
    if cache:
        for t, ref in zip(CACHE_TILES, cache_refs[:-2]):
            def store_cached(rows, acc, ref=ref):
                o_ref[rows, :] = acc.astype(BF16)
                store_batches(ref, rows, acc)
            special = special | (j == t)
            tile_kind(j == t, store_cached)
    tile_kind(jnp.logical_not(special), store_plain)


def _qkv_proj(x, norm_w, mod3, w_in_r, l, rope_tabs, cache, tm):
    n = x.shape[0]
    tiles_per_mod = n // mod3.shape[0] // tm
    rope = rope_tabs is not None
    in_specs = [
        pl.BlockSpec((tm, D_MODEL), lambda i, j: (i, 0)),
        pl.BlockSpec((1, D_MODEL), lambda i, j: (0, 0)),
        pl.BlockSpec((1, 6, D_MODEL), lambda i, j: (i // tiles_per_mod, 0, 0)),
        pl.BlockSpec((None, D_MODEL, COL_TILE), lambda i, j: (l, 0, j)),
    ]
    args = [x, norm_w.reshape(1, D_MODEL), mod3, w_in_r]
    if rope:
        tiles_per_seq = rope_tabs[0].shape[0] // tm
        in_specs += [pl.BlockSpec((tm, LANES), lambda i, j: (i % tiles_per_seq, 0))] * 2
        args += list(rope_tabs)
    row = lambda w: pl.BlockSpec((tm, w), lambda i, j: (i, 0))
    out_shape = [jax.ShapeDtypeStruct((n, N_PROJ), BF16), jax.ShapeDtypeStruct((n, LANES), F32)]
    out_specs = [pl.BlockSpec((tm, COL_TILE), lambda i, j: (i, j)), row(LANES)]
    aliases = {}
    n_aliased = 0
    if cache:
        seq, layer = cache["seq"], cache["layer"]
        widths = [COL_TILE] * len(CACHE_TILES) + [LANES, LANES]
        out_shape += [jax.ShapeDtypeStruct((n // seq, DEPTH, seq, w), F32) for w in widths]
        out_specs += [pl.BlockSpec((tm // seq, None, seq, w), lambda i, j: (i, layer, 0, 0)) for w in widths]
        prev = cache["prev"] or [jnp.zeros(s.shape, s.dtype) for s in out_shape[2:]]
        n_aliased = len(widths)
        aliases = {len(args) + k: 2 + k for k in range(n_aliased)}
        in_specs += [pl.BlockSpec(memory_space=pl.ANY)] * n_aliased
        args += list(prev)
    return pl.pallas_call(
        functools.partial(_qkv_kernel, rope=rope, cache=bool(cache), n_aliased=n_aliased),
        input_output_aliases=aliases,
        out_shape=tuple(out_shape),
        grid=(n // tm, N_COL_TILES),
        in_specs=in_specs,
        out_specs=tuple(out_specs),
        scratch_shapes=[pltpu.VMEM((tm, D_MODEL), BF16)],
        compiler_params=_cparams("parallel", "arbitrary"),
        name="qkv_rope" if rope else "qkv",
    )(*args)


def _log_sigmoid(x):
    return jnp.minimum(x, 0.0) - jnp.log1p(jnp.exp(-jnp.abs(x)))


def _mlstm_chunk(c, h, d, state, q_ref, k_ref, v_ref, gr_ref, gc_ref, out_ref):
    L = MLSTM_CHUNK
    C, n, m = state
    row = lax.broadcasted_iota(jnp.int32, (L, L), 0)
    col = lax.broadcasted_iota(jnp.int32, (L, L), 1)
    tri = (col <= row) if d == 0 else (col >= row)
    tri_t = (row <= col) if d == 0 else (row >= col)
    off = pl.multiple_of(c * L, L)
    hs = slice(h * DH_A, (h + 1) * DH_A)
    qb = q_ref[pl.ds(off, L), hs]
    kb = k_ref[pl.ds(off, L), hs]
    vb = v_ref[pl.ds(off, L), hs]
    kx, vx = kb.astype(F32), vb.astype(F32)
    g_rows = gr_ref[h, c]
    g_cols = gc_ref[h, c]
    ig_r = g_rows[2 * d:2 * d + 1, :]
    lf_r = _log_sigmoid(g_rows[2 * d + 1:2 * d + 2, :])
    ig_c = g_cols[:, 2 * d:2 * d + 1]
    lf_c = _log_sigmoid(g_cols[:, 2 * d + 1:2 * d + 2])
    b_c = jnp.sum(jnp.where(tri, lf_r, 0.0), axis=1, keepdims=True)
    b_r = jnp.sum(jnp.where(tri_t, lf_c, 0.0), axis=0, keepdims=True)
    qk = _dot_nt(qb, kb)
    c_ext = jnp.concatenate([C.astype(BF16), jnp.broadcast_to(n.astype(BF16), (DH_A, DH_A))], axis=0)
    qc = _dot_nt(qb, c_ext)
    yield
    logd = jnp.where(tri, b_c - b_r + ig_r, NEG_INF)
    inter = jnp.broadcast_to(b_c + m, (L, L))
    m_t = jnp.maximum(inter, jnp.broadcast_to(jnp.max(logd, axis=1, keepdims=True), (L, L)))
    yield
    a = jnp.exp(inter - m_t)
    s = qk * jnp.exp(logd - m_t)
    sv = _dot(s.astype(BF16), _with_ones(vb))
    num = a * qc[:, :DH_A] + sv[:, :DH_A]
    den = a * qc[:, DH_A:] + sv[:, DH_A:]
    yield
    out_ref[pl.ds(off, L), hs] = num / jnp.maximum(jnp.abs(den), jnp.exp(-m_t))
    b_tot = jnp.sum(lf_r, axis=1, keepdims=True)
    g_c = b_tot - b_c + ig_c
    m_new = jnp.maximum(b_tot + m, jnp.max(g_c, axis=0, keepdims=True))
    yield
    decay = jnp.exp(b_tot + m - m_new)
    wk = jnp.exp(g_c - m_new)
    C_new = decay * C + _dot_tn((wk * vx).astype(BF16), kb)
    n_new = decay * n + jnp.sum(wk * kx, axis=0, keepdims=True)
    return C_new, n_new, m_new


def _run_staged(gens):
    results = [None] * len(gens)
    live = list(range(len(gens)))
    while live:
        for i in list(live):
            try:
                next(gens[i])
            except StopIteration as stop:
                results[i] = stop.value
                live.remove(i)
    return results


def _mlstm_kernel(q_ref, k_ref, v_ref, gr_ref, gc_ref, c0_ref, n0_ref, m0_ref,
                  h_ref, cs_ref, ns_ref, ms_ref, hb_scr, *, nc):
    chains = [(h, d) for h in range(H_A) for d in range(2)]

    def body(ci, states):
        gens = [_mlstm_chunk(ci if d == 0 else nc - 1 - ci, h, d, state, q_ref, k_ref, v_ref, gr_ref, gc_ref,
                             h_ref if d == 0 else hb_scr) for (h, d), state in zip(chains, states)]
        return tuple(_run_staged(gens))

    init = tuple((c0_ref[0, d, h], n0_ref[0, h, d:d + 1, :], m0_ref[0, h, d:d + 1, 0:1]) for h, d in chains)
    final = lax.fori_loop(0, nc, body, init)
    for (h, d), (C, n, m) in zip(chains, final):
        cs_ref[0, d, h] = C
        ns_ref[0, h, d:d + 1, :] = n
        ms_ref[0, h, d:d + 1, :] = jnp.broadcast_to(m, (1, LANES))
    h_ref[...] += hb_scr[...]


def _mlstm(qkv, gates, C0, l, n0, m0, batch, seq):
    nc = seq // MLSTM_CHUNK
    L = MLSTM_CHUNK
    ig = gates[:, :8].reshape(batch, nc, L, 2, H_A)
    fg = gates[:, 8:].reshape(batch, nc, L, 2, H_A)
    g4 = jnp.stack([ig[..., 0, :], fg[..., 0, :], ig[..., 1, :], fg[..., 1, :]], axis=-1)
    g_cols = jnp.transpose(g4, (3, 0, 1, 2, 4)).reshape(H_A, batch * nc, L, 4)
    g_rows = jnp.transpose(g4, (3, 0, 1, 4, 2)).reshape(H_A, batch * nc, 4, L)
    n0t = jnp.transpose(n0, (0, 2, 1, 3))
    m0t = jnp.broadcast_to(jnp.transpose(m0, (0, 2, 1))[..., None], (batch, H_A, 2, LANES))
    qkv_spec = lambda tile: pl.BlockSpec((seq, COL_TILE), lambda b: (b, tile))
    c_spec = pl.BlockSpec((1, 2, H_A, DH_A, DH_A), lambda b: (b, 0, 0, 0, 0))
    c0_spec = pl.BlockSpec((1, None, 2, H_A, DH_A, DH_A), lambda b: (b, l, 0, 0, 0, 0))
    state_spec = pl.BlockSpec((1, H_A, 2, LANES), lambda b: (b, 0, 0, 0))
    h, Cs, ns, ms = pl.pallas_call(
        functools.partial(_mlstm_kernel, nc=nc),
        out_shape=(
            jax.ShapeDtypeStruct((batch * seq, H_A * DH_A), F32),
            jax.ShapeDtypeStruct((batch, 2, H_A, DH_A, DH_A), F32),
            jax.ShapeDtypeStruct((batch, H_A, 2, DH_A), F32),
            jax.ShapeDtypeStruct((batch, H_A, 2, LANES), F32),
        ),
        grid=(batch,),
        in_specs=[
            qkv_spec(T_AQ), qkv_spec(T_AK), qkv_spec(T_AV),
            pl.BlockSpec((H_A, nc, 4, L), lambda b: (0, b, 0, 0)),
            pl.BlockSpec((H_A, nc, L, 4), lambda b: (0, b, 0, 0)),
            c0_spec, state_spec, state_spec,
        ],
        out_specs=(pl.BlockSpec((seq, H_A * DH_A), lambda b: (b, 0)), c_spec, state_spec, state_spec),
        scratch_shapes=[pltpu.VMEM((seq, H_A * DH_A), F32)],
        compiler_params=_cparams("parallel"),
        name="mlstm",
    )(qkv, qkv, qkv, g_rows, g_cols, C0, n0t, m0t)
    return h, Cs, jnp.transpose(ns, (0, 2, 1, 3)), jnp.transpose(ms[..., 0], (0, 2, 1))


ATTN_SCALE = DH_B ** -0.5


def _half_mask(shape, half):
    lane = lax.broadcasted_iota(jnp.int32, shape, 1)
    return (lane >= HALF) if half else (lane < HALF)


def _select_half(q, half):
    return jnp.where(_half_mask(q.shape, half), q, jnp.zeros_like(q))


def _swap_halves(x):
    return pltpu.roll(x.astype(F32), HALF, 1).astype(x.dtype)


def _with_ones(v):
    return jnp.concatenate([v, jnp.ones_like(v)], axis=1)


def _softmax_pv(scores, vs, extra=None):
    m = functools.reduce(jnp.maximum, [jnp.max(s, axis=1, keepdims=True) for s in scores])
    if extra is not None:
        m = jnp.maximum(m, extra)
    acc = functools.reduce(lambda x, y: x + y,
                           [_dot(jnp.exp((s - m).astype(BF16)), v) for s, v in zip(scores, vs)])
    denom = acc[:, LANES:]
    if extra is not None:
        denom = denom + jnp.exp(extra - m)
    return acc[:, :LANES] / denom


def _diff_head(q, ks, vs, lam):
    scores = [[_dot_nt(_select_half(q, comp), k) for k in ks] for comp in range(2)]
    o1, o2 = [_softmax_pv(s, vs) for s in scores]
    return o1 - lam * o2


def _ctx_attn_kernel(lam_ref, sink_ref, bq, bk, bv, cq, misc, dq, dk, dv, yb_ref, yc_ref, yd_ref):
    pairs = [slice(p * LANES, (p + 1) * LANES) for p in range(H_B // 2)]
    scores = [[[_dot_nt(_select_half(bq[:, sl] * ATTN_SCALE, half), bk[:, sl])] for half in range(2)] for sl in pairs]
    for sl, pair_scores in zip(pairs, scores):
        v = _with_ones(bv[:, sl])
        outs = [_softmax_pv(s, [v]) for s in pair_scores]
        yb_ref[:, sl] = jnp.where(_half_mask(outs[0].shape, 0), outs[0], outs[1]).astype(yb_ref.dtype)

    kc = misc[:, 0:LANES]
    vc = _with_ones(misc[:, LANES:2 * LANES])
    kv_of = lambda head: head // (H_C // KV_C)
    scores = []
    for p, sl in enumerate(pairs):
        pair_scores = []
        for half in range(2):
            qm = _select_half(cq[:, sl] * ATTN_SCALE, half)
            if kv_of(2 * p + half) != half:
                qm = _swap_halves(qm)
            pair_scores.append([_dot_nt(qm, kc)])
        scores.append(pair_scores)
    for p, (sl, pair_scores) in enumerate(zip(pairs, scores)):
        outs = []
        for half, s in enumerate(pair_scores):
            o = _softmax_pv(s, [vc], extra=sink_ref[2 * p + half])
            outs.append(o if kv_of(2 * p + half) == half else _swap_halves(o))
        yc_ref[:, sl] = jnp.where(_half_mask(outs[0].shape, 0), outs[0], outs[1]).astype(yc_ref.dtype)

    lam = lam_ref[0]
    for h in range(H_D):
        sl = slice(h * LANES, (h + 1) * LANES)
        yd_ref[:, sl] = _diff_head(dq[:, sl] * ATTN_SCALE, [dk[:, sl]], [_with_ones(dv[:, sl])], lam)


def _ctx_attention(qkv, lam, sink, batch, seq):
    tile = lambda t: pl.BlockSpec((seq, COL_TILE), lambda b: (b, t))
    smem = pl.BlockSpec(memory_space=pltpu.SMEM)
    out = lambda dt: jax.ShapeDtypeStruct((batch * seq, BRANCH_W), dt)
    out_spec = pl.BlockSpec((seq, BRANCH_W), lambda b: (b, 0))
    return pl.pallas_call(
        _ctx_attn_kernel,
        out_shape=(out(BF16), out(BF16), out(F32)),
        grid=(batch,),
        in_specs=[smem, smem, tile(T_BQ), tile(T_BK), tile(T_BV), tile(T_CQ), tile(T_MISC),
                  tile(T_DQ), tile(T_DK), tile(T_DV)],
        out_specs=(out_spec, out_spec, out_spec),
        compiler_params=_cparams("parallel"),
        name="ctx_attention",
    )(lam, sink, *([qkv] * 8))


NAT_QROWS = 4
NAT_KROWS = 12
NAT_RI = 2 * NA_ROWS - 1


def _nat_kernel(q_ref, k_ref, v_ref, kc_ref, vc_ref, cb_ref, o_ref):
    rows = k_ref.shape[0] // GRID_W
    r0 = pl.program_id(2) * NAT_QROWS
    kb = jnp.clip(r0 - NA_ROWS // 2, 0, rows - NAT_KROWS)
    koff = pl.multiple_of(kb * GRID_W, GRID_W)
    nq = NAT_QROWS * GRID_W
    nk = NAT_KROWS * GRID_W
    q = q_ref[...] * ATTN_SCALE
    k = k_ref[pl.ds(koff, nk), :]
    v = _with_ones(v_ref[pl.ds(koff, nk), :])
    kc = kc_ref[0].astype(BF16)
    vc = _with_ones(vc_ref[0].astype(BF16))
    left = _half_mask((GRID_W, LANES), 0)
    scores = []
    for half in range(2):
        bias_rows = []
        for a in range(NAT_QROWS):
            r = r0 + a
            rs = jnp.clip(r - NA_ROWS // 2, 0, rows - NA_ROWS)
            blocks = []
            for ip in range(NAT_KROWS // 2):
                sides = []
                for side in range(2):
                    kr = kb + 2 * ip + side
                    ok = (kr >= rs) & (kr < rs + NA_ROWS)
                    ri = jnp.clip(kr - r + NA_ROWS - 1, 0, NAT_RI - 1)
                    sides.append(jnp.where(ok, cb_ref[half, ri], NEG_INF))
                blocks.append(jnp.where(left, sides[0], sides[1]))
            bias_rows.append(jnp.concatenate(blocks, axis=1))
        bias = jnp.concatenate(bias_rows, axis=0)
        qm = _select_half(q, half)
        scores.append([_dot_nt(qm, k) + bias, _dot_nt(qm, kc)])
    outs = [_softmax_pv(s, [v, vc]) for s in scores]
    o_ref[...] = jnp.where(_half_mask((nq, LANES), 0), outs[0], outs[1]).astype(o_ref.dtype)


def _nat_col_table(rpb):
    qc = np.arange(GRID_W)[:, None]
    kc = np.arange(GRID_W)[None, :]
    cs = np.clip(qc - NA_COLS // 2, 0, GRID_W - NA_COLS)
    valid = (kc >= cs) & (kc < cs + NA_COLS)
    n_ci = 2 * NA_COLS - 1
    lo = GRID_W - 1 - (NA_COLS - 1)
    v = jnp.pad(rpb, ((0, 0), (0, 0), (lo, 2 * GRID_W - n_ci - lo)))
    flat = jnp.tile(v, (1, 1, GRID_W))[..., :GRID_W * (2 * GRID_W - 1)]
    toep = flat.reshape(rpb.shape[0], rpb.shape[1], GRID_W, 2 * GRID_W - 1)[..., GRID_W - 1:]
    tab = jnp.where(valid[None, None], toep, NEG_INF)
    return jnp.concatenate([tab, tab], axis=-1)


def _nat_latent(qkv, kc, vc, cb, l, batch, seq):
    nq = NAT_QROWS * GRID_W
    tpb = seq // nq
    past = kc.shape[2]
    return pl.pallas_call(
        _nat_kernel,
        out_shape=jax.ShapeDtypeStruct((batch * seq, BRANCH_W), BF16),
        grid=(batch, H_B // 2, tpb),
        in_specs=[
            pl.BlockSpec((nq, LANES), lambda b, p, i: (b * tpb + i, T_BQ * 4 + p)),
            pl.BlockSpec((seq, LANES), lambda b, p, i: (b, T_BK * 4 + p)),
            pl.BlockSpec((seq, LANES), lambda b, p, i: (b, T_BV * 4 + p)),
            pl.BlockSpec((1, None, past, LANES), lambda b, p, i: (b, l, 0, p)),
            pl.BlockSpec((1, None, past, LANES), lambda b, p, i: (b, l, 0, p)),
            pl.BlockSpec((2, NAT_RI, GRID_W, LANES), lambda b, p, i: (l * (H_B // 2) + p, 0, 0, 0)),
        ],
        out_specs=pl.BlockSpec((nq, LANES), lambda b, p, i: (b * tpb + i, p)),
        compiler_params=_cparams("parallel", "parallel", "parallel"),
        name="nat_latent",
    )(qkv, qkv, qkv, kc, vc, cb)


SWA_QBLK = 512
SWA_KBLK = SWA_QBLK + 2 * SWA_WINDOW


def _swa_kernel(sink_ref, q_ref, kv_ref, kc_ref, vc_ref, o_ref):
    seq = kv_ref.shape[0]
    p = pl.program_id(1)
    t0 = pl.program_id(2) * SWA_QBLK
    k0 = pl.multiple_of(jnp.clip(t0 - SWA_WINDOW, 0, seq - SWA_KBLK), SWA_WINDOW)
    q = q_ref[...] * ATTN_SCALE
    k = kv_ref[pl.ds(k0, SWA_KBLK), 0:LANES]
    v = _with_ones(kv_ref[pl.ds(k0, SWA_KBLK), LANES:2 * LANES])
    kc = kc_ref[0].astype(BF16)
    vc = _with_ones(vc_ref[0].astype(BF16))
    tq = t0 + lax.broadcasted_iota(jnp.int32, (SWA_QBLK, SWA_KBLK), 0)
    tk = k0 + lax.broadcasted_iota(jnp.int32, (SWA_QBLK, SWA_KBLK), 1)
    inside = jnp.abs(tq - tk) <= SWA_WINDOW
    kv_half = p // ((H_C // 2) // KV_C)
    scores = []
    for half in range(2):
        qm = _select_half(q, half)
        qm = jnp.where(kv_half == half, qm, _swap_halves(qm))
        scores.append([jnp.where(inside, _dot_nt(qm, k), NEG_INF), _dot_nt(qm, kc)])
    outs = []
    for half in range(2):
        o = _softmax_pv(scores[half], [v, vc], extra=sink_ref[2 * p + half])
        outs.append(jnp.where(kv_half == half, o, _swap_halves(o)))
    o_ref[...] = jnp.where(_half_mask(q.shape, 0), outs[0], outs[1]).astype(o_ref.dtype)


def _swa_latent(qkv, kc, vc, sink, l, batch, seq):
    tpb = seq // SWA_QBLK
    past = kc.shape[2]
    return pl.pallas_call(
        _swa_kernel,
        out_shape=jax.ShapeDtypeStruct((batch * seq, BRANCH_W), BF16),
        grid=(batch, H_C // 2, tpb),
        in_specs=[
            pl.BlockSpec(memory_space=pltpu.SMEM),
            pl.BlockSpec((SWA_QBLK, LANES), lambda b, p, i: (b * tpb + i, T_CQ * 4 + p)),
            pl.BlockSpec((seq, COL_TILE), lambda b, p, i: (b, T_MISC)),
            pl.BlockSpec((1, None, past, LANES), lambda b, p, i: (b, l, 0, 0)),
            pl.BlockSpec((1, None, past, LANES), lambda b, p, i: (b, l, 0, 0)),
        ],
        out_specs=pl.BlockSpec((SWA_QBLK, LANES), lambda b, p, i: (b * tpb + i, p)),
        compiler_params=_cparams("parallel", "parallel", "parallel"),
        name="swa_latent",
    )(sink, qkv, qkv, kc, vc)


DIFF_QBLK = 512


def _diff_kernel(lam_ref, q_ref, k_ref, v_ref, kc_ref, vc_ref, o_ref):
    q = q_ref[...] * ATTN_SCALE
    ks = [k_ref[...], kc_ref[0].astype(BF16)]
    vs = [_with_ones(v_ref[...]), _with_ones(vc_ref[0].astype(BF16))]
    o_ref[...] = _diff_head(q, ks, vs, lam_ref[0])


def _diff_latent(qkv, kc, vc, lam, l, batch, seq):
    tpb = seq // DIFF_QBLK
    past = kc.shape[2]
    return pl.pallas_call(
        _diff_kernel,
        out_shape=jax.ShapeDtypeStruct((batch * seq, BRANCH_W), F32),
        grid=(batch, H_D, tpb),
        in_specs=[
            pl.BlockSpec(memory_space=pltpu.SMEM),
            pl.BlockSpec((DIFF_QBLK, LANES), lambda b, h, i: (b * tpb + i, T_DQ * 4 + h)),
            pl.BlockSpec((seq, LANES), lambda b, h, i: (b, T_DK * 4 + h)),
            pl.BlockSpec((seq, LANES), lambda b, h, i: (b, T_DV * 4 + h)),
            pl.BlockSpec((1, None, past, LANES), lambda b, h, i: (b, l, 0, h)),
            pl.BlockSpec((1, None, past, LANES), lambda b, h, i: (b, l, 0, h)),
        ],
        out_specs=pl.BlockSpec((DIFF_QBLK, LANES), lambda b, h, i: (b * tpb + i, h)),
        compiler_params=_cparams("parallel", "parallel", "parallel"),
        name="diff_latent",
    )(lam, qkv, qkv, qkv, kc, vc)


def _rms_heads(y, g_ref, per_head_gain):
    outs = []
    for c in range(BRANCH_W // LANES):
        sl = slice(c * LANES, (c + 1) * LANES)
        g = g_ref[:, sl] if per_head_gain else g_ref[...]
        outs.append(_rms(y[:, sl], g))
    return jnp.concatenate(outs, axis=1)


def _first_lane(hit, lane_f):
    return jnp.min(jnp.where(hit, lane_f, float(LANES)), axis=1, keepdims=True)


def _top_group(logits):
    lane = lax.broadcasted_iota(jnp.int32, logits.shape, 1)
    gl = jnp.where(lane < N_GROUPS, logits, NEG_INF)
    return _first_lane(gl == jnp.max(gl, axis=1, keepdims=True), lane.astype(F32))


def _gate_weights(logits, group):
    lane = lax.broadcasted_iota(jnp.int32, logits.shape, 1)
    lane_f = lane.astype(F32)
    gl = jnp.where(lane < N_GROUPS, logits, NEG_INF)
    gmax = jnp.max(gl, axis=1, keepdims=True)
    g_logit = jnp.sum(jnp.where(lane_f == group, logits, 0.0), axis=1, keepdims=True)
    gw = jnp.exp(g_logit - gmax) / jnp.sum(jnp.exp(gl - gmax), axis=1, keepdims=True)
    e_lane = lane - N_GROUPS
    e_group = lax.shift_right_arithmetic(e_lane, EXP_PER_GROUP.bit_length() - 1).astype(F32)
    in_group = (e_lane >= 0) & (e_lane < N_EXPERTS) & (e_group == group)
    el = jnp.where(in_group, logits, NEG_INF)
    e1 = jnp.max(el, axis=1, keepdims=True)
    i1 = _first_lane(el == e1, lane_f)
    el2 = jnp.where(lane_f == i1, NEG_INF, el)
    e2 = jnp.max(el2, axis=1, keepdims=True)
    i2 = _first_lane(el2 == e2, lane_f)
    t = jnp.exp(e2 - e1)
    w1 = gw / (1.0 + t)
    return jnp.where(lane_f == i1, w1, jnp.where(lane_f == i2, w1 * t, 0.0))


def _merge_kernel(ya, yb, yc, yd, ao, gt0, gt1, gt2, gt3, x_ref, mod_ref, anorm, subln, nffn,
                  wbr, wout, wr, xo_ref, h2_ref, group_ref, *, yd_scale):
    a = _sigmoid(ao[...].astype(F32)) * _rms_heads(ya[...], anorm, True)
    d = _rms_heads(yd[...], subln, False) * yd_scale
    branches = (a, yb[...], yc[...], d)
    mix = None
    for n, (br, gt) in enumerate(zip(branches, (gt0, gt1, gt2, gt3))):
        term = _sigmoid(gt[...].astype(F32)) * _dot(br.astype(BF16), wbr[n])
        mix = term if mix is None else mix + term
    out = _dot(mix.astype(BF16), wout[...])
    xn = x_ref[...] + mod_ref[0, 2:3, :] * out
    xo_ref[...] = xn
    h2 = _rms(xn, nffn[...]) * (1.0 + mod_ref[0, 4:5, :]) + mod_ref[0, 3:4, :]
    h2_ref[...] = h2
    group_ref[...] = jnp.broadcast_to(_top_group(_dot(h2.astype(BF16), wr[...])), group_ref.shape)


def _merge(ys, qkv, x, mod3, a_norm, subln, norm_ffn, w_branch, w_out, w_route, lam_init, l):
    n = x.shape[0]
    tm = 512
    tiles_per_mod = n // mod3.shape[0] // tm
    y_spec = pl.BlockSpec((tm, BRANCH_W), lambda i: (i, 0))
    gt_spec = lambda k: pl.BlockSpec((tm, D_MODEL), lambda i: (i, T_GT * COL_TILE // D_MODEL + k))
    full = lambda shape: pl.BlockSpec(shape, lambda i: (0,) * len(shape))
    layer = lambda shape: pl.BlockSpec((None,) + shape, lambda i: (l,) + (0,) * len(shape))
    x_spec = pl.BlockSpec((tm, D_MODEL), lambda i: (i, 0))
    return pl.pallas_call(
        functools.partial(_merge_kernel, yd_scale=1.0 - lam_init),
        out_shape=(
            jax.ShapeDtypeStruct((n, D_MODEL), F32),
            jax.ShapeDtypeStruct((n, D_MODEL), F32),
            jax.ShapeDtypeStruct((n, LANES), F32),
        ),
        grid=(n // tm,),
        in_specs=[
            y_spec, y_spec, y_spec, y_spec,
            pl.BlockSpec((tm, COL_TILE), lambda i: (i, T_AO)),
            gt_spec(0), gt_spec(1), gt_spec(2), gt_spec(3),
            x_spec,
            pl.BlockSpec((1, 6, D_MODEL), lambda i: (i // tiles_per_mod, 0, 0)),
            full((1, BRANCH_W)), full((1, LANES)), full((1, D_MODEL)),
            layer((N_BRANCH, BRANCH_W, D_MODEL)), layer((D_MODEL, D_MODEL)), layer((D_MODEL, LANES)),
        ],
        out_specs=(x_spec, x_spec, pl.BlockSpec((tm, LANES), lambda i: (i, 0))),
        compiler_params=_cparams("parallel"),
        name="merge_route",
    )(*ys, qkv, qkv, qkv, qkv, qkv, x, mod3, a_norm.reshape(1, BRANCH_W), subln.reshape(1, LANES),
      norm_ffn.reshape(1, D_MODEL), w_branch, w_out, w_route)


MOE_TILE = 512
ROW_DMA_UNROLL = 16


def _row_copies(src_row, dst_row, sem, n):
    def copy(i):
        return pltpu.make_async_copy(src_row(i), dst_row(i), sem)

    def start(i, c):
        copy(i).start()
        return c

    def wait(i, c):
        copy(i).wait()
        return c

    lax.fori_loop(0, n, start, 0, unroll=ROW_DMA_UNROLL)
    lax.fori_loop(0, n, wait, 0, unroll=ROW_DMA_UNROLL)


def _moe_scatter_kernel(slot_ref, h_ref, init_ref, sorted_ref, sem):
    del init_ref
    base = pl.program_id(0) * MOE_TILE
    _row_copies(lambda i: h_ref.at[pl.ds(i, 1), :],
                lambda i: sorted_ref.at[pl.ds(slot_ref[base + i], 1), :], sem, MOE_TILE)


def _moe_group_kernel(tile_group_ref, n_used_ref, h_ref, wr_ref, wg_f32, wu_f32, wd_f32, o_ref,
                      wg_ref, wu_ref, wd_ref):
    t = pl.program_id(0)
    group = tile_group_ref[t]

    @pl.when((t == 0) | (group != tile_group_ref[jnp.maximum(t - 1, 0)]))
    def _():
        for e in range(EXP_PER_GROUP):
            wg_ref[e] = wg_f32[0, e].astype(BF16)
            wu_ref[e] = wu_f32[0, e].astype(BF16)
            wd_ref[e] = wd_f32[0, e].astype(BF16)

    @pl.when(t < n_used_ref[0])
    def _():
        h = h_ref[...].astype(BF16)
        gates = _gate_weights(_dot(h, wr_ref[...]), group.astype(F32))
        lane = lax.broadcasted_iota(jnp.int32, gates.shape, 1)
        acc = None
        for e in range(EXP_PER_GROUP):
            g = jnp.sum(jnp.where(lane == N_GROUPS + group * EXP_PER_GROUP + e, gates, 0.0), axis=1, keepdims=True)
            hg = _dot(h, wg_ref[e])
            hid = hg * _sigmoid(hg) * _dot(h, wu_ref[e]) * g
            term = _dot(hid.astype(BF16), wd_ref[e])
            acc = term if acc is None else acc + term
        o_ref[...] = acc

    @pl.when(t >= n_used_ref[0])
    def _():
        o_ref[...] = jnp.zeros_like(o_ref)


def _moe_gather_kernel(slot_ref, y_ref, x_ref, mod_ref, nf_ref, o_ref, buf, sem, *, final_norm):
    t = pl.program_id(0)
    nt = pl.num_programs(0)

    def copy(tile, b, i):
        return pltpu.make_async_copy(y_ref.at[pl.ds(slot_ref[tile * MOE_TILE + i], 1), :],
                                     buf.at[b, pl.ds(i, 1), :], sem.at[b])

    def start_tile(tile, b):
        def start(i, c):
            copy(tile, b, i).start()
            return c

        lax.fori_loop(0, MOE_TILE, start, 0, unroll=ROW_DMA_UNROLL)

    cur = t % 2

    @pl.when(t == 0)
    def _():
        start_tile(0, 0)

    @pl.when(t + 1 < nt)
    def _():
        start_tile(t + 1, 1 - cur)

    def wait(i, c):
        copy(t, cur, i).wait()
        return c

    lax.fori_loop(0, MOE_TILE, wait, 0, unroll=ROW_DMA_UNROLL)
    y = x_ref[...] + mod_ref[0, 5:6, :] * buf[cur]
    o_ref[...] = _rms(y, nf_ref[...]) if final_norm else y


def _moe(h2, group, x, mod3, w_route, wg, wu, wd, norm_final, final_norm, l, init=None):
    n = x.shape[0]
    tm = MOE_TILE
    n_tiles = n // tm + N_GROUPS
    onehot = (group[:, None] == jnp.arange(N_GROUPS, dtype=jnp.int32)[None]).astype(jnp.int32)
    csum = jnp.cumsum(onehot, axis=0)
    counts = csum[-1]
    rank = jnp.sum((csum - onehot) * onehot, axis=1)
    group_tiles = (counts + tm - 1) // tm
    tile_end = jnp.cumsum(group_tiles)
    group_start = (tile_end - group_tiles) * tm
    slot = (jnp.sum(onehot * group_start[None], axis=1) + rank).astype(jnp.int32)
    tile_group = jnp.minimum(jnp.sum(jnp.arange(n_tiles, dtype=jnp.int32)[:, None] >= tile_end[None], axis=1),
                             N_GROUPS - 1).astype(jnp.int32)
    n_used = tile_end[-1:].astype(jnp.int32)

    sorted_h = pl.pallas_call(
        _moe_scatter_kernel,
        out_shape=jax.ShapeDtypeStruct((n_tiles * tm, D_MODEL), F32),
        grid_spec=pltpu.PrefetchScalarGridSpec(
            num_scalar_prefetch=1, grid=(n // tm,),
            in_specs=[pl.BlockSpec((tm, D_MODEL), lambda i, slot: (i, 0)), pl.BlockSpec(memory_space=pl.ANY)],
            out_specs=pl.BlockSpec(memory_space=pl.ANY),
            scratch_shapes=[pltpu.SemaphoreType.DMA],
        ),
        input_output_aliases={2: 0},
        compiler_params=_cparams("arbitrary"),
        name="moe_scatter",
    )(slot, h2, jnp.zeros((n_tiles * tm, D_MODEL), F32) if init is None else init)

    grp = lambda shape: pl.BlockSpec((None, 1) + shape, lambda t, tg, nu: (l, tg[t], 0, 0, 0))
    sorted_y = pl.pallas_call(
        _moe_group_kernel,
        out_shape=jax.ShapeDtypeStruct((n_tiles * tm, D_MODEL), F32),
        grid_spec=pltpu.PrefetchScalarGridSpec(
            num_scalar_prefetch=2, grid=(n_tiles,),
            in_specs=[
                pl.BlockSpec((tm, D_MODEL), lambda t, tg, nu: (t, 0)),
                pl.BlockSpec((None, D_MODEL, LANES), lambda t, tg, nu: (l, 0, 0)),
                grp((EXP_PER_GROUP, D_MODEL, D_EXPERT)), grp((EXP_PER_GROUP, D_MODEL, D_EXPERT)),
                grp((EXP_PER_GROUP, D_EXPERT, D_MODEL)),
            ],
            out_specs=pl.BlockSpec((tm, D_MODEL), lambda t, tg, nu: (t, 0)),
            scratch_shapes=[pltpu.VMEM((EXP_PER_GROUP, D_MODEL, D_EXPERT), BF16),
                            pltpu.VMEM((EXP_PER_GROUP, D_MODEL, D_EXPERT), BF16),
                            pltpu.VMEM((EXP_PER_GROUP, D_EXPERT, D_MODEL), BF16)],
        ),
        compiler_params=_cparams("arbitrary"),
        name="moe_experts",
    )(tile_group, n_used, sorted_h, w_route, wg, wu, wd)

    tiles_per_mod = n // mod3.shape[0] // tm
    row = pl.BlockSpec((tm, D_MODEL), lambda i, slot: (i, 0))
    y = pl.pallas_call(
        functools.partial(_moe_gather_kernel, final_norm=final_norm),
        out_shape=jax.ShapeDtypeStruct((n, D_MODEL), F32),
        grid_spec=pltpu.PrefetchScalarGridSpec(
            num_scalar_prefetch=1, grid=(n // tm,),
            in_specs=[
                pl.BlockSpec(memory_space=pl.ANY), row,
                pl.BlockSpec((1, 6, D_MODEL), lambda i, slot: (i // tiles_per_mod, 0, 0)),
                pl.BlockSpec((1, D_MODEL), lambda i, slot: (0, 0)),
            ],
            out_specs=row,
            scratch_shapes=[pltpu.VMEM((2, tm, D_MODEL), F32), pltpu.SemaphoreType.DMA((2,))],
        ),
        compiler_params=_cparams("arbitrary"),
        name="moe_gather",
    )(slot, sorted_y, x, mod3, norm_final.reshape(1, D_MODEL))
    return y, sorted_h


W_IN_GATES = 2048
W_IN_N_GATES = 16
W_IN_TILE_SRC = {T_AQ: 0, T_AK: 512, T_AV: 1024, T_AO: 1536, T_BQ: 2064, T_BK: 2576, T_BV: 3088, T_CQ: 3600,
                 T_DQ: 4368, T_DK: 4880, T_DV: 5392, T_MISC: 4112, **{T_GT + k: 5904 + COL_TILE * k for k in range(8)}}


def _reorder_kernel(src_ref, wt_hbm, gates_ref, o_ref, buf, sem):
    step = pl.program_id(0) * N_COL_TILES + pl.program_id(1)
    n_steps = pl.num_programs(0) * N_COL_TILES

    def copy(s):
        start = pl.multiple_of(src_ref[s % N_COL_TILES], 8)
        return pltpu.make_async_copy(wt_hbm.at[s // N_COL_TILES, pl.ds(start, COL_TILE), :], buf.at[s % 2],
                                     sem.at[s % 2])

    @pl.when(step == 0)
    def _():
        copy(step).start()

    @pl.when(step + 1 < n_steps)
    def _():
        copy(step + 1).start()

    copy(step).wait()
    rows = buf[step % 2]

    @pl.when(pl.program_id(1) == T_MISC)
    def _():
        pad = jnp.zeros((COL_TILE - 2 * LANES - W_IN_N_GATES, D_MODEL), F32)
        tile = jnp.concatenate([rows[:2 * LANES], gates_ref[...], pad], axis=0)
        o_ref[...] = tile.T.astype(BF16)

    @pl.when(pl.program_id(1) != T_MISC)
    def _():
        o_ref[...] = rows.T.astype(BF16)


def _reorder_w_in(w_in):
    src = np.array([W_IN_TILE_SRC[t] for t in range(N_COL_TILES)], np.int32)
    wt = jnp.swapaxes(w_in, 1, 2)
    return pl.pallas_call(
        _reorder_kernel,
        out_shape=jax.ShapeDtypeStruct((DEPTH, D_MODEL, N_PROJ), BF16),
        grid_spec=pltpu.PrefetchScalarGridSpec(
            num_scalar_prefetch=1, grid=(DEPTH, N_COL_TILES),
            in_specs=[pl.BlockSpec(memory_space=pl.ANY),
                      pl.BlockSpec((None, W_IN_N_GATES, D_MODEL), lambda l, j, src: (l, W_IN_GATES // W_IN_N_GATES, 0))],
            out_specs=pl.BlockSpec((None, D_MODEL, COL_TILE), lambda l, j, src: (l, 0, j)),
            scratch_shapes=[pltpu.VMEM((2, COL_TILE, D_MODEL), F32), pltpu.SemaphoreType.DMA((2,))],
        ),
        compiler_params=_cparams("arbitrary", "arbitrary"),
        name="reorder_w_in",
    )(jnp.asarray(src), wt, wt)


def _rope_tables(seq):
    nf = DH_C // 4
    inv = ROPE_BASE ** (-jnp.arange(nf, dtype=F32) / nf)
    t = jnp.arange(seq)
    ang_r = (t // GRID_W).astype(F32)[:, None] * inv[None, :]
    ang_c = (t % GRID_W).astype(F32)[:, None] * inv[None, :]
    cos = jnp.concatenate([jnp.cos(ang_r)] * 2 + [jnp.cos(ang_c)] * 2, axis=1)
    sin = jnp.concatenate([-jnp.sin(ang_r), jnp.sin(ang_r), -jnp.sin(ang_c), jnp.sin(ang_c)], axis=1)
    return jnp.tile(cos, (1, 2)), jnp.tile(sin, (1, 2))


def _layer(l, x, mod3, seq, lat, p, prev_cache=None, moe_init=None):
    batch = x.shape[0] // seq
    lam_init = 0.8 - 0.6 * math.exp(-0.3 * l)
    ctx = lat is None
    proj = _qkv_proj(x, p["norm_mix"][l], mod3, p["w_in"], l, None if ctx else p["rope"],
                     dict(layer=l, seq=seq, prev=prev_cache) if ctx else None, tm=1024 if ctx else 2048)
    qkv, gates_raw = proj[:2]
    gates = gates_raw[:, :16] + jnp.concatenate([p["i_bias"][l].reshape(-1), p["f_bias"][l].reshape(-1)])[None]
    lam = (jnp.exp(jnp.sum(p["lq1"][l] * p["lk1"][l])) - jnp.exp(jnp.sum(p["lq2"][l] * p["lk2"][l]))
           + lam_init).reshape(1).astype(F32)
    sink = p["sink"][l]
    new = None
    if ctx:
        C0 = jnp.zeros((batch, 1, 2, H_A, DH_A, DH_A), F32)
        n0 = jnp.zeros((batch, 2, H_A, DH_A), F32)
        m0 = jnp.zeros((batch, 2, H_A), F32)
        ya, Cs, ns, ms = _mlstm(qkv, gates, C0, 0, n0, m0, batch, seq)
        yb, yc, yd = _ctx_attention(qkv, lam, sink, batch, seq)
        new = (proj[2:], (Cs, ns, ms))
    else:
        nat_k, nat_v, swa_k, swa_v, diff_k, diff_v, C0, n0, m0 = lat
        flat = lambda a: a.reshape(a.shape[:3] + (-1,))
        ya, _, _, _ = _mlstm(qkv, gates, C0, l, n0[:, l], m0[:, l], batch, seq)
        yb = _nat_latent(qkv, flat(nat_k), flat(nat_v), p["nat_table"], l, batch, seq)
        yc = _swa_latent(qkv, flat(swa_k), flat(swa_v), sink, l, batch, seq)
        yd = _diff_latent(qkv, flat(diff_k), flat(diff_v), lam, l, batch, seq)
    xn, h2, group = _merge((ya, yb, yc, yd), qkv, x, mod3, p["a_norm"][l], p["subln"][l],
                           p["norm_ffn"][l], p["w_branch"], p["w_out"], p["w_route"], lam_init, l)
    y, moe_buf = _moe(h2, group[:, 0].astype(jnp.int32), xn, mod3, p["w_route"], p["wg"], p["wu"], p["wd"],
                      p["norm_final"], final_norm=(l == DEPTH - 1), l=l, init=moe_init)
    return y, new, moe_buf


def kernel(x_prompt, x_sample, cache_nat_k, cache_nat_v, cache_swa_k, cache_swa_v, cache_diff_k, cache_diff_v, state_mlstm_C, state_mlstm_n, state_mlstm_m, c, c_ctx, norm_mix, norm_ffn, norm_final, w_mod, b_mod, w_in, mlstm_i_bias, mlstm_f_bias, mlstm_norm, nat_rpb, swa_sink, diff_lq1, diff_lk1, diff_lq2, diff_lk2, diff_subln, w_branch, w_out, router_group, router_expert, w_exp_gate, w_exp_up, w_exp_down):
    b_ctx, s_ctx, _ = x_prompt.shape
    b_lat, s_lat, _ = x_sample.shape
    cond8 = jnp.concatenate([c_ctx[None], c, jnp.zeros((8 - 1 - b_lat, D_MODEL), F32)], axis=0)
    mod = _modulation(cond8, w_mod, b_mod).reshape(DEPTH, 8, 6, D_MODEL)
    route_pad = jnp.zeros((DEPTH, D_MODEL, LANES - N_GROUPS - N_EXPERTS), F32)
    p = dict(
        norm_mix=norm_mix, norm_ffn=norm_ffn, norm_final=norm_final,
        w_in=_reorder_w_in(w_in),
        i_bias=mlstm_i_bias, f_bias=mlstm_f_bias, a_norm=mlstm_norm, sink=swa_sink,
        nat_table=_nat_col_table(nat_rpb.reshape((DEPTH * H_B,) + nat_rpb.shape[2:])),
        lq1=diff_lq1, lk1=diff_lk1, lq2=diff_lq2, lk2=diff_lk2, subln=diff_subln,
        w_branch=w_branch.astype(BF16), w_out=w_out.astype(BF16),
        w_route=jnp.concatenate([router_group, router_expert, route_pad], axis=-1).astype(BF16),
        wg=w_exp_gate.reshape(DEPTH, N_GROUPS, EXP_PER_GROUP, D_MODEL, D_EXPERT),
        wu=w_exp_up.reshape(DEPTH, N_GROUPS, EXP_PER_GROUP, D_MODEL, D_EXPERT),
        wd=w_exp_down.reshape(DEPTH, N_GROUPS, EXP_PER_GROUP, D_EXPERT, D_MODEL),
        rope=_rope_tables(s_lat),
    )
    yp = x_prompt.reshape(b_ctx * s_ctx, D_MODEL)
    ys = x_sample.reshape(b_lat * s_lat, D_MODEL)
    states = []
    cache = buf_ctx = buf_lat = None
    for l in range(DEPTH):
        yp, (cache, state), buf_ctx = _layer(l, yp, mod[l, 0:1], s_ctx, None, p, cache, buf_ctx)
        states.append(state)
        lat = (cache_nat_k, cache_nat_v, cache_swa_k, cache_swa_v, cache_diff_k, cache_diff_v,
               state_mlstm_C, state_mlstm_n, state_mlstm_m)
        ys, _, buf_lat = _layer(l, ys, mod[l, 1:1 + b_lat], s_lat, lat, p, moe_init=buf_lat)
    def stack_layers(parts):
        flat = [a.reshape(a.shape[0], 1, -1) for a in parts]
        return jnp.concatenate(flat, axis=1).reshape((parts[0].shape[0], DEPTH) + parts[0].shape[1:])

    bk, bv, dk, dv, ck, cv = cache
    heads = lambda a, h: a.reshape(a.shape[:3] + (h, a.shape[3] // h))
    caches = (heads(bk, H_B), heads(bv, H_B), heads(ck, KV_C), heads(cv, KV_C), heads(dk, H_D), heads(dv, H_D))
    stacked = tuple(stack_layers([states[l][k] for l in range(DEPTH)]) for k in range(3))
    return (yp.reshape(b_ctx, s_ctx, D_MODEL), ys.reshape(b_lat, s_lat, D_MODEL)) + caches + stacked
```

```python
import functools
import math

import numpy as np
import jax
import jax.numpy as jnp
from jax import lax
from jax.experimental import pallas as pl
from jax.experimental.pallas import tpu as pltpu

F32 = jnp.float32
BF16 = jnp.bfloat16

D_MODEL = 1024
DEPTH = 2
GRID_W = 64
BRANCH_W = 512
N_BRANCH = 4
H_A, DH_A, MLSTM_CHUNK = 4, 128, 128
H_B, DH_B = 8, 64
NA_ROWS, NA_COLS = 8, 16
H_C, KV_C, DH_C = 8, 2, 64
SWA_WINDOW = 128
H_D, DH_D = 4, 64
N_GROUPS, EXP_PER_GROUP = 4, 4
N_EXPERTS = N_GROUPS * EXP_PER_GROUP
D_EXPERT = 256
ROPE_BASE = 10000.0
EPS = 1e-6

LANES = 128
HALF = 64
COL_TILE = 512
N_COL_TILES = 20
N_PROJ = COL_TILE * N_COL_TILES
T_AQ, T_AK, T_AV, T_AO, T_BQ, T_BK, T_BV, T_CQ, T_DQ, T_DK, T_DV, T_MISC, T_GT = range(13)
MISC_GATE_OFF = 256
VMEM_LIMIT = 56 * 1024 * 1024
NEG_INF = float("-inf")

_NT = (((1,), (1,)), ((), ()))
_TN = (((0,), (0,)), ((), ()))


def _cparams(*sem):
    return pltpu.CompilerParams(dimension_semantics=sem, vmem_limit_bytes=VMEM_LIMIT)


def _dot(a, b):
    return jnp.dot(a, b, preferred_element_type=F32)


def _dot_nt(a, b):
    return lax.dot_general(a, b, _NT, preferred_element_type=F32)


def _dot_tn(a, b):
    return lax.dot_general(a, b, _TN, preferred_element_type=F32)


def _rms(x, g):
    return x * lax.rsqrt(jnp.mean(x * x, axis=-1, keepdims=True) + EPS) * g


def _sigmoid(x):
    return 0.5 * jnp.tanh(0.5 * x) + 0.5


def _mod_kernel(c_ref, w_ref, b_ref, o_ref):
    c = c_ref[...]
    s = (c * _sigmoid(c)).astype(BF16)
    o_ref[0] = _dot(s, w_ref[0].astype(BF16)) + b_ref[0]


def _modulation(cond8, w_mod, b_mod):
    tn = 1536
    n_out = 6 * D_MODEL
    return pl.pallas_call(
        _mod_kernel,
        out_shape=jax.ShapeDtypeStruct((DEPTH, 8, n_out), F32),
        grid=(DEPTH, n_out // tn),
        in_specs=[
            pl.BlockSpec((8, D_MODEL), lambda l, j: (0, 0)),
            pl.BlockSpec((1, D_MODEL, tn), lambda l, j: (l, 0, j)),
            pl.BlockSpec((1, 1, tn), lambda l, j: (l, 0, j)),
        ],
        out_specs=pl.BlockSpec((1, 8, tn), lambda l, j: (l, 0, j)),
        compiler_params=_cparams("parallel", "parallel"),
        name="modulation",
    )(cond8, w_mod, b_mod.reshape(DEPTH, 1, n_out))


CACHE_TILES = (T_BK, T_BV, T_DK, T_DV)
QKV_ROW_CHUNK = 512


def _rope128(x, cos, sin):
    lane = lax.broadcasted_iota(jnp.int32, x.shape, 1)
    partner = jnp.where((lane % 32) < 16, pltpu.roll(x, LANES - 16, 1), pltpu.roll(x, 16, 1))
    return x * cos + partner * sin


def _qkv_kernel(x_ref, nw_ref, mod_ref, w_ref, *rest, rope, cache, n_aliased):
    rest = list(rest)
    if rope:
        cos_ref, sin_ref = rest[:2]
        rest = rest[2:]
    rest = rest[n_aliased:]
    o_ref, g_ref = rest[:2]
    cache_refs = rest[2:-1]
    h_scr = rest[-1]
    j = pl.program_id(1)

    @pl.when(j == 0)
    def _():
        h = _rms(x_ref[...], nw_ref[...]) * (1.0 + mod_ref[0, 1:2, :]) + mod_ref[0, 0:1, :]
        h_scr[...] = h.astype(BF16)

    def tile_kind(cond, store):
        @pl.when(cond)
        def _():
            for r in range(h_scr.shape[0] // QKV_ROW_CHUNK):
                rows = slice(r * QKV_ROW_CHUNK, (r + 1) * QKV_ROW_CHUNK)
                store(rows, _dot(h_scr[rows, :], w_ref[...]))

    def store_batches(ref, rows, val):
        seq = ref.shape[1]
        for bb in range(QKV_ROW_CHUNK // seq):
            ref[rows.start // seq + bb] = val[bb * seq:(bb + 1) * seq]

    def roped(rows, acc, c):
        sl = slice(c * LANES, (c + 1) * LANES)
        return _rope128(acc[:, sl], cos_ref[rows, :], sin_ref[rows, :]).astype(BF16)

    def store_plain(rows, acc):
        o_ref[rows, :] = acc.astype(BF16)

    def store_ak(rows, acc):
        o_ref[rows, :] = (acc * (DH_A ** -0.5)).astype(BF16)

    def store_misc(rows, acc):
        o_ref[rows, :] = acc.astype(BF16)
        if rope:
            o_ref[rows, :LANES] = roped(rows, acc, 0)
        g_ref[rows, :] = acc[:, MISC_GATE_OFF:MISC_GATE_OFF + LANES]
        if cache:
            store_batches(cache_refs[-2], rows, acc[:, :LANES])
            store_batches(cache_refs[-1], rows, acc[:, LANES:2 * LANES])

    def store_rope(rows, acc):
        for c in range(COL_TILE // LANES):
            o_ref[rows, c * LANES:(c + 1) * LANES] = roped(rows, acc, c)

    special = (j == T_AK) | (j == T_MISC)
    tile_kind(j == T_AK, store_ak)
    tile_kind(j == T_MISC, store_misc)
    if rope:
        is_rope = (j == T_CQ) | (j == T_DQ) | (j == T_DK)
        special = special | is_rope
        tile_kind(is_rope, store_rope)
    if cache:
        for t, ref in zip(CACHE_TILES, cache_refs[:-2]):
            def store_cached(rows, acc, ref=ref):
                o_ref[rows, :] = acc.astype(BF16)
                store_batches(ref, rows, acc)
            special = special | (j == t)
            tile_kind(j == t, store_cached)
    tile_kind(jnp.logical_not(special), store_plain)


def _qkv_proj(x, norm_w, mod3, w_in_r, l, rope_tabs, cache, tm):
    n = x.shape[0]
    tiles_per_mod = n // mod3.shape[0] // tm
    rope = rope_tabs is not None
    in_specs = [
        pl.BlockSpec((tm, D_MODEL), lambda i, j: (i, 0)),
        pl.BlockSpec((1, D_MODEL), lambda i, j: (0, 0)),
        pl.BlockSpec((1, 6, D_MODEL), lambda i, j: (i // tiles_per_mod, 0, 0)),
        pl.BlockSpec((None, D_MODEL, COL_TILE), lambda i, j: (l, 0, j)),
    ]
    args = [x, norm_w.reshape(1, D_MODEL), mod3, w_in_r]
    if rope:
        tiles_per_seq = rope_tabs[0].shape[0] // tm
        in_specs += [pl.BlockSpec((tm, LANES), lambda i, j: (i % tiles_per_seq, 0))] * 2
        args += list(rope_tabs)
    row = lambda w: pl.BlockSpec((tm, w), lambda i, j: (i, 0))
    out_shape = [jax.ShapeDtypeStruct((n, N_PROJ), BF16), jax.ShapeDtypeStruct((n, LANES), F32)]
    out_specs = [pl.BlockSpec((tm, COL_TILE), lambda i, j: (i, j)), row(LANES)]
    aliases = {}
    n_aliased = 0
    if cache:
        seq, layer = cache["seq"], cache["layer"]
        widths = [COL_TILE] * len(CACHE_TILES) + [LANES, LANES]
        out_shape += [jax.ShapeDtypeStruct((n // seq, DEPTH, seq, w), F32) for w in widths]
        out_specs += [pl.BlockSpec((tm // seq, None, seq, w), lambda i, j: (i, layer, 0, 0)) for w in widths]
        prev = cache["prev"] or [jnp.zeros(s.shape, s.dtype) for s in out_shape[2:]]
        n_aliased = len(widths)
        aliases = {len(args) + k: 2 + k for k in range(n_aliased)}
        in_specs += [pl.BlockSpec(memory_space=pl.ANY)] * n_aliased
        args += list(prev)
    return pl.pallas_call(
        functools.partial(_qkv_kernel, rope=rope, cache=bool(cache), n_aliased=n_aliased),
        input_output_aliases=aliases,
        out_shape=tuple(out_shape),
        grid=(n // tm, N_COL_TILES),
        in_specs=in_specs,
        out_specs=tuple(out_specs),
        scratch_shapes=[pltpu.VMEM((tm, D_MODEL), BF16)],
        compiler_params=_cparams("parallel", "arbitrary"),
        name="qkv_rope" if rope else "qkv",
    )(*args)


def _log_sigmoid(x):
    return jnp.minimum(x, 0.0) - jnp.log1p(jnp.exp(-jnp.abs(x)))


def _mlstm_chunk(c, h, d, state, q_ref, k_ref, v_ref, gr_ref, gc_ref, out_ref):
    L = MLSTM_CHUNK
    C, n, m = state
    row = lax.broadcasted_iota(jnp.int32, (L, L), 0)
    col = lax.broadcasted_iota(jnp.int32, (L, L), 1)
    tri = (col <= row) if d == 0 else (col >= row)
    tri_t = (row <= col) if d == 0 else (row >= col)
    off = pl.multiple_of(c * L, L)
    hs = slice(h * DH_A, (h + 1) * DH_A)
    qb = q_ref[pl.ds(off, L), hs]
    kb = k_ref[pl.ds(off, L), hs]
    vb = v_ref[pl.ds(off, L), hs]
    kx, vx = kb.astype(F32), vb.astype(F32)
    g_rows = gr_ref[h, c]
    g_cols = gc_ref[h, c]
    ig_r = g_rows[2 * d:2 * d + 1, :]
    lf_r = _log_sigmoid(g_rows[2 * d + 1:2 * d + 2, :])
    ig_c = g_cols[:, 2 * d:2 * d + 1]
    lf_c = _log_sigmoid(g_cols[:, 2 * d + 1:2 * d + 2])
    b_c = jnp.sum(jnp.where(tri, lf_r, 0.0), axis=1, keepdims=True)
    b_r = jnp.sum(jnp.where(tri_t, lf_c, 0.0), axis=0, keepdims=True)
    qk = _dot_nt(qb, kb)
    c_ext = jnp.concatenate([C.astype(BF16), jnp.broadcast_to(n.astype(BF16), (DH_A, DH_A))], axis=0)
    qc = _dot_nt(qb, c_ext)
    yield
    logd = jnp.where(tri, b_c - b_r + ig_r, NEG_INF)
    inter = jnp.broadcast_to(b_c + m, (L, L))
    m_t = jnp.maximum(inter, jnp.broadcast_to(jnp.max(logd, axis=1, keepdims=True), (L, L)))
    yield
    a = jnp.exp(inter - m_t)
    s = qk * jnp.exp(logd - m_t)
    sv = _dot(s.astype(BF16), _with_ones(vb))
    num = a * qc[:, :DH_A] + sv[:, :DH_A]
    den = a * qc[:, DH_A:] + sv[:, DH_A:]
    yield
    out_ref[pl.ds(off, L), hs] = num / jnp.maximum(jnp.abs(den), jnp.exp(-m_t))
    b_tot = jnp.sum(lf_r, axis=1, keepdims=True)
    g_c = b_tot - b_c + ig_c
    m_new = jnp.maximum(b_tot + m, jnp.max(g_c, axis=0, keepdims=True))
    yield
    decay = jnp.exp(b_tot + m - m_new)
    wk = jnp.exp(g_c - m_new)
    C_new = decay * C + _dot_tn((wk * vx).astype(BF16), kb)
    n_new = decay * n + jnp.sum(wk * kx, axis=0, keepdims=True)
    return C_new, n_new, m_new


def _run_staged(gens):
    results = [None] * len(gens)
    live = list(range(len(gens)))
    while live:
        for i in list(live):
            try:
                next(gens[i])
            except StopIteration as stop:
                results[i] = stop.value
                live.remove(i)
    return results


def _mlstm_kernel(q_ref, k_ref, v_ref, gr_ref, gc_ref, c0_ref, n0_ref, m0_ref,
                  h_ref, cs_ref, ns_ref, ms_ref, hb_scr, *, nc):
    chains = [(h, d) for h in range(H_A) for d in range(2)]

    def body(ci, states):
        gens = [_mlstm_chunk(ci if d == 0 else nc - 1 - ci, h, d, state, q_ref, k_ref, v_ref, gr_ref, gc_ref,
                             h_ref if d == 0 else hb_scr) for (h, d), state in zip(chains, states)]
        return tuple(_run_staged(gens))

    init = tuple((c0_ref[0, d, h], n0_ref[0, h, d:d + 1, :], m0_ref[0, h, d:d + 1, 0:1]) for h, d in chains)
    final = lax.fori_loop(0, nc, body, init)
    for (h, d), (C, n, m) in zip(chains, final):
        cs_ref[0, d, h] = C
        ns_ref[0, h, d:d + 1, :] = n
        ms_ref[0, h, d:d + 1, :] = jnp.broadcast_to(m, (1, LANES))
    h_ref[...] += hb_scr[...]


def _mlstm(qkv, gates, C0, l, n0, m0, batch, seq):
    nc = seq // MLSTM_CHUNK
    L = MLSTM_CHUNK
    ig = gates[:, :8].reshape(batch, nc, L, 2, H_A)
    fg = gates[:, 8:].reshape(batch, nc, L, 2, H_A)
    g4 = jnp.stack([ig[..., 0, :], fg[..., 0, :], ig[..., 1, :], fg[..., 1, :]], axis=-1)
    g_cols = jnp.transpose(g4, (3, 0, 1, 2, 4)).reshape(H_A, batch * nc, L, 4)
    g_rows = jnp.transpose(g4, (3, 0, 1, 4, 2)).reshape(H_A, batch * nc, 4, L)
    n0t = jnp.transpose(n0, (0, 2, 1, 3))
    m0t = jnp.broadcast_to(jnp.transpose(m0, (0, 2, 1))[..., None], (batch, H_A, 2, LANES))
    qkv_spec = lambda tile: pl.BlockSpec((seq, COL_TILE), lambda b: (b, tile))
    c_spec = pl.BlockSpec((1, 2, H_A, DH_A, DH_A), lambda b: (b, 0, 0, 0, 0))
    c0_spec = pl.BlockSpec((1, None, 2, H_A, DH_A, DH_A), lambda b: (b, l, 0, 0, 0, 0))
    state_spec = pl.BlockSpec((1, H_A, 2, LANES), lambda b: (b, 0, 0, 0))
    h, Cs, ns, ms = pl.pallas_call(
        functools.partial(_mlstm_kernel, nc=nc),
        out_shape=(
            jax.ShapeDtypeStruct((batch * seq, H_A * DH_A), F32),
            jax.ShapeDtypeStruct((batch, 2, H_A, DH_A, DH_A), F32),
            jax.ShapeDtypeStruct((batch, H_A, 2, DH_A), F32),
            jax.ShapeDtypeStruct((batch, H_A, 2, LANES), F32),
        ),
        grid=(batch,),
        in_specs=[
            qkv_spec(T_AQ), qkv_spec(T_AK), qkv_spec(T_AV),
            pl.BlockSpec((H_A, nc, 4, L), lambda b: (0, b, 0, 0)),
            pl.BlockSpec((H_A, nc, L, 4), lambda b: (0, b, 0, 0)),
            c0_spec, state_spec, state_spec,
        ],
        out_specs=(pl.BlockSpec((seq, H_A * DH_A), lambda b: (b, 0)), c_spec, state_spec, state_spec),
        scratch_shapes=[pltpu.VMEM((seq, H_A * DH_A), F32)],
        compiler_params=_cparams("parallel"),
        name="mlstm",
    )(qkv, qkv, qkv, g_rows, g_cols, C0, n0t, m0t)
    return h, Cs, jnp.transpose(ns, (0, 2, 1, 3)), jnp.transpose(ms[..., 0], (0, 2, 1))


ATTN_SCALE = DH_B ** -0.5


def _half_mask(shape, half):
    lane = lax.broadcasted_iota(jnp.int32, shape, 1)
    return (lane >= HALF) if half else (lane < HALF)


def _select_half(q, half):
    return jnp.where(_half_mask(q.shape, half), q, jnp.zeros_like(q))


def _swap_halves(x):
    return pltpu.roll(x.astype(F32), HALF, 1).astype(x.dtype)


def _with_ones(v):
    return jnp.concatenate([v, jnp.ones_like(v)], axis=1)


def _softmax_pv(scores, vs, extra=None):
    m = functools.reduce(jnp.maximum, [jnp.max(s, axis=1, keepdims=True) for s in scores])
    if extra is not None:
        m = jnp.maximum(m, extra)
    acc = functools.reduce(lambda x, y: x + y,
                           [_dot(jnp.exp((s - m).astype(BF16)), v) for s, v in zip(scores, vs)])
    denom = acc[:, LANES:]
    if extra is not None:
        denom = denom + jnp.exp(extra - m)
    return acc[:, :LANES] / denom


def _diff_head(q, ks, vs, lam):
    scores = [[_dot_nt(_select_half(q, comp), k) for k in ks] for comp in range(2)]
    o1, o2 = [_softmax_pv(s, vs) for s in scores]
    return o1 - lam * o2


def _ctx_attn_kernel(lam_ref, sink_ref, bq, bk, bv, cq, misc, dq, dk, dv, yb_ref, yc_ref, yd_ref):
    pairs = [slice(p * LANES, (p + 1) * LANES) for p in range(H_B // 2)]
    scores = [[[_dot_nt(_select_half(bq[:, sl] * ATTN_SCALE, half), bk[:, sl])] for half in range(2)] for sl in pairs]
    for sl, pair_scores in zip(pairs, scores):
        v = _with_ones(bv[:, sl])
        outs = [_softmax_pv(s, [v]) for s in pair_scores]
        yb_ref[:, sl] = jnp.where(_half_mask(outs[0].shape, 0), outs[0], outs[1]).astype(yb_ref.dtype)

    kc = misc[:, 0:LANES]
    vc = _with_ones(misc[:, LANES:2 * LANES])
    kv_of = lambda head: head // (H_C // KV_C)
    scores = []
    for p, sl in enumerate(pairs):
        pair_scores = []
        for half in range(2):
            qm = _select_half(cq[:, sl] * ATTN_SCALE, half)
            if kv_of(2 * p + half) != half:
                qm = _swap_halves(qm)
            pair_scores.append([_dot_nt(qm, kc)])
        scores.append(pair_scores)
    for p, (sl, pair_scores) in enumerate(zip(pairs, scores)):
        outs = []
        for half, s in enumerate(pair_scores):
            o = _softmax_pv(s, [vc], extra=sink_ref[2 * p + half])
            outs.append(o if kv_of(2 * p + half) == half else _swap_halves(o))
        yc_ref[:, sl] = jnp.where(_half_mask(outs[0].shape, 0), outs[0], outs[1]).astype(yc_ref.dtype)

    lam = lam_ref[0]
    for h in range(H_D):
        sl = slice(h * LANES, (h + 1) * LANES)
        yd_ref[:, sl] = _diff_head(dq[:, sl] * ATTN_SCALE, [dk[:, sl]], [_with_ones(dv[:, sl])], lam)


def _ctx_attention(qkv, lam, sink, batch, seq):
    tile = lambda t: pl.BlockSpec((seq, COL_TILE), lambda b: (b, t))
    smem = pl.BlockSpec(memory_space=pltpu.SMEM)
    out = lambda dt: jax.ShapeDtypeStruct((batch * seq, BRANCH_W), dt)
    out_spec = pl.BlockSpec((seq, BRANCH_W), lambda b: (b, 0))
    return pl.pallas_call(
        _ctx_attn_kernel,
        out_shape=(out(BF16), out(BF16), out(F32)),
        grid=(batch,),
        in_specs=[smem, smem, tile(T_BQ), tile(T_BK), tile(T_BV), tile(T_CQ), tile(T_MISC),
                  tile(T_DQ), tile(T_DK), tile(T_DV)],
        out_specs=(out_spec, out_spec, out_spec),
        compiler_params=_cparams("parallel"),
        name="ctx_attention",
    )(lam, sink, *([qkv] * 8))


NAT_QROWS = 4
NAT_KROWS = 12
NAT_RI = 2 * NA_ROWS - 1
NAT_BLOCKS_PER_STEP = 4


def _nat_kernel(q_ref, k_ref, v_ref, kc_ref, vc_ref, cb_ref, o_ref):
    rows = k_ref.shape[0] // GRID_W
    nq = NAT_QROWS * GRID_W
    nk = NAT_KROWS * GRID_W
    kc = kc_ref[0].astype(BF16)
    vc = _with_ones(vc_ref[0].astype(BF16))
    left = _half_mask((GRID_W, LANES), 0)
    scores, values = [], []
    for sub in range(NAT_BLOCKS_PER_STEP):
        r0 = (pl.program_id(2) * NAT_BLOCKS_PER_STEP + sub) * NAT_QROWS
        kb = jnp.clip(r0 - NA_ROWS // 2, 0, rows - NAT_KROWS)
        koff = pl.multiple_of(kb * GRID_W, GRID_W)
        q = q_ref[sub * nq:(sub + 1) * nq, :] * ATTN_SCALE
        k = k_ref[pl.ds(koff, nk), :]
        v = _with_ones(v_ref[pl.ds(koff, nk), :])
        for half in range(2):
            bias_rows = []
            for a in range(NAT_QROWS):
                r = r0 + a
                rs = jnp.clip(r - NA_ROWS // 2, 0, rows - NA_ROWS)
                blocks = []
                for ip in range(NAT_KROWS // 2):
                    sides = []
                    for side in range(2):
                        kr = kb + 2 * ip + side
                        ok = (kr >= rs) & (kr < rs + NA_ROWS)
                        ri = jnp.clip(kr - r + NA_ROWS - 1, 0, NAT_RI - 1)
                        sides.append(jnp.where(ok, cb_ref[half, ri], NEG_INF))
                    blocks.append(jnp.where(left, sides[0], sides[1]))
                bias_rows.append(jnp.concatenate(blocks, axis=1))
            bias = jnp.concatenate(bias_rows, axis=0)
            qm = _select_half(q, half)
            scores.append([_dot_nt(qm, k) + bias, _dot_nt(qm, kc)])
            values.append([v, vc])
    outs = [_softmax_pv(s, vs) for s, vs in zip(scores, values)]
    for sub in range(NAT_BLOCKS_PER_STEP):
        o_ref[sub * nq:(sub + 1) * nq, :] = jnp.where(_half_mask((nq, LANES), 0), outs[2 * sub],
                                                      outs[2 * sub + 1]).astype(o_ref.dtype)


def _nat_col_table(rpb):
    qc = np.arange(GRID_W)[:, None]
    kc = np.arange(GRID_W)[None, :]
    cs = np.clip(qc - NA_COLS // 2, 0, GRID_W - NA_COLS)
    valid = (kc >= cs) & (kc < cs + NA_COLS)
    n_ci = 2 * NA_COLS - 1
    lo = GRID_W - 1 - (NA_COLS - 1)
    v = jnp.pad(rpb, ((0, 0), (0, 0), (lo, 2 * GRID_W - n_ci - lo)))
    flat = jnp.tile(v, (1, 1, GRID_W))[..., :GRID_W * (2 * GRID_W - 1)]
    toep = flat.reshape(rpb.shape[0], rpb.shape[1], GRID_W, 2 * GRID_W - 1)[..., GRID_W - 1:]
    tab = jnp.where(valid[None, None], toep, NEG_INF)
    return jnp.concatenate([tab, tab], axis=-1)


def _nat_latent(qkv, kc, vc, cb, l, batch, seq):
    nq = NAT_BLOCKS_PER_STEP * NAT_QROWS * GRID_W
    tpb = seq // nq
    past = kc.shape[2]
    return pl.pallas_call(
        _nat_kernel,
        out_shape=jax.ShapeDtypeStruct((batch * seq, BRANCH_W), BF16),
        grid=(batch, H_B // 2, tpb),
        in_specs=[
            pl.BlockSpec((nq, LANES), lambda b, p, i: (b * tpb + i, T_BQ * 4 + p)),
            pl.BlockSpec((seq, LANES), lambda b, p, i: (b, T_BK * 4 + p)),
            pl.BlockSpec((seq, LANES), lambda b, p, i: (b, T_BV * 4 + p)),
            pl.BlockSpec((1, None, past, LANES), lambda b, p, i: (b, l, 0, p)),
            pl.BlockSpec((1, None, past, LANES), lambda b, p, i: (b, l, 0, p)),
            pl.BlockSpec((2, NAT_RI, GRID_W, LANES), lambda b, p, i: (l * (H_B // 2) + p, 0, 0, 0)),
        ],
        out_specs=pl.BlockSpec((nq, LANES), lambda b, p, i: (b * tpb + i, p)),
        compiler_params=_cparams("parallel", "parallel", "parallel"),
        name="nat_latent",
    )(qkv, qkv, qkv, kc, vc, cb)


SWA_QBLK = 256
SWA_KBLK = SWA_QBLK + 2 * SWA_WINDOW
SWA_BLOCKS_PER_STEP = 4


def _swa_kernel(sink_ref, q_ref, kv_ref, kc_ref, vc_ref, o_ref):
    seq = kv_ref.shape[0]
    p = pl.program_id(1)
    kc = kc_ref[0].astype(BF16)
    vc = _with_ones(vc_ref[0].astype(BF16))
    kv_half = p // ((H_C // 2) // KV_C)
    scores, values = [], []
    for sub in range(SWA_BLOCKS_PER_STEP):
        t0 = (pl.program_id(2) * SWA_BLOCKS_PER_STEP + sub) * SWA_QBLK
        k0 = pl.multiple_of(jnp.clip(t0 - SWA_WINDOW, 0, seq - SWA_KBLK), SWA_WINDOW)
        q = q_ref[sub * SWA_QBLK:(sub + 1) * SWA_QBLK, :] * ATTN_SCALE
        k = kv_ref[pl.ds(k0, SWA_KBLK), 0:LANES]
        v = _with_ones(kv_ref[pl.ds(k0, SWA_KBLK), LANES:2 * LANES])
        tq = t0 + lax.broadcasted_iota(jnp.int32, (SWA_QBLK, SWA_KBLK), 0)
        tk = k0 + lax.broadcasted_iota(jnp.int32, (SWA_QBLK, SWA_KBLK), 1)
        inside = jnp.abs(tq - tk) <= SWA_WINDOW
        for half in range(2):
            qm = _select_half(q, half)
            qm = jnp.where(kv_half == half, qm, _swap_halves(qm))
            scores.append([jnp.where(inside, _dot_nt(qm, k), NEG_INF), _dot_nt(qm, kc)])
            values.append([v, vc])
    for sub in range(SWA_BLOCKS_PER_STEP):
        outs = []
        for half in range(2):
            o = _softmax_pv(scores[2 * sub + half], values[2 * sub + half], extra=sink_ref[2 * p + half])
            outs.append(jnp.where(kv_half == half, o, _swap_halves(o)))
        o_ref[sub * SWA_QBLK:(sub + 1) * SWA_QBLK, :] = jnp.where(_half_mask(outs[0].shape, 0), outs[0],
                                                                   outs[1]).astype(o_ref.dtype)


def _swa_latent(qkv, kc, vc, sink, l, batch, seq):
    nq = SWA_BLOCKS_PER_STEP * SWA_QBLK
    tpb = seq // nq
    past = kc.shape[2]
    return pl.pallas_call(
        _swa_kernel,
        out_shape=jax.ShapeDtypeStruct((batch * seq, BRANCH_W), BF16),
        grid=(batch, H_C // 2, tpb),
        in_specs=[
            pl.BlockSpec(memory_space=pltpu.SMEM),
            pl.BlockSpec((nq, LANES), lambda b, p, i: (b * tpb + i, T_CQ * 4 + p)),
            pl.BlockSpec((seq, COL_TILE), lambda b, p, i: (b, T_MISC)),
            pl.BlockSpec((1, None, past, LANES), lambda b, p, i: (b, l, 0, 0)),
            pl.BlockSpec((1, None, past, LANES), lambda b, p, i: (b, l, 0, 0)),
        ],
        out_specs=pl.BlockSpec((nq, LANES), lambda b, p, i: (b * tpb + i, p)),
        compiler_params=_cparams("parallel", "parallel", "parallel"),
        name="swa_latent",
    )(sink, qkv, qkv, kc, vc)


DIFF_QBLK = 1024


def _diff_kernel(lam_ref, q_ref, k_ref, v_ref, kc_ref, vc_ref, o_ref):
    q = q_ref[...] * ATTN_SCALE
    ks = [k_ref[...], kc_ref[0].astype(BF16)]
    vs = [_with_ones(v_ref[...]), _with_ones(vc_ref[0].astype(BF16))]
    o_ref[...] = _diff_head(q, ks, vs, lam_ref[0])


def _diff_latent(qkv, kc, vc, lam, l, batch, seq):
    tpb = seq // DIFF_QBLK
    past = kc.shape[2]
    return pl.pallas_call(
        _diff_kernel,
        out_shape=jax.ShapeDtypeStruct((batch * seq, BRANCH_W), F32),
        grid=(batch, H_D, tpb),
        in_specs=[
            pl.BlockSpec(memory_space=pltpu.SMEM),
            pl.BlockSpec((DIFF_QBLK, LANES), lambda b, h, i: (b * tpb + i, T_DQ * 4 + h)),
            pl.BlockSpec((seq, LANES), lambda b, h, i: (b, T_DK * 4 + h)),
            pl.BlockSpec((seq, LANES), lambda b, h, i: (b, T_DV * 4 + h)),
            pl.BlockSpec((1, None, past, LANES), lambda b, h, i: (b, l, 0, h)),
            pl.BlockSpec((1, None, past, LANES), lambda b, h, i: (b, l, 0, h)),
        ],
        out_specs=pl.BlockSpec((DIFF_QBLK, LANES), lambda b, h, i: (b * tpb + i, h)),
        compiler_params=_cparams("parallel", "parallel", "parallel"),
        name="diff_latent",
    )(lam, qkv, qkv, qkv, kc, vc)


def _rms_heads(y, g_ref, per_head_gain):
    outs = []
    for c in range(BRANCH_W // LANES):
        sl = slice(c * LANES, (c + 1) * LANES)
        g = g_ref[:, sl] if per_head_gain else g_ref[...]
        outs.append(_rms(y[:, sl], g))
    return jnp.concatenate(outs, axis=1)


def _first_lane(hit, lane_f):
    return jnp.min(jnp.where(hit, lane_f, float(LANES)), axis=1, keepdims=True)


def _top_group(logits):
    lane = lax.broadcasted_iota(jnp.int32, logits.shape, 1)
    gl = jnp.where(lane < N_GROUPS, logits, NEG_INF)
    return _first_lane(gl == jnp.max(gl, axis=1, keepdims=True), lane.astype(F32))


def _gate_weights(logits, group):
    lane = lax.broadcasted_iota(jnp.int32, logits.shape, 1)
    lane_f = lane.astype(F32)
    gl = jnp.where(lane < N_GROUPS, logits, NEG_INF)
    gmax = jnp.max(gl, axis=1, keepdims=True)
    g_logit = jnp.sum(jnp.where(lane_f == group, logits, 0.0), axis=1, keepdims=True)
    gw = jnp.exp(g_logit - gmax) / jnp.sum(jnp.exp(gl - gmax), axis=1, keepdims=True)
    e_lane = lane - N_GROUPS
    e_group = lax.shift_right_arithmetic(e_lane, EXP_PER_GROUP.bit_length() - 1).astype(F32)
    in_group = (e_lane >= 0) & (e_lane < N_EXPERTS) & (e_group == group)
    el = jnp.where(in_group, logits, NEG_INF)
    e1 = jnp.max(el, axis=1, keepdims=True)
    i1 = _first_lane(el == e1, lane_f)
    el2 = jnp.where(lane_f == i1, NEG_INF, el)
    e2 = jnp.max(el2, axis=1, keepdims=True)
    i2 = _first_lane(el2 == e2, lane_f)
    t = jnp.exp(e2 - e1)
    w1 = gw / (1.0 + t)
    return jnp.where(lane_f == i1, w1, jnp.where(lane_f == i2, w1 * t, 0.0))


def _merge_kernel(ya, yb, yc, yd, ao, gt0, gt1, gt2, gt3, x_ref, mod_ref, anorm, subln, nffn,
                  wbr, wout, wr, xo_ref, h2_ref, group_ref, *, yd_scale):
    a = _sigmoid(ao[...].astype(F32)) * _rms_heads(ya[...], anorm, True)
    d = _rms_heads(yd[...], subln, False) * yd_scale
    branches = (a, yb[...], yc[...], d)
    mix = None
    for n, (br, gt) in enumerate(zip(branches, (gt0, gt1, gt2, gt3))):
        term = _sigmoid(gt[...].astype(F32)) * _dot(br.astype(BF16), wbr[n])
        mix = term if mix is None else mix + term
    out = _dot(mix.astype(BF16), wout[...])
    xn = x_ref[...] + mod_ref[0, 2:3, :] * out
    xo_ref[...] = xn
    h2 = _rms(xn, nffn[...]) * (1.0 + mod_ref[0, 4:5, :]) + mod_ref[0, 3:4, :]
    h2_ref[...] = h2
    group_ref[...] = jnp.broadcast_to(_top_group(_dot(h2.astype(BF16), wr[...])), group_ref.shape)


def _merge(ys, qkv, x, mod3, a_norm, subln, norm_ffn, w_branch, w_out, w_route, lam_init, l):
    n = x.shape[0]
    tm = 512
    tiles_per_mod = n // mod3.shape[0] // tm
    y_spec = pl.BlockSpec((tm, BRANCH_W), lambda i: (i, 0))
    gt_spec = lambda k: pl.BlockSpec((tm, D_MODEL), lambda i: (i, T_GT * COL_TILE // D_MODEL + k))
    full = lambda shape: pl.BlockSpec(shape, lambda i: (0,) * len(shape))
    layer = lambda shape: pl.BlockSpec((None,) + shape, lambda i: (l,) + (0,) * len(shape))
    x_spec = pl.BlockSpec((tm, D_MODEL), lambda i: (i, 0))
    return pl.pallas_call(
        functools.partial(_merge_kernel, yd_scale=1.0 - lam_init),
        out_shape=(
            jax.ShapeDtypeStruct((n, D_MODEL), F32),
            jax.ShapeDtypeStruct((n, D_MODEL), F32),
            jax.ShapeDtypeStruct((n, LANES), F32),
        ),
        grid=(n // tm,),
        in_specs=[
            y_spec, y_spec, y_spec, y_spec,
            pl.BlockSpec((tm, COL_TILE), lambda i: (i, T_AO)),
            gt_spec(0), gt_spec(1), gt_spec(2), gt_spec(3),
            x_spec,
            pl.BlockSpec((1, 6, D_MODEL), lambda i: (i // tiles_per_mod, 0, 0)),
            full((1, BRANCH_W)), full((1, LANES)), full((1, D_MODEL)),
            layer((N_BRANCH, BRANCH_W, D_MODEL)), layer((D_MODEL, D_MODEL)), layer((D_MODEL, LANES)),
        ],
        out_specs=(x_spec, x_spec, pl.BlockSpec((tm, LANES), lambda i: (i, 0))),
        compiler_params=_cparams("parallel"),
        name="merge_route",
    )(*ys, qkv, qkv, qkv, qkv, qkv, x, mod3, a_norm.reshape(1, BRANCH_W), subln.reshape(1, LANES),
      norm_ffn.reshape(1, D_MODEL), w_branch, w_out, w_route)


MOE_TILE = 512
ROW_DMA_UNROLL = 16


def _row_copies(src_row, dst_row, sem, n):
    def copy(i):
        return pltpu.make_async_copy(src_row(i), dst_row(i), sem)

    def start(i, c):
        copy(i).start()
        return c

    def wait(i, c):
        copy(i).wait()
        return c

    lax.fori_loop(0, n, start, 0, unroll=ROW_DMA_UNROLL)
    lax.fori_loop(0, n, wait, 0, unroll=ROW_DMA_UNROLL)


def _moe_scatter_kernel(slot_ref, h_ref, init_ref, sorted_ref, sem):
    del init_ref
    base = pl.program_id(0) * MOE_TILE
    _row_copies(lambda i: h_ref.at[pl.ds(i, 1), :],
                lambda i: sorted_ref.at[pl.ds(slot_ref[base + i], 1), :], sem, MOE_TILE)


def _moe_group_kernel(tile_group_ref, n_used_ref, h_ref, wr_ref, wg_f32, wu_f32, wd_f32, o_ref,
                      wg_ref, wu_ref, wd_ref):
    t = pl.program_id(0)
    group = tile_group_ref[t]

    @pl.when((t == 0) | (group != tile_group_ref[jnp.maximum(t - 1, 0)]))
    def _():
        for e in range(EXP_PER_GROUP):
            wg_ref[e] = wg_f32[0, e].astype(BF16)
            wu_ref[e] = wu_f32[0, e].astype(BF16)
            wd_ref[e] = wd_f32[0, e].astype(BF16)

    @pl.when(t < n_used_ref[0])
    def _():
        h = h_ref[...].astype(BF16)
        gates = _gate_weights(_dot(h, wr_ref[...]), group.astype(F32))
        lane = lax.broadcasted_iota(jnp.int32, gates.shape, 1)
        acc = None
        for e in range(EXP_PER_GROUP):
            g = jnp.sum(jnp.where(lane == N_GROUPS + group * EXP_PER_GROUP + e, gates, 0.0), axis=1, keepdims=True)
            hg = _dot(h, wg_ref[e])
            hid = hg * _sigmoid(hg) * _dot(h, wu_ref[e]) * g
            term = _dot(hid.astype(BF16), wd_ref[e])
            acc = term if acc is None else acc + term
        o_ref[...] = acc

    @pl.when(t >= n_used_ref[0])
    def _():
        o_ref[...] = jnp.zeros_like(o_ref)


def _moe_gather_kernel(slot_ref, y_ref, x_ref, mod_ref, nf_ref, o_ref, buf, sem, *, final_norm):
    t = pl.program_id(0)
    nt = pl.num_programs(0)

    def copy(tile, b, i):
        return pltpu.make_async_copy(y_ref.at[pl.ds(slot_ref[tile * MOE_TILE + i], 1), :],
                                     buf.at[b, pl.ds(i, 1), :], sem.at[b])

    def start_tile(tile, b):
        def start(i, c):
            copy(tile, b, i).start()
            return c

        lax.fori_loop(0, MOE_TILE, start, 0, unroll=ROW_DMA_UNROLL)

    cur = t % 2

    @pl.when(t == 0)
    def _():
        start_tile(0, 0)

    @pl.when(t + 1 < nt)
    def _():
        start_tile(t + 1, 1 - cur)

    def wait(i, c):
        copy(t, cur, i).wait()
        return c

    lax.fori_loop(0, MOE_TILE, wait, 0, unroll=ROW_DMA_UNROLL)
    y = x_ref[...] + mod_ref[0, 5:6, :] * buf[cur]
    o_ref[...] = _rms(y, nf_ref[...]) if final_norm else y


def _moe(h2, group, x, mod3, w_route, wg, wu, wd, norm_final, final_norm, l, init=None):
    n = x.shape[0]
    tm = MOE_TILE
    n_tiles = n // tm + N_GROUPS
    onehot = (group[:, None] == jnp.arange(N_GROUPS, dtype=jnp.int32)[None]).astype(jnp.int32)
    csum = jnp.cumsum(onehot, axis=0)
    counts = csum[-1]
    rank = jnp.sum((csum - onehot) * onehot, axis=1)
    group_tiles = (counts + tm - 1) // tm
    tile_end = jnp.cumsum(group_tiles)
    group_start = (tile_end - group_tiles) * tm
    slot = (jnp.sum(onehot * group_start[None], axis=1) + rank).astype(jnp.int32)
    tile_group = jnp.minimum(jnp.sum(jnp.arange(n_tiles, dtype=jnp.int32)[:, None] >= tile_end[None], axis=1),
                             N_GROUPS - 1).astype(jnp.int32)
    n_used = tile_end[-1:].astype(jnp.int32)

    sorted_h = pl.pallas_call(
        _moe_scatter_kernel,
        out_shape=jax.ShapeDtypeStruct((n_tiles * tm, D_MODEL), F32),
        grid_spec=pltpu.PrefetchScalarGridSpec(
            num_scalar_prefetch=1, grid=(n // tm,),
            in_specs=[pl.BlockSpec((tm, D_MODEL), lambda i, slot: (i, 0)), pl.BlockSpec(memory_space=pl.ANY)],
            out_specs=pl.BlockSpec(memory_space=pl.ANY),
            scratch_shapes=[pltpu.SemaphoreType.DMA],
        ),
        input_output_aliases={2: 0},
        compiler_params=_cparams("arbitrary"),
        name="moe_scatter",
    )(slot, h2, jnp.zeros((n_tiles * tm, D_MODEL), F32) if init is None else init)

    grp = lambda shape: pl.BlockSpec((None, 1) + shape, lambda t, tg, nu: (l, tg[t], 0, 0, 0))
    sorted_y = pl.pallas_call(
        _moe_group_kernel,
        out_shape=jax.ShapeDtypeStruct((n_tiles * tm, D_MODEL), F32),
        grid_spec=pltpu.PrefetchScalarGridSpec(
            num_scalar_prefetch=2, grid=(n_tiles,),
            in_specs=[
                pl.BlockSpec((tm, D_MODEL), lambda t, tg, nu: (t, 0)),
                pl.BlockSpec((None, D_MODEL, LANES), lambda t, tg, nu: (l, 0, 0)),
                grp((EXP_PER_GROUP, D_MODEL, D_EXPERT)), grp((EXP_PER_GROUP, D_MODEL, D_EXPERT)),
                grp((EXP_PER_GROUP, D_EXPERT, D_MODEL)),
            ],
            out_specs=pl.BlockSpec((tm, D_MODEL), lambda t, tg, nu: (t, 0)),
            scratch_shapes=[pltpu.VMEM((EXP_PER_GROUP, D_MODEL, D_EXPERT), BF16),
                            pltpu.VMEM((EXP_PER_GROUP, D_MODEL, D_EXPERT), BF16),
                            pltpu.VMEM((EXP_PER_GROUP, D_EXPERT, D_MODEL), BF16)],
        ),
        compiler_params=_cparams("arbitrary"),
        name="moe_experts",
    )(tile_group, n_used, sorted_h, w_route, wg, wu, wd)

    tiles_per_mod = n // mod3.shape[0] // tm
    row = pl.BlockSpec((tm, D_MODEL), lambda i, slot: (i, 0))
    y = pl.pallas_call(
        functools.partial(_moe_gather_kernel, final_norm=final_norm),
        out_shape=jax.ShapeDtypeStruct((n, D_MODEL), F32),
        grid_spec=pltpu.PrefetchScalarGridSpec(
            num_scalar_prefetch=1, grid=(n // tm,),
            in_specs=[
                pl.BlockSpec(memory_space=pl.ANY), row,
                pl.BlockSpec((1, 6, D_MODEL), lambda i, slot: (i // tiles_per_mod, 0, 0)),
                pl.BlockSpec((1, D_MODEL), lambda i, slot: (0, 0)),
            ],
            out_specs=row,
            scratch_shapes=[pltpu.VMEM((2, tm, D_MODEL), F32), pltpu.SemaphoreType.DMA((2,))],
        ),
        compiler_params=_cparams("arbitrary"),
        name="moe_gather",
    )(slot, sorted_y, x, mod3, norm_final.reshape(1, D_MODEL))
    return y, sorted_h


W_IN_GATES = 2048
W_IN_N_GATES = 16
W_IN_TILE_SRC = {T_AQ: 0, T_AK: 512, T_AV: 1024, T_AO: 1536, T_BQ: 2064, T_BK: 2576, T_BV: 3088, T_CQ: 3600,
                 T_DQ: 4368, T_DK: 4880, T_DV: 5392, T_MISC: 4112, **{T_GT + k: 5904 + COL_TILE * k for k in range(8)}}


def _reorder_kernel(src_ref, wt_hbm, gates_ref, o_ref, buf, sem):
    step = pl.program_id(0) * N_COL_TILES + pl.program_id(1)
    n_steps = pl.num_programs(0) * N_COL_TILES

    def copy(s):
        start = pl.multiple_of(src_ref[s % N_COL_TILES], 8)
        return pltpu.make_async_copy(wt_hbm.at[s // N_COL_TILES, pl.ds(start, COL_TILE), :], buf.at[s % 2],
                                     sem.at[s % 2])

    @pl.when(step == 0)
    def _():
        copy(step).start()

    @pl.when(step + 1 < n_steps)
    def _():
        copy(step + 1).start()

    copy(step).wait()
    rows = buf[step % 2]

    @pl.when(pl.program_id(1) == T_MISC)
    def _():
        pad = jnp.zeros((COL_TILE - 2 * LANES - W_IN_N_GATES, D_MODEL), F32)
        tile = jnp.concatenate([rows[:2 * LANES], gates_ref[...], pad], axis=0)
        o_ref[...] = tile.T.astype(BF16)

    @pl.when(pl.program_id(1) != T_MISC)
    def _():
        o_ref[...] = rows.T.astype(BF16)


def _reorder_w_in(w_in):
    src = np.array([W_IN_TILE_SRC[t] for t in range(N_COL_TILES)], np.int32)
    wt = jnp.swapaxes(w_in, 1, 2)
    return pl.pallas_call(
        _reorder_kernel,
        out_shape=jax.ShapeDtypeStruct((DEPTH, D_MODEL, N_PROJ), BF16),
        grid_spec=pltpu.PrefetchScalarGridSpec(
            num_scalar_prefetch=1, grid=(DEPTH, N_COL_TILES),
            in_specs=[pl.BlockSpec(memory_space=pl.ANY),
                      pl.BlockSpec((None, W_IN_N_GATES, D_MODEL), lambda l, j, src: (l, W_IN_GATES // W_IN_N_GATES, 0))],
            out_specs=pl.BlockSpec((None, D_MODEL, COL_TILE), lambda l, j, src: (l, 0, j)),
            scratch_shapes=[pltpu.VMEM((2, COL_TILE, D_MODEL), F32), pltpu.SemaphoreType.DMA((2,))],
        ),
        compiler_params=_cparams("arbitrary", "arbitrary"),
        name="reorder_w_in",
    )(jnp.asarray(src), wt, wt)


def _rope_tables(seq):
    nf = DH_C // 4
    inv = ROPE_BASE ** (-jnp.arange(nf, dtype=F32) / nf)
    t = jnp.arange(seq)
    ang_r = (t // GRID_W).astype(F32)[:, None] * inv[None, :]
    ang_c = (t % GRID_W).astype(F32)[:, None] * inv[None, :]
    cos = jnp.concatenate([jnp.cos(ang_r)] * 2 + [jnp.cos(ang_c)] * 2, axis=1)
    sin = jnp.concatenate([-jnp.sin(ang_r), jnp.sin(ang_r), -jnp.sin(ang_c), jnp.sin(ang_c)], axis=1)
    return jnp.tile(cos, (1, 2)), jnp.tile(sin, (1, 2))


def _layer(l, x, mod3, seq, lat, p, prev_cache=None, moe_init=None):
    batch = x.shape[0] // seq
    lam_init = 0.8 - 0.6 * math.exp(-0.3 * l)
    ctx = lat is None
    proj = _qkv_proj(x, p["norm_mix"][l], mod3, p["w_in"], l, None if ctx else p["rope"],
                     dict(layer=l, seq=seq, prev=prev_cache) if ctx else None, tm=1024 if ctx else 2048)
    qkv, gates_raw = proj[:2]
    gates = gates_raw[:, :16] + jnp.concatenate([p["i_bias"][l].reshape(-1), p["f_bias"][l].reshape(-1)])[None]
    lam = (jnp.exp(jnp.sum(p["lq1"][l] * p["lk1"][l])) - jnp.exp(jnp.sum(p["lq2"][l] * p["lk2"][l]))
           + lam_init).reshape(1).astype(F32)
    sink = p["sink"][l]
    new = None
    if ctx:
        C0 = jnp.zeros((batch, 1, 2, H_A, DH_A, DH_A), F32)
        n0 = jnp.zeros((batch, 2, H_A, DH_A), F32)
        m0 = jnp.zeros((batch, 2, H_A), F32)
        ya, Cs, ns, ms = _mlstm(qkv, gates, C0, 0, n0, m0, batch, seq)
        yb, yc, yd = _ctx_attention(qkv, lam, sink, batch, seq)
        new = (proj[2:], (Cs, ns, ms))
    else:
        nat_k, nat_v, swa_k, swa_v, diff_k, diff_v, C0, n0, m0 = lat
        flat = lambda a: a.reshape(a.shape[:3] + (-1,))
        ya, _, _, _ = _mlstm(qkv, gates, C0, l, n0[:, l], m0[:, l], batch, seq)
        yb = _nat_latent(qkv, flat(nat_k), flat(nat_v), p["nat_table"], l, batch, seq)
        yc = _swa_latent(qkv, flat(swa_k), flat(swa_v), sink, l, batch, seq)
        yd = _diff_latent(qkv, flat(diff_k), flat(diff_v), lam, l, batch, seq)
    xn, h2, group = _merge((ya, yb, yc, yd), qkv, x, mod3, p["a_norm"][l], p["subln"][l],
                           p["norm_ffn"][l], p["w_branch"], p["w_out"], p["w_route"], lam_init, l)
    y, moe_buf = _moe(h2, group[:, 0].astype(jnp.int32), xn, mod3, p["w_route"], p["wg"], p["wu"], p["wd"],
                      p["norm_final"], final_norm=(l == DEPTH - 1), l=l, init=moe_init)
    return y, new, moe_buf


def kernel(x_prompt, x_sample, cache_nat_k, cache_nat_v, cache_swa_k, cache_swa_v, cache_diff_k, cache_diff_v, state_mlstm_C, state_mlstm_n, state_mlstm_m, c, c_ctx, norm_mix, norm_ffn, norm_final, w_mod, b_mod, w_in, mlstm_i_bias, mlstm_f_bias, mlstm_norm, nat_rpb, swa_sink, diff_lq1, diff_lk1, diff_lq2, diff_lk2, diff_subln, w_branch, w_out, router_group, router_expert, w_exp_gate, w_exp_up, w_exp_down):
    b_ctx, s_ctx, _ = x_prompt.shape
    b_lat, s_lat, _ = x_sample.shape
    cond8 = jnp.concatenate([c_ctx[None], c, jnp.zeros((8 - 1 - b_lat, D_MODEL), F32)], axis=0)
    mod = _modulation(cond8, w_mod, b_mod).reshape(DEPTH, 8, 6, D_MODEL)
    route_pad = jnp.zeros((DEPTH, D_MODEL, LANES - N_GROUPS - N_EXPERTS), F32)
    p = dict(
        norm_mix=norm_mix, norm_ffn=norm_ffn, norm_final=norm_final,
        w_in=_reorder_w_in(w_in),
        i_bias=mlstm_i_bias, f_bias=mlstm_f_bias, a_norm=mlstm_norm, sink=swa_sink,
        nat_table=_nat_col_table(nat_rpb.reshape((DEPTH * H_B,) + nat_rpb.shape[2:])),
        lq1=diff_lq1, lk1=diff_lk1, lq2=diff_lq2, lk2=diff_lk2, subln=diff_subln,
        w_branch=w_branch.astype(BF16), w_out=w_out.astype(BF16),
        w_route=jnp.concatenate([router_group, router_expert, route_pad], axis=-1).astype(BF16),
        wg=w_exp_gate.reshape(DEPTH, N_GROUPS, EXP_PER_GROUP, D_MODEL, D_EXPERT),
        wu=w_exp_up.reshape(DEPTH, N_GROUPS, EXP_PER_GROUP, D_MODEL, D_EXPERT),
        wd=w_exp_down.reshape(DEPTH, N_GROUPS, EXP_PER_GROUP, D_EXPERT, D_MODEL),
        rope=_rope_tables(s_lat),
    )
    yp = x_prompt.reshape(b_ctx * s_ctx, D_MODEL)
    ys = x_sample.reshape(b_lat * s_lat, D_MODEL)
    states = []
    cache = buf_ctx = buf_lat = None
    for l in range(DEPTH):
        yp, (cache, state), buf_ctx = _layer(l, yp, mod[l, 0:1], s_ctx, None, p, cache, buf_ctx)
        states.append(state)
        lat = (cache_nat_k, cache_nat_v, cache_swa_k, cache_swa_v, cache_diff_k, cache_diff_v,
               state_mlstm_C, state_mlstm_n, state_mlstm_m)
        ys, _, buf_lat = _layer(l, ys, mod[l, 1:1 + b_lat], s_lat, lat, p, moe_init=buf_lat)
    def stack_layers(parts):
        flat = [a.reshape(a.shape[0], 1, -1) for a in parts]
        return jnp.concatenate(flat, axis=1).reshape((parts[0].shape[0], DEPTH) + parts[0].shape[1:])

    bk, bv, dk, dv, ck, cv = cache
    heads = lambda a, h: a.reshape(a.shape[:3] + (h, a.shape[3] // h))
    caches = (heads(bk, H_B), heads(bv, H_B), heads(ck, KV_C), heads(cv, KV_C), heads(dk, H_D), heads(dv, H_D))
    stacked = tuple(stack_layers([states[l][k] for l in range(DEPTH)]) for k in range(3))
    return (yp.reshape(b_ctx, s_ctx, D_MODEL), ys.reshape(b_lat, s_lat, D_MODEL)) + caches + stacked
```

```python
import functools
import math

import numpy as np
import jax
import jax.numpy as jnp
from jax import lax
from jax.experimental import pallas as pl
from jax.experimental.pallas import tpu as pltpu

F32 = jnp.float32
BF16 = jnp.bfloat16

D_MODEL = 1024
DEPTH = 2
GRID_W = 64
BRANCH_W = 512
N_BRANCH = 4
H_A, DH_A, MLSTM_CHUNK = 4, 128, 128
H_B, DH_B = 8, 64
NA_ROWS, NA_COLS = 8, 16
H_C, KV_C, DH_C = 8, 2, 64
SWA_WINDOW = 128
H_D, DH_D = 4, 64
N_GROUPS, EXP_PER_GROUP = 4, 4
N_EXPERTS = N_GROUPS * EXP_PER_GROUP
D_EXPERT = 256
ROPE_BASE = 10000.0
EPS = 1e-6

LANES = 128
HALF = 64
COL_TILE = 512
N_COL_TILES = 20
N_PROJ = COL_TILE * N_COL_TILES
T_AQ, T_AK, T_AV, T_AO, T_BQ, T_BK, T_BV, T_CQ, T_DQ, T_DK, T_DV, T_MISC, T_GT = range(13)
MISC_GATE_OFF = 256
VMEM_LIMIT = 56 * 1024 * 1024
NEG_INF = float("-inf")

_NT = (((1,), (1,)), ((), ()))
_TN = (((0,), (0,)), ((), ()))


def _cparams(*sem):
    return pltpu.CompilerParams(dimension_semantics=sem, vmem_limit_bytes=VMEM_LIMIT)


def _dot(a, b):
    return jnp.dot(a, b, preferred_element_type=F32)


def _dot_nt(a, b):
    return lax.dot_general(a, b, _NT, preferred_element_type=F32)


def _dot_tn(a, b):
    return lax.dot_general(a, b, _TN, preferred_element_type=F32)


def _rms(x, g):
    return x * lax.rsqrt(jnp.mean(x * x, axis=-1, keepdims=True) + EPS) * g


def _sigmoid(x):
    return 0.5 * jnp.tanh(0.5 * x) + 0.5


def _mod_kernel(c_ref, w_ref, b_ref, o_ref):
    c = c_ref[...]
    s = (c * _sigmoid(c)).astype(BF16)
    o_ref[0] = _dot(s, w_ref[0].astype(BF16)) + b_ref[0]


def _modulation(cond8, w_mod, b_mod):
    tn = 1536
    n_out = 6 * D_MODEL
    return pl.pallas_call(
        _mod_kernel,
        out_shape=jax.ShapeDtypeStruct((DEPTH, 8, n_out), F32),
        grid=(DEPTH, n_out // tn),
        in_specs=[
            pl.BlockSpec((8, D_MODEL), lambda l, j: (0, 0)),
            pl.BlockSpec((1, D_MODEL, tn), lambda l, j: (l, 0, j)),
            pl.BlockSpec((1, 1, tn), lambda l, j: (l, 0, j)),
        ],
        out_specs=pl.BlockSpec((1, 8, tn), lambda l, j: (l, 0, j)),
        compiler_params=_cparams("parallel", "parallel"),
        name="modulation",
    )(cond8, w_mod, b_mod.reshape(DEPTH, 1, n_out))


CACHE_TILES = (T_BK, T_BV, T_DK, T_DV)
QKV_ROW_CHUNK = 512


def _rope128(x, cos, sin):
    lane = lax.broadcasted_iota(jnp.int32, x.shape, 1)
    partner = jnp.where((lane % 32) < 16, pltpu.roll(x, LANES - 16, 1), pltpu.roll(x, 16, 1))
    return x * cos + partner * sin


def _qkv_kernel(x_ref, nw_ref, mod_ref, w_ref, *rest, rope, cache, n_aliased):
    rest = list(rest)
    if rope:
        cos_ref, sin_ref = rest[:2]
        rest = rest[2:]
    rest = rest[n_aliased:]
    o_ref, g_ref = rest[:2]
    cache_refs = rest[2:-1]
    h_scr = rest[-1]
    j = pl.program_id(1)

    @pl.when(j == 0)
    def _():
        h = _rms(x_ref[...], nw_ref[...]) * (1.0 + mod_ref[0, 1:2, :]) + mod_ref[0, 0:1, :]
        h_scr[...] = h.astype(BF16)

    def tile_kind(cond, store):
        @pl.when(cond)
        def _():
            for r in range(h_scr.shape[0] // QKV_ROW_CHUNK):
                rows = slice(r * QKV_ROW_CHUNK, (r + 1) * QKV_ROW_CHUNK)
                store(rows, _dot(h_scr[rows, :], w_ref[...]))

    def store_batches(ref, rows, val):
        seq = ref.shape[1]
        for bb in range(QKV_ROW_CHUNK // seq):
            ref[rows.start // seq + bb] = val[bb * seq:(bb + 1) * seq]

    def roped(rows, acc, c):
        sl = slice(c * LANES, (c + 1) * LANES)
        return _rope128(acc[:, sl], cos_ref[rows, :], sin_ref[rows, :]).astype(BF16)

    def store_plain(rows, acc):
        o_ref[rows, :] = acc.astype(BF16)

    def store_ak(rows, acc):
        o_ref[rows, :] = (acc * (DH_A ** -0.5)).astype(BF16)

    def store_misc(rows, acc):
        o_ref[rows, :] = acc.astype(BF16)
        if rope:
            o_ref[rows, :LANES] = roped(rows, acc, 0)
        g_ref[rows, :] = acc[:, MISC_GATE_OFF:MISC_GATE_OFF + LANES]
        if cache:
            store_batches(cache_refs[-2], rows, acc[:, :LANES])
            store_batches(cache_refs[-1], rows, acc[:, LANES:2 * LANES])

    def store_rope(rows, acc):
        for c in range(COL_TILE // LANES):
            o_ref[rows, c * LANES:(c + 1) * LANES] = roped(rows, acc, c)

    special = (j == T_AK) | (j == T_MISC)
    tile_kind(j == T_AK, store_ak)
    tile_kind(j == T_MISC, store_misc)
    if rope:
        is_rope = (j == T_CQ) | (j == T_DQ) | (j == T_DK)
        special = special | is_rope
        tile_kind(is_rope, store_rope)
    if cache:
        for t, ref in zip(CACHE_TILES, cache_refs[:-2]):
            def store_cached(rows, acc, ref=ref):
                o_ref[rows, :] = acc.astype(BF16)
                store_batches(ref, rows, acc)
            special = special | (j == t)
            tile_kind(j == t, store_cached)
    tile_kind(jnp.logical_not(special), store_plain)


def _qkv_proj(x, norm_w, mod3, w_in_r, l, rope_tabs, cache, tm):
    n = x.shape[0]
    tiles_per_mod = n // mod3.shape[0] // tm
    rope = rope_tabs is not None
    in_specs = [
        pl.BlockSpec((tm, D_MODEL), lambda i, j: (i, 0)),
        pl.BlockSpec((1, D_MODEL), lambda i, j: (0, 0)),
        pl.BlockSpec((1, 6, D_MODEL), lambda i, j: (i // tiles_per_mod, 0, 0)),
        pl.BlockSpec((None, D_MODEL, COL_TILE), lambda i, j: (l, 0, j)),
    ]
    args = [x, norm_w.reshape(1, D_MODEL), mod3, w_in_r]
    if rope:
        tiles_per_seq = rope_tabs[0].shape[0] // tm
        in_specs += [pl.BlockSpec((tm, LANES), lambda i, j: (i % tiles_per_seq, 0))] * 2
        args += list(rope_tabs)
    row = lambda w: pl.BlockSpec((tm, w), lambda i, j: (i, 0))
    out_shape = [jax.ShapeDtypeStruct((n, N_PROJ), BF16), jax.ShapeDtypeStruct((n, LANES), F32)]
    out_specs = [pl.BlockSpec((tm, COL_TILE), lambda i, j: (i, j)), row(LANES)]
    aliases = {}
    n_aliased = 0
    if cache:
        seq, layer = cache["seq"], cache["layer"]
        widths = [COL_TILE] * len(CACHE_TILES) + [LANES, LANES]
        out_shape += [jax.ShapeDtypeStruct((n // seq, DEPTH, seq, w), F32) for w in widths]
        out_specs += [pl.BlockSpec((tm // seq, None, seq, w), lambda i, j: (i, layer, 0, 0)) for w in widths]
        prev = cache["prev"] or [jnp.zeros(s.shape, s.dtype) for s in out_shape[2:]]
        n_aliased = len(widths)
        aliases = {len(args) + k: 2 + k for k in range(n_aliased)}
        in_specs += [pl.BlockSpec(memory_space=pl.ANY)] * n_aliased
        args += list(prev)
    return pl.pallas_call(
        functools.partial(_qkv_kernel, rope=rope, cache=bool(cache), n_aliased=n_aliased),
        input_output_aliases=aliases,
        out_shape=tuple(out_shape),
        grid=(n // tm, N_COL_TILES),
        in_specs=in_specs,
        out_specs=tuple(out_specs),
        scratch_shapes=[pltpu.VMEM((tm, D_MODEL), BF16)],
        compiler_params=_cparams("parallel", "arbitrary"),
        name="qkv_rope" if rope else "qkv",
    )(*args)


def _log_sigmoid(x):
    return jnp.minimum(x, 0.0) - jnp.log1p(jnp.exp(-jnp.abs(x)))


def _mlstm_chunk(c, h, d, state, q_ref, k_ref, v_ref, gr_ref, gc_ref, out_ref):
    L = MLSTM_CHUNK
    C, n, m = state
    row = lax.broadcasted_iota(jnp.int32, (L, L), 0)
    col = lax.broadcasted_iota(jnp.int32, (L, L), 1)
    tri = (col <= row) if d == 0 else (col >= row)
    tri_t = (row <= col) if d == 0 else (row >= col)
    off = pl.multiple_of(c * L, L)
    hs = slice(h * DH_A, (h + 1) * DH_A)
    qb = q_ref[pl.ds(off, L), hs]
    kb = k_ref[pl.ds(off, L), hs]
    vb = v_ref[pl.ds(off, L), hs]
    kx, vx = kb.astype(F32), vb.astype(F32)
    g_rows = gr_ref[h, c]
    g_cols = gc_ref[h, c]
    ig_r = g_rows[2 * d:2 * d + 1, :]
    lf_r = _log_sigmoid(g_rows[2 * d + 1:2 * d + 2, :])
    ig_c = g_cols[:, 2 * d:2 * d + 1]
    lf_c = _log_sigmoid(g_cols[:, 2 * d + 1:2 * d + 2])
    b_c = jnp.sum(jnp.where(tri, lf_r, 0.0), axis=1, keepdims=True)
    b_r = jnp.sum(jnp.where(tri_t, lf_c, 0.0), axis=0, keepdims=True)
    qk = _dot_nt(qb, kb)
    c_ext = jnp.concatenate([C.astype(BF16), jnp.broadcast_to(n.astype(BF16), (DH_A, DH_A))], axis=0)
    qc = _dot_nt(qb, c_ext)
    yield
    logd = jnp.where(tri, b_c - b_r + ig_r, NEG_INF)
    inter = jnp.broadcast_to(b_c + m, (L, L))
    m_t = jnp.maximum(inter, jnp.broadcast_to(jnp.max(logd, axis=1, keepdims=True), (L, L)))
    yield
    a = jnp.exp(inter - m_t)
    s = qk * jnp.exp(logd - m_t)
    sv = _dot(s.astype(BF16), _with_ones(vb))
    num = a * qc[:, :DH_A] + sv[:, :DH_A]
    den = a * qc[:, DH_A:] + sv[:, DH_A:]
    yield
    out_ref[pl.ds(off, L), hs] = num / jnp.maximum(jnp.abs(den), jnp.exp(-m_t))
    b_tot = jnp.sum(lf_r, axis=1, keepdims=True)
    g_c = b_tot - b_c + ig_c
    m_new = jnp.maximum(b_tot + m, jnp.max(g_c, axis=0, keepdims=True))
    yield
    decay = jnp.exp(b_tot + m - m_new)
    wk = jnp.exp(g_c - m_new)
    C_new = decay * C + _dot_tn((wk * vx).astype(BF16), kb)
    n_new = decay * n + jnp.sum(wk * kx, axis=0, keepdims=True)
    return C_new, n_new, m_new


def _run_staged(gens):
    results = [None] * len(gens)
    live = list(range(len(gens)))
    while live:
        for i in list(live):
            try:
                next(gens[i])
            except StopIteration as stop:
                results[i] = stop.value
                live.remove(i)
    return results


def _mlstm_kernel(q_ref, k_ref, v_ref, gr_ref, gc_ref, c0_ref, n0_ref, m0_ref,
                  h_ref, cs_ref, ns_ref, ms_ref, hb_scr, *, nc):
    chains = [(h, d) for h in range(H_A) for d in range(2)]

    def body(ci, states):
        gens = [_mlstm_chunk(ci if d == 0 else nc - 1 - ci, h, d, state, q_ref, k_ref, v_ref, gr_ref, gc_ref,
                             h_ref if d == 0 else hb_scr) for (h, d), state in zip(chains, states)]
        return tuple(_run_staged(gens))

    init = tuple((c0_ref[0, d, h], n0_ref[0, h, d:d + 1, :], m0_ref[0, h, d:d + 1, 0:1]) for h, d in chains)
    final = lax.fori_loop(0, nc, body, init)
    for (h, d), (C, n, m) in zip(chains, final):
        cs_ref[0, d, h] = C
        ns_ref[0, h, d:d + 1, :] = n
        ms_ref[0, h, d:d + 1, :] = jnp.broadcast_to(m, (1, LANES))
    h_ref[...] += hb_scr[...]


def _mlstm(qkv, gates, C0, l, n0, m0, batch, seq):
    nc = seq // MLSTM_CHUNK
    L = MLSTM_CHUNK
    ig = gates[:, :8].reshape(batch, nc, L, 2, H_A)
    fg = gates[:, 8:].reshape(batch, nc, L, 2, H_A)
    g4 = jnp.stack([ig[..., 0, :], fg[..., 0, :], ig[..., 1, :], fg[..., 1, :]], axis=-1)
    g_cols = jnp.transpose(g4, (3, 0, 1, 2, 4)).reshape(H_A, batch * nc, L, 4)
    g_rows = jnp.transpose(g4, (3, 0, 1, 4, 2)).reshape(H_A, batch * nc, 4, L)
    n0t = jnp.transpose(n0, (0, 2, 1, 3))
    m0t = jnp.broadcast_to(jnp.transpose(m0, (0, 2, 1))[..., None], (batch, H_A, 2, LANES))
    qkv_spec = lambda tile: pl.BlockSpec((seq, COL_TILE), lambda b: (b, tile))
    c_spec = pl.BlockSpec((1, 2, H_A, DH_A, DH_A), lambda b: (b, 0, 0, 0, 0))
    c0_spec = pl.BlockSpec((1, None, 2, H_A, DH_A, DH_A), lambda b: (b, l, 0, 0, 0, 0))
    state_spec = pl.BlockSpec((1, H_A, 2, LANES), lambda b: (b, 0, 0, 0))
    h, Cs, ns, ms = pl.pallas_call(
        functools.partial(_mlstm_kernel, nc=nc),
        out_shape=(
            jax.ShapeDtypeStruct((batch * seq, H_A * DH_A), F32),
            jax.ShapeDtypeStruct((batch, 2, H_A, DH_A, DH_A), F32),
            jax.ShapeDtypeStruct((batch, H_A, 2, DH_A), F32),
            jax.ShapeDtypeStruct((batch, H_A, 2, LANES), F32),
        ),
        grid=(batch,),
        in_specs=[
            qkv_spec(T_AQ), qkv_spec(T_AK), qkv_spec(T_AV),
            pl.BlockSpec((H_A, nc, 4, L), lambda b: (0, b, 0, 0)),
            pl.BlockSpec((H_A, nc, L, 4), lambda b: (0, b, 0, 0)),
            c0_spec, state_spec, state_spec,
        ],
        out_specs=(pl.BlockSpec((seq, H_A * DH_A), lambda b: (b, 0)), c_spec, state_spec, state_spec),
        scratch_shapes=[pltpu.VMEM((seq, H_A * DH_A), F32)],
        compiler_params=_cparams("parallel"),
        name="mlstm",
    )(qkv, qkv, qkv, g_rows, g_cols, C0, n0t, m0t)
    return h, Cs, jnp.transpose(ns, (0, 2, 1, 3)), jnp.transpose(ms[..., 0], (0, 2, 1))


ATTN_SCALE = DH_B ** -0.5


def _half_mask(shape, half):
    lane = lax.broadcasted_iota(jnp.int32, shape, 1)
    return (lane >= HALF) if half else (lane < HALF)


def _select_half(q, half):
    return jnp.where(_half_mask(q.shape, half), q, jnp.zeros_like(q))


def _swap_halves(x):
    return pltpu.roll(x.astype(F32), HALF, 1).astype(x.dtype)


def _with_ones(v):
    return jnp.concatenate([v, jnp.ones_like(v)], axis=1)


def _softmax_pv(scores, vs, extra=None):
    m = functools.reduce(jnp.maximum, [jnp.max(s, axis=1, keepdims=True) for s in scores])
    if extra is not None:
        m = jnp.maximum(m, extra)
    acc = functools.reduce(lambda x, y: x + y,
                           [_dot(jnp.exp((s - m).astype(BF16)), v) for s, v in zip(scores, vs)])
    denom = acc[:, LANES:]
    if extra is not None:
        denom = denom + jnp.exp(extra - m)
    return acc[:, :LANES] / denom


def _diff_head(q, ks, vs, lam):
    scores = [[_dot_nt(_select_half(q, comp), k) for k in ks] for comp in range(2)]
    o1, o2 = [_softmax_pv(s, vs) for s in scores]
    return o1 - lam * o2


def _ctx_attn_kernel(lam_ref, sink_ref, bq, bk, bv, cq, misc, dq, dk, dv, yb_ref, yc_ref, yd_ref):
    pairs = [slice(p * LANES, (p + 1) * LANES) for p in range(H_B // 2)]
    scores = [[[_dot_nt(_select_half(bq[:, sl] * ATTN_SCALE, half), bk[:, sl])] for half in range(2)] for sl in pairs]
    for sl, pair_scores in zip(pairs, scores):
        v = _with_ones(bv[:, sl])
        outs = [_softmax_pv(s, [v]) for s in pair_scores]
        yb_ref[:, sl] = jnp.where(_half_mask(outs[0].shape, 0), outs[0], outs[1]).astype(yb_ref.dtype)

    kc = misc[:, 0:LANES]
    vc = _with_ones(misc[:, LANES:2 * LANES])
    kv_of = lambda head: head // (H_C // KV_C)
    scores = []
    for p, sl in enumerate(pairs):
        pair_scores = []
        for half in range(2):
            qm = _select_half(cq[:, sl] * ATTN_SCALE, half)
            if kv_of(2 * p + half) != half:
                qm = _swap_halves(qm)
            pair_scores.append([_dot_nt(qm, kc)])
        scores.append(pair_scores)
    for p, (sl, pair_scores) in enumerate(zip(pairs, scores)):
        outs = []
        for half, s in enumerate(pair_scores):
            o = _softmax_pv(s, [vc], extra=sink_ref[2 * p + half])
            outs.append(o if kv_of(2 * p + half) == half else _swap_halves(o))
        yc_ref[:, sl] = jnp.where(_half_mask(outs[0].shape, 0), outs[0], outs[1]).astype(yc_ref.dtype)

    lam = lam_ref[0]
    for h in range(H_D):
        sl = slice(h * LANES, (h + 1) * LANES)
        yd_ref[:, sl] = _diff_head(dq[:, sl] * ATTN_SCALE, [dk[:, sl]], [_with_ones(dv[:, sl])], lam)


def _ctx_attention(qkv, lam, sink, batch, seq):
    tile = lambda t: pl.BlockSpec((seq, COL_TILE), lambda b: (b, t))
    smem = pl.BlockSpec(memory_space=pltpu.SMEM)
    out = lambda dt: jax.ShapeDtypeStruct((batch * seq, BRANCH_W), dt)
    out_spec = pl.BlockSpec((seq, BRANCH_W), lambda b: (b, 0))
    return pl.pallas_call(
        _ctx_attn_kernel,
        out_shape=(out(BF16), out(BF16), out(F32)),
        grid=(batch,),
        in_specs=[smem, smem, tile(T_BQ), tile(T_BK), tile(T_BV), tile(T_CQ), tile(T_MISC),
                  tile(T_DQ), tile(T_DK), tile(T_DV)],
        out_specs=(out_spec, out_spec, out_spec),
        compiler_params=_cparams("parallel"),
        name="ctx_attention",
    )(lam, sink, *([qkv] * 8))


NAT_QROWS = 4
NAT_KROWS = 12
NAT_RI = 2 * NA_ROWS - 1
NAT_BLOCKS_PER_STEP = 8


def _nat_kernel(q_ref, k_ref, v_ref, kc_ref, vc_ref, cb_ref, o_ref):
    rows = k_ref.shape[0] // GRID_W
    nq = NAT_QROWS * GRID_W
    nk = NAT_KROWS * GRID_W
    kc = kc_ref[0].astype(BF16)
    vc = _with_ones(vc_ref[0].astype(BF16))
    left = _half_mask((GRID_W, LANES), 0)
    scores, values = [], []
    for sub in range(NAT_BLOCKS_PER_STEP):
        r0 = (pl.program_id(2) * NAT_BLOCKS_PER_STEP + sub) * NAT_QROWS
        kb = jnp.clip(r0 - NA_ROWS // 2, 0, rows - NAT_KROWS)
        koff = pl.multiple_of(kb * GRID_W, GRID_W)
        q = q_ref[sub * nq:(sub + 1) * nq, :] * ATTN_SCALE
        k = k_ref[pl.ds(koff, nk), :]
        v = _with_ones(v_ref[pl.ds(koff, nk), :])
        for half in range(2):
            bias_rows = []
            for a in range(NAT_QROWS):
                r = r0 + a
                rs = jnp.clip(r - NA_ROWS // 2, 0, rows - NA_ROWS)
                blocks = []
                for ip in range(NAT_KROWS // 2):
                    sides = []
                    for side in range(2):
                        kr = kb + 2 * ip + side
                        ok = (kr >= rs) & (kr < rs + NA_ROWS)
                        ri = jnp.clip(kr - r + NA_ROWS - 1, 0, NAT_RI - 1)
                        sides.append(jnp.where(ok, cb_ref[half, ri], NEG_INF))
                    blocks.append(jnp.where(left, sides[0], sides[1]))
                bias_rows.append(jnp.concatenate(blocks, axis=1))
            bias = jnp.concatenate(bias_rows, axis=0)
            qm = _select_half(q, half)
            scores.append([_dot_nt(qm, k) + bias, _dot_nt(qm, kc)])
            values.append([v, vc])
    outs = [_softmax_pv(s, vs) for s, vs in zip(scores, values)]
    for sub in range(NAT_BLOCKS_PER_STEP):
        o_ref[sub * nq:(sub + 1) * nq, :] = jnp.where(_half_mask((nq, LANES), 0), outs[2 * sub],
                                                      outs[2 * sub + 1]).astype(o_ref.dtype)


def _nat_col_table(rpb):
    qc = np.arange(GRID_W)[:, None]
    kc = np.arange(GRID_W)[None, :]
    cs = np.clip(qc - NA_COLS // 2, 0, GRID_W - NA_COLS)
    valid = (kc >= cs) & (kc < cs + NA_COLS)
    n_ci = 2 * NA_COLS - 1
    lo = GRID_W - 1 - (NA_COLS - 1)
    v = jnp.pad(rpb, ((0, 0), (0, 0), (lo, 2 * GRID_W - n_ci - lo)))
    flat = jnp.tile(v, (1, 1, GRID_W))[..., :GRID_W * (2 * GRID_W - 1)]
    toep = flat.reshape(rpb.shape[0], rpb.shape[1], GRID_W, 2 * GRID_W - 1)[..., GRID_W - 1:]
    tab = jnp.where(valid[None, None], toep, NEG_INF)
    return jnp.concatenate([tab, tab], axis=-1)


def _nat_latent(qkv, kc, vc, cb, l, batch, seq):
    nq = NAT_BLOCKS_PER_STEP * NAT_QROWS * GRID_W
    tpb = seq // nq
    past = kc.shape[2]
    return pl.pallas_call(
        _nat_kernel,
        out_shape=jax.ShapeDtypeStruct((batch * seq, BRANCH_W), BF16),
        grid=(batch, H_B // 2, tpb),
        in_specs=[
            pl.BlockSpec((nq, LANES), lambda b, p, i: (b * tpb + i, T_BQ * 4 + p)),
            pl.BlockSpec((seq, LANES), lambda b, p, i: (b, T_BK * 4 + p)),
            pl.BlockSpec((seq, LANES), lambda b, p, i: (b, T_BV * 4 + p)),
            pl.BlockSpec((1, None, past, LANES), lambda b, p, i: (b, l, 0, p)),
            pl.BlockSpec((1, None, past, LANES), lambda b, p, i: (b, l, 0, p)),
            pl.BlockSpec((2, NAT_RI, GRID_W, LANES), lambda b, p, i: (l * (H_B // 2) + p, 0, 0, 0)),
        ],
        out_specs=pl.BlockSpec((nq, LANES), lambda b, p, i: (b * tpb + i, p)),
        compiler_params=_cparams("parallel", "parallel", "parallel"),
        name="nat_latent",
    )(qkv, qkv, qkv, kc, vc, cb)


SWA_QBLK = 256
SWA_KBLK = SWA_QBLK + 2 * SWA_WINDOW
SWA_BLOCKS_PER_STEP = 8


def _swa_kernel(sink_ref, q_ref, kv_ref, kc_ref, vc_ref, o_ref):
    seq = kv_ref.shape[0]
    p = pl.program_id(1)
    kc = kc_ref[0].astype(BF16)
    vc = _with_ones(vc_ref[0].astype(BF16))
    kv_half = p // ((H_C // 2) // KV_C)
    scores, values = [], []
    for sub in range(SWA_BLOCKS_PER_STEP):
        t0 = (pl.program_id(2) * SWA_BLOCKS_PER_STEP + sub) * SWA_QBLK
        k0 = pl.multiple_of(jnp.clip(t0 - SWA_WINDOW, 0, seq - SWA_KBLK), SWA_WINDOW)
        q = q_ref[sub * SWA_QBLK:(sub + 1) * SWA_QBLK, :] * ATTN_SCALE
        k = kv_ref[pl.ds(k0, SWA_KBLK), 0:LANES]
        v = _with_ones(kv_ref[pl.ds(k0, SWA_KBLK), LANES:2 * LANES])
        tq = t0 + lax.broadcasted_iota(jnp.int32, (SWA_QBLK, SWA_KBLK), 0)
        tk = k0 + lax.broadcasted_iota(jnp.int32, (SWA_QBLK, SWA_KBLK), 1)
        inside = jnp.abs(tq - tk) <= SWA_WINDOW
        for half in range(2):
            qm = _select_half(q, half)
            qm = jnp.where(kv_half == half, qm, _swap_halves(qm))
            scores.append([jnp.where(inside, _dot_nt(qm, k), NEG_INF), _dot_nt(qm, kc)])
            values.append([v, vc])
    for sub in range(SWA_BLOCKS_PER_STEP):
        outs = []
        for half in range(2):
            o = _softmax_pv(scores[2 * sub + half], values[2 * sub + half], extra=sink_ref[2 * p + half])
            outs.append(jnp.where(kv_half == half, o, _swap_halves(o)))
        o_ref[sub * SWA_QBLK:(sub + 1) * SWA_QBLK, :] = jnp.where(_half_mask(outs[0].shape, 0), outs[0],
                                                                   outs[1]).astype(o_ref.dtype)


def _swa_latent(qkv, kc, vc, sink, l, batch, seq):
    nq = SWA_BLOCKS_PER_STEP * SWA_QBLK
    tpb = seq // nq
    past = kc.shape[2]
    return pl.pallas_call(
        _swa_kernel,
        out_shape=jax.ShapeDtypeStruct((batch * seq, BRANCH_W), BF16),
        grid=(batch, H_C // 2, tpb),
        in_specs=[
            pl.BlockSpec(memory_space=pltpu.SMEM),
            pl.BlockSpec((nq, LANES), lambda b, p, i: (b * tpb + i, T_CQ * 4 + p)),
            pl.BlockSpec((seq, COL_TILE), lambda b, p, i: (b, T_MISC)),
            pl.BlockSpec((1, None, past, LANES), lambda b, p, i: (b, l, 0, 0)),
            pl.BlockSpec((1, None, past, LANES), lambda b, p, i: (b, l, 0, 0)),
        ],
        out_specs=pl.BlockSpec((nq, LANES), lambda b, p, i: (b * tpb + i, p)),
        compiler_params=_cparams("parallel", "parallel", "parallel"),
        name="swa_latent",
    )(sink, qkv, qkv, kc, vc)


DIFF_QBLK = 1024


def _diff_kernel(lam_ref, q_ref, k_ref, v_ref, kc_ref, vc_ref, o_ref):
    q = q_ref[...] * ATTN_SCALE
    ks = [k_ref[...], kc_ref[0].astype(BF16)]
    vs = [_with_ones(v_ref[...]), _with_ones(vc_ref[0].astype(BF16))]
    o_ref[...] = _diff_head(q, ks, vs, lam_ref[0])


def _diff_latent(qkv, kc, vc, lam, l, batch, seq):
    tpb = seq // DIFF_QBLK
    past = kc.shape[2]
    return pl.pallas_call(
        _diff_kernel,
        out_shape=jax.ShapeDtypeStruct((batch * seq, BRANCH_W), F32),
        grid=(batch, H_D, tpb),
        in_specs=[
            pl.BlockSpec(memory_space=pltpu.SMEM),
            pl.BlockSpec((DIFF_QBLK, LANES), lambda b, h, i: (b * tpb + i, T_DQ * 4 + h)),
            pl.BlockSpec((seq, LANES), lambda b, h, i: (b, T_DK * 4 + h)),
            pl.BlockSpec((seq, LANES), lambda b, h, i: (b, T_DV * 4 + h)),
            pl.BlockSpec((1, None, past, LANES), lambda b, h, i: (b, l, 0, h)),
            pl.BlockSpec((1, None, past, LANES), lambda b, h, i: (b, l, 0, h)),
        ],
        out_specs=pl.BlockSpec((DIFF_QBLK, LANES), lambda b, h, i: (b * tpb + i, h)),
        compiler_params=_cparams("parallel", "parallel", "parallel"),
        name="diff_latent",
    )(lam, qkv, qkv, qkv, kc, vc)


def _rms_heads(y, g_ref, per_head_gain):
    outs = []
    for c in range(BRANCH_W // LANES):
        sl = slice(c * LANES, (c + 1) * LANES)
        g = g_ref[:, sl] if per_head_gain else g_ref[...]
        outs.append(_rms(y[:, sl], g))
    return jnp.concatenate(outs, axis=1)


def _first_lane(hit, lane_f):
    return jnp.min(jnp.where(hit, lane_f, float(LANES)), axis=1, keepdims=True)


def _top_group(logits):
    lane = lax.broadcasted_iota(jnp.int32, logits.shape, 1)
    gl = jnp.where(lane < N_GROUPS, logits, NEG_INF)
    return _first_lane(gl == jnp.max(gl, axis=1, keepdims=True), lane.astype(F32))


def _gate_weights(logits, group):
    lane = lax.broadcasted_iota(jnp.int32, logits.shape, 1)
    lane_f = lane.astype(F32)
    gl = jnp.where(lane < N_GROUPS, logits, NEG_INF)
    gmax = jnp.max(gl, axis=1, keepdims=True)
    g_logit = jnp.sum(jnp.where(lane_f == group, logits, 0.0), axis=1, keepdims=True)
    gw = jnp.exp(g_logit - gmax) / jnp.sum(jnp.exp(gl - gmax), axis=1, keepdims=True)
    e_lane = lane - N_GROUPS
    e_group = lax.shift_right_arithmetic(e_lane, EXP_PER_GROUP.bit_length() - 1).astype(F32)
    in_group = (e_lane >= 0) & (e_lane < N_EXPERTS) & (e_group == group)
    el = jnp.where(in_group, logits, NEG_INF)
    e1 = jnp.max(el, axis=1, keepdims=True)
    i1 = _first_lane(el == e1, lane_f)
    el2 = jnp.where(lane_f == i1, NEG_INF, el)
    e2 = jnp.max(el2, axis=1, keepdims=True)
    i2 = _first_lane(el2 == e2, lane_f)
    t = jnp.exp(e2 - e1)
    w1 = gw / (1.0 + t)
    return jnp.where(lane_f == i1, w1, jnp.where(lane_f == i2, w1 * t, 0.0))


def _merge_kernel(ya, yb, yc, yd, ao, gt0, gt1, gt2, gt3, x_ref, mod_ref, anorm, subln, nffn,
                  wbr, wout, wr, xo_ref, h2_ref, group_ref, *, yd_scale):
    a = _sigmoid(ao[...].astype(F32)) * _rms_heads(ya[...], anorm, True)
    d = _rms_heads(yd[...], subln, False) * yd_scale
    branches = (a, yb[...], yc[...], d)
    mix = None
    for n, (br, gt) in enumerate(zip(branches, (gt0, gt1, gt2, gt3))):
        term = _sigmoid(gt[...].astype(F32)) * _dot(br.astype(BF16), wbr[n])
        mix = term if mix is None else mix + term
    out = _dot(mix.astype(BF16), wout[...])
    xn = x_ref[...] + mod_ref[0, 2:3, :] * out
    xo_ref[...] = xn
    h2 = _rms(xn, nffn[...]) * (1.0 + mod_ref[0, 4:5, :]) + mod_ref[0, 3:4, :]
    h2_ref[...] = h2
    group_ref[...] = jnp.broadcast_to(_top_group(_dot(h2.astype(BF16), wr[...])), group_ref.shape)


def _merge(ys, qkv, x, mod3, a_norm, subln, norm_ffn, w_branch, w_out, w_route, lam_init, l):
    n = x.shape[0]
    tm = 512
    tiles_per_mod = n // mod3.shape[0] // tm
    y_spec = pl.BlockSpec((tm, BRANCH_W), lambda i: (i, 0))
    gt_spec = lambda k: pl.BlockSpec((tm, D_MODEL), lambda i: (i, T_GT * COL_TILE // D_MODEL + k))
    full = lambda shape: pl.BlockSpec(shape, lambda i: (0,) * len(shape))
    layer = lambda shape: pl.BlockSpec((None,) + shape, lambda i: (l,) + (0,) * len(shape))
    x_spec = pl.BlockSpec((tm, D_MODEL), lambda i: (i, 0))
    return pl.pallas_call(
        functools.partial(_merge_kernel, yd_scale=1.0 - lam_init),
        out_shape=(
            jax.ShapeDtypeStruct((n, D_MODEL), F32),
            jax.ShapeDtypeStruct((n, D_MODEL), F32),
            jax.ShapeDtypeStruct((n, LANES), F32),
        ),
        grid=(n // tm,),
        in_specs=[
            y_spec, y_spec, y_spec, y_spec,
            pl.BlockSpec((tm, COL_TILE), lambda i: (i, T_AO)),
            gt_spec(0), gt_spec(1), gt_spec(2), gt_spec(3),
            x_spec,
            pl.BlockSpec((1, 6, D_MODEL), lambda i: (i // tiles_per_mod, 0, 0)),
            full((1, BRANCH_W)), full((1, LANES)), full((1, D_MODEL)),
            layer((N_BRANCH, BRANCH_W, D_MODEL)), layer((D_MODEL, D_MODEL)), layer((D_MODEL, LANES)),
        ],
        out_specs=(x_spec, x_spec, pl.BlockSpec((tm, LANES), lambda i: (i, 0))),
        compiler_params=_cparams("parallel"),
        name="merge_route",
    )(*ys, qkv, qkv, qkv, qkv, qkv, x, mod3, a_norm.reshape(1, BRANCH_W), subln.reshape(1, LANES),
      norm_ffn.reshape(1, D_MODEL), w_branch, w_out, w_route)


MOE_TILE = 512
ROW_DMA_UNROLL = 16


def _row_copies(src_row, dst_row, sem, n):
    def copy(i):
        return pltpu.make_async_copy(src_row(i), dst_row(i), sem)

    def start(i, c):
        copy(i).start()
        return c

    def wait(i, c):
        copy(i).wait()
        return c

    lax.fori_loop(0, n, start, 0, unroll=ROW_DMA_UNROLL)
    lax.fori_loop(0, n, wait, 0, unroll=ROW_DMA_UNROLL)


def _moe_scatter_kernel(slot_ref, h_ref, init_ref, sorted_ref, sem):
    del init_ref
    base = pl.program_id(0) * MOE_TILE
    _row_copies(lambda i: h_ref.at[pl.ds(i, 1), :],
                lambda i: sorted_ref.at[pl.ds(slot_ref[base + i], 1), :], sem, MOE_TILE)


def _moe_group_kernel(tile_group_ref, n_used_ref, h_ref, wr_ref, wg_f32, wu_f32, wd_f32, o_ref,
                      wg_ref, wu_ref, wd_ref):
    t = pl.program_id(0)
    group = tile_group_ref[t]

    @pl.when((t == 0) | (group != tile_group_ref[jnp.maximum(t - 1, 0)]))
    def _():
        for e in range(EXP_PER_GROUP):
            wg_ref[e] = wg_f32[0, e].astype(BF16)
            wu_ref[e] = wu_f32[0, e].astype(BF16)
            wd_ref[e] = wd_f32[0, e].astype(BF16)

    @pl.when(t < n_used_ref[0])
    def _():
        h = h_ref[...].astype(BF16)
        gates = _gate_weights(_dot(h, wr_ref[...]), group.astype(F32))
        lane = lax.broadcasted_iota(jnp.int32, gates.shape, 1)
        acc = None
        for e in range(EXP_PER_GROUP):
            g = jnp.sum(jnp.where(lane == N_GROUPS + group * EXP_PER_GROUP + e, gates, 0.0), axis=1, keepdims=True)
            hg = _dot(h, wg_ref[e])
            hid = hg * _sigmoid(hg) * _dot(h, wu_ref[e]) * g
            term = _dot(hid.astype(BF16), wd_ref[e])
            acc = term if acc is None else acc + term
        o_ref[...] = acc

    @pl.when(t >= n_used_ref[0])
    def _():
        o_ref[...] = jnp.zeros_like(o_ref)


def _moe_gather_kernel(slot_ref, y_ref, x_ref, mod_ref, nf_ref, o_ref, buf, sem, *, final_norm):
    t = pl.program_id(0)
    nt = pl.num_programs(0)

    def copy(tile, b, i):
        return pltpu.make_async_copy(y_ref.at[pl.ds(slot_ref[tile * MOE_TILE + i], 1), :],
                                     buf.at[b, pl.ds(i, 1), :], sem.at[b])

    def start_tile(tile, b):
        def start(i, c):
            copy(tile, b, i).start()
            return c

        lax.fori_loop(0, MOE_TILE, start, 0, unroll=ROW_DMA_UNROLL)

    cur = t % 2

    @pl.when(t == 0)
    def _():
        start_tile(0, 0)

    @pl.when(t + 1 < nt)
    def _():
        start_tile(t + 1, 1 - cur)

    def wait(i, c):
        copy(t, cur, i).wait()
        return c

    lax.fori_loop(0, MOE_TILE, wait, 0, unroll=ROW_DMA_UNROLL)
    y = x_ref[...] + mod_ref[0, 5:6, :] * buf[cur]
    o_ref[...] = _rms(y, nf_ref[...]) if final_norm else y


def _moe(h2, group, x, mod3, w_route, wg, wu, wd, norm_final, final_norm, l, init=None):
    n = x.shape[0]
    tm = MOE_TILE
    n_tiles = n // tm + N_GROUPS
    onehot = (group[:, None] == jnp.arange(N_GROUPS, dtype=jnp.int32)[None]).astype(jnp.int32)
    csum = jnp.cumsum(onehot, axis=0)
    counts = csum[-1]
    rank = jnp.sum((csum - onehot) * onehot, axis=1)
    group_tiles = (counts + tm - 1) // tm
    tile_end = jnp.cumsum(group_tiles)
    group_start = (tile_end - group_tiles) * tm
    slot = (jnp.sum(onehot * group_start[None], axis=1) + rank).astype(jnp.int32)
    tile_group = jnp.minimum(jnp.sum(jnp.arange(n_tiles, dtype=jnp.int32)[:, None] >= tile_end[None], axis=1),
                             N_GROUPS - 1).astype(jnp.int32)
    n_used = tile_end[-1:].astype(jnp.int32)

    sorted_h = pl.pallas_call(
        _moe_scatter_kernel,
        out_shape=jax.ShapeDtypeStruct((n_tiles * tm, D_MODEL), F32),
        grid_spec=pltpu.PrefetchScalarGridSpec(
            num_scalar_prefetch=1, grid=(n // tm,),
            in_specs=[pl.BlockSpec((tm, D_MODEL), lambda i, slot: (i, 0)), pl.BlockSpec(memory_space=pl.ANY)],
            out_specs=pl.BlockSpec(memory_space=pl.ANY),
            scratch_shapes=[pltpu.SemaphoreType.DMA],
        ),
        input_output_aliases={2: 0},
        compiler_params=_cparams("arbitrary"),
        name="moe_scatter",
    )(slot, h2, jnp.zeros((n_tiles * tm, D_MODEL), F32) if init is None else init)

    grp = lambda shape: pl.BlockSpec((None, 1) + shape, lambda t, tg, nu: (l, tg[t], 0, 0, 0))
    sorted_y = pl.pallas_call(
        _moe_group_kernel,
        out_shape=jax.ShapeDtypeStruct((n_tiles * tm, D_MODEL), F32),
        grid_spec=pltpu.PrefetchScalarGridSpec(
            num_scalar_prefetch=2, grid=(n_tiles,),
            in_specs=[
                pl.BlockSpec((tm, D_MODEL), lambda t, tg, nu: (t, 0)),
                pl.BlockSpec((None, D_MODEL, LANES), lambda t, tg, nu: (l, 0, 0)),
                grp((EXP_PER_GROUP, D_MODEL, D_EXPERT)), grp((EXP_PER_GROUP, D_MODEL, D_EXPERT)),
                grp((EXP_PER_GROUP, D_EXPERT, D_MODEL)),
            ],
            out_specs=pl.BlockSpec((tm, D_MODEL), lambda t, tg, nu: (t, 0)),
            scratch_shapes=[pltpu.VMEM((EXP_PER_GROUP, D_MODEL, D_EXPERT), BF16),
                            pltpu.VMEM((EXP_PER_GROUP, D_MODEL, D_EXPERT), BF16),
                            pltpu.VMEM((EXP_PER_GROUP, D_EXPERT, D_MODEL), BF16)],
        ),
        compiler_params=_cparams("arbitrary"),
        name="moe_experts",
    )(tile_group, n_used, sorted_h, w_route, wg, wu, wd)

    tiles_per_mod = n // mod3.shape[0] // tm
    row = pl.BlockSpec((tm, D_MODEL), lambda i, slot: (i, 0))
    y = pl.pallas_call(
        functools.partial(_moe_gather_kernel, final_norm=final_norm),
        out_shape=jax.ShapeDtypeStruct((n, D_MODEL), F32),
        grid_spec=pltpu.PrefetchScalarGridSpec(
            num_scalar_prefetch=1, grid=(n // tm,),
            in_specs=[
                pl.BlockSpec(memory_space=pl.ANY), row,
                pl.BlockSpec((1, 6, D_MODEL), lambda i, slot: (i // tiles_per_mod, 0, 0)),
                pl.BlockSpec((1, D_MODEL), lambda i, slot: (0, 0)),
            ],
            out_specs=row,
            scratch_shapes=[pltpu.VMEM((2, tm, D_MODEL), F32), pltpu.SemaphoreType.DMA((2,))],
        ),
        compiler_params=_cparams("arbitrary"),
        name="moe_gather",
    )(slot, sorted_y, x, mod3, norm_final.reshape(1, D_MODEL))
    return y, sorted_h


W_IN_GATES = 2048
W_IN_N_GATES = 16
W_IN_TILE_SRC = {T_AQ: 0, T_AK: 512, T_AV: 1024, T_AO: 1536, T_BQ: 2064, T_BK: 2576, T_BV: 3088, T_CQ: 3600,
                 T_DQ: 4368, T_DK: 4880, T_DV: 5392, T_MISC: 4112, **{T_GT + k: 5904 + COL_TILE * k for k in range(8)}}


def _reorder_kernel(src_ref, wt_hbm, gates_ref, o_ref, buf, sem):
    step = pl.program_id(0) * N_COL_TILES + pl.program_id(1)
    n_steps = pl.num_programs(0) * N_COL_TILES

    def copy(s):
        start = pl.multiple_of(src_ref[s % N_COL_TILES], 8)
        return pltpu.make_async_copy(wt_hbm.at[s // N_COL_TILES, pl.ds(start, COL_TILE), :], buf.at[s % 2],
                                     sem.at[s % 2])

    @pl.when(step == 0)
    def _():
        copy(step).start()

    @pl.when(step + 1 < n_steps)
    def _():
        copy(step + 1).start()

    copy(step).wait()
    rows = buf[step % 2]

    @pl.when(pl.program_id(1) == T_MISC)
    def _():
        pad = jnp.zeros((COL_TILE - 2 * LANES - W_IN_N_GATES, D_MODEL), F32)
        tile = jnp.concatenate([rows[:2 * LANES], gates_ref[...], pad], axis=0)
        o_ref[...] = tile.T.astype(BF16)

    @pl.when(pl.program_id(1) != T_MISC)
    def _():
        o_ref[...] = rows.T.astype(BF16)


def _reorder_w_in(w_in):
    src = np.array([W_IN_TILE_SRC[t] for t in range(N_COL_TILES)], np.int32)
    wt = jnp.swapaxes(w_in, 1, 2)
    return pl.pallas_call(
        _reorder_kernel,
        out_shape=jax.ShapeDtypeStruct((DEPTH, D_MODEL, N_PROJ), BF16),
        grid_spec=pltpu.PrefetchScalarGridSpec(
            num_scalar_prefetch=1, grid=(DEPTH, N_COL_TILES),
            in_specs=[pl.BlockSpec(memory_space=pl.ANY),
                      pl.BlockSpec((None, W_IN_N_GATES, D_MODEL), lambda l, j, src: (l, W_IN_GATES // W_IN_N_GATES, 0))],
            out_specs=pl.BlockSpec((None, D_MODEL, COL_TILE), lambda l, j, src: (l, 0, j)),
            scratch_shapes=[pltpu.VMEM((2, COL_TILE, D_MODEL), F32), pltpu.SemaphoreType.DMA((2,))],
        ),
        compiler_params=_cparams("arbitrary", "arbitrary"),
        name="reorder_w_in",
    )(jnp.asarray(src), wt, wt)


def _rope_tables(seq):
    nf = DH_C // 4
    inv = ROPE_BASE ** (-jnp.arange(nf, dtype=F32) / nf)
    t = jnp.arange(seq)
    ang_r = (t // GRID_W).astype(F32)[:, None] * inv[None, :]
    ang_c = (t % GRID_W).astype(F32)[:, None] * inv[None, :]
    cos = jnp.concatenate([jnp.cos(ang_r)] * 2 + [jnp.cos(ang_c)] * 2, axis=1)
    sin = jnp.concatenate([-jnp.sin(ang_r), jnp.sin(ang_r), -jnp.sin(ang_c), jnp.sin(ang_c)], axis=1)
    return jnp.tile(cos, (1, 2)), jnp.tile(sin, (1, 2))


def _layer(l, x, mod3, seq, lat, p, prev_cache=None, moe_init=None):
    batch = x.shape[0] // seq
    lam_init = 0.8 - 0.6 * math.exp(-0.3 * l)
    ctx = lat is None
    proj = _qkv_proj(x, p["norm_mix"][l], mod3, p["w_in"], l, None if ctx else p["rope"],
                     dict(layer=l, seq=seq, prev=prev_cache) if ctx else None, tm=1024 if ctx else 2048)
    qkv, gates_raw = proj[:2]
    gates = gates_raw[:, :16] + jnp.concatenate([p["i_bias"][l].reshape(-1), p["f_bias"][l].reshape(-1)])[None]
    lam = (jnp.exp(jnp.sum(p["lq1"][l] * p["lk1"][l])) - jnp.exp(jnp.sum(p["lq2"][l] * p["lk2"][l]))
           + lam_init).reshape(1).astype(F32)
    sink = p["sink"][l]
    new = None
    if ctx:
        C0 = jnp.zeros((batch, 1, 2, H_A, DH_A, DH_A), F32)
        n0 = jnp.zeros((batch, 2, H_A, DH_A), F32)
        m0 = jnp.zeros((batch, 2, H_A), F32)
        ya, Cs, ns, ms = _mlstm(qkv, gates, C0, 0, n0, m0, batch, seq)
        yb, yc, yd = _ctx_attention(qkv, lam, sink, batch, seq)
        new = (proj[2:], (Cs, ns, ms))
    else:
        nat_k, nat_v, swa_k, swa_v, diff_k, diff_v, C0, n0, m0 = lat
        flat = lambda a: a.reshape(a.shape[:3] + (-1,))
        ya, _, _, _ = _mlstm(qkv, gates, C0, l, n0[:, l], m0[:, l], batch, seq)
        yb = _nat_latent(qkv, flat(nat_k), flat(nat_v), p["nat_table"], l, batch, seq)
        yc = _swa_latent(qkv, flat(swa_k), flat(swa_v), sink, l, batch, seq)
        yd = _diff_latent(qkv, flat(diff_k), flat(diff_v), lam, l, batch, seq)
    xn, h2, group = _merge((ya, yb, yc, yd), qkv, x, mod3, p["a_norm"][l], p["subln"][l],
                           p["norm_ffn"][l], p["w_branch"], p["w_out"], p["w_route"], lam_init, l)
    y, moe_buf = _moe(h2, group[:, 0].astype(jnp.int32), xn, mod3, p["w_route"], p["wg"], p["wu"], p["wd"],
                      p["norm_final"], final_norm=(l == DEPTH - 1), l=l, init=moe_init)
    return y, new, moe_buf


def kernel(x_prompt, x_sample, cache_nat_k, cache_nat_v, cache_swa_k, cache_swa_v, cache_diff_k, cache_diff_v, state_mlstm_C, state_mlstm_n, state_mlstm_m, c, c_ctx, norm_mix, norm_ffn, norm_final, w_mod, b_mod, w_in, mlstm_i_bias, mlstm_f_bias, mlstm_norm, nat_rpb, swa_sink, diff_lq1, diff_lk1, diff_lq2, diff_lk2, diff_subln, w_branch, w_out, router_group, router_expert, w_exp_gate, w_exp_up, w_exp_down):
    b_ctx, s_ctx, _ = x_prompt.shape
    b_lat, s_lat, _ = x_sample.shape
    cond8 = jnp.concatenate([c_ctx[None], c, jnp.zeros((8 - 1 - b_lat, D_MODEL), F32)], axis=0)
    mod = _modulation(cond8, w_mod, b_mod).reshape(DEPTH, 8, 6, D_MODEL)
    route_pad = jnp.zeros((DEPTH, D_MODEL, LANES - N_GROUPS - N_EXPERTS), F32)
    p = dict(
        norm_mix=norm_mix, norm_ffn=norm_ffn, norm_final=norm_final,
        w_in=_reorder_w_in(w_in),
        i_bias=mlstm_i_bias, f_bias=mlstm_f_bias, a_norm=mlstm_norm, sink=swa_sink,
        nat_table=_nat_col_table(nat_rpb.reshape((DEPTH * H_B,) + nat_rpb.shape[2:])),
        lq1=diff_lq1, lk1=diff_lk1, lq2=diff_lq2, lk2=diff_lk2, subln=diff_subln,
        w_branch=w_branch.astype(BF16), w_out=w_out.astype(BF16),
        w_route=jnp.concatenate([router_group, router_expert, route_pad], axis=-1).astype(BF16),
        wg=w_exp_gate.reshape(DEPTH, N_GROUPS, EXP_PER_GROUP, D_MODEL, D_EXPERT),
        wu=w_exp_up.reshape(DEPTH, N_GROUPS, EXP_PER_GROUP, D_MODEL, D_EXPERT),
        wd=w_exp_down.reshape(DEPTH, N_GROUPS, EXP_PER_GROUP, D_EXPERT, D_MODEL),
        rope=_rope_tables(s_lat),
    )
    yp = x_prompt.reshape(b_ctx * s_ctx, D_MODEL)
    ys = x_sample.reshape(b_lat * s_lat, D_MODEL)
    states = []
    cache = buf_ctx = buf_lat = None
    for l in range(DEPTH):
        yp, (cache, state), buf_ctx = _layer(l, yp, mod[l, 0:1], s_ctx, None, p, cache, buf_ctx)
        states.append(state)
        lat = (cache_nat_k, cache_nat_v, cache_swa_k, cache_swa_v, cache_diff_k, cache_diff_v,
               state_mlstm_C, state_mlstm_n, state_mlstm_m)
        ys, _, buf_lat = _layer(l, ys, mod[l, 1:1 + b_lat], s_lat, lat, p, moe_init=buf_lat)
    def stack_layers(parts):
        flat = [a.reshape(a.shape[0], 1, -1) for a in parts]
        return jnp.concatenate(flat, axis=1).reshape((parts[0].shape[0], DEPTH) + parts[0].shape[1:])

    bk, bv, dk, dv, ck, cv = cache
    heads = lambda a, h: a.reshape(a.shape[:3] + (h, a.shape[3] // h))
    caches = (heads(bk, H_B), heads(bv, H_B), heads(ck, KV_C), heads(cv, KV_C), heads(dk, H_D), heads(dv, H_D))
    stacked = tuple(stack_layers([states[l][k] for l in range(DEPTH)]) for k in range(3))
    return (yp.reshape(b_ctx, s_ctx, D_MODEL), ys.reshape(b_lat, s_lat, D_MODEL)) + caches + stacked
```
